```python
import math
import functools
import jax
import jax.numpy as jnp
from jax import lax
import numpy as np

D_MODEL = 1024
BATCH = 16
SEQ = 2048
DEPTH = 2

F32 = jnp.float32
GRID_W = 64
CTX_LEN = 256
N_BRANCH = 4
MIX_W = 512
HEAD_DIM = 64
RMS_EPS = 1e-6

HY_BANDS = 16
HY_EMB = 1 + 2 * HY_BANDS
HY_FFN = 64
HY_DECAY_TARGET = 1e-2
HY_SHORT_PCT = 0.3
HY_LONG_PCT = 1.5

RW_HEADS = MIX_W // HEAD_DIM
RW_DECAY_RANK = 64
RW_A_RANK = 64
RW_G_RANK = 128
RW_DECAY_SCALE = 0.606531
RW_GN_EPS = 64e-5
RW_SPLITS = (MIX_W, MIX_W, MIX_W, RW_DECAY_RANK, RW_DECAY_RANK, RW_A_RANK, RW_A_RANK, RW_G_RANK)
RW_IN = sum(RW_SPLITS)

S5_GROUP = 16
S5_GROUPS = MIX_W // S5_GROUP
S5_STATE = 64
S5_DT_MIN = 1e-3
S5_DT_MAX = 1e-1

GLA_HEADS = 4
GLA_DK = MIX_W // 2
GLA_DV = MIX_W
GLA_HK = GLA_DK // GLA_HEADS
GLA_HV = GLA_DV // GLA_HEADS
GLA_GATE_RANK = 16
GLA_GATE_NORM = 16.0
GLA_CHUNK = 64
GLA_SPLITS = (GLA_DK, GLA_DK, GLA_DV, GLA_DV, GLA_GATE_RANK, GLA_GATE_RANK)
GLA_IN = sum(GLA_SPLITS)

D_FF = 2816
N_EXPERTS = 8
TOP_K = 2
D_FF_EXPERT = 3584
MOE_BLOCK = 128
N_DENSE = (DEPTH + 1) // 2
N_MOE = DEPTH // 2

IN_SPLITS = (3 * MIX_W, RW_IN, MIX_W, GLA_IN, N_BRANCH * D_MODEL)
IN_TOTAL = sum(IN_SPLITS)

kernel_name = 'hybrid_hyena_rwkv7_s5_gla_moe_dit'


def _split(t, sizes):
    return jnp.split(t, np.cumsum(sizes)[:-1].tolist(), axis=-1)


def _rmsnorm(x, g):
    xf = x.astype(F32)
    return xf * lax.rsqrt(jnp.mean(xf * xf, axis=-1, keepdims=True) + RMS_EPS) * g


def _swiglu(h, w1, w3, w2):
    return (jax.nn.silu(h @ w1) * (h @ w3)) @ w2


def _flip(t):
    return jnp.flip(t, axis=1)


def _ident(t):
    return t


def _to_col_major(t):
    b, l, ch = t.shape
    rows = l // GRID_W
    return jnp.swapaxes(t.reshape(b, rows, GRID_W, ch), 1, 2).reshape(b, l, ch)


def _to_raster(t):
    b, l, ch = t.shape
    rows = l // GRID_W
    return jnp.swapaxes(t.reshape(b, GRID_W, rows, ch), 1, 2).reshape(b, l, ch)


def _conv3(u, w):
    up = jnp.pad(u, ((0, 0), (1, 1), (0, 0)))
    return up[:, :-2] * w[0] + up[:, 1:-1] * w[1] + up[:, 2:] * w[2]


def _hyena_filter(L, fw1, fb1, fw2, fb2, fw3, fb3):
    pos = jnp.arange(L, dtype=F32)
    t = pos / max(L - 1, 1)
    freqs = jnp.linspace(1e-4, HY_BANDS - 1, HY_BANDS, dtype=F32)
    ang = (2.0 * math.pi / L) * pos[:, None] * freqs[None, :]
    z = jnp.concatenate([t[:, None], jnp.cos(ang), -jnp.sin(ang)], axis=-1)
    hdn = jnp.sin(z @ fw1 + fb1)
    hdn = jnp.sin(hdn @ fw2 + fb2)
    filt = hdn @ fw3 + fb3
    rates = jnp.abs(jnp.linspace(math.log(HY_DECAY_TARGET) / HY_LONG_PCT,
                                 math.log(HY_DECAY_TARGET) / HY_SHORT_PCT, MIX_W, dtype=F32))
    rates = jnp.concatenate([rates, rates])
    return filt * jnp.exp(-t[:, None] * rates[None, :])


def _bidir_long_conv(u, filt, bias):
    b, L, ch = u.shape
    h_f, h_b = filt[:, :ch], filt[:, ch:]
    k = jnp.concatenate([h_f, jnp.zeros((1, ch), F32), h_b[:0:-1]], axis=0)
    U = jnp.fft.rfft(u, n=2 * L, axis=1)
    K = jnp.fft.rfft(k, n=2 * L, axis=0)
    y = jnp.fft.irfft(U * K[None], n=2 * L, axis=1)[:, :L]
    return y + u * bias


def _hyena_mixer(p, conv_w, conv_b, fw1, fb1, fw2, fb2, fw3, fb3, bias):
    u = _conv3(p.astype(F32), conv_w) + conv_b
    x0, x1, v = _split(u, (MIX_W, MIX_W, MIX_W))
    filt = _hyena_filter(p.shape[1], fw1, fb1, fw2, fb2, fw3, fb3)
    return x0 * _bidir_long_conv(v * x1, filt, bias)


def _centred_shift(p, mu):
    pp = jnp.pad(p, ((0, 0), (1, 1), (0, 0)))
    return p + mu[0] * (pp[:, :-2] - p) + mu[1] * (pp[:, 2:] - p)


def _rwkv7_inputs(p, mu, w0, w_up, a0, a_up, g_up, k_k, k_a):
    b, L, _ = p.shape
    p = _centred_shift(p.astype(F32), mu)
    r, k, v, wd_f, wd_b, ad_f, ad_b, gd = _split(p, RW_SPLITS)
    heads = lambda t: t.reshape(b, L, RW_HEADS, HEAD_DIM)
    g = jax.nn.sigmoid(gd) @ g_up
    kk = heads(k * k_k)
    kk = kk / jnp.maximum(jnp.sqrt(jnp.sum(kk * kk, axis=-1, keepdims=True)), 1e-12)
    dirs = []
    for d, (wd, ad) in enumerate(((wd_f, ad_f), (wd_b, ad_b))):
        logw = -RW_DECAY_SCALE * jax.nn.sigmoid(w0[d] + jnp.tanh(wd) @ w_up[d])
        a = jax.nn.sigmoid(a0[d] + ad @ a_up[d])
        k_d = k * (1.0 + (a - 1.0) * k_a)
        dirs.append((heads(jnp.exp(logw)), heads(k_d), heads(a)))
    return heads(r), heads(v), kk, g, dirs


def _rwkv7_scan(s0, seq, d, reverse):
    r, v, kk, _, dirs = seq
    w, k, a = dirs[d]

    def step(s, inp):
        r_t, w_t, k_t, v_t, kk_t, a_t = inp
        s = (s * w_t[:, :, None, :]
             - jnp.einsum('bhvk,bhk->bhv', s, kk_t)[..., None] * (kk_t * a_t)[:, :, None, :]
             + v_t[..., None] * k_t[:, :, None, :])
        return s, jnp.einsum('bhvk,bhk->bhv', s, r_t)

    xs = tuple(jnp.swapaxes(t, 0, 1) for t in (r, w, k, v, kk, a))
    s_fin, o = lax.scan(step, s0, xs, reverse=reverse)
    return s_fin, jnp.swapaxes(o, 0, 1)


def _rwkv7_readout(o, seq, r_k, ln_g, ln_b):
    r, v, _, g, dirs = seq
    b, L = o.shape[:2]
    mean = jnp.mean(o, axis=-1, keepdims=True)
    var = jnp.mean(jnp.square(o - mean), axis=-1, keepdims=True)
    on = ((o - mean) * lax.rsqrt(var + RW_GN_EPS)).reshape(b, L, MIX_W) * ln_g + ln_b
    bonus = sum(jnp.sum(r * k_d * r_k, axis=-1, keepdims=True) * v for (_, k_d, _) in dirs)
    return (on + bonus.reshape(b, L, MIX_W)) * g


def _rwkv7_mixer(pc, pl, want_ctx, mu, w0, w_up, a0, a_up, g_up, k_k, k_a, r_k, ln_g, ln_b):
    seq_c = _rwkv7_inputs(pc, mu, w0, w_up, a0, a_up, g_up, k_k, k_a)
    seq_l = _rwkv7_inputs(pl, mu, w0, w_up, a0, a_up, g_up, k_k, k_a)
    s0 = jnp.zeros((pl.shape[0], RW_HEADS, HEAD_DIM, HEAD_DIM), F32)
    o_c = 0.0
    o_l = 0.0
    for d in range(2):
        s_c, oc = _rwkv7_scan(s0, seq_c, d, d == 1)
        _, ol = _rwkv7_scan(s_c, seq_l, d, d == 1)
        o_c = o_c + oc
        o_l = o_l + ol
    y_l = _rwkv7_readout(o_l, seq_l, r_k, ln_g, ln_b)
    y_c = _rwkv7_readout(o_c, seq_c, r_k, ln_g, ln_b) if want_ctx else None
    return y_c, y_l


def _s5_discretise(a_re, a_im, log_dt):
    dt = jnp.exp(log_dt)[:, None]
    mag = jnp.exp(a_re * dt)
    ang = a_im * dt
    ab_re, ab_im = mag * jnp.cos(ang), mag * jnp.sin(ang)
    nr, ni = ab_re - 1.0, ab_im
    den = a_re * a_re + a_im * a_im
    return ab_re, ab_im, (nr * a_re + ni * a_im) / den, (ni * a_re - nr * a_im) / den


def _cplx_affine(e1, e2):
    a1r, a1i, b1r, b1i = e1
    a2r, a2i, b2r, b2i = e2
    return (a1r * a2r - a1i * a2i, a1r * a2i + a1i * a2r,
            a2r * b1r - a2i * b1i + b2r, a2r * b1i + a2i * b1r + b2i)


def _s5_scan(ab_re, ab_im, x_re, x_im, h_re, h_im, reverse):
    if reverse:
        x_re, x_im = _flip(x_re), _flip(x_im)
    x_re = x_re.at[:, 0].add(ab_re * h_re - ab_im * h_im)
    x_im = x_im.at[:, 0].add(ab_re * h_im + ab_im * h_re)
    L = x_re.shape[1]
    a_re = jnp.broadcast_to(ab_re, (1, L) + ab_re.shape)
    a_im = jnp.broadcast_to(ab_im, (1, L) + ab_im.shape)
    _, _, s_re, s_im = lax.associative_scan(_cplx_affine, (a_re, a_im, x_re, x_im), axis=1)
    if reverse:
        s_re, s_im = _flip(s_re), _flip(s_im)
    return s_re, s_im


def _s5_readout(s_re, s_im, c_re, c_im):
    return jnp.einsum('gpn,blgn->blgp', c_re, s_re) - jnp.einsum('gpn,blgn->blgp', c_im, s_im)


def _s5_output(y, u, d_skip, glu_w, glu_b):
    y = jax.nn.gelu(y.reshape(u.shape) + d_skip * u)
    return y * jax.nn.sigmoid(y @ glu_w + glu_b)


def _s5_mixer(pc, pl, want_ctx, a_re, a_im, log_dt, b_re, b_im, c_re, c_im, d_skip, glu_w, glu_b):
    def drive(p):
        u = p.astype(F32)
        ug = u.reshape(u.shape[0], u.shape[1], S5_GROUPS, S5_GROUP)
        return u, jnp.einsum('blgp,gnp->blgn', ug, b_re), jnp.einsum('blgp,gnp->blgn', ug, b_im)

    uc, bc_re, bc_im = drive(pc)
    ul, bl_re, bl_im = drive(pl)
    zero = jnp.zeros((ul.shape[0], S5_GROUPS, S5_STATE), F32)
    y_c = 0.0
    y_l = 0.0
    for d in range(2):
        rev = d == 1
        ab_re, ab_im, f_re, f_im = _s5_discretise(a_re[d], a_im[d], log_dt[d])
        sc_re, sc_im = _s5_scan(ab_re, ab_im, f_re * bc_re - f_im * bc_im, f_re * bc_im + f_im * bc_re,
                                zero, zero, rev)
        edge = 0 if rev else -1
        sl_re, sl_im = _s5_scan(ab_re, ab_im, f_re * bl_re - f_im * bl_im, f_re * bl_im + f_im * bl_re,
                                sc_re[:, edge], sc_im[:, edge], rev)
        y_l = y_l + _s5_readout(sl_re, sl_im, c_re, c_im)
        if want_ctx:
            y_c = y_c + _s5_readout(sc_re, sc_im, c_re, c_im)
    out_l = _s5_output(y_l, ul, d_skip, glu_w, glu_b)
    out_c = _s5_output(y_c, uc, d_skip, glu_w, glu_b) if want_ctx else None
    return out_c, out_l


def _gla_prep(p, gate_up, gate_b):
    b, L, _ = p.shape
    q, k, v, g, gd_f, gd_b = _split(p.astype(F32), GLA_SPLITS)
    hk = lambda t: t.reshape(b, L, GLA_HEADS, GLA_HK)
    logs = tuple(hk(jax.nn.log_sigmoid(gd @ gate_up[d] + gate_b[d]) / GLA_GATE_NORM)
                 for d, gd in enumerate((gd_f, gd_b)))
    return hk(q) * (GLA_HK ** -0.5), hk(k), v.reshape(b, L, GLA_HEADS, GLA_HV), g, logs


def _gla_chunked(q, k, v, logg, s0):
    b, L, h, dk = q.shape
    dv = v.shape[-1]
    C = min(GLA_CHUNK, L)
    n = L // C
    q, k, v, logg = (t.reshape(b, n, C, h, t.shape[-1]) for t in (q, k, v, logg))
    cum = jnp.cumsum(logg, axis=2)
    ref = cum[:, :, C // 2 - 1:C // 2]
    last = cum[:, :, C - 1:]
    scores = jnp.einsum('bnihd,bnjhd->bnhij', q * jnp.exp(cum - ref), k * jnp.exp(ref - cum))
    scores = jnp.where(jnp.tril(jnp.ones((C, C), bool)), scores, 0.0)
    o = jnp.einsum('bnhij,bnjhe->bnihe', scores, v)
    u = jnp.einsum('bnjhd,bnjhe->bnhde', k * jnp.exp(last - cum), v)
    decay = jnp.exp(last[:, :, 0])

    def step(s, inp):
        u_n, g_n = inp
        return s * g_n[..., None] + u_n, s

    s_fin, s_start = lax.scan(step, s0, (jnp.swapaxes(u, 0, 1), jnp.swapaxes(decay, 0, 1)))
    o = o + jnp.einsum('bnihd,nbhde->bnihe', q * jnp.exp(cum), s_start)
    return o.reshape(b, L, h, dv), s_fin


def _gla_mixer(pc, pl, want_ctx, gate_up, gate_b, norm_g):
    qc, kc, vc, gc, lc = _gla_prep(pc, gate_up, gate_b)
    ql, kl, vl, gl, ll = _gla_prep(pl, gate_up, gate_b)
    s0 = jnp.zeros((pl.shape[0], GLA_HEADS, GLA_HK, GLA_HV), F32)
    o_c = 0.0
    o_l = 0.0
    for d in range(2):
        fl = _flip if d == 1 else _ident
        oc, s_c = _gla_chunked(fl(qc), fl(kc), fl(vc), fl(lc[d]), s0)
        ol, _ = _gla_chunked(fl(ql), fl(kl), fl(vl), fl(ll[d]), s_c)
        o_c = o_c + fl(oc)
        o_l = o_l + fl(ol)

    def readout(o, g):
        b, L = o.shape[:2]
        on = o * lax.rsqrt(jnp.mean(o * o, axis=-1, keepdims=True) + RMS_EPS) * norm_g
        return on.reshape(b, L, GLA_DV) * jax.nn.silu(g)

    return (readout(o_c, gc) if want_ctx else None), readout(o_l, gl)


def _merge(branches, gate_pre, w_branch, w_out):
    g = jax.nn.sigmoid(gate_pre.astype(F32))
    acc = 0.0
    for m, y in enumerate(branches):
        acc = acc + g[..., m * D_MODEL:(m + 1) * D_MODEL] * (y @ w_branch[m])
    return acc @ w_out


def _mixer_block(hc, hl, want_ctx, lp):
    pc = hc @ lp['w_in']
    pl = hl @ lp['w_in']
    hy_c, rw_c, s5_c, gla_c, gate_c = _split(pc, IN_SPLITS)
    hy_l, rw_l, s5_l, gla_l, gate_l = _split(pl, IN_SPLITS)
    y_rw_c, y_rw_l = _rwkv7_mixer(rw_c, rw_l, want_ctx, *lp['rw'])
    y_s5_c, y_s5_l = _s5_mixer(s5_c, s5_l, want_ctx, *lp['s5'])
    y_gla_c, y_gla_l = _gla_mixer(gla_c, _to_col_major(gla_l), want_ctx, *lp['gla'])
    branches_l = (_hyena_mixer(hy_l, *lp['hy']), y_rw_l, y_s5_l, _to_raster(y_gla_l))
    out_l = _merge(branches_l, gate_l, lp['w_branch'], lp['w_out'])
    if not want_ctx:
        return None, out_l
    branches_c = (_hyena_mixer(hy_c, *lp['hy']), y_rw_c, y_s5_c, y_gla_c)
    return _merge(branches_c, gate_c, lp['w_branch'], lp['w_out']), out_l


def _moe_ffn(h, router_w, router_b, w1, w3, w2):
    shape = h.shape
    tok = h.reshape(-1, shape[-1])
    n_tok = tok.shape[0]
    logits = tok.astype(F32) @ router_w + router_b
    top_v, top_i = lax.top_k(logits, TOP_K)
    gates = jax.nn.softmax(top_v, axis=-1)
    n_assign = n_tok * TOP_K
    e_flat = top_i.reshape(-1)
    order = jnp.argsort(e_flat)
    e_sorted = e_flat[order]
    counts = jnp.bincount(e_flat, length=N_EXPERTS)
    padded = (counts + MOE_BLOCK - 1) // MOE_BLOCK * MOE_BLOCK
    start = jnp.cumsum(counts) - counts
    pend = jnp.cumsum(padded)
    pstart = pend - padded
    dest = pstart[e_sorted] + jnp.arange(n_assign) - start[e_sorted]
    n_blocks = -(-n_assign // MOE_BLOCK) + N_EXPERTS
    row_tok = jnp.full((n_blocks * MOE_BLOCK,), n_tok, jnp.int32).at[dest].set((order // TOP_K).astype(jnp.int32))
    block_exp = jnp.minimum(jnp.searchsorted(pend, jnp.arange(n_blocks) * MOE_BLOCK, side='right'), N_EXPERTS - 1)
    tok_pad = jnp.concatenate([tok, jnp.zeros((1, shape[-1]), tok.dtype)], axis=0)
    xb = tok_pad[row_tok].reshape(n_blocks, MOE_BLOCK, shape[-1])
    yb = lax.map(lambda a: _swiglu(a[0], w1[a[1]], w3[a[1]], w2[a[1]]), (xb, block_exp))
    yb = yb.reshape(n_blocks * MOE_BLOCK, shape[-1])
    dest_orig = jnp.zeros((n_assign,), dest.dtype).at[order].set(dest)
    y = yb[dest_orig].reshape(n_tok, TOP_K, shape[-1])
    return jnp.einsum('tkd,tk->td', y, gates.astype(y.dtype)).reshape(shape)


def _trunk_layer(h, hc, mod_l, mod_c, g_mix, g_ffn, lp, ffn, last):
    sh1, sc1, gt1, sh2, sc2, gt2 = jnp.split(mod_l[:, None, :], 6, axis=-1)
    csh1, csc1, cgt1, csh2, csc2, cgt2 = jnp.split(mod_c, 6, axis=-1)
    yc, yl = _mixer_block(_rmsnorm(hc, g_mix) * (1.0 + csc1) + csh1,
                          _rmsnorm(h, g_mix) * (1.0 + sc1) + sh1, not last, lp)
    h = h + gt1 * yl
    h = h + gt2 * ffn(_rmsnorm(h, g_ffn) * (1.0 + sc2) + sh2)
    if not last:
        hc = hc + cgt1 * yc
        hc = hc + cgt2 * ffn(_rmsnorm(hc, g_ffn) * (1.0 + csc2) + csh2)
    return h, hc


def setup_inputs(seed: int = 0) -> dict:
    key = jax.random.key(seed)
    ks = jax.random.split(key, 64)
    ctr = [0]

    def nxt():
        k = ks[ctr[0]]
        ctr[0] += 1
        return k

    def nrm(shape, scale):
        return scale * jax.random.normal(nxt(), shape, F32)

    def unif(shape, lo, hi):
        return jax.random.uniform(nxt(), shape, F32, lo, hi)

    D = D_MODEL
    G, N = S5_GROUPS, S5_STATE
    a_im0 = jnp.broadcast_to(math.pi * jnp.arange(N, dtype=F32), (DEPTH, 2, G, N))
    return {
        'x': nrm((BATCH, SEQ, D), 1.0),
        'c': nrm((BATCH, D), 1.0),
        'ctx': nrm((BATCH, CTX_LEN, D), 1.0),
        'c_ctx': nrm((D,), 1.0),
        'ada_w': nrm((DEPTH, D, 6 * D), 0.5 * D ** -0.5),
        'ada_b': nrm((DEPTH, 6 * D), 0.02),
        'norm_mix_g': 1.0 + nrm((DEPTH, D), 0.02),
        'norm_ffn_g': 1.0 + nrm((DEPTH, D), 0.02),
        'w_in': nrm((DEPTH, D, IN_TOTAL), D ** -0.5),
        'hy_conv_w': nrm((DEPTH, 3, 3 * MIX_W), 3 ** -0.5),
        'hy_conv_b': nrm((DEPTH, 3 * MIX_W), 0.02),
        'hy_f_w1': nrm((DEPTH, HY_EMB, HY_FFN), HY_EMB ** -0.5),
        'hy_f_b1': nrm((DEPTH, HY_FFN), 0.1),
        'hy_f_w2': nrm((DEPTH, HY_FFN, HY_FFN), HY_FFN ** -0.5),
        'hy_f_b2': nrm((DEPTH, HY_FFN), 0.1),
        'hy_f_w3': nrm((DEPTH, HY_FFN, 2 * MIX_W), 0.02),
        'hy_f_b3': nrm((DEPTH, 2 * MIX_W), 0.01),
        'hy_bias': nrm((DEPTH, MIX_W), 0.5),
        'rw_mu': unif((DEPTH, 2, RW_IN), 0.0, 0.5),
        'rw_w0': unif((DEPTH, 2, MIX_W), -1.0, 2.0),
        'rw_w_up': nrm((DEPTH, 2, RW_DECAY_RANK, MIX_W), 0.1),
        'rw_a0': nrm((DEPTH, 2, MIX_W), 0.3),
        'rw_a_up': nrm((DEPTH, 2, RW_A_RANK, MIX_W), 0.1),
        'rw_g_up': nrm((DEPTH, RW_G_RANK, MIX_W), RW_G_RANK ** -0.5),
        'rw_k_k': 0.85 + nrm((DEPTH, MIX_W), 0.02),
        'rw_k_a': 1.0 + nrm((DEPTH, MIX_W), 0.02),
        'rw_r_k': nrm((DEPTH, RW_HEADS, HEAD_DIM), 0.1),
        'rw_ln_g': 1.0 + nrm((DEPTH, MIX_W), 0.02),
        'rw_ln_b': nrm((DEPTH, MIX_W), 0.02),
        's5_a_re': -0.5 + nrm((DEPTH, 2, G, N), 0.01),
        's5_a_im': a_im0 + nrm((DEPTH, 2, G, N), 0.01),
        's5_log_dt': unif((DEPTH, 2, G), math.log(S5_DT_MIN), math.log(S5_DT_MAX)),
        's5_b_re': nrm((DEPTH, G, N, S5_GROUP), (2 * S5_GROUP) ** -0.5),
        's5_b_im': nrm((DEPTH, G, N, S5_GROUP), (2 * S5_GROUP) ** -0.5),
        's5_c_re': nrm((DEPTH, G, S5_GROUP, N), (2 * N) ** -0.5),
        's5_c_im': nrm((DEPTH, G, S5_GROUP, N), (2 * N) ** -0.5),
        's5_d': nrm((DEPTH, MIX_W), 1.0),
        's5_glu_w': nrm((DEPTH, MIX_W, MIX_W), MIX_W ** -0.5),
        's5_glu_b': nrm((DEPTH, MIX_W), 0.02),
        'gla_gate_up': nrm((DEPTH, 2, GLA_GATE_RANK, GLA_DK), GLA_GATE_RANK ** -0.5),
        'gla_gate_b': nrm((DEPTH, 2, GLA_DK), 0.1),
        'gla_norm_g': 1.0 + nrm((DEPTH, GLA_HV), 0.02),
        'w_branch': nrm((DEPTH, N_BRANCH, MIX_W, D), MIX_W ** -0.5),
        'w_out': nrm((DEPTH, D, D), D ** -0.5),
        'ffn_w1': nrm((N_DENSE, D, D_FF), D ** -0.5),
        'ffn_w3': nrm((N_DENSE, D, D_FF), D ** -0.5),
        'ffn_w2': nrm((N_DENSE, D_FF, D), D_FF ** -0.5),
        'moe_router_w': nrm((N_MOE, D, N_EXPERTS), D ** -0.5),
        'moe_router_b': nrm((N_MOE, N_EXPERTS), 0.01),
        'moe_w1': nrm((N_MOE, N_EXPERTS, D, D_FF_EXPERT), D ** -0.5),
        'moe_w3': nrm((N_MOE, N_EXPERTS, D, D_FF_EXPERT), D ** -0.5),
        'moe_w2': nrm((N_MOE, N_EXPERTS, D_FF_EXPERT, D), D_FF_EXPERT ** -0.5),
        'final_norm_g': 1.0 + nrm((D,), 0.02),
    }


def reference(x, c, ctx, c_ctx, ada_w, ada_b, norm_mix_g, norm_ffn_g, w_in,
              hy_conv_w, hy_conv_b, hy_f_w1, hy_f_b1, hy_f_w2, hy_f_b2, hy_f_w3, hy_f_b3, hy_bias,
              rw_mu, rw_w0, rw_w_up, rw_a0, rw_a_up, rw_g_up, rw_k_k, rw_k_a, rw_r_k, rw_ln_g, rw_ln_b,
              s5_a_re, s5_a_im, s5_log_dt, s5_b_re, s5_b_im, s5_c_re, s5_c_im, s5_d, s5_glu_w, s5_glu_b,
              gla_gate_up, gla_gate_b, gla_norm_g, w_branch, w_out,
              ffn_w1, ffn_w3, ffn_w2, moe_router_w, moe_router_b, moe_w1, moe_w3, moe_w2,
              final_norm_g):
    s_lat = jax.nn.silu(c.astype(F32))
    s_ctx = jax.nn.silu(c_ctx.astype(F32))
    h = x.astype(F32)
    hc = ctx.astype(F32)
    for l in range(DEPTH):
        lp = {
            'w_in': w_in[l],
            'hy': (hy_conv_w[l], hy_conv_b[l], hy_f_w1[l], hy_f_b1[l], hy_f_w2[l], hy_f_b2[l],
                   hy_f_w3[l], hy_f_b3[l], hy_bias[l]),
            'rw': (rw_mu[l], rw_w0[l], rw_w_up[l], rw_a0[l], rw_a_up[l], rw_g_up[l], rw_k_k[l],
                   rw_k_a[l], rw_r_k[l], rw_ln_g[l], rw_ln_b[l]),
            's5': (s5_a_re[l], s5_a_im[l], s5_log_dt[l], s5_b_re[l], s5_b_im[l], s5_c_re[l],
                   s5_c_im[l], s5_d[l], s5_glu_w[l], s5_glu_b[l]),
            'gla': (gla_gate_up[l], gla_gate_b[l], gla_norm_g[l]),
            'w_branch': w_branch[l],
            'w_out': w_out[l],
        }
        i = l // 2
        if l % 2 == 0:
            ffn = functools.partial(_swiglu, w1=ffn_w1[i], w3=ffn_w3[i], w2=ffn_w2[i])
        else:
            ffn = functools.partial(_moe_ffn, router_w=moe_router_w[i], router_b=moe_router_b[i],
                                    w1=moe_w1[i], w3=moe_w3[i], w2=moe_w2[i])
        mod_l = s_lat @ ada_w[l] + ada_b[l]
        mod_c = s_ctx @ ada_w[l] + ada_b[l]
        h, hc = _trunk_layer(h, hc, mod_l, mod_c, norm_mix_g[l], norm_ffn_g[l], lp, ffn, l == DEPTH - 1)
    return _rmsnorm(h, final_norm_g).astype(x.dtype)
```

```python
import functools
import math

import numpy as np
import jax
import jax.numpy as jnp
from jax import lax
from jax.experimental import pallas as pl
from jax.experimental.pallas import tpu as pltpu

F32 = jnp.float32
BF16 = jnp.bfloat16

D_MODEL = 1024
GRID_W = 64
MIX_W = 512
RMS_EPS = 1e-6
HY_BANDS = 16
HY_DECAY_TARGET = 1e-2
HY_SHORT_PCT = 0.3
HY_LONG_PCT = 1.5
RW_HEADS = 8
RW_HEAD_DIM = 64
RW_DECAY_SCALE = 0.606531
RW_GN_EPS = 64e-5
RW_IN = 1920
S5_GROUP = 16
S5_GROUPS = 32
S5_STATE = 64
S5_CHUNK = 16
GLA_HEADS = 4
GLA_HK = 64
GLA_HV = 128
GLA_DK = 256
GLA_DV = 512
GLA_GATE_RANK = 16
GLA_GATE_NORM = 16.0
GLA_IN_PAD = 1664
N_EXPERTS = 8
TOP_K = 2
MOE_BLOCK_ROWS = 256
TOKEN_TILE = 256
SEQ_CHUNK = 64
VMEM_LIMIT = 56 * 1024 * 1024

NT_DIMS = (((1,), (1,)), ((), ()))
TN_DIMS = (((0,), (0,)), ((), ()))


def _cp(*sem):
    return pltpu.CompilerParams(dimension_semantics=sem, vmem_limit_bytes=VMEM_LIMIT)


def _mm(a, b, dims=None):
    a = a.astype(BF16)
    b = b.astype(BF16)
    if dims is None:
        return jnp.dot(a, b, preferred_element_type=F32)
    return lax.dot_general(a, b, dims, preferred_element_type=F32)


def _split_bf16(x):
    hi = x.astype(BF16)
    lo = (x - hi.astype(F32)).astype(BF16)
    return hi, lo


def _mm_hp(a, b):
    ah, al = _split_bf16(a)
    bh, bl = _split_bf16(b)
    d = functools.partial(jnp.dot, preferred_element_type=F32)
    return d(ah, bh) + (d(ah, bl) + d(al, bh))


def _mm_exact01(m01, x):
    m = m01.astype(BF16)
    hi = x.astype(BF16)
    r1 = x - hi.astype(F32)
    mid = r1.astype(BF16)
    lo = (r1 - mid.astype(F32)).astype(BF16)
    d = functools.partial(jnp.dot, preferred_element_type=F32)
    return d(m, hi) + (d(m, mid) + d(m, lo))


def _x_exact01(x, m01):
    m = m01.astype(BF16)
    hi = x.astype(BF16)
    r1 = x - hi.astype(F32)
    mid = r1.astype(BF16)
    lo = (r1 - mid.astype(F32)).astype(BF16)
    d = functools.partial(jnp.dot, preferred_element_type=F32)
    return d(hi, m) + (d(mid, m) + d(lo, m))


def _sigmoid(x):
    return 1.0 / (1.0 + jnp.exp(-x))


def _silu(x):
    return x * _sigmoid(x)


def _tri(c, rev, strict):
    i = lax.broadcasted_iota(jnp.int32, (c, c), 0)
    j = lax.broadcasted_iota(jnp.int32, (c, c), 1)
    if rev:
        return (j > i) if strict else (j >= i)
    return (j < i) if strict else (j <= i)


def _head_blockdiag(n, head):
    i = lax.broadcasted_iota(jnp.int32, (n, n), 0) // head
    j = lax.broadcasted_iota(jnp.int32, (n, n), 1) // head
    return (i == j).astype(F32)


def _shift_rows(x, prev_row, next_row):
    n = x.shape[0]
    row = lax.broadcasted_iota(jnp.int32, x.shape, 0)
    xm = jnp.where(row == 0, prev_row, pltpu.roll(x, 1, 0))
    xp = jnp.where(row == n - 1, next_row, pltpu.roll(x, n - 1, 0))
    return xm, xp


def _ada_kernel(c_ref, w_ref, b_ref, o_ref):
    o_ref[0] = _mm_hp(_silu(c_ref[...]), w_ref[0]) + b_ref[0]


def _ada_mod(c_all, ada_w, ada_b):
    depth, d, n = ada_w.shape
    rows = c_all.shape[0]
    return pl.pallas_call(
        _ada_kernel,
        grid=(depth, n // d),
        in_specs=[pl.BlockSpec((rows, d), lambda l, j: (0, 0)),
                  pl.BlockSpec((1, d, d), lambda l, j: (l, 0, j)),
                  pl.BlockSpec((1, 1, d), lambda l, j: (l, 0, j))],
        out_specs=pl.BlockSpec((1, rows, d), lambda l, j: (l, 0, j)),
        out_shape=jax.ShapeDtypeStruct((depth, rows, n), F32),
        compiler_params=_cp("arbitrary", "arbitrary"),
        name="ada_mod",
    )(c_all, ada_w, ada_b.reshape(depth, 1, n))


def _norm_mod(x, g, mod, sh_row):
    xn = x * lax.rsqrt(jnp.mean(x * x, axis=-1, keepdims=True) + RMS_EPS) * g
    return xn * (1.0 + mod[sh_row + 1:sh_row + 2]) + mod[sh_row:sh_row + 1]


def _in_proj_kernel(x_ref, g_ref, mod_ref, w_hy, w_rw, w_s5, w_gla, w_gate, o_hy, o_rw, o_s5, o_gla, o_gate):
    xm = _norm_mod(x_ref[0], g_ref[...], mod_ref[0, 0], 0).astype(BF16)
    for w, o in ((w_hy, o_hy), (w_rw, o_rw), (w_s5, o_s5), (w_gla, o_gla), (w_gate, o_gate)):
        o[0] = jnp.dot(xm, w[...], preferred_element_type=F32)


def _mod_index(n_lat_tiles):
    return lambda b, t: (b, t // n_lat_tiles, 0, 0)


def _in_proj(h_all, g, modsel, ws, n_lat_tiles):
    b, t, d = h_all.shape
    nt = t // TOKEN_TILE
    wspec = [pl.BlockSpec(w.shape, lambda b, t: (0, 0), pipeline_mode=pl.Buffered(1)) for w in ws]
    return pl.pallas_call(
        _in_proj_kernel,
        grid=(b, nt),
        in_specs=[pl.BlockSpec((1, TOKEN_TILE, d), lambda b, t: (b, t, 0)),
                  pl.BlockSpec((1, d), lambda b, t: (0, 0)),
                  pl.BlockSpec((1, 1, 6, d), _mod_index(n_lat_tiles))] + wspec,
        out_specs=[pl.BlockSpec((1, TOKEN_TILE, w.shape[1]), lambda b, t: (b, t, 0)) for w in ws],
        out_shape=[jax.ShapeDtypeStruct((b, t, w.shape[1]), F32) for w in ws],
        compiler_params=_cp("arbitrary", "arbitrary"),
        name="in_proj",
    )(h_all, g.reshape(1, d), modsel, *ws)


def _hy_filter_kernel(z_ref, w1, b1, w2, b2, w3, b3, dec_ref, o_ref):
    h = jnp.sin(_mm_hp(z_ref[...], w1[...]) + b1[...])
    h = jnp.sin(_mm_hp(h, w2[...]) + b2[...])
    o_ref[...] = (_mm_hp(h, w3[...]) + b3[...]) * dec_ref[...]


def _hy_features(length):
    pos = jnp.arange(length, dtype=F32)
    t = pos / max(length - 1, 1)
    freqs = jnp.linspace(1e-4, HY_BANDS - 1, HY_BANDS, dtype=F32)
    ang = (2.0 * math.pi / length) * pos[:, None] * freqs[None, :]
    z = jnp.concatenate([t[:, None], jnp.cos(ang), -jnp.sin(ang)], axis=-1)
    rates = jnp.abs(jnp.linspace(math.log(HY_DECAY_TARGET) / HY_LONG_PCT,
                                 math.log(HY_DECAY_TARGET) / HY_SHORT_PCT, MIX_W, dtype=F32))
    rates = jnp.concatenate([rates, rates])
    decay = jnp.exp(-t[:, None] * rates[None, :])
    return jnp.pad(z, ((0, 0), (0, 128 - z.shape[1]))), decay


def _hy_filter(length, fw1, fb1, fw2, fb2, fw3, fb3):
    z, decay = _hy_features(length)
    w1 = jnp.pad(fw1, ((0, 128 - fw1.shape[0]), (0, 128 - fw1.shape[1])))
    b1 = jnp.pad(fb1, (0, 128 - fb1.shape[0])).reshape(1, 128)
    w2 = jnp.pad(fw2, ((0, 128 - fw2.shape[0]), (0, 128 - fw2.shape[1])))
    b2 = jnp.pad(fb2, (0, 128 - fb2.shape[0])).reshape(1, 128)
    w3 = jnp.pad(fw3, ((0, 128 - fw3.shape[0]), (0, 0)))
    n = fw3.shape[1]
    rows = min(length, 512)
    full = lambda s: pl.BlockSpec(s, lambda i: (0, 0))
    return pl.pallas_call(
        _hy_filter_kernel,
        grid=(length // rows,),
        in_specs=[pl.BlockSpec((rows, 128), lambda i: (i, 0)), full((128, 128)), full((1, 128)), full((128, 128)),
                  full((1, 128)), full((128, n)), full((1, n)), pl.BlockSpec((rows, n), lambda i: (i, 0))],
        out_specs=pl.BlockSpec((rows, n), lambda i: (i, 0)),
        out_shape=jax.ShapeDtypeStruct((length, n), F32),
        compiler_params=_cp("arbitrary"),
        name="hy_filter",
    )(z, w1, b1, w2, b2, w3, fb3.reshape(1, n), decay)


def _dft_tables(length):
    n = 2 * length
    k = jnp.arange(length, dtype=jnp.int32)
    m = (k[:, None] * k[None, :]) % n
    ang = m.astype(F32) * (2.0 * math.pi / n)
    wc = jnp.cos(ang)
    ws = jnp.sin(ang)
    nyq = jnp.where(k % 2 == 0, 1.0, -1.0).astype(F32)
    ws = ws.at[0].set(nyq)
    return wc.astype(BF16), ws.astype(BF16), wc.T.astype(BF16), ws.T.astype(BF16)


def _dft_fwd_kernel(wc_ref, ws_ref, xh_ref, xl_ref, oc_ref, os_ref):
    d = functools.partial(jnp.dot, preferred_element_type=F32)
    oc_ref[...] = d(wc_ref[...], xh_ref[...]) + d(wc_ref[...], xl_ref[...])
    os_ref[...] = d(ws_ref[...], xh_ref[...]) + d(ws_ref[...], xl_ref[...])


def _hy_spectrum(filt, wc, ws):
    length, c2 = filt.shape
    c = c2 // 2
    n = 2 * length
    hb = filt[:, c:].at[0].set(0.0)
    x = jnp.concatenate([filt[:, :c], hb], axis=1)
    xh, xl = _split_bf16(x)
    fr = min(length, 512)
    fc, fs = pl.pallas_call(
        _dft_fwd_kernel,
        grid=(length // fr,),
        in_specs=[pl.BlockSpec((fr, length), lambda i: (i, 0)), pl.BlockSpec((fr, length), lambda i: (i, 0)),
                  pl.BlockSpec((length, c2), lambda i: (0, 0)), pl.BlockSpec((length, c2), lambda i: (0, 0))],
        out_specs=[pl.BlockSpec((fr, c2), lambda i: (i, 0)), pl.BlockSpec((fr, c2), lambda i: (i, 0))],
        out_shape=[jax.ShapeDtypeStruct((length, c2), F32)] * 2,
        compiler_params=_cp("arbitrary"),
        name="hy_spectrum",
    )(wc, ws, xh, xl)
    k_re = fc[:, :c] + fc[:, c:]
    k_im = fs[:, c:] - fs[:, :c]
    k_nyq = fs[0, :c] + fs[0, c:]
    ka = (k_re * (2.0 / n)).at[0].set(k_re[0] / n)
    kb = (k_im * (2.0 / n)).at[0].set(0.0)
    ka2 = ka.at[0].set(k_nyq / n)
    return ka, ka2, kb


def _hyena_kernel(p_ref, cw_ref, cb_ref, bias_ref, wc_ref, ws_ref, wct_ref, wst_ref, ka_ref, ka2_ref, kb_ref,
                  o_ref, z_scr, acc_scr):
    f = pl.program_id(1)
    c = MIX_W

    def conv3(lo):
        p = p_ref[0, :, lo:lo + c]
        zero = jnp.zeros((1, c), F32)
        pm, pp = _shift_rows(p, zero, zero)
        return pm * cw_ref[0:1, lo:lo + c] + p * cw_ref[1:2, lo:lo + c] + pp * cw_ref[2:3, lo:lo + c] + cb_ref[0:1, lo:lo + c]

    @pl.when(f == 0)
    def _():
        z_scr[...] = (conv3(2 * c) * conv3(c)).astype(BF16)
        acc_scr[...] = jnp.zeros_like(acc_scr)

    z = z_scr[...]
    zc = jnp.dot(wc_ref[...], z, preferred_element_type=F32)
    zs = jnp.dot(ws_ref[...], z, preferred_element_type=F32)
    a = zc * ka_ref[...] + zs * kb_ref[...]
    bv = zs * ka2_ref[...] - zc * kb_ref[...]
    acc_scr[...] += (jnp.dot(wct_ref[...], a.astype(BF16), preferred_element_type=F32)
                     + jnp.dot(wst_ref[...], bv.astype(BF16), preferred_element_type=F32))

    @pl.when(f == pl.num_programs(1) - 1)
    def _():
        zf = conv3(2 * c) * conv3(c)
        o_ref[0] = conv3(0) * (acc_scr[...] + zf * bias_ref[...])


def _hyena(p_hy, y_prev, row_block, length, conv_w, conv_b, bias, tables, spec):
    b, t, c3 = p_hy.shape
    wc, ws, wct, wst = tables
    ka, ka2, kb = spec
    fr = min(length, 256)
    nf = length // fr
    c = MIX_W
    args = [p_hy, conv_w, conv_b.reshape(1, c3), bias.reshape(1, c), wc, ws, wct, wst, ka, ka2, kb]
    in_specs = [pl.BlockSpec((1, length, c3), lambda b, f: (b, row_block, 0), pipeline_mode=pl.Buffered(1)),
                pl.BlockSpec((3, c3), lambda b, f: (0, 0)),
                pl.BlockSpec((1, c3), lambda b, f: (0, 0)),
                pl.BlockSpec((1, c), lambda b, f: (0, 0)),
                pl.BlockSpec((fr, length), lambda b, f: (f, 0)),
                pl.BlockSpec((fr, length), lambda b, f: (f, 0)),
                pl.BlockSpec((length, fr), lambda b, f: (0, f)),
                pl.BlockSpec((length, fr), lambda b, f: (0, f)),
                pl.BlockSpec((fr, c), lambda b, f: (f, 0)),
                pl.BlockSpec((fr, c), lambda b, f: (f, 0)),
                pl.BlockSpec((fr, c), lambda b, f: (f, 0))]
    kwargs = {}
    if y_prev is not None:
        args.append(y_prev)
        in_specs.append(pl.BlockSpec(memory_space=pl.ANY))
        kwargs["input_output_aliases"] = {len(args) - 1: 0}
        kern = lambda *refs: _hyena_kernel(*refs[:11], *refs[12:])
    else:
        kern = _hyena_kernel
    return pl.pallas_call(
        kern,
        grid=(b, nf),
        in_specs=in_specs,
        out_specs=pl.BlockSpec((1, length, c), lambda b, f: (b, row_block, 0)),
        out_shape=jax.ShapeDtypeStruct((b, t, c), F32),
        scratch_shapes=[pltpu.VMEM((length, c), BF16), pltpu.VMEM((length, c), F32)],
        compiler_params=_cp("arbitrary", "arbitrary"),
        name="hyena_%d" % length,
        **kwargs,
    )(*args)


def _rw_prep_kernel(p_ref, pprev_ref, pnext_ref, mu_ref, w0_ref, wup_ref, a0_ref, aup_ref, gup_ref, kk_ref, ka_ref, rk_ref,
                    r_o, v_o, kap_o, lwf_o, lwb_o, kf_o, kb_o, bf_o, bb_o, g_o, bonus_o, *, n_lat_tiles):
    t = pl.program_id(1)
    nt = pl.num_programs(1)
    p = p_ref[0]
    first = jnp.logical_or(t == 0, t == n_lat_tiles)
    last = jnp.logical_or(t == n_lat_tiles - 1, t == nt - 1)
    prev_row = jnp.where(first, 0.0, pprev_ref[0, 7:8, :])
    next_row = jnp.where(last, 0.0, pnext_ref[0, 0:1, :])
    pm, pp = _shift_rows(p, prev_row, next_row)
    p = p + mu_ref[0:1] * (pm - p) + mu_ref[1:2] * (pp - p)
    c = MIX_W
    r, k, v = p[:, 0:c], p[:, c:2 * c], p[:, 2 * c:3 * c]
    wd, ad, gd = p[:, 3 * c:3 * c + 128], p[:, 3 * c + 128:3 * c + 256], p[:, 3 * c + 256:3 * c + 384]
    bd = _head_blockdiag(c, RW_HEAD_DIM)
    g = _mm_hp(_sigmoid(gd), gup_ref[...])
    kk = k * kk_ref[...]
    nrm = jnp.sqrt(_x_exact01(kk * kk, bd))
    kap = kk / jnp.maximum(nrm, 1e-12)
    logw = -RW_DECAY_SCALE * _sigmoid(w0_ref[...] + _mm_hp(jnp.tanh(wd), wup_ref[...]))
    a = _sigmoid(a0_ref[...] + _mm_hp(ad, aup_ref[...]))
    bonus_s = 0.0
    outs_k, outs_b = (kf_o, kb_o), (bf_o, bb_o)
    for d in range(2):
        a_d = a[:, d * c:(d + 1) * c]
        k_d = k * (1.0 + (a_d - 1.0) * ka_ref[...])
        outs_k[d][0] = k_d
        outs_b[d][0] = a_d * kap
        bonus_s = bonus_s + _x_exact01(r * k_d * rk_ref[...], bd)
    r_o[0] = r
    v_o[0] = v
    kap_o[0] = kap
    lwf_o[0] = logw[:, 0:c]
    lwb_o[0] = logw[:, c:2 * c]
    g_o[0] = g
    bonus_o[0] = bonus_s * v


def _blockdiag2(a, b):
    z1 = jnp.zeros((a.shape[0], b.shape[1]), a.dtype)
    z2 = jnp.zeros((b.shape[0], a.shape[1]), a.dtype)
    return jnp.concatenate([jnp.concatenate([a, z1], 1), jnp.concatenate([z2, b], 1)], 0)


def _rw_prep(p_rw, n_lat_tiles, mu, w0, w_up, a0, a_up, g_up, k_k, k_a, r_k):
    b, t, n = p_rw.shape
    nt = t // TOKEN_TILE
    c = MIX_W
    sub = TOKEN_TILE // 8
    n8 = t // 8
    consts = [mu, w0.reshape(1, 2 * c), _blockdiag2(w_up[0], w_up[1]), a0.reshape(1, 2 * c), _blockdiag2(a_up[0], a_up[1]),
              g_up, k_k.reshape(1, c), k_a.reshape(1, c), r_k.reshape(1, c)]
    cspecs = [pl.BlockSpec(x.shape, lambda b, t: (0, 0)) for x in consts]
    tile = lambda w: pl.BlockSpec((1, TOKEN_TILE, w), lambda b, t: (b, t, 0))
    return pl.pallas_call(
        functools.partial(_rw_prep_kernel, n_lat_tiles=n_lat_tiles),
        grid=(b, nt),
        in_specs=[tile(n),
                  pl.BlockSpec((1, 8, n), lambda b, t: (b, jnp.maximum(t * sub - 1, 0), 0)),
                  pl.BlockSpec((1, 8, n), lambda b, t: (b, jnp.minimum((t + 1) * sub, n8 - 1), 0))] + cspecs,
        out_specs=[tile(c)] * 11,
        out_shape=[jax.ShapeDtypeStruct((b, t, c), F32)] * 11,
        compiler_params=_cp("arbitrary", "arbitrary"),
        name="rw_prep",
    )(p_rw, p_rw, p_rw, *consts)


def _rw_head_chunk(r, v, kap, k, bb, logw, cum, s, rev):
    c = r.shape[0]
    incl = _tri(c, rev, False)
    strict = _tri(c, rev, True)
    tot = cum[0:1] if rev else cum[c - 1:c]
    ginv = jnp.exp(-cum)
    lhs = jnp.concatenate([r * jnp.exp(cum), kap * jnp.exp(cum - logw)], 0)
    rhs = jnp.concatenate([k * ginv, bb * ginv], 0)
    a1 = _mm(lhs, rhs, NT_DIMS)
    a2 = _mm(lhs, s, NT_DIMS)
    lb = jnp.where(strict, a1[c:, c:], 0.0)
    u = a2[c:] + _mm(jnp.where(strict, a1[c:, :c], 0.0), v)
    u = u - _mm(lb, u)
    pw = _mm(lb, lb)
    n = int(math.log2(c)) - 1
    for i in range(n):
        u = u + _mm(pw, u)
        if i < n - 1:
            pw = _mm(pw, pw)
    o = a2[:c] + _mm(jnp.where(incl, a1[:c, :c], 0.0), v) - _mm(jnp.where(incl, a1[:c, c:], 0.0), u)
    gt = jnp.exp(tot - cum)
    s_new = s * jnp.exp(tot) + _mm(jnp.concatenate([v, -u], 0), jnp.concatenate([k * gt, bb * gt], 0), TN_DIMS)
    return o, s_new


def _rw_scan_kernel(rf, vf, kapf, lwf, kf, bf, rb, vb, kapb, lwb, kb, bb, of_ref, ob_ref, s_scr):
    @pl.when(pl.program_id(1) == 0)
    def _():
        s_scr[...] = jnp.zeros_like(s_scr)

    hd = RW_HEAD_DIM
    c = rf.shape[1]
    for d, (r, v, kap, lw, k, b, o_ref) in enumerate(((rf, vf, kapf, lwf, kf, bf, of_ref), (rb, vb, kapb, lwb, kb, bb, ob_ref))):
        rev = d == 1
        logw = lw[0]
        cum = _mm_exact01(_tri(c, rev, False).astype(F32), logw)
        outs = []
        for h in range(RW_HEADS):
            sl = slice(h * hd, (h + 1) * hd)
            o, s_new = _rw_head_chunk(r[0, :, sl], v[0, :, sl], kap[0, :, sl], k[0, :, sl], b[0, :, sl],
                                      logw[:, sl], cum[:, sl], s_scr[d * RW_HEADS + h], rev)
            s_scr[d * RW_HEADS + h] = s_new
            outs.append(o)
        o_ref[0] = jnp.concatenate(outs, axis=1)


def _chunk_orders(n_lat, n_ctx):
    n = n_lat + n_ctx
    fwd = lambda b, s: (b, (s + n_lat) % n, 0)
    rev = lambda b, s: (b, n - 1 - s, 0)
    return fwd, rev


def _rw_scan(r, v, kap, lwf, lwb, kf, kb, bf, bb, n_lat_rows):
    b, t, c = r.shape
    ch = SEQ_CHUNK
    n = t // ch
    fwd, rev = _chunk_orders(n_lat_rows // ch, (t - n_lat_rows) // ch)
    blk = lambda im: pl.BlockSpec((1, ch, c), im)
    return pl.pallas_call(
        _rw_scan_kernel,
        grid=(b, n),
        in_specs=[blk(fwd)] * 6 + [blk(rev)] * 6,
        out_specs=[blk(fwd), blk(rev)],
        out_shape=[jax.ShapeDtypeStruct((b, t, c), F32)] * 2,
        scratch_shapes=[pltpu.VMEM((2 * RW_HEADS, RW_HEAD_DIM, RW_HEAD_DIM), F32)],
        compiler_params=_cp("arbitrary", "arbitrary"),
        name="rw_scan",
    )(r, v, kap, lwf, kf, bf, r, v, kap, lwb, kb, bb)


def _rw_readout(of, ob, bonus, g, ln_g, ln_b):
    o = of + ob
    bd = _head_blockdiag(MIX_W, RW_HEAD_DIM)
    mean = _x_exact01(o, bd) * (1.0 / RW_HEAD_DIM)
    oc = o - mean
    var = _x_exact01(oc * oc, bd) * (1.0 / RW_HEAD_DIM)
    on = oc * lax.rsqrt(var + RW_GN_EPS) * ln_g + ln_b
    return (on + bonus) * g


def _s5_matrices(a_re, a_im, log_dt, b_re, b_im, c_re, c_im):
    tc, g, n, p = S5_CHUNK, S5_GROUPS, S5_STATE, S5_GROUP
    hp = lax.Precision.HIGHEST
    dt = jnp.exp(log_dt)[:, :, None]
    lam_re, lam_im = a_re * dt, a_im * dt
    mag1 = jnp.exp(lam_re)
    ab_re, ab_im = mag1 * jnp.cos(lam_im), mag1 * jnp.sin(lam_im)
    nr, ni = ab_re - 1.0, ab_im
    den = a_re * a_re + a_im * a_im
    f_re, f_im = (nr * a_re + ni * a_im) / den, (ni * a_re - nr * a_im) / den
    tau = jnp.arange(tc + 1, dtype=F32)[:, None, None, None]
    pw_mag = jnp.exp(lam_re[None] * tau)
    pw_re, pw_im = pw_mag * jnp.cos(lam_im[None] * tau), pw_mag * jnp.sin(lam_im[None] * tau)
    fb_re = f_re[..., None] * b_re[None] - f_im[..., None] * b_im[None]
    fb_im = f_re[..., None] * b_im[None] + f_im[..., None] * b_re[None]
    e_re = pw_re[..., None] * fb_re[None] - pw_im[..., None] * fb_im[None]
    e_im = pw_re[..., None] * fb_im[None] + pw_im[..., None] * fb_re[None]
    m = (jnp.einsum('gpn,tdgnq->tdgpq', c_re, e_re[:tc], precision=hp)
         - jnp.einsum('gpn,tdgnq->tdgpq', c_im, e_im[:tc], precision=hp))
    i = jnp.arange(tc)
    lag = i[None, :] - i[:, None]
    mf = jnp.where((lag >= 0)[:, :, None, None, None], m[jnp.clip(lag, 0, tc - 1), 0], 0.0)
    mb = jnp.where((lag <= 0)[:, :, None, None, None], m[jnp.clip(-lag, 0, tc - 1), 1], 0.0)
    w = jnp.transpose(mf + mb, (2, 0, 4, 1, 3)).reshape(g, tc * p, tc * p)
    idx_f = tc - 1 - i
    pf_re, pf_im = e_re[idx_f, 0], e_im[idx_f, 0]
    pb_re, pb_im = e_re[i, 1], e_im[i, 1]
    pm = jnp.concatenate([pf_re, pb_re, pf_im, pb_im], axis=2)
    pm = jnp.transpose(pm, (1, 0, 3, 2)).reshape(g, tc * p, 4 * n)
    cf_re = c_re[None, :, :, :] * pw_re[i + 1, 0][:, :, None, :] - c_im[None] * pw_im[i + 1, 0][:, :, None, :]
    cf_im = c_re[None] * pw_im[i + 1, 0][:, :, None, :] + c_im[None] * pw_re[i + 1, 0][:, :, None, :]
    cb_re = c_re[None] * pw_re[tc - i, 1][:, :, None, :] - c_im[None] * pw_im[tc - i, 1][:, :, None, :]
    cb_im = c_re[None] * pw_im[tc - i, 1][:, :, None, :] + c_im[None] * pw_re[tc - i, 1][:, :, None, :]
    q = jnp.concatenate([cf_re, cb_re, -cf_im, -cb_im], axis=3)
    q = jnp.transpose(q, (1, 3, 0, 2)).reshape(g, 4 * n, tc * p)
    dec_re = jnp.concatenate([pw_re[tc, 0], pw_re[tc, 1]], axis=1)
    dec_im = jnp.concatenate([pw_im[tc, 0], pw_im[tc, 1]], axis=1)
    dec = jnp.stack([dec_re, dec_im], axis=1)
    return w.astype(BF16), pm.astype(BF16), q.astype(BF16), dec


def _s5_kernel(u_ref, w_ref, p_ref, q_ref, dec_ref, y_ref, inc_re, inc_im, sin_re, sin_im, *, n_lat, n_ctx, bsz):
    u = u_ref[0].astype(BF16)
    inc = jnp.dot(u, p_ref[0], preferred_element_type=F32)
    n2 = 2 * S5_STATE
    inc_re[...] = inc[:, :n2]
    inc_im[...] = inc[:, n2:]
    n = n_lat + n_ctx
    d_re = dec_ref[0, 0:1, :]
    d_im = dec_ref[0, 1:2, :]
    is_f = lax.broadcasted_iota(jnp.int32, (bsz, n2), 1) < S5_STATE

    def body(s, carry):
        s_re, s_im = carry
        cf = (s + n_lat) % n
        cb = n - 1 - s
        rf = pl.ds(pl.multiple_of(cf * bsz, bsz), bsz)
        rb = pl.ds(pl.multiple_of(cb * bsz, bsz), bsz)
        sin_re[rf, :] = jnp.where(is_f, s_re, sin_re[rf, :])
        sin_im[rf, :] = jnp.where(is_f, s_im, sin_im[rf, :])
        sin_re[rb, :] = jnp.where(is_f, sin_re[rb, :], s_re)
        sin_im[rb, :] = jnp.where(is_f, sin_im[rb, :], s_im)
        i_re = jnp.where(is_f, inc_re[rf, :], inc_re[rb, :])
        i_im = jnp.where(is_f, inc_im[rf, :], inc_im[rb, :])
        return (d_re * s_re - d_im * s_im + i_re, d_re * s_im + d_im * s_re + i_im)

    zero = jnp.zeros((bsz, n2), F32)
    sin_re[...] = jnp.zeros_like(sin_re)
    sin_im[...] = jnp.zeros_like(sin_im)
    lax.fori_loop(0, n, body, (zero, zero))
    s_all = jnp.concatenate([sin_re[...], sin_im[...]], axis=1).astype(BF16)
    y_ref[0] = jnp.dot(u, w_ref[0], preferred_element_type=F32) + jnp.dot(s_all, q_ref[0], preferred_element_type=F32)


def _s5_core(p_s5, n_lat_rows, mats):
    b, t, c = p_s5.shape
    tc, g, p = S5_CHUNK, S5_GROUPS, S5_GROUP
    nch = t // tc
    w, pm, q, dec = mats
    u = p_s5.reshape(b, nch, tc, g, p).transpose(3, 1, 0, 2, 4).reshape(g, nch * b, tc * p)
    rows = nch * b
    k = tc * p
    y = pl.pallas_call(
        functools.partial(_s5_kernel, n_lat=n_lat_rows // tc, n_ctx=(t - n_lat_rows) // tc, bsz=b),
        grid=(g,),
        in_specs=[pl.BlockSpec((1, rows, k), lambda i: (i, 0, 0)),
                  pl.BlockSpec((1, k, k), lambda i: (i, 0, 0)),
                  pl.BlockSpec((1, k, 4 * S5_STATE), lambda i: (i, 0, 0)),
                  pl.BlockSpec((1, 4 * S5_STATE, k), lambda i: (i, 0, 0)),
                  pl.BlockSpec((1, 2, 2 * S5_STATE), lambda i: (i, 0, 0))],
        out_specs=pl.BlockSpec((1, rows, k), lambda i: (i, 0, 0)),
        out_shape=jax.ShapeDtypeStruct((g, rows, k), F32),
        scratch_shapes=[pltpu.VMEM((rows, 2 * S5_STATE), F32)] * 4,
        compiler_params=_cp("arbitrary"),
        name="s5_core",
    )(u, w, pm, q, dec)
    return y.reshape(g, nch, b, tc, p).transpose(2, 1, 3, 0, 4).reshape(b, t, c)


def _s5_output(y, u, d_skip, glu_w, glu_b):
    z = y + d_skip * u
    z = 0.5 * z * (1.0 + jnp.tanh(math.sqrt(2.0 / math.pi) * (z + 0.044715 * (z * z * z))))
    return z * _sigmoid(_mm(z, glu_w) + glu_b)


def _gla_head_chunk(q, k, v, cum, st, rev):
    c = q.shape[0]
    incl = _tri(c, rev, False)
    mid = c // 2
    ref = cum[mid:mid + 1] if rev else cum[mid - 1:mid]
    tot = cum[0:1] if rev else cum[c - 1:c]
    scores = jnp.where(incl, _mm(q * jnp.exp(cum - ref), k * jnp.exp(ref - cum), NT_DIMS), 0.0)
    o = _mm(scores, v) + _mm(q * jnp.exp(cum), st, NT_DIMS)
    st_new = st * jnp.exp(tot) + _mm(v, k * jnp.exp(tot - cum), TN_DIMS)
    return o, st_new


def _gla_scan_kernel(pf_ref, pb_ref, gup_ref, gb_ref, of_ref, ob_ref, s_scr):
    @pl.when(pl.program_id(1) == 0)
    def _():
        s_scr[...] = jnp.zeros_like(s_scr)

    c = pf_ref.shape[1]
    for d, (p_ref, o_ref) in enumerate(((pf_ref, of_ref), (pb_ref, ob_ref))):
        rev = d == 1
        gd = p_ref[0, :, 2 * GLA_DK + 2 * GLA_DV:]
        x = _mm_hp(gd, gup_ref[...])[:, d * GLA_DK:(d + 1) * GLA_DK] + gb_ref[0:1, d * GLA_DK:(d + 1) * GLA_DK]
        logg = (jnp.minimum(x, 0.0) - jnp.log(1.0 + jnp.exp(-jnp.abs(x)))) * (1.0 / GLA_GATE_NORM)
        cum = _mm_exact01(_tri(c, rev, False).astype(F32), logg)
        outs = []
        for h in range(GLA_HEADS):
            q = p_ref[0, :, h * GLA_HK:(h + 1) * GLA_HK] * (GLA_HK ** -0.5)
            k = p_ref[0, :, GLA_DK + h * GLA_HK:GLA_DK + (h + 1) * GLA_HK]
            v = p_ref[0, :, 2 * GLA_DK + h * GLA_HV:2 * GLA_DK + (h + 1) * GLA_HV]
            o, s_new = _gla_head_chunk(q, k, v, cum[:, h * GLA_HK:(h + 1) * GLA_HK], s_scr[d * GLA_HEADS + h], rev)
            s_scr[d * GLA_HEADS + h] = s_new
            outs.append(o)
        o_ref[0] = jnp.concatenate(outs, axis=1)


def _gla_scan(p_gla, n_lat_rows, gate_up, gate_b):
    b, t, n = p_gla.shape
    ch = SEQ_CHUNK
    fwd, rev = _chunk_orders(n_lat_rows // ch, (t - n_lat_rows) // ch)
    gup = jnp.zeros((128, 2 * GLA_DK), F32)
    gup = gup.at[0:GLA_GATE_RANK, 0:GLA_DK].set(gate_up[0]).at[GLA_GATE_RANK:2 * GLA_GATE_RANK, GLA_DK:].set(gate_up[1])
    return pl.pallas_call(
        _gla_scan_kernel,
        grid=(b, t // ch),
        in_specs=[pl.BlockSpec((1, ch, n), fwd), pl.BlockSpec((1, ch, n), rev),
                  pl.BlockSpec((128, 2 * GLA_DK), lambda b, s: (0, 0)),
                  pl.BlockSpec((1, 2 * GLA_DK), lambda b, s: (0, 0))],
        out_specs=[pl.BlockSpec((1, ch, GLA_DV), fwd), pl.BlockSpec((1, ch, GLA_DV), rev)],
        out_shape=[jax.ShapeDtypeStruct((b, t, GLA_DV), F32)] * 2,
        scratch_shapes=[pltpu.VMEM((2 * GLA_HEADS, GLA_HV, GLA_HK), F32)],
        compiler_params=_cp("arbitrary", "arbitrary"),
        name="gla_scan",
    )(p_gla, p_gla, gup, gate_b.reshape(1, 2 * GLA_DK))


def _gla_readout_kernel(of_ref, ob_ref, g_ref, ng_ref, y_ref):
    o = of_ref[0] + ob_ref[0]
    bd = _head_blockdiag(GLA_DV, GLA_HV)
    ms = _x_exact01(o * o, bd) * (1.0 / GLA_HV)
    y_ref[0] = o * lax.rsqrt(ms + RMS_EPS) * ng_ref[...] * _silu(g_ref[0])


def _gla_readout(of, ob, p_gla, norm_g):
    b, t, c = of.shape
    tile = pl.BlockSpec((1, TOKEN_TILE, c), lambda b, t: (b, t, 0))
    return pl.pallas_call(
        _gla_readout_kernel,
        grid=(b, t // TOKEN_TILE),
        in_specs=[tile, tile, pl.BlockSpec((1, TOKEN_TILE, c), lambda b, t: (b, t, (2 * GLA_DK + GLA_DV) // c)),
                  pl.BlockSpec((1, c), lambda b, t: (0, 0))],
        out_specs=tile,
        out_shape=jax.ShapeDtypeStruct((b, t, c), F32),
        compiler_params=_cp("arbitrary", "arbitrary"),
        name="gla_readout",
    )(of, ob, p_gla, jnp.tile(norm_g, GLA_HEADS).reshape(1, c))


def _to_col_major(x, n_lat_rows):
    b, t, c = x.shape
    lat = x[:, :n_lat_rows].reshape(b, n_lat_rows // GRID_W, GRID_W, c).swapaxes(1, 2).reshape(b, n_lat_rows, c)
    return jnp.concatenate([lat, x[:, n_lat_rows:]], axis=1)


def _to_raster(x, n_lat_rows):
    b, t, c = x.shape
    lat = x[:, :n_lat_rows].reshape(b, GRID_W, n_lat_rows // GRID_W, c).swapaxes(1, 2).reshape(b, n_lat_rows, c)
    return jnp.concatenate([lat, x[:, n_lat_rows:]], axis=1)


def _merge_kernel(h_ref, mod_ref, gate_ref, yhy_ref, rof_ref, rob_ref, rbonus_ref, rg_ref, ys5_ref, us5_ref, ygla_ref,
                  lng_ref, lnb_ref, dskip_ref, gluw_ref, glub_ref, wbr_ref, wout_ref, o_ref):
    d = D_MODEL
    y_rw = _rw_readout(rof_ref[0], rob_ref[0], rbonus_ref[0], rg_ref[0], lng_ref[...], lnb_ref[...])
    y_s5 = _s5_output(ys5_ref[0], us5_ref[0], dskip_ref[...], gluw_ref[...], glub_ref[...])
    acc = jnp.zeros((h_ref.shape[1], d), F32)
    for m, y in enumerate((yhy_ref[0], y_rw, y_s5, ygla_ref[0])):
        acc = acc + _sigmoid(gate_ref[0, :, m * d:(m + 1) * d]) * jnp.dot(y.astype(BF16), wbr_ref[m], preferred_element_type=F32)
    out = jnp.dot(acc.astype(BF16), wout_ref[...], preferred_element_type=F32)
    o_ref[0] = h_ref[0] + mod_ref[0, 0, 2:3, :] * out


def _merge(h_all, modsel, n_tiles, n_lat_tiles, gate, y_hy, rof, rob, rbonus, rg, y_s5, u_s5, y_gla,
           ln_g, ln_b, d_skip, glu_w, glu_b, w_branch, w_out):
    b, t, d = h_all.shape
    c = MIX_W
    tile = lambda w: pl.BlockSpec((1, TOKEN_TILE, w), lambda b, t: (b, t, 0))
    vec = lambda x: x.reshape(1, -1)
    consts = [vec(ln_g), vec(ln_b), vec(d_skip), glu_w.astype(BF16), vec(glu_b), w_branch.astype(BF16), w_out.astype(BF16)]
    cspecs = [pl.BlockSpec(x.shape, (lambda n: lambda b, t: (0,) * n)(x.ndim)) for x in consts]
    return pl.pallas_call(
        _merge_kernel,
        grid=(b, n_tiles),
        in_specs=[tile(d), pl.BlockSpec((1, 1, 6, d), _mod_index(n_lat_tiles)), tile(4 * d)] + [tile(c)] * 8 + cspecs,
        out_specs=tile(d),
        out_shape=jax.ShapeDtypeStruct((b, t, d), F32),
        input_output_aliases={0: 0},
        compiler_params=_cp("arbitrary", "arbitrary"),
        name="merge",
    )(h_all, modsel, gate, y_hy, rof, rob, rbonus, rg, y_s5, u_s5, y_gla, *consts)


def _ffn_kernel(h_ref, g_ref, mod_ref, w1_ref, w3_ref, w2_ref, o_ref):
    h = h_ref[0]
    xm = _norm_mod(h, g_ref[...], mod_ref[0, 0], 3).astype(BF16)
    a = jnp.dot(xm, w1_ref[...], preferred_element_type=F32)
    b = jnp.dot(xm, w3_ref[...], preferred_element_type=F32)
    y = jnp.dot((_silu(a) * b).astype(BF16), w2_ref[...], preferred_element_type=F32)
    o_ref[0] = h + mod_ref[0, 0, 5:6, :] * y


def _ffn_dense(h_all, g, modsel, n_lat_tiles, w1, w3, w2):
    b, t, d = h_all.shape
    tile = pl.BlockSpec((1, TOKEN_TILE, d), lambda b, t: (b, t, 0))
    ws = [w1.astype(BF16), w3.astype(BF16), w2.astype(BF16)]
    wspec = [pl.BlockSpec(w.shape, lambda b, t: (0, 0), pipeline_mode=pl.Buffered(1)) for w in ws]
    return pl.pallas_call(
        _ffn_kernel,
        grid=(b, t // TOKEN_TILE),
        in_specs=[tile, pl.BlockSpec((1, d), lambda b, t: (0, 0)), pl.BlockSpec((1, 1, 6, d), _mod_index(n_lat_tiles))] + wspec,
        out_specs=tile,
        out_shape=jax.ShapeDtypeStruct((b, t, d), F32),
        input_output_aliases={0: 0},
        compiler_params=_cp("arbitrary", "arbitrary"),
        name="ffn_dense",
    )(h_all, g.reshape(1, d), modsel, *ws)


def _router_kernel(h_ref, g_ref, mod_ref, rw_ref, rb_ref, x_ref, logit_ref):
    xm = _norm_mod(h_ref[0], g_ref[...], mod_ref[0, 0], 3)
    x_ref[0] = xm.astype(BF16)
    logit_ref[0] = _mm_hp(xm, rw_ref[...]) + rb_ref[...]


def _router(h_all, g, modsel, n_lat_tiles, router_w, router_b):
    b, t, d = h_all.shape
    n_lat = n_lat_tiles * TOKEN_TILE
    rw = jnp.pad(router_w, ((0, 0), (0, 128 - N_EXPERTS)))
    rb = jnp.pad(router_b, (0, 128 - N_EXPERTS), constant_values=-1e30).reshape(1, 128)
    tile = lambda w: pl.BlockSpec((1, TOKEN_TILE, w), lambda b, t: (b, t, 0))
    return pl.pallas_call(
        _router_kernel,
        grid=(b, n_lat_tiles),
        in_specs=[tile(d), pl.BlockSpec((1, d), lambda b, t: (0, 0)), pl.BlockSpec((1, 1, 6, d), _mod_index(n_lat_tiles)),
                  pl.BlockSpec((d, 128), lambda b, t: (0, 0)), pl.BlockSpec((1, 128), lambda b, t: (0, 0))],
        out_specs=[tile(d), tile(128)],
        out_shape=[jax.ShapeDtypeStruct((b, n_lat, d), BF16), jax.ShapeDtypeStruct((b, n_lat, 128), F32)],
        compiler_params=_cp("arbitrary", "arbitrary"),
        name="moe_router",
    )(h_all, g.reshape(1, d), modsel, rw, rb)


def _moe_expert_kernel(be_ref, x_ref, w1_ref, w3_ref, w2_ref, o_ref):
    x = x_ref[...]
    a = jnp.dot(x, w1_ref[0], preferred_element_type=F32)
    b = jnp.dot(x, w3_ref[0], preferred_element_type=F32)
    o_ref[...] = jnp.dot((_silu(a) * b).astype(BF16), w2_ref[0], preferred_element_type=F32)


def _moe_experts(xb, block_exp, w1, w3, w2):
    rows, d = xb.shape
    nb = rows // MOE_BLOCK_ROWS
    f = w1.shape[2]
    grid_spec = pltpu.PrefetchScalarGridSpec(
        num_scalar_prefetch=1,
        grid=(nb,),
        in_specs=[pl.BlockSpec((MOE_BLOCK_ROWS, d), lambda i, be: (i, 0)),
                  pl.BlockSpec((1, d, f), lambda i, be: (be[i], 0, 0), pipeline_mode=pl.Buffered(1)),
                  pl.BlockSpec((1, d, f), lambda i, be: (be[i], 0, 0), pipeline_mode=pl.Buffered(1)),
                  pl.BlockSpec((1, f, d), lambda i, be: (be[i], 0, 0), pipeline_mode=pl.Buffered(1))],
        out_specs=pl.BlockSpec((MOE_BLOCK_ROWS, d), lambda i, be: (i, 0)),
    )
    return pl.pallas_call(
        _moe_expert_kernel,
        grid_spec=grid_spec,
        out_shape=jax.ShapeDtypeStruct((rows, d), F32),
        compiler_params=_cp("arbitrary"),
        name="moe_experts",
    )(block_exp, xb, w1, w3, w2)


def _moe_combine_kernel(h_ref, mod_ref, y_ref, gate_ref, fg_ref, o_ref):
    d = h_ref.shape[2]
    y = y_ref[0, :, :d] * gate_ref[0, :, 0:1] + y_ref[0, :, d:] * gate_ref[0, :, 1:2]
    h = h_ref[0] + mod_ref[0, 0, 5:6, :] * y
    o_ref[0] = h * lax.rsqrt(jnp.mean(h * h, axis=-1, keepdims=True) + RMS_EPS) * fg_ref[...]


def _moe_combine(h_all, modsel, n_lat_tiles, y2, gates, final_g):
    b, t, d = h_all.shape
    n_lat = n_lat_tiles * TOKEN_TILE
    tile = lambda w: pl.BlockSpec((1, TOKEN_TILE, w), lambda b, t: (b, t, 0))
    return pl.pallas_call(
        _moe_combine_kernel,
        grid=(b, n_lat_tiles),
        in_specs=[tile(d), pl.BlockSpec((1, 1, 6, d), _mod_index(n_lat_tiles)),
                  tile(TOP_K * d),
                  tile(128), pl.BlockSpec((1, d), lambda b, t: (0, 0))],
        out_specs=tile(d),
        out_shape=jax.ShapeDtypeStruct((b, n_lat, d), F32),
        compiler_params=_cp("arbitrary", "arbitrary"),
        name="moe_combine",
    )(h_all, modsel, y2, gates, final_g.reshape(1, d))


def _moe_ffn_and_final_norm(h_all, g, modsel, n_lat_tiles, router_w, router_b, w1, w3, w2, final_g):
    b, t, d = h_all.shape
    n_lat = n_lat_tiles * TOKEN_TILE
    n_tok = b * n_lat
    xn, logits = _router(h_all, g, modsel, n_lat_tiles, router_w, router_b)
    logits = logits.reshape(n_tok, 128)[:, :N_EXPERTS]
    top_v, top_i = lax.top_k(logits, TOP_K)
    gates = jax.nn.softmax(top_v, axis=-1)
    n_assign = n_tok * TOP_K
    e_flat = top_i.reshape(-1)
    order = jnp.argsort(e_flat)
    e_sorted = e_flat[order]
    counts = jnp.bincount(e_flat, length=N_EXPERTS)
    padded = (counts + MOE_BLOCK_ROWS - 1) // MOE_BLOCK_ROWS * MOE_BLOCK_ROWS
    start = jnp.cumsum(counts) - counts
    pend = jnp.cumsum(padded)
    pstart = pend - padded
    dest = (pstart[e_sorted] + jnp.arange(n_assign) - start[e_sorted]).astype(jnp.int32)
    n_blocks = n_assign // MOE_BLOCK_ROWS + N_EXPERTS
    n_rows = n_blocks * MOE_BLOCK_ROWS
    row_tok = jnp.full((n_rows,), n_tok, jnp.int32).at[dest].set((order // TOP_K).astype(jnp.int32))
    block_exp = jnp.minimum(jnp.searchsorted(pend, jnp.arange(n_blocks) * MOE_BLOCK_ROWS, side='right'),
                            N_EXPERTS - 1).astype(jnp.int32)
    tok_pad = jnp.concatenate([xn.reshape(n_tok, d), jnp.zeros((1, d), BF16)], axis=0)
    xb = jnp.take(tok_pad, row_tok, axis=0)
    yb = _moe_experts(xb, block_exp, w1.astype(BF16), w3.astype(BF16), w2.astype(BF16))
    dest_orig = jnp.zeros((n_assign,), jnp.int32).at[order].set(dest)
    y2 = jnp.take(yb, dest_orig, axis=0).reshape(b, n_lat, TOP_K * d)
    gates_pad = jnp.pad(gates, ((0, 0), (0, 128 - TOP_K))).reshape(b, n_lat, 128)
    return _moe_combine(h_all, modsel, n_lat_tiles, y2, gates_pad, final_g)


def _pad_cols(w, n):
    return jnp.pad(w, ((0, 0), (0, n - w.shape[1])))


def kernel(x, c, ctx, c_ctx, ada_w, ada_b, norm_mix_g, norm_ffn_g, w_in, hy_conv_w, hy_conv_b, hy_f_w1, hy_f_b1, hy_f_w2, hy_f_b2, hy_f_w3, hy_f_b3, hy_bias, rw_mu, rw_w0, rw_w_up, rw_a0, rw_a_up, rw_g_up, rw_k_k, rw_k_a, rw_r_k, rw_ln_g, rw_ln_b, s5_a_re, s5_a_im, s5_log_dt, s5_b_re, s5_b_im, s5_c_re, s5_c_im, s5_d, s5_glu_w, s5_glu_b, gla_gate_up, gla_gate_b, gla_norm_g, w_branch, w_out, ffn_w1, ffn_w3, ffn_w2, moe_router_w, moe_router_b, moe_w1, moe_w3, moe_w2, final_norm_g):
    bsz, n_lat, d = x.shape
    n_ctx = ctx.shape[1]
    depth = w_in.shape[0]
    assert depth == 2 and d == D_MODEL
    assert n_lat % TOKEN_TILE == 0 and n_ctx % TOKEN_TILE == 0 and n_lat % GRID_W == 0
    t_all = n_lat + n_ctx
    n_lat_tiles = n_lat // TOKEN_TILE
    n_tiles = t_all // TOKEN_TILE
    assert n_tiles <= 2 * n_lat_tiles

    rows = (bsz + 1 + 7) // 8 * 8
    c_all = jnp.zeros((rows, d), F32).at[:bsz].set(c.astype(F32)).at[bsz].set(c_ctx.astype(F32))
    mod = _ada_mod(c_all, ada_w, ada_b)
    h_all = jnp.concatenate([x.astype(F32), ctx.astype(F32)], axis=1)

    dft_lat = _dft_tables(n_lat)
    dft_ctx = _dft_tables(n_ctx)
    seg = np.cumsum([0, 3 * MIX_W, RW_IN, MIX_W, 2 * GLA_DK + 2 * GLA_DV + 2 * GLA_GATE_RANK, 4 * d])

    out = None
    for l in range(depth):
        last = l == depth - 1
        mod_l = mod[l, :bsz].reshape(bsz, 6, d)
        mod_c = jnp.broadcast_to(mod[l, bsz].reshape(1, 6, d), (bsz, 6, d))
        modsel = jnp.stack([mod_l, mod_c], axis=1)

        wl = w_in[l].astype(BF16)
        ws = [wl[:, seg[0]:seg[1]], wl[:, seg[1]:seg[2]], wl[:, seg[2]:seg[3]],
              _pad_cols(wl[:, seg[3]:seg[4]], GLA_IN_PAD), wl[:, seg[4]:seg[5]]]
        p_hy, p_rw, p_s5, p_gla, gate = _in_proj(h_all, norm_mix_g[l], modsel, ws, n_lat_tiles)

        filt_args = (hy_f_w1[l], hy_f_b1[l], hy_f_w2[l], hy_f_b2[l], hy_f_w3[l], hy_f_b3[l])
        spec_lat = _hy_spectrum(_hy_filter(n_lat, *filt_args), dft_lat[0], dft_lat[1])
        y_hy = _hyena(p_hy, None, 0, n_lat, hy_conv_w[l], hy_conv_b[l], hy_bias[l], dft_lat, spec_lat)
        if not last:
            spec_ctx = _hy_spectrum(_hy_filter(n_ctx, *filt_args), dft_ctx[0], dft_ctx[1])
            y_hy = _hyena(p_hy, y_hy, n_lat // n_ctx, n_ctx, hy_conv_w[l], hy_conv_b[l], hy_bias[l], dft_ctx, spec_ctx)

        r, v, kap, lwf, lwb, kf, kb, bf, bb, rg, rbonus = _rw_prep(
            p_rw, n_lat_tiles, rw_mu[l], rw_w0[l], rw_w_up[l], rw_a0[l], rw_a_up[l], rw_g_up[l], rw_k_k[l], rw_k_a[l], rw_r_k[l])
        rof, rob = _rw_scan(r, v, kap, lwf, lwb, kf, kb, bf, bb, n_lat)

        y_s5 = _s5_core(p_s5, n_lat, _s5_matrices(s5_a_re[l], s5_a_im[l], s5_log_dt[l], s5_b_re[l], s5_b_im[l], s5_c_re[l], s5_c_im[l]))

        p_gla_cm = _to_col_major(p_gla, n_lat)
        gof, gob = _gla_scan(p_gla_cm, n_lat, gla_gate_up[l], gla_gate_b[l])
        y_gla = _to_raster(_gla_readout(gof, gob, p_gla_cm, gla_norm_g[l]), n_lat)

        nt_l = n_lat_tiles if last else n_tiles
        h_all = _merge(h_all, modsel, nt_l, n_lat_tiles, gate, y_hy, rof, rob, rbonus, rg, y_s5, p_s5, y_gla,
                       rw_ln_g[l], rw_ln_b[l], s5_d[l], s5_glu_w[l], s5_glu_b[l], w_branch[l], w_out[l])

        i = l // 2
        if l % 2 == 0:
            h_all = _ffn_dense(h_all, norm_ffn_g[l], modsel, n_lat_tiles, ffn_w1[i], ffn_w3[i], ffn_w2[i])
        else:
            out = _moe_ffn_and_final_norm(h_all, norm_ffn_g[l], modsel, n_lat_tiles, moe_router_w[i], moe_router_b[i],
                                          moe_w1[i], moe_w3[i], moe_w2[i], final_norm_g)
    return out.astype(x.dtype)
```

```python
import functools
import math

import numpy as np
import jax
import jax.numpy as jnp
from jax import lax
from jax.experimental import pallas as pl
from jax.experimental.pallas import tpu as pltpu

F32 = jnp.float32
BF16 = jnp.bfloat16

D_MODEL = 1024
GRID_W = 64
MIX_W = 512
RMS_EPS = 1e-6
HY_BANDS = 16
HY_DECAY_TARGET = 1e-2
HY_SHORT_PCT = 0.3
HY_LONG_PCT = 1.5
RW_HEADS = 8
RW_HEAD_DIM = 64
RW_DECAY_SCALE = 0.606531
RW_GN_EPS = 64e-5
RW_IN = 1920
S5_GROUP = 16
S5_GROUPS = 32
S5_STATE = 64
S5_CHUNK = 16
GLA_HEADS = 4
GLA_HK = 64
GLA_HV = 128
GLA_DK = 256
GLA_DV = 512
GLA_GATE_RANK = 16
GLA_GATE_NORM = 16.0
GLA_IN_PAD = 1664
N_EXPERTS = 8
TOP_K = 2
MOE_BLOCK_ROWS = 256
TOKEN_TILE = 256
SEQ_CHUNK = 64
VMEM_LIMIT = 56 * 1024 * 1024

NT_DIMS = (((1,), (1,)), ((), ()))
TN_DIMS = (((0,), (0,)), ((), ()))


def _cp(*sem):
    return pltpu.CompilerParams(dimension_semantics=sem, vmem_limit_bytes=VMEM_LIMIT)


def _mm(a, b, dims=None):
    a = a.astype(BF16)
    b = b.astype(BF16)
    if dims is None:
        return jnp.dot(a, b, preferred_element_type=F32)
    return lax.dot_general(a, b, dims, preferred_element_type=F32)


def _split_bf16(x):
    hi = x.astype(BF16)
    lo = (x - hi.astype(F32)).astype(BF16)
    return hi, lo


def _mm_hp(a, b):
    ah, al = _split_bf16(a)
    bh, bl = _split_bf16(b)
    d = functools.partial(jnp.dot, preferred_element_type=F32)
    return d(ah, bh) + (d(ah, bl) + d(al, bh))


def _mm_exact01(m01, x):
    m = m01.astype(BF16)
    hi = x.astype(BF16)
    r1 = x - hi.astype(F32)
    mid = r1.astype(BF16)
    lo = (r1 - mid.astype(F32)).astype(BF16)
    d = functools.partial(jnp.dot, preferred_element_type=F32)
    return d(m, hi) + (d(m, mid) + d(m, lo))


def _x_exact01(x, m01):
    m = m01.astype(BF16)
    hi = x.astype(BF16)
    r1 = x - hi.astype(F32)
    mid = r1.astype(BF16)
    lo = (r1 - mid.astype(F32)).astype(BF16)
    d = functools.partial(jnp.dot, preferred_element_type=F32)
    return d(hi, m) + (d(mid, m) + d(lo, m))


def _sigmoid(x):
    return 1.0 / (1.0 + jnp.exp(-x))


def _silu(x):
    return x * _sigmoid(x)


def _tri(c, rev, strict):
    i = lax.broadcasted_iota(jnp.int32, (c, c), 0)
    j = lax.broadcasted_iota(jnp.int32, (c, c), 1)
    if rev:
        return (j > i) if strict else (j >= i)
    return (j < i) if strict else (j <= i)


def _head_blockdiag(n, head):
    i = lax.broadcasted_iota(jnp.int32, (n, n), 0) // head
    j = lax.broadcasted_iota(jnp.int32, (n, n), 1) // head
    return (i == j).astype(F32)


def _shift_rows(x, prev_row, next_row):
    n = x.shape[0]
    row = lax.broadcasted_iota(jnp.int32, x.shape, 0)
    xm = jnp.where(row == 0, prev_row, pltpu.roll(x, 1, 0))
    xp = jnp.where(row == n - 1, next_row, pltpu.roll(x, n - 1, 0))
    return xm, xp


def _ada_kernel(c_ref, w_ref, b_ref, o_ref):
    o_ref[0] = _mm_hp(_silu(c_ref[...]), w_ref[0]) + b_ref[0]


def _ada_mod(c_all, ada_w, ada_b):
    depth, d, n = ada_w.shape
    rows = c_all.shape[0]
    return pl.pallas_call(
        _ada_kernel,
        grid=(depth, n // d),
        in_specs=[pl.BlockSpec((rows, d), lambda l, j: (0, 0)),
                  pl.BlockSpec((1, d, d), lambda l, j: (l, 0, j)),
                  pl.BlockSpec((1, 1, d), lambda l, j: (l, 0, j))],
        out_specs=pl.BlockSpec((1, rows, d), lambda l, j: (l, 0, j)),
        out_shape=jax.ShapeDtypeStruct((depth, rows, n), F32),
        compiler_params=_cp("arbitrary", "arbitrary"),
        name="ada_mod",
    )(c_all, ada_w, ada_b.reshape(depth, 1, n))


def _norm_mod(x, g, mod, sh_row):
    xn = x * lax.rsqrt(jnp.mean(x * x, axis=-1, keepdims=True) + RMS_EPS) * g
    return xn * (1.0 + mod[sh_row + 1:sh_row + 2]) + mod[sh_row:sh_row + 1]


def _in_proj_kernel(x_ref, g_ref, mod_ref, w_hy, w_rw, w_s5, w_gla, w_gate, o_hy, o_rw, o_s5, o_gla, o_gate):
    xm = _norm_mod(x_ref[0], g_ref[...], mod_ref[0, 0], 0).astype(BF16)
    for w, o in ((w_hy, o_hy), (w_rw, o_rw), (w_s5, o_s5), (w_gla, o_gla), (w_gate, o_gate)):
        o[0] = jnp.dot(xm, w[...], preferred_element_type=F32)


def _mod_index(n_lat_tiles):
    return lambda b, t: (b, t // n_lat_tiles, 0, 0)


def _in_proj(h_all, g, modsel, ws, n_lat_tiles):
    b, t, d = h_all.shape
    nt = t // TOKEN_TILE
    wspec = [pl.BlockSpec(w.shape, lambda b, t: (0, 0), pipeline_mode=pl.Buffered(1)) for w in ws]
    return pl.pallas_call(
        _in_proj_kernel,
        grid=(b, nt),
        in_specs=[pl.BlockSpec((1, TOKEN_TILE, d), lambda b, t: (b, t, 0)),
                  pl.BlockSpec((1, d), lambda b, t: (0, 0)),
                  pl.BlockSpec((1, 1, 6, d), _mod_index(n_lat_tiles))] + wspec,
        out_specs=[pl.BlockSpec((1, TOKEN_TILE, w.shape[1]), lambda b, t: (b, t, 0)) for w in ws],
        out_shape=[jax.ShapeDtypeStruct((b, t, w.shape[1]), F32) for w in ws],
        compiler_params=_cp("arbitrary", "arbitrary"),
        name="in_proj",
    )(h_all, g.reshape(1, d), modsel, *ws)


def _hy_filter_kernel(z_ref, w1, b1, w2, b2, w3, b3, dec_ref, o_ref):
    h = jnp.sin(_mm_hp(z_ref[...], w1[...]) + b1[...])
    h = jnp.sin(_mm_hp(h, w2[...]) + b2[...])
    o_ref[...] = (_mm_hp(h, w3[...]) + b3[...]) * dec_ref[...]


def _hy_features(length):
    pos = jnp.arange(length, dtype=F32)
    t = pos / max(length - 1, 1)
    freqs = jnp.linspace(1e-4, HY_BANDS - 1, HY_BANDS, dtype=F32)
    ang = (2.0 * math.pi / length) * pos[:, None] * freqs[None, :]
    z = jnp.concatenate([t[:, None], jnp.cos(ang), -jnp.sin(ang)], axis=-1)
    rates = jnp.abs(jnp.linspace(math.log(HY_DECAY_TARGET) / HY_LONG_PCT,
                                 math.log(HY_DECAY_TARGET) / HY_SHORT_PCT, MIX_W, dtype=F32))
    rates = jnp.concatenate([rates, rates])
    decay = jnp.exp(-t[:, None] * rates[None, :])
    return jnp.pad(z, ((0, 0), (0, 128 - z.shape[1]))), decay


def _hy_filter(length, fw1, fb1, fw2, fb2, fw3, fb3):
    z, decay = _hy_features(length)
    w1 = jnp.pad(fw1, ((0, 128 - fw1.shape[0]), (0, 128 - fw1.shape[1])))
    b1 = jnp.pad(fb1, (0, 128 - fb1.shape[0])).reshape(1, 128)
    w2 = jnp.pad(fw2, ((0, 128 - fw2.shape[0]), (0, 128 - fw2.shape[1])))
    b2 = jnp.pad(fb2, (0, 128 - fb2.shape[0])).reshape(1, 128)
    w3 = jnp.pad(fw3, ((0, 128 - fw3.shape[0]), (0, 0)))
    n = fw3.shape[1]
    rows = min(length, 512)
    full = lambda s: pl.BlockSpec(s, lambda i: (0, 0))
    return pl.pallas_call(
        _hy_filter_kernel,
        grid=(length // rows,),
        in_specs=[pl.BlockSpec((rows, 128), lambda i: (i, 0)), full((128, 128)), full((1, 128)), full((128, 128)),
                  full((1, 128)), full((128, n)), full((1, n)), pl.BlockSpec((rows, n), lambda i: (i, 0))],
        out_specs=pl.BlockSpec((rows, n), lambda i: (i, 0)),
        out_shape=jax.ShapeDtypeStruct((length, n), F32),
        compiler_params=_cp("arbitrary"),
        name="hy_filter",
    )(z, w1, b1, w2, b2, w3, fb3.reshape(1, n), decay)


def _dft_tables(length):
    n = 2 * length
    k = jnp.arange(length, dtype=jnp.int32)
    m = (k[:, None] * k[None, :]) % n
    ang = m.astype(F32) * (2.0 * math.pi / n)
    wc = jnp.cos(ang)
    ws = jnp.sin(ang)
    nyq = jnp.where(k % 2 == 0, 1.0, -1.0).astype(F32)
    ws = ws.at[0].set(nyq)
    return wc.astype(BF16), ws.astype(BF16), wc.T.astype(BF16), ws.T.astype(BF16)


def _dft_fwd_kernel(wc_ref, ws_ref, xh_ref, xl_ref, oc_ref, os_ref):
    d = functools.partial(jnp.dot, preferred_element_type=F32)
    oc_ref[...] = d(wc_ref[...], xh_ref[...]) + d(wc_ref[...], xl_ref[...])
    os_ref[...] = d(ws_ref[...], xh_ref[...]) + d(ws_ref[...], xl_ref[...])


def _hy_spectrum(filt, wc, ws):
    length, c2 = filt.shape
    c = c2 // 2
    n = 2 * length
    hb = filt[:, c:].at[0].set(0.0)
    x = jnp.concatenate([filt[:, :c], hb], axis=1)
    xh, xl = _split_bf16(x)
    fr = min(length, 512)
    fc, fs = pl.pallas_call(
        _dft_fwd_kernel,
        grid=(length // fr,),
        in_specs=[pl.BlockSpec((fr, length), lambda i: (i, 0)), pl.BlockSpec((fr, length), lambda i: (i, 0)),
                  pl.BlockSpec((length, c2), lambda i: (0, 0)), pl.BlockSpec((length, c2), lambda i: (0, 0))],
        out_specs=[pl.BlockSpec((fr, c2), lambda i: (i, 0)), pl.BlockSpec((fr, c2), lambda i: (i, 0))],
        out_shape=[jax.ShapeDtypeStruct((length, c2), F32)] * 2,
        compiler_params=_cp("arbitrary"),
        name="hy_spectrum",
    )(wc, ws, xh, xl)
    k_re = fc[:, :c] + fc[:, c:]
    k_im = fs[:, c:] - fs[:, :c]
    k_nyq = fs[0, :c] + fs[0, c:]
    ka = (k_re * (2.0 / n)).at[0].set(k_re[0] / n)
    kb = (k_im * (2.0 / n)).at[0].set(0.0)
    ka2 = ka.at[0].set(k_nyq / n)
    return ka, ka2, kb


def _hyena_kernel(p_ref, cw_ref, cb_ref, bias_ref, wc_ref, ws_ref, wct_ref, wst_ref, ka_ref, ka2_ref, kb_ref,
                  o_ref, z_scr, acc_scr):
    f = pl.program_id(1)
    c = MIX_W

    def conv3(lo):
        p = p_ref[0, :, lo:lo + c]
        zero = jnp.zeros((1, c), F32)
        pm, pp = _shift_rows(p, zero, zero)
        return pm * cw_ref[0:1, lo:lo + c] + p * cw_ref[1:2, lo:lo + c] + pp * cw_ref[2:3, lo:lo + c] + cb_ref[0:1, lo:lo + c]

    @pl.when(f == 0)
    def _():
        z_scr[...] = (conv3(2 * c) * conv3(c)).astype(BF16)
        acc_scr[...] = jnp.zeros_like(acc_scr)

    z = z_scr[...]
    zc = jnp.dot(wc_ref[...], z, preferred_element_type=F32)
    zs = jnp.dot(ws_ref[...], z, preferred_element_type=F32)
    a = zc * ka_ref[...] + zs * kb_ref[...]
    bv = zs * ka2_ref[...] - zc * kb_ref[...]
    acc_scr[...] += (jnp.dot(wct_ref[...], a.astype(BF16), preferred_element_type=F32)
                     + jnp.dot(wst_ref[...], bv.astype(BF16), preferred_element_type=F32))

    @pl.when(f == pl.num_programs(1) - 1)
    def _():
        zf = conv3(2 * c) * conv3(c)
        o_ref[0] = conv3(0) * (acc_scr[...] + zf * bias_ref[...])


def _hyena(p_hy, y_prev, row_block, length, conv_w, conv_b, bias, tables, spec):
    b, t, c3 = p_hy.shape
    wc, ws, wct, wst = tables
    ka, ka2, kb = spec
    fr = min(length, 256)
    nf = length // fr
    c = MIX_W
    args = [p_hy, conv_w, conv_b.reshape(1, c3), bias.reshape(1, c), wc, ws, wct, wst, ka, ka2, kb]
    in_specs = [pl.BlockSpec((1, length, c3), lambda b, f: (b, row_block, 0), pipeline_mode=pl.Buffered(1)),
                pl.BlockSpec((3, c3), lambda b, f: (0, 0)),
                pl.BlockSpec((1, c3), lambda b, f: (0, 0)),
                pl.BlockSpec((1, c), lambda b, f: (0, 0)),
                pl.BlockSpec((fr, length), lambda b, f: (f, 0)),
                pl.BlockSpec((fr, length), lambda b, f: (f, 0)),
                pl.BlockSpec((length, fr), lambda b, f: (0, f)),
                pl.BlockSpec((length, fr), lambda b, f: (0, f)),
                pl.BlockSpec((fr, c), lambda b, f: (f, 0)),
                pl.BlockSpec((fr, c), lambda b, f: (f, 0)),
                pl.BlockSpec((fr, c), lambda b, f: (f, 0))]
    kwargs = {}
    if y_prev is not None:
        args.append(y_prev)
        in_specs.append(pl.BlockSpec(memory_space=pl.ANY))
        kwargs["input_output_aliases"] = {len(args) - 1: 0}
        kern = lambda *refs: _hyena_kernel(*refs[:11], *refs[12:])
    else:
        kern = _hyena_kernel
    return pl.pallas_call(
        kern,
        grid=(b, nf),
        in_specs=in_specs,
        out_specs=pl.BlockSpec((1, length, c), lambda b, f: (b, row_block, 0)),
        out_shape=jax.ShapeDtypeStruct((b, t, c), F32),
        scratch_shapes=[pltpu.VMEM((length, c), BF16), pltpu.VMEM((length, c), F32)],
        compiler_params=_cp("arbitrary", "arbitrary"),
        name="hyena_%d" % length,
        **kwargs,
    )(*args)


def _rw_prep_kernel(p_ref, pprev_ref, pnext_ref, mu_ref, w0_ref, wup_ref, a0_ref, aup_ref, gup_ref, kk_ref, ka_ref, rk_ref,
                    r_o, v_o, kap_o, lwf_o, lwb_o, kf_o, kb_o, bf_o, bb_o, g_o, bonus_o, *, n_lat_tiles):
    t = pl.program_id(1)
    nt = pl.num_programs(1)
    p = p_ref[0]
    first = jnp.logical_or(t == 0, t == n_lat_tiles)
    last = jnp.logical_or(t == n_lat_tiles - 1, t == nt - 1)
    prev_row = jnp.where(first, 0.0, pprev_ref[0, 7:8, :])
    next_row = jnp.where(last, 0.0, pnext_ref[0, 0:1, :])
    pm, pp = _shift_rows(p, prev_row, next_row)
    p = p + mu_ref[0:1] * (pm - p) + mu_ref[1:2] * (pp - p)
    c = MIX_W
    r, k, v = p[:, 0:c], p[:, c:2 * c], p[:, 2 * c:3 * c]
    wd, ad, gd = p[:, 3 * c:3 * c + 128], p[:, 3 * c + 128:3 * c + 256], p[:, 3 * c + 256:3 * c + 384]
    bd = _head_blockdiag(c, RW_HEAD_DIM)
    g = _mm_hp(_sigmoid(gd), gup_ref[...])
    kk = k * kk_ref[...]
    nrm = jnp.sqrt(_x_exact01(kk * kk, bd))
    kap = kk / jnp.maximum(nrm, 1e-12)
    logw = -RW_DECAY_SCALE * _sigmoid(w0_ref[...] + _mm_hp(jnp.tanh(wd), wup_ref[...]))
    a = _sigmoid(a0_ref[...] + _mm_hp(ad, aup_ref[...]))
    bonus_s = 0.0
    outs_k, outs_b = (kf_o, kb_o), (bf_o, bb_o)
    for d in range(2):
        a_d = a[:, d * c:(d + 1) * c]
        k_d = k * (1.0 + (a_d - 1.0) * ka_ref[...])
        outs_k[d][0] = k_d
        outs_b[d][0] = a_d * kap
        bonus_s = bonus_s + _x_exact01(r * k_d * rk_ref[...], bd)
    r_o[0] = r
    v_o[0] = v
    kap_o[0] = kap
    lwf_o[0] = logw[:, 0:c]
    lwb_o[0] = logw[:, c:2 * c]
    g_o[0] = g
    bonus_o[0] = bonus_s * v


def _blockdiag2(a, b):
    z1 = jnp.zeros((a.shape[0], b.shape[1]), a.dtype)
    z2 = jnp.zeros((b.shape[0], a.shape[1]), a.dtype)
    return jnp.concatenate([jnp.concatenate([a, z1], 1), jnp.concatenate([z2, b], 1)], 0)


def _rw_prep(p_rw, n_lat_tiles, mu, w0, w_up, a0, a_up, g_up, k_k, k_a, r_k):
    b, t, n = p_rw.shape
    nt = t // TOKEN_TILE
    c = MIX_W
    sub = TOKEN_TILE // 8
    n8 = t // 8
    consts = [mu, w0.reshape(1, 2 * c), _blockdiag2(w_up[0], w_up[1]), a0.reshape(1, 2 * c), _blockdiag2(a_up[0], a_up[1]),
              g_up, k_k.reshape(1, c), k_a.reshape(1, c), r_k.reshape(1, c)]
    cspecs = [pl.BlockSpec(x.shape, lambda b, t: (0, 0)) for x in consts]
    tile = lambda w: pl.BlockSpec((1, TOKEN_TILE, w), lambda b, t: (b, t, 0))
    return pl.pallas_call(
        functools.partial(_rw_prep_kernel, n_lat_tiles=n_lat_tiles),
        grid=(b, nt),
        in_specs=[tile(n),
                  pl.BlockSpec((1, 8, n), lambda b, t: (b, jnp.maximum(t * sub - 1, 0), 0)),
                  pl.BlockSpec((1, 8, n), lambda b, t: (b, jnp.minimum((t + 1) * sub, n8 - 1), 0))] + cspecs,
        out_specs=[tile(c)] * 11,
        out_shape=[jax.ShapeDtypeStruct((b, t, c), F32)] * 11,
        compiler_params=_cp("arbitrary", "arbitrary"),
        name="rw_prep",
    )(p_rw, p_rw, p_rw, *consts)


def _rw_scan_kernel(rf, vf, kapf, lwf, kf, bbf, rb, vb, kapb, lwb, kb, bbb, of_ref, ob_ref, s_scr):
    @pl.when(pl.program_id(1) == 0)
    def _():
        s_scr[...] = jnp.zeros_like(s_scr)

    c = rf.shape[1]
    pw_ = 2 * RW_HEAD_DIM
    n_pairs = MIX_W // pw_
    lane_a = lax.broadcasted_iota(jnp.int32, (c, pw_), 1) < RW_HEAD_DIM
    lane_a2 = lax.broadcasted_iota(jnp.int32, (2 * c, pw_), 1) < RW_HEAD_DIM
    blk = ((lax.broadcasted_iota(jnp.int32, (pw_, pw_), 0) < RW_HEAD_DIM)
           == (lax.broadcasted_iota(jnp.int32, (pw_, pw_), 1) < RW_HEAD_DIM))

    def sel(x):
        return jnp.where(lane_a, x[:c], x[c:])

    def bf(x):
        return x.astype(BF16)

    chains = []
    for d, (r, v, kap, lw, k, b) in enumerate(((rf, vf, kapf, lwf, kf, bbf), (rb, vb, kapb, lwb, kb, bbb))):
        rev = d == 1
        incl = _tri(c, rev, False)
        strict = _tri(c, rev, True)
        logw = lw[0]
        cum = _mm_exact01(incl.astype(F32), logw)
        tot = cum[0:1] if rev else cum[c - 1:c]
        ginv = jnp.exp(-cum)
        gt = jnp.exp(tot - cum)
        rt = r[0] * jnp.exp(cum)
        kt = kap[0] * jnp.exp(cum - logw)
        kd, bd = k[0] * ginv, b[0] * ginv
        kg, bg = k[0] * gt, b[0] * gt
        dec = jnp.exp(tot)
        vv = v[0]
        for p in range(n_pairs):
            sl = slice(p * pw_, (p + 1) * pw_)
            lhs2 = jnp.concatenate([rt[:, sl], kt[:, sl]], 0)
            chains.append(dict(
                idx=d * n_pairs + p, incl=incl, strict=strict,
                lhs2=bf(lhs2),
                lhs4=bf(jnp.concatenate([jnp.where(lane_a2, lhs2, 0.0), jnp.where(lane_a2, 0.0, lhs2)], 0)),
                kd=bf(kd[:, sl]), bd=bf(bd[:, sl]), v=bf(vv[:, sl]),
                kgbg=bf(jnp.concatenate([kg[:, sl], bg[:, sl]], 0)), dec=dec[:, sl]))

    dot = functools.partial(lax.dot_general, preferred_element_type=F32)
    nn = (((1,), (0,)), ((), ()))
    for ch in chains:
        ch["s"] = s_scr[ch["idx"]]
    for ch in chains:
        ch["a1k"] = dot(ch["lhs4"], ch["kd"], NT_DIMS)
        ch["a1b"] = dot(ch["lhs4"], ch["bd"], NT_DIMS)
        ch["a2"] = dot(ch["lhs2"], bf(ch["s"]), NT_DIMS)
    for ch in chains:
        a1k, a1b, st = ch["a1k"], ch["a1b"], ch["strict"]
        lk = jnp.where(jnp.concatenate([st, st], 0), jnp.concatenate([a1k[c:2 * c], a1k[3 * c:]], 0), 0.0)
        ch["lb_a"] = bf(jnp.where(st, a1b[c:2 * c], 0.0))
        ch["lb_b"] = bf(jnp.where(st, a1b[3 * c:], 0.0))
        ch["u"] = ch["a2"][c:] + sel(dot(bf(lk), ch["v"], nn))
    for ch in chains:
        ch["u"] = ch["u"] - sel(dot(jnp.concatenate([ch["lb_a"], ch["lb_b"]], 0), bf(ch["u"]), nn))
        ch["p_a"] = bf(dot(ch["lb_a"], ch["lb_a"], nn))
        ch["p_b"] = bf(dot(ch["lb_b"], ch["lb_b"], nn))
    n = int(math.log2(c)) - 1
    for i in range(n):
        for ch in chains:
            ch["u"] = ch["u"] + sel(dot(jnp.concatenate([ch["p_a"], ch["p_b"]], 0), bf(ch["u"]), nn))
            if i < n - 1:
                ch["p_a"] = bf(dot(ch["p_a"], ch["p_a"], nn))
                ch["p_b"] = bf(dot(ch["p_b"], ch["p_b"], nn))
    outs = [[None] * n_pairs, [None] * n_pairs]
    for ch in chains:
        a1k, a1b, inc = ch["a1k"], ch["a1b"], ch["incl"]
        inc2 = jnp.concatenate([inc, inc], 0)
        rk = jnp.where(inc2, jnp.concatenate([a1k[:c], a1k[2 * c:3 * c]], 0), 0.0)
        rb_ = jnp.where(inc2, jnp.concatenate([a1b[:c], a1b[2 * c:3 * c]], 0), 0.0)
        ub = bf(ch["u"])
        o = ch["a2"][:c] + sel(dot(bf(rk), ch["v"], nn)) - sel(dot(bf(rb_), ub, nn))
        outs[ch["idx"] // n_pairs][ch["idx"] % n_pairs] = o
        upd = dot(jnp.concatenate([ch["v"], -ub], 0), ch["kgbg"], TN_DIMS)
        s_scr[ch["idx"]] = ch["s"] * ch["dec"] + jnp.where(blk, upd, 0.0)
    of_ref[0] = jnp.concatenate(outs[0], axis=1)
    ob_ref[0] = jnp.concatenate(outs[1], axis=1)


def _chunk_orders(n_lat, n_ctx):
    n = n_lat + n_ctx
    fwd = lambda b, s: (b, (s + n_lat) % n, 0)
    rev = lambda b, s: (b, n - 1 - s, 0)
    return fwd, rev


def _rw_scan(r, v, kap, lwf, lwb, kf, kb, bf, bb, n_lat_rows):
    b, t, c = r.shape
    ch = SEQ_CHUNK
    n = t // ch
    fwd, rev = _chunk_orders(n_lat_rows // ch, (t - n_lat_rows) // ch)
    blk = lambda im: pl.BlockSpec((1, ch, c), im)
    return pl.pallas_call(
        _rw_scan_kernel,
        grid=(b, n),
        in_specs=[blk(fwd)] * 6 + [blk(rev)] * 6,
        out_specs=[blk(fwd), blk(rev)],
        out_shape=[jax.ShapeDtypeStruct((b, t, c), F32)] * 2,
        scratch_shapes=[pltpu.VMEM((RW_HEADS, 2 * RW_HEAD_DIM, 2 * RW_HEAD_DIM), F32)],
        compiler_params=_cp("arbitrary", "arbitrary"),
        name="rw_scan",
    )(r, v, kap, lwf, kf, bf, r, v, kap, lwb, kb, bb)


def _rw_readout(of, ob, bonus, g, ln_g, ln_b):
    o = of + ob
    bd = _head_blockdiag(MIX_W, RW_HEAD_DIM)
    mean = _x_exact01(o, bd) * (1.0 / RW_HEAD_DIM)
    oc = o - mean
    var = _x_exact01(oc * oc, bd) * (1.0 / RW_HEAD_DIM)
    on = oc * lax.rsqrt(var + RW_GN_EPS) * ln_g + ln_b
    return (on + bonus) * g


def _s5_matrices(a_re, a_im, log_dt, b_re, b_im, c_re, c_im):
    tc, g, n, p = S5_CHUNK, S5_GROUPS, S5_STATE, S5_GROUP
    hp = lax.Precision.HIGHEST
    dt = jnp.exp(log_dt)[:, :, None]
    lam_re, lam_im = a_re * dt, a_im * dt
    mag1 = jnp.exp(lam_re)
    ab_re, ab_im = mag1 * jnp.cos(lam_im), mag1 * jnp.sin(lam_im)
    nr, ni = ab_re - 1.0, ab_im
    den = a_re * a_re + a_im * a_im
    f_re, f_im = (nr * a_re + ni * a_im) / den, (ni * a_re - nr * a_im) / den
    tau = jnp.arange(tc + 1, dtype=F32)[:, None, None, None]
    pw_mag = jnp.exp(lam_re[None] * tau)
    pw_re, pw_im = pw_mag * jnp.cos(lam_im[None] * tau), pw_mag * jnp.sin(lam_im[None] * tau)
    fb_re = f_re[..., None] * b_re[None] - f_im[..., None] * b_im[None]
    fb_im = f_re[..., None] * b_im[None] + f_im[..., None] * b_re[None]
    e_re = pw_re[..., None] * fb_re[None] - pw_im[..., None] * fb_im[None]
    e_im = pw_re[..., None] * fb_im[None] + pw_im[..., None] * fb_re[None]
    m = (jnp.einsum('gpn,tdgnq->tdgpq', c_re, e_re[:tc], precision=hp)
         - jnp.einsum('gpn,tdgnq->tdgpq', c_im, e_im[:tc], precision=hp))
    i = jnp.arange(tc)
    lag = i[None, :] - i[:, None]
    mf = jnp.where((lag >= 0)[:, :, None, None, None], m[jnp.clip(lag, 0, tc - 1), 0], 0.0)
    mb = jnp.where((lag <= 0)[:, :, None, None, None], m[jnp.clip(-lag, 0, tc - 1), 1], 0.0)
    w = jnp.transpose(mf + mb, (2, 0, 4, 1, 3)).reshape(g, tc * p, tc * p)
    idx_f = tc - 1 - i
    pf_re, pf_im = e_re[idx_f, 0], e_im[idx_f, 0]
    pb_re, pb_im = e_re[i, 1], e_im[i, 1]
    pm = jnp.concatenate([pf_re, pb_re, pf_im, pb_im], axis=2)
    pm = jnp.transpose(pm, (1, 0, 3, 2)).reshape(g, tc * p, 4 * n)
    cf_re = c_re[None, :, :, :] * pw_re[i + 1, 0][:, :, None, :] - c_im[None] * pw_im[i + 1, 0][:, :, None, :]
    cf_im = c_re[None] * pw_im[i + 1, 0][:, :, None, :] + c_im[None] * pw_re[i + 1, 0][:, :, None, :]
    cb_re = c_re[None] * pw_re[tc - i, 1][:, :, None, :] - c_im[None] * pw_im[tc - i, 1][:, :, None, :]
    cb_im = c_re[None] * pw_im[tc - i, 1][:, :, None, :] + c_im[None] * pw_re[tc - i, 1][:, :, None, :]
    q = jnp.concatenate([cf_re, cb_re, -cf_im, -cb_im], axis=3)
    q = jnp.transpose(q, (1, 3, 0, 2)).reshape(g, 4 * n, tc * p)
    dec_re = jnp.concatenate([pw_re[tc, 0], pw_re[tc, 1]], axis=1)
    dec_im = jnp.concatenate([pw_im[tc, 0], pw_im[tc, 1]], axis=1)
    dec = jnp.stack([dec_re, dec_im], axis=1)
    return w.astype(BF16), pm.astype(BF16), q.astype(BF16), dec


def _s5_kernel(u_ref, w_ref, p_ref, q_ref, dec_ref, y_ref, inc_re, inc_im, sin_re, sin_im, *, n_lat, n_ctx, bsz):
    u = u_ref[0].astype(BF16)
    inc = jnp.dot(u, p_ref[0], preferred_element_type=F32)
    n2 = 2 * S5_STATE
    inc_re[...] = inc[:, :n2]
    inc_im[...] = inc[:, n2:]
    n = n_lat + n_ctx
    d_re = dec_ref[0, 0:1, :]
    d_im = dec_ref[0, 1:2, :]
    is_f = lax.broadcasted_iota(jnp.int32, (bsz, n2), 1) < S5_STATE

    def body(s, carry):
        s_re, s_im = carry
        cf = (s + n_lat) % n
        cb = n - 1 - s
        rf = pl.ds(pl.multiple_of(cf * bsz, bsz), bsz)
        rb = pl.ds(pl.multiple_of(cb * bsz, bsz), bsz)
        sin_re[rf, :] = jnp.where(is_f, s_re, sin_re[rf, :])
        sin_im[rf, :] = jnp.where(is_f, s_im, sin_im[rf, :])
        sin_re[rb, :] = jnp.where(is_f, sin_re[rb, :], s_re)
        sin_im[rb, :] = jnp.where(is_f, sin_im[rb, :], s_im)
        i_re = jnp.where(is_f, inc_re[rf, :], inc_re[rb, :])
        i_im = jnp.where(is_f, inc_im[rf, :], inc_im[rb, :])
        return (d_re * s_re - d_im * s_im + i_re, d_re * s_im + d_im * s_re + i_im)

    zero = jnp.zeros((bsz, n2), F32)
    sin_re[...] = jnp.zeros_like(sin_re)
    sin_im[...] = jnp.zeros_like(sin_im)
    lax.fori_loop(0, n, body, (zero, zero))
    s_all = jnp.concatenate([sin_re[...], sin_im[...]], axis=1).astype(BF16)
    y_ref[0] = jnp.dot(u, w_ref[0], preferred_element_type=F32) + jnp.dot(s_all, q_ref[0], preferred_element_type=F32)


def _s5_core(p_s5, n_lat_rows, mats):
    b, t, c = p_s5.shape
    tc, g, p = S5_CHUNK, S5_GROUPS, S5_GROUP
    nch = t // tc
    w, pm, q, dec = mats
    u = p_s5.reshape(b, nch, tc, g, p).transpose(3, 1, 0, 2, 4).reshape(g, nch * b, tc * p)
    rows = nch * b
    k = tc * p
    y = pl.pallas_call(
        functools.partial(_s5_kernel, n_lat=n_lat_rows // tc, n_ctx=(t - n_lat_rows) // tc, bsz=b),
        grid=(g,),
        in_specs=[pl.BlockSpec((1, rows, k), lambda i: (i, 0, 0)),
                  pl.BlockSpec((1, k, k), lambda i: (i, 0, 0)),
                  pl.BlockSpec((1, k, 4 * S5_STATE), lambda i: (i, 0, 0)),
                  pl.BlockSpec((1, 4 * S5_STATE, k), lambda i: (i, 0, 0)),
                  pl.BlockSpec((1, 2, 2 * S5_STATE), lambda i: (i, 0, 0))],
        out_specs=pl.BlockSpec((1, rows, k), lambda i: (i, 0, 0)),
        out_shape=jax.ShapeDtypeStruct((g, rows, k), F32),
        scratch_shapes=[pltpu.VMEM((rows, 2 * S5_STATE), F32)] * 4,
        compiler_params=_cp("arbitrary"),
        name="s5_core",
    )(u, w, pm, q, dec)
    return y.reshape(g, nch, b, tc, p).transpose(2, 1, 3, 0, 4).reshape(b, t, c)


def _s5_output(y, u, d_skip, glu_w, glu_b):
    z = y + d_skip * u
    z = 0.5 * z * (1.0 + jnp.tanh(math.sqrt(2.0 / math.pi) * (z + 0.044715 * (z * z * z))))
    return z * _sigmoid(_mm(z, glu_w) + glu_b)


def _gla_scan_kernel(pf_ref, pb_ref, gup_ref, gb_ref, of_ref, ob_ref, s_scr):
    @pl.when(pl.program_id(1) == 0)
    def _():
        s_scr[...] = jnp.zeros_like(s_scr)

    c = pf_ref.shape[1]
    pw_ = 2 * GLA_HK
    n_pairs = GLA_DK // pw_
    lane_a = lax.broadcasted_iota(jnp.int32, (c, pw_), 1) < GLA_HK
    lane_av = lax.broadcasted_iota(jnp.int32, (GLA_HV, pw_), 1) < GLA_HK
    dot = functools.partial(lax.dot_general, preferred_element_type=F32)
    nn = (((1,), (0,)), ((), ()))
    chains = []
    for d, p_ref in enumerate((pf_ref, pb_ref)):
        rev = d == 1
        incl = _tri(c, rev, False)
        gd = p_ref[0, :, 2 * GLA_DK + 2 * GLA_DV:]
        x = _mm_hp(gd, gup_ref[...])[:, d * GLA_DK:(d + 1) * GLA_DK] + gb_ref[0:1, d * GLA_DK:(d + 1) * GLA_DK]
        logg = (jnp.minimum(x, 0.0) - jnp.log(1.0 + jnp.exp(-jnp.abs(x)))) * (1.0 / GLA_GATE_NORM)
        cum = _mm_exact01(incl.astype(F32), logg)
        mid = c // 2
        ref = cum[mid:mid + 1] if rev else cum[mid - 1:mid]
        tot = cum[0:1] if rev else cum[c - 1:c]
        q = p_ref[0, :, 0:GLA_DK] * (GLA_HK ** -0.5)
        k = p_ref[0, :, GLA_DK:2 * GLA_DK]
        qs = q * jnp.exp(cum - ref)
        ks = k * jnp.exp(ref - cum)
        qc = q * jnp.exp(cum)
        kg = k * jnp.exp(tot - cum)
        dec = jnp.exp(tot)
        for p in range(n_pairs):
            sl = slice(p * pw_, (p + 1) * pw_)
            mask2 = lambda a: jnp.concatenate([jnp.where(lane_a, a, 0.0), jnp.where(lane_a, 0.0, a)], 0).astype(BF16)
            chains.append(dict(
                idx=d * n_pairs + p, incl=incl, qs4=mask2(qs[:, sl]), qc4=mask2(qc[:, sl]), ks=ks[:, sl].astype(BF16),
                kg=kg[:, sl].astype(BF16), dec=dec[:, sl],
                va=p_ref[0, :, 2 * GLA_DK + 2 * p * GLA_HV:2 * GLA_DK + (2 * p + 1) * GLA_HV].astype(BF16),
                vb=p_ref[0, :, 2 * GLA_DK + (2 * p + 1) * GLA_HV:2 * GLA_DK + (2 * p + 2) * GLA_HV].astype(BF16)))
    for ch in chains:
        ch["s"] = s_scr[ch["idx"]]
    for ch in chains:
        ch["sc"] = dot(ch["qs4"], ch["ks"], NT_DIMS)
        ch["st"] = dot(ch["qc4"], ch["s"].astype(BF16), NT_DIMS)
    outs = [[None] * GLA_HEADS, [None] * GLA_HEADS]
    for ch in chains:
        d, p = ch["idx"] // n_pairs, ch["idx"] % n_pairs
        sa = jnp.where(ch["incl"], ch["sc"][:c], 0.0).astype(BF16)
        sb = jnp.where(ch["incl"], ch["sc"][c:], 0.0).astype(BF16)
        outs[d][2 * p] = ch["st"][:c] + dot(sa, ch["va"], nn)
        outs[d][2 * p + 1] = ch["st"][c:] + dot(sb, ch["vb"], nn)
        upd = jnp.where(lane_av, dot(ch["va"], ch["kg"], TN_DIMS), dot(ch["vb"], ch["kg"], TN_DIMS))
        s_scr[ch["idx"]] = ch["s"] * ch["dec"] + upd
    of_ref[0] = jnp.concatenate(outs[0], axis=1)
    ob_ref[0] = jnp.concatenate(outs[1], axis=1)


def _gla_scan(p_gla, n_lat_rows, gate_up, gate_b):
    b, t, n = p_gla.shape
    ch = SEQ_CHUNK
    fwd, rev = _chunk_orders(n_lat_rows // ch, (t - n_lat_rows) // ch)
    gup = jnp.zeros((128, 2 * GLA_DK), F32)
    gup = gup.at[0:GLA_GATE_RANK, 0:GLA_DK].set(gate_up[0]).at[GLA_GATE_RANK:2 * GLA_GATE_RANK, GLA_DK:].set(gate_up[1])
    return pl.pallas_call(
        _gla_scan_kernel,
        grid=(b, t // ch),
        in_specs=[pl.BlockSpec((1, ch, n), fwd), pl.BlockSpec((1, ch, n), rev),
                  pl.BlockSpec((128, 2 * GLA_DK), lambda b, s: (0, 0)),
                  pl.BlockSpec((1, 2 * GLA_DK), lambda b, s: (0, 0))],
        out_specs=[pl.BlockSpec((1, ch, GLA_DV), fwd), pl.BlockSpec((1, ch, GLA_DV), rev)],
        out_shape=[jax.ShapeDtypeStruct((b, t, GLA_DV), F32)] * 2,
        scratch_shapes=[pltpu.VMEM((GLA_HEADS, GLA_HV, 2 * GLA_HK), F32)],
        compiler_params=_cp("arbitrary", "arbitrary"),
        name="gla_scan",
    )(p_gla, p_gla, gup, gate_b.reshape(1, 2 * GLA_DK))


def _gla_readout_kernel(of_ref, ob_ref, g_ref, ng_ref, y_ref):
    o = of_ref[0] + ob_ref[0]
    bd = _head_blockdiag(GLA_DV, GLA_HV)
    ms = _x_exact01(o * o, bd) * (1.0 / GLA_HV)
    y_ref[0] = o * lax.rsqrt(ms + RMS_EPS) * ng_ref[...] * _silu(g_ref[0])


def _gla_readout(of, ob, p_gla, norm_g):
    b, t, c = of.shape
    tile = pl.BlockSpec((1, TOKEN_TILE, c), lambda b, t: (b, t, 0))
    return pl.pallas_call(
        _gla_readout_kernel,
        grid=(b, t // TOKEN_TILE),
        in_specs=[tile, tile, pl.BlockSpec((1, TOKEN_TILE, c), lambda b, t: (b, t, (2 * GLA_DK + GLA_DV) // c)),
                  pl.BlockSpec((1, c), lambda b, t: (0, 0))],
        out_specs=tile,
        out_shape=jax.ShapeDtypeStruct((b, t, c), F32),
        compiler_params=_cp("arbitrary", "arbitrary"),
        name="gla_readout",
    )(of, ob, p_gla, jnp.tile(norm_g, GLA_HEADS).reshape(1, c))


def _to_col_major(x, n_lat_rows):
    b, t, c = x.shape
    lat = x[:, :n_lat_rows].reshape(b, n_lat_rows // GRID_W, GRID_W, c).swapaxes(1, 2).reshape(b, n_lat_rows, c)
    return jnp.concatenate([lat, x[:, n_lat_rows:]], axis=1)


def _to_raster(x, n_lat_rows):
    b, t, c = x.shape
    lat = x[:, :n_lat_rows].reshape(b, GRID_W, n_lat_rows // GRID_W, c).swapaxes(1, 2).reshape(b, n_lat_rows, c)
    return jnp.concatenate([lat, x[:, n_lat_rows:]], axis=1)


def _merge_kernel(h_ref, mod_ref, gate_ref, yhy_ref, rof_ref, rob_ref, rbonus_ref, rg_ref, ys5_ref, us5_ref, ygla_ref,
                  lng_ref, lnb_ref, dskip_ref, gluw_ref, glub_ref, wbr_ref, wout_ref, o_ref):
    d = D_MODEL
    y_rw = _rw_readout(rof_ref[0], rob_ref[0], rbonus_ref[0], rg_ref[0], lng_ref[...], lnb_ref[...])
    y_s5 = _s5_output(ys5_ref[0], us5_ref[0], dskip_ref[...], gluw_ref[...], glub_ref[...])
    acc = jnp.zeros((h_ref.shape[1], d), F32)
    for m, y in enumerate((yhy_ref[0], y_rw, y_s5, ygla_ref[0])):
        acc = acc + _sigmoid(gate_ref[0, :, m * d:(m + 1) * d]) * jnp.dot(y.astype(BF16), wbr_ref[m], preferred_element_type=F32)
    out = jnp.dot(acc.astype(BF16), wout_ref[...], preferred_element_type=F32)
    o_ref[0] = h_ref[0] + mod_ref[0, 0, 2:3, :] * out


def _merge(h_all, modsel, n_tiles, n_lat_tiles, gate, y_hy, rof, rob, rbonus, rg, y_s5, u_s5, y_gla,
           ln_g, ln_b, d_skip, glu_w, glu_b, w_branch, w_out):
    b, t, d = h_all.shape
    c = MIX_W
    tile = lambda w: pl.BlockSpec((1, TOKEN_TILE, w), lambda b, t: (b, t, 0))
    vec = lambda x: x.reshape(1, -1)
    consts = [vec(ln_g), vec(ln_b), vec(d_skip), glu_w.astype(BF16), vec(glu_b), w_branch.astype(BF16), w_out.astype(BF16)]
    cspecs = [pl.BlockSpec(x.shape, (lambda n: lambda b, t: (0,) * n)(x.ndim)) for x in consts]
    return pl.pallas_call(
        _merge_kernel,
        grid=(b, n_tiles),
        in_specs=[tile(d), pl.BlockSpec((1, 1, 6, d), _mod_index(n_lat_tiles)), tile(4 * d)] + [tile(c)] * 8 + cspecs,
        out_specs=tile(d),
        out_shape=jax.ShapeDtypeStruct((b, t, d), F32),
        input_output_aliases={0: 0},
        compiler_params=_cp("arbitrary", "arbitrary"),
        name="merge",
    )(h_all, modsel, gate, y_hy, rof, rob, rbonus, rg, y_s5, u_s5, y_gla, *consts)


def _ffn_kernel(h_ref, g_ref, mod_ref, w1_ref, w3_ref, w2_ref, o_ref):
    h = h_ref[0]
    xm = _norm_mod(h, g_ref[...], mod_ref[0, 0], 3).astype(BF16)
    a = jnp.dot(xm, w1_ref[...], preferred_element_type=F32)
    b = jnp.dot(xm, w3_ref[...], preferred_element_type=F32)
    y = jnp.dot((_silu(a) * b).astype(BF16), w2_ref[...], preferred_element_type=F32)
    o_ref[0] = h + mod_ref[0, 0, 5:6, :] * y


def _ffn_dense(h_all, g, modsel, n_lat_tiles, w1, w3, w2):
    b, t, d = h_all.shape
    tile = pl.BlockSpec((1, TOKEN_TILE, d), lambda b, t: (b, t, 0))
    ws = [w1.astype(BF16), w3.astype(BF16), w2.astype(BF16)]
    wspec = [pl.BlockSpec(w.shape, lambda b, t: (0, 0), pipeline_mode=pl.Buffered(1)) for w in ws]
    return pl.pallas_call(
        _ffn_kernel,
        grid=(b, t // TOKEN_TILE),
        in_specs=[tile, pl.BlockSpec((1, d), lambda b, t: (0, 0)), pl.BlockSpec((1, 1, 6, d), _mod_index(n_lat_tiles))] + wspec,
        out_specs=tile,
        out_shape=jax.ShapeDtypeStruct((b, t, d), F32),
        input_output_aliases={0: 0},
        compiler_params=_cp("arbitrary", "arbitrary"),
        name="ffn_dense",
    )(h_all, g.reshape(1, d), modsel, *ws)


def _router_kernel(h_ref, g_ref, mod_ref, rw_ref, rb_ref, x_ref, logit_ref):
    xm = _norm_mod(h_ref[0], g_ref[...], mod_ref[0, 0], 3)
    x_ref[0] = xm.astype(BF16)
    logit_ref[0] = _mm_hp(xm, rw_ref[...]) + rb_ref[...]


def _router(h_all, g, modsel, n_lat_tiles, router_w, router_b):
    b, t, d = h_all.shape
    n_lat = n_lat_tiles * TOKEN_TILE
    rw = jnp.pad(router_w, ((0, 0), (0, 128 - N_EXPERTS)))
    rb = jnp.pad(router_b, (0, 128 - N_EXPERTS), constant_values=-1e30).reshape(1, 128)
    tile = lambda w: pl.BlockSpec((1, TOKEN_TILE, w), lambda b, t: (b, t, 0))
    return pl.pallas_call(
        _router_kernel,
        grid=(b, n_lat_tiles),
        in_specs=[tile(d), pl.BlockSpec((1, d), lambda b, t: (0, 0)), pl.BlockSpec((1, 1, 6, d), _mod_index(n_lat_tiles)),
                  pl.BlockSpec((d, 128), lambda b, t: (0, 0)), pl.BlockSpec((1, 128), lambda b, t: (0, 0))],
        out_specs=[tile(d), tile(128)],
        out_shape=[jax.ShapeDtypeStruct((b, n_lat, d), BF16), jax.ShapeDtypeStruct((b, n_lat, 128), F32)],
        compiler_params=_cp("arbitrary", "arbitrary"),
        name="moe_router",
    )(h_all, g.reshape(1, d), modsel, rw, rb)


def _moe_expert_kernel(be_ref, x_ref, w1_ref, w3_ref, w2_ref, o_ref):
    x = x_ref[...]
    a = jnp.dot(x, w1_ref[0], preferred_element_type=F32)
    b = jnp.dot(x, w3_ref[0], preferred_element_type=F32)
    o_ref[...] = jnp.dot((_silu(a) * b).astype(BF16), w2_ref[0], preferred_element_type=F32)


def _moe_experts(xb, block_exp, w1, w3, w2):
    rows, d = xb.shape
    nb = rows // MOE_BLOCK_ROWS
    f = w1.shape[2]
    grid_spec = pltpu.PrefetchScalarGridSpec(
        num_scalar_prefetch=1,
        grid=(nb,),
        in_specs=[pl.BlockSpec((MOE_BLOCK_ROWS, d), lambda i, be: (i, 0)),
                  pl.BlockSpec((1, d, f), lambda i, be: (be[i], 0, 0), pipeline_mode=pl.Buffered(1)),
                  pl.BlockSpec((1, d, f), lambda i, be: (be[i], 0, 0), pipeline_mode=pl.Buffered(1)),
                  pl.BlockSpec((1, f, d), lambda i, be: (be[i], 0, 0), pipeline_mode=pl.Buffered(1))],
        out_specs=pl.BlockSpec((MOE_BLOCK_ROWS, d), lambda i, be: (i, 0)),
    )
    return pl.pallas_call(
        _moe_expert_kernel,
        grid_spec=grid_spec,
        out_shape=jax.ShapeDtypeStruct((rows, d), F32),
        compiler_params=_cp("arbitrary"),
        name="moe_experts",
    )(block_exp, xb, w1, w3, w2)


def _moe_combine_kernel(h_ref, mod_ref, y_ref, gate_ref, fg_ref, o_ref):
    d = h_ref.shape[2]
    y = y_ref[0, :, :d] * gate_ref[0, :, 0:1] + y_ref[0, :, d:] * gate_ref[0, :, 1:2]
    h = h_ref[0] + mod_ref[0, 0, 5:6, :] * y
    o_ref[0] = h * lax.rsqrt(jnp.mean(h * h, axis=-1, keepdims=True) + RMS_EPS) * fg_ref[...]


def _moe_combine(h_all, modsel, n_lat_tiles, y2, gates, final_g):
    b, t, d = h_all.shape
    n_lat = n_lat_tiles * TOKEN_TILE
    tile = lambda w: pl.BlockSpec((1, TOKEN_TILE, w), lambda b, t: (b, t, 0))
    return pl.pallas_call(
        _moe_combine_kernel,
        grid=(b, n_lat_tiles),
        in_specs=[tile(d), pl.BlockSpec((1, 1, 6, d), _mod_index(n_lat_tiles)),
                  tile(TOP_K * d),
                  tile(128), pl.BlockSpec((1, d), lambda b, t: (0, 0))],
        out_specs=tile(d),
        out_shape=jax.ShapeDtypeStruct((b, n_lat, d), F32),
        compiler_params=_cp("arbitrary", "arbitrary"),
        name="moe_combine",
    )(h_all, modsel, y2, gates, final_g.reshape(1, d))


def _moe_ffn_and_final_norm(h_all, g, modsel, n_lat_tiles, router_w, router_b, w1, w3, w2, final_g):
    b, t, d = h_all.shape
    n_lat = n_lat_tiles * TOKEN_TILE
    n_tok = b * n_lat
    xn, logits = _router(h_all, g, modsel, n_lat_tiles, router_w, router_b)
    logits = logits.reshape(n_tok, 128)[:, :N_EXPERTS]
    top_v, top_i = lax.top_k(logits, TOP_K)
    gates = jax.nn.softmax(top_v, axis=-1)
    n_assign = n_tok * TOP_K
    e_flat = top_i.reshape(-1)
    order = jnp.argsort(e_flat)
    e_sorted = e_flat[order]
    counts = jnp.bincount(e_flat, length=N_EXPERTS)
    padded = (counts + MOE_BLOCK_ROWS - 1) // MOE_BLOCK_ROWS * MOE_BLOCK_ROWS
    start = jnp.cumsum(counts) - counts
    pend = jnp.cumsum(padded)
    pstart = pend - padded
    dest = (pstart[e_sorted] + jnp.arange(n_assign) - start[e_sorted]).astype(jnp.int32)
    n_blocks = n_assign // MOE_BLOCK_ROWS + N_EXPERTS
    n_rows = n_blocks * MOE_BLOCK_ROWS
    row_tok = jnp.full((n_rows,), n_tok, jnp.int32).at[dest].set((order // TOP_K).astype(jnp.int32))
    block_exp = jnp.minimum(jnp.searchsorted(pend, jnp.arange(n_blocks) * MOE_BLOCK_ROWS, side='right'),
                            N_EXPERTS - 1).astype(jnp.int32)
    tok_pad = jnp.concatenate([xn.reshape(n_tok, d), jnp.zeros((1, d), BF16)], axis=0)
    xb = jnp.take(tok_pad, row_tok, axis=0)
    yb = _moe_experts(xb, block_exp, w1.astype(BF16), w3.astype(BF16), w2.astype(BF16))
    dest_orig = jnp.zeros((n_assign,), jnp.int32).at[order].set(dest)
    y2 = jnp.take(yb, dest_orig, axis=0).reshape(b, n_lat, TOP_K * d)
    gates_pad = jnp.pad(gates, ((0, 0), (0, 128 - TOP_K))).reshape(b, n_lat, 128)
    return _moe_combine(h_all, modsel, n_lat_tiles, y2, gates_pad, final_g)


def _pad_cols(w, n):
    return jnp.pad(w, ((0, 0), (0, n - w.shape[1])))


def kernel(x, c, ctx, c_ctx, ada_w, ada_b, norm_mix_g, norm_ffn_g, w_in, hy_conv_w, hy_conv_b, hy_f_w1, hy_f_b1, hy_f_w2, hy_f_b2, hy_f_w3, hy_f_b3, hy_bias, rw_mu, rw_w0, rw_w_up, rw_a0, rw_a_up, rw_g_up, rw_k_k, rw_k_a, rw_r_k, rw_ln_g, rw_ln_b, s5_a_re, s5_a_im, s5_log_dt, s5_b_re, s5_b_im, s5_c_re, s5_c_im, s5_d, s5_glu_w, s5_glu_b, gla_gate_up, gla_gate_b, gla_norm_g, w_branch, w_out, ffn_w1, ffn_w3, ffn_w2, moe_router_w, moe_router_b, moe_w1, moe_w3, moe_w2, final_norm_g):
    bsz, n_lat, d = x.shape
    n_ctx = ctx.shape[1]
    depth = w_in.shape[0]
    assert depth == 2 and d == D_MODEL
    assert n_lat % TOKEN_TILE == 0 and n_ctx % TOKEN_TILE == 0 and n_lat % GRID_W == 0
    t_all = n_lat + n_ctx
    n_lat_tiles = n_lat // TOKEN_TILE
    n_tiles = t_all // TOKEN_TILE
    assert n_tiles <= 2 * n_lat_tiles

    rows = (bsz + 1 + 7) // 8 * 8
    c_all = jnp.zeros((rows, d), F32).at[:bsz].set(c.astype(F32)).at[bsz].set(c_ctx.astype(F32))
    mod = _ada_mod(c_all, ada_w, ada_b)
    h_all = jnp.concatenate([x.astype(F32), ctx.astype(F32)], axis=1)

    dft_lat = _dft_tables(n_lat)
    dft_ctx = _dft_tables(n_ctx)
    seg = np.cumsum([0, 3 * MIX_W, RW_IN, MIX_W, 2 * GLA_DK + 2 * GLA_DV + 2 * GLA_GATE_RANK, 4 * d])

    out = None
    for l in range(depth):
        last = l == depth - 1
        mod_l = mod[l, :bsz].reshape(bsz, 6, d)
        mod_c = jnp.broadcast_to(mod[l, bsz].reshape(1, 6, d), (bsz, 6, d))
        modsel = jnp.stack([mod_l, mod_c], axis=1)

        wl = w_in[l].astype(BF16)
        ws = [wl[:, seg[0]:seg[1]], wl[:, seg[1]:seg[2]], wl[:, seg[2]:seg[3]],
              _pad_cols(wl[:, seg[3]:seg[4]], GLA_IN_PAD), wl[:, seg[4]:seg[5]]]
        p_hy, p_rw, p_s5, p_gla, gate = _in_proj(h_all, norm_mix_g[l], modsel, ws, n_lat_tiles)

        filt_args = (hy_f_w1[l], hy_f_b1[l], hy_f_w2[l], hy_f_b2[l], hy_f_w3[l], hy_f_b3[l])
        spec_lat = _hy_spectrum(_hy_filter(n_lat, *filt_args), dft_lat[0], dft_lat[1])
        y_hy = _hyena(p_hy, None, 0, n_lat, hy_conv_w[l], hy_conv_b[l], hy_bias[l], dft_lat, spec_lat)
        if not last:
            spec_ctx = _hy_spectrum(_hy_filter(n_ctx, *filt_args), dft_ctx[0], dft_ctx[1])
            y_hy = _hyena(p_hy, y_hy, n_lat // n_ctx, n_ctx, hy_conv_w[l], hy_conv_b[l], hy_bias[l], dft_ctx, spec_ctx)

        r, v, kap, lwf, lwb, kf, kb, bf, bb, rg, rbonus = _rw_prep(
            p_rw, n_lat_tiles, rw_mu[l], rw_w0[l], rw_w_up[l], rw_a0[l], rw_a_up[l], rw_g_up[l], rw_k_k[l], rw_k_a[l], rw_r_k[l])
        rof, rob = _rw_scan(r, v, kap, lwf, lwb, kf, kb, bf, bb, n_lat)

        y_s5 = _s5_core(p_s5, n_lat, _s5_matrices(s5_a_re[l], s5_a_im[l], s5_log_dt[l], s5_b_re[l], s5_b_im[l], s5_c_re[l], s5_c_im[l]))

        p_gla_cm = _to_col_major(p_gla, n_lat)
        gof, gob = _gla_scan(p_gla_cm, n_lat, gla_gate_up[l], gla_gate_b[l])
        y_gla = _to_raster(_gla_readout(gof, gob, p_gla_cm, gla_norm_g[l]), n_lat)

        nt_l = n_lat_tiles if last else n_tiles
        h_all = _merge(h_all, modsel, nt_l, n_lat_tiles, gate, y_hy, rof, rob, rbonus, rg, y_s5, p_s5, y_gla,
                       rw_ln_g[l], rw_ln_b[l], s5_d[l], s5_glu_w[l], s5_glu_b[l], w_branch[l], w_out[l])

        i = l // 2
        if l % 2 == 0:
            h_all = _ffn_dense(h_all, norm_ffn_g[l], modsel, n_lat_tiles, ffn_w1[i], ffn_w3[i], ffn_w2[i])
        else:
            out = _moe_ffn_and_final_norm(h_all, norm_ffn_g[l], modsel, n_lat_tiles, moe_router_w[i], moe_router_b[i],
                                          moe_w1[i], moe_w3[i], moe_w2[i], final_norm_g)
    return out.astype(x.dtype)
```

```python
import functools
import math

import numpy as np
import jax
import jax.numpy as jnp
from jax import lax
from jax.experimental import pallas as pl
from jax.experimental.pallas import tpu as pltpu

F32 = jnp.float32
BF16 = jnp.bfloat16

D_MODEL = 1024
GRID_W = 64
MIX_W = 512
RMS_EPS = 1e-6
HY_BANDS = 16
HY_DECAY_TARGET = 1e-2
HY_SHORT_PCT = 0.3
HY_LONG_PCT = 1.5
RW_HEADS = 8
RW_HEAD_DIM = 64
RW_DECAY_SCALE = 0.606531
RW_GN_EPS = 64e-5
RW_IN = 1920
S5_GROUP = 16
S5_GROUPS = 32
S5_STATE = 64
S5_CHUNK = 16
GLA_HEADS = 4
GLA_HK = 64
GLA_HV = 128
GLA_DK = 256
GLA_DV = 512
GLA_GATE_RANK = 16
GLA_GATE_NORM = 16.0
GLA_IN_PAD = 1664
N_EXPERTS = 8
TOP_K = 2
MOE_BLOCK_ROWS = 256
TOKEN_TILE = 256
SEQ_CHUNK = 64
SCAN_BATCH = 2
VMEM_LIMIT = 56 * 1024 * 1024

NT_DIMS = (((1,), (1,)), ((), ()))
TN_DIMS = (((0,), (0,)), ((), ()))


def _cp(*sem):
    return pltpu.CompilerParams(dimension_semantics=sem, vmem_limit_bytes=VMEM_LIMIT)


def _mm(a, b, dims=None):
    a = a.astype(BF16)
    b = b.astype(BF16)
    if dims is None:
        return jnp.dot(a, b, preferred_element_type=F32)
    return lax.dot_general(a, b, dims, preferred_element_type=F32)


def _split_bf16(x):
    hi = x.astype(BF16)
    lo = (x - hi.astype(F32)).astype(BF16)
    return hi, lo


def _mm_hp(a, b):
    ah, al = _split_bf16(a)
    bh, bl = _split_bf16(b)
    d = functools.partial(jnp.dot, preferred_element_type=F32)
    return d(ah, bh) + (d(ah, bl) + d(al, bh))


def _mm_exact01(m01, x):
    m = m01.astype(BF16)
    hi = x.astype(BF16)
    r1 = x - hi.astype(F32)
    mid = r1.astype(BF16)
    lo = (r1 - mid.astype(F32)).astype(BF16)
    d = functools.partial(jnp.dot, preferred_element_type=F32)
    return d(m, hi) + (d(m, mid) + d(m, lo))


def _x_exact01(x, m01):
    m = m01.astype(BF16)
    hi = x.astype(BF16)
    mid = (x - hi.astype(F32)).astype(BF16)
    d = functools.partial(jnp.dot, preferred_element_type=F32)
    return d(hi, m) + d(mid, m)


def _sigmoid(x):
    return 1.0 / (1.0 + jnp.exp(-x))


def _silu(x):
    return x * _sigmoid(x)


def _tri(c, rev, strict):
    i = lax.broadcasted_iota(jnp.int32, (c, c), 0)
    j = lax.broadcasted_iota(jnp.int32, (c, c), 1)
    if rev:
        return (j > i) if strict else (j >= i)
    return (j < i) if strict else (j <= i)


def _head_blockdiag(n, head):
    i = lax.broadcasted_iota(jnp.int32, (n, n), 0) // head
    j = lax.broadcasted_iota(jnp.int32, (n, n), 1) // head
    return (i == j).astype(F32)


def _shift_rows(x, prev_row, next_row):
    n = x.shape[0]
    row = lax.broadcasted_iota(jnp.int32, x.shape, 0)
    xm = jnp.where(row == 0, prev_row, pltpu.roll(x, 1, 0))
    xp = jnp.where(row == n - 1, next_row, pltpu.roll(x, n - 1, 0))
    return xm, xp


def _ada_kernel(c_ref, w_ref, b_ref, o_ref):
    o_ref[0] = _mm_hp(_silu(c_ref[...]), w_ref[0]) + b_ref[0]


def _ada_mod(c_all, ada_w, ada_b):
    depth, d, n = ada_w.shape
    rows = c_all.shape[0]
    return pl.pallas_call(
        _ada_kernel,
        grid=(depth, n // d),
        in_specs=[pl.BlockSpec((rows, d), lambda l, j: (0, 0)),
                  pl.BlockSpec((1, d, d), lambda l, j: (l, 0, j)),
                  pl.BlockSpec((1, 1, d), lambda l, j: (l, 0, j))],
        out_specs=pl.BlockSpec((1, rows, d), lambda l, j: (l, 0, j)),
        out_shape=jax.ShapeDtypeStruct((depth, rows, n), F32),
        compiler_params=_cp("arbitrary", "arbitrary"),
        name="ada_mod",
    )(c_all, ada_w, ada_b.reshape(depth, 1, n))


def _norm_mod(x, g, mod, sh_row):
    xn = x * lax.rsqrt(jnp.mean(x * x, axis=-1, keepdims=True) + RMS_EPS) * g
    return xn * (1.0 + mod[sh_row + 1:sh_row + 2]) + mod[sh_row:sh_row + 1]


def _in_proj_kernel(x_ref, g_ref, mod_ref, w_hy, w_rw, w_s5, w_gate, o_hy, o_rw, o_s5, o_gate):
    xm = _norm_mod(x_ref[0], g_ref[...], mod_ref[0, 0], 0).astype(BF16)
    for w, o in ((w_hy, o_hy), (w_rw, o_rw), (w_s5, o_s5), (w_gate, o_gate)):
        o[0] = jnp.dot(xm, w[...], preferred_element_type=F32).astype(o.dtype)


def _mod_index(n_lat_tiles):
    return lambda b, t: (b, t // n_lat_tiles, 0, 0)


def _in_proj(h_all, g, modsel, ws, n_lat_tiles):
    b, t, d = h_all.shape
    nt = t // TOKEN_TILE
    wspec = [pl.BlockSpec(w.shape, lambda b, t: (0, 0), pipeline_mode=pl.Buffered(1)) for w in ws]
    dtypes = [F32, F32, F32, BF16]
    return pl.pallas_call(
        _in_proj_kernel,
        grid=(b, nt),
        in_specs=[pl.BlockSpec((1, TOKEN_TILE, d), lambda b, t: (b, t, 0)),
                  pl.BlockSpec((1, d), lambda b, t: (0, 0)),
                  pl.BlockSpec((1, 1, 6, d), _mod_index(n_lat_tiles))] + wspec,
        out_specs=[pl.BlockSpec((1, TOKEN_TILE, w.shape[1]), lambda b, t: (b, t, 0)) for w in ws],
        out_shape=[jax.ShapeDtypeStruct((b, t, w.shape[1]), dt) for w, dt in zip(ws, dtypes)],
        compiler_params=_cp("arbitrary", "arbitrary"),
        name="in_proj",
    )(h_all, g.reshape(1, d), modsel, *ws)


def _grid_cols_per_tile(n_lat_rows):
    return TOKEN_TILE // (n_lat_rows // GRID_W)


def _in_proj_gla_kernel(hv_ref, hn_ref, g_ref, mod_ref, w_ref, o_ref, x_scr, *, n_lat_tiles, cols):
    t = pl.program_id(1)
    d = hn_ref.shape[2]

    @pl.when(t < n_lat_tiles)
    def _():
        x_scr[...] = jnp.concatenate([hv_ref[0, :, j * d:(j + 1) * d] for j in range(cols)], axis=0)

    @pl.when(t >= n_lat_tiles)
    def _():
        x_scr[...] = hn_ref[0]

    xm = _norm_mod(x_scr[...], g_ref[...], mod_ref[0, 0], 0).astype(BF16)
    o_ref[0] = jnp.dot(xm, w_ref[...], preferred_element_type=F32)


def _in_proj_gla(h_all, g, modsel, w, n_lat_rows):
    b, t, d = h_all.shape
    nt = t // TOKEN_TILE
    n_lat_tiles = n_lat_rows // TOKEN_TILE
    grid_rows = n_lat_rows // GRID_W
    cols = _grid_cols_per_tile(n_lat_rows)
    assert grid_rows % 8 == 0 and GRID_W % cols == 0 and (t - n_lat_rows) == TOKEN_TILE
    h_view = h_all.reshape(b, t // GRID_W, GRID_W * d)
    return pl.pallas_call(
        functools.partial(_in_proj_gla_kernel, n_lat_tiles=n_lat_tiles, cols=cols),
        grid=(b, nt),
        in_specs=[pl.BlockSpec((1, grid_rows, cols * d), lambda b, t: (b, 0, jnp.minimum(t, n_lat_tiles - 1))),
                  pl.BlockSpec((1, TOKEN_TILE, d), lambda b, t: (b, n_lat_tiles, 0)),
                  pl.BlockSpec((1, d), lambda b, t: (0, 0)),
                  pl.BlockSpec((1, 1, 6, d), _mod_index(n_lat_tiles)),
                  pl.BlockSpec(w.shape, lambda b, t: (0, 0))],
        out_specs=pl.BlockSpec((1, TOKEN_TILE, w.shape[1]), lambda b, t: (b, t, 0)),
        out_shape=jax.ShapeDtypeStruct((b, t, w.shape[1]), F32),
        scratch_shapes=[pltpu.VMEM((TOKEN_TILE, d), F32)],
        compiler_params=_cp("arbitrary", "arbitrary"),
        name="in_proj_gla",
    )(h_view, h_all, g.reshape(1, d), modsel, w)


def _hy_filter_kernel(z_ref, w1, b1, w2, b2, w3, b3, dec_ref, o_ref):
    h = jnp.sin(_mm_hp(z_ref[...], w1[...]) + b1[...])
    h = jnp.sin(_mm_hp(h, w2[...]) + b2[...])
    o_ref[...] = (_mm_hp(h, w3[...]) + b3[...]) * dec_ref[...]


def _hy_features(length):
    pos = jnp.arange(length, dtype=F32)
    t = pos / max(length - 1, 1)
    freqs = jnp.linspace(1e-4, HY_BANDS - 1, HY_BANDS, dtype=F32)
    ang = (2.0 * math.pi / length) * pos[:, None] * freqs[None, :]
    z = jnp.concatenate([t[:, None], jnp.cos(ang), -jnp.sin(ang)], axis=-1)
    rates = jnp.abs(jnp.linspace(math.log(HY_DECAY_TARGET) / HY_LONG_PCT,
                                 math.log(HY_DECAY_TARGET) / HY_SHORT_PCT, MIX_W, dtype=F32))
    rates = jnp.concatenate([rates, rates])
    decay = jnp.exp(-t[:, None] * rates[None, :])
    return jnp.pad(z, ((0, 0), (0, 128 - z.shape[1]))), decay


def _hy_filter(length, fw1, fb1, fw2, fb2, fw3, fb3):
    z, decay = _hy_features(length)
    w1 = jnp.pad(fw1, ((0, 128 - fw1.shape[0]), (0, 128 - fw1.shape[1])))
    b1 = jnp.pad(fb1, (0, 128 - fb1.shape[0])).reshape(1, 128)
    w2 = jnp.pad(fw2, ((0, 128 - fw2.shape[0]), (0, 128 - fw2.shape[1])))
    b2 = jnp.pad(fb2, (0, 128 - fb2.shape[0])).reshape(1, 128)
    w3 = jnp.pad(fw3, ((0, 128 - fw3.shape[0]), (0, 0)))
    n = fw3.shape[1]
    rows = min(length, 512)
    full = lambda s: pl.BlockSpec(s, lambda i: (0, 0))
    return pl.pallas_call(
        _hy_filter_kernel,
        grid=(length // rows,),
        in_specs=[pl.BlockSpec((rows, 128), lambda i: (i, 0)), full((128, 128)), full((1, 128)), full((128, 128)),
                  full((1, 128)), full((128, n)), full((1, n)), pl.BlockSpec((rows, n), lambda i: (i, 0))],
        out_specs=pl.BlockSpec((rows, n), lambda i: (i, 0)),
        out_shape=jax.ShapeDtypeStruct((length, n), F32),
        compiler_params=_cp("arbitrary"),
        name="hy_filter",
    )(z, w1, b1, w2, b2, w3, fb3.reshape(1, n), decay)


def _dft_tables(length):
    n = 2 * length
    k = jnp.arange(length, dtype=jnp.int32)
    m = (k[:, None] * k[None, :]) % n
    ang = m.astype(F32) * (2.0 * math.pi / n)
    wc = jnp.cos(ang)
    ws = jnp.sin(ang)
    nyq = jnp.where(k % 2 == 0, 1.0, -1.0).astype(F32)
    ws = ws.at[0].set(nyq)
    return wc.astype(BF16), ws.astype(BF16), wc.T.astype(BF16), ws.T.astype(BF16)


def _dft_fwd_kernel(wc_ref, ws_ref, xh_ref, xl_ref, oc_ref, os_ref):
    d = functools.partial(jnp.dot, preferred_element_type=F32)
    oc_ref[...] = d(wc_ref[...], xh_ref[...]) + d(wc_ref[...], xl_ref[...])
    os_ref[...] = d(ws_ref[...], xh_ref[...]) + d(ws_ref[...], xl_ref[...])


def _hy_spectrum(filt, wc, ws):
    length, c2 = filt.shape
    c = c2 // 2
    n = 2 * length
    hb = filt[:, c:].at[0].set(0.0)
    x = jnp.concatenate([filt[:, :c], hb], axis=1)
    xh, xl = _split_bf16(x)
    fr = min(length, 512)
    fc, fs = pl.pallas_call(
        _dft_fwd_kernel,
        grid=(length // fr,),
        in_specs=[pl.BlockSpec((fr, length), lambda i: (i, 0)), pl.BlockSpec((fr, length), lambda i: (i, 0)),
                  pl.BlockSpec((length, c2), lambda i: (0, 0)), pl.BlockSpec((length, c2), lambda i: (0, 0))],
        out_specs=[pl.BlockSpec((fr, c2), lambda i: (i, 0)), pl.BlockSpec((fr, c2), lambda i: (i, 0))],
        out_shape=[jax.ShapeDtypeStruct((length, c2), F32)] * 2,
        compiler_params=_cp("arbitrary"),
        name="hy_spectrum",
    )(wc, ws, xh, xl)
    k_re = fc[:, :c] + fc[:, c:]
    k_im = fs[:, c:] - fs[:, :c]
    k_nyq = fs[0, :c] + fs[0, c:]
    ka = (k_re * (2.0 / n)).at[0].set(k_re[0] / n)
    kb = (k_im * (2.0 / n)).at[0].set(0.0)
    ka2 = ka.at[0].set(k_nyq / n)
    return ka, ka2, kb


def _hyena_kernel(p_ref, cw_ref, cb_ref, bias_ref, wc_ref, ws_ref, wct_ref, wst_ref, ka_ref, ka2_ref, kb_ref,
                  o_ref, z_scr, acc_scr, x0_scr):
    f = pl.program_id(1)
    c = MIX_W

    def conv3(lo):
        p = p_ref[0, :, lo:lo + c]
        zero = jnp.zeros((1, c), F32)
        pm, pp = _shift_rows(p, zero, zero)
        return pm * cw_ref[0:1, lo:lo + c] + p * cw_ref[1:2, lo:lo + c] + pp * cw_ref[2:3, lo:lo + c] + cb_ref[0:1, lo:lo + c]

    @pl.when(f == 0)
    def _():
        zf = conv3(2 * c) * conv3(c)
        z_scr[...] = zf.astype(BF16)
        acc_scr[...] = zf * bias_ref[...]
        x0_scr[...] = conv3(0)

    z = z_scr[...]
    zc = jnp.dot(wc_ref[...], z, preferred_element_type=F32)
    zs = jnp.dot(ws_ref[...], z, preferred_element_type=F32)
    a = zc * ka_ref[...] + zs * kb_ref[...]
    bv = zs * ka2_ref[...] - zc * kb_ref[...]
    acc_scr[...] += (jnp.dot(wct_ref[...], a.astype(BF16), preferred_element_type=F32)
                     + jnp.dot(wst_ref[...], bv.astype(BF16), preferred_element_type=F32))

    @pl.when(f == pl.num_programs(1) - 1)
    def _():
        o_ref[0] = (x0_scr[...] * acc_scr[...]).astype(o_ref.dtype)


def _hyena(p_hy, row_block, length, conv_w, conv_b, bias, tables, spec):
    b, t, c3 = p_hy.shape
    wc, ws, wct, wst = tables
    ka, ka2, kb = spec
    fr = min(length, 256)
    nf = length // fr
    c = MIX_W
    args = [p_hy, conv_w, conv_b.reshape(1, c3), bias.reshape(1, c), wc, ws, wct, wst, ka, ka2, kb]
    in_specs = [pl.BlockSpec((1, length, c3), lambda b, f: (b, row_block, 0), pipeline_mode=pl.Buffered(1)),
                pl.BlockSpec((3, c3), lambda b, f: (0, 0)),
                pl.BlockSpec((1, c3), lambda b, f: (0, 0)),
                pl.BlockSpec((1, c), lambda b, f: (0, 0)),
                pl.BlockSpec((fr, length), lambda b, f: (f, 0)),
                pl.BlockSpec((fr, length), lambda b, f: (f, 0)),
                pl.BlockSpec((length, fr), lambda b, f: (0, f)),
                pl.BlockSpec((length, fr), lambda b, f: (0, f)),
                pl.BlockSpec((fr, c), lambda b, f: (f, 0)),
                pl.BlockSpec((fr, c), lambda b, f: (f, 0)),
                pl.BlockSpec((fr, c), lambda b, f: (f, 0))]
    return pl.pallas_call(
        _hyena_kernel,
        grid=(b, nf),
        in_specs=in_specs,
        out_specs=pl.BlockSpec((1, length, c), lambda b, f: (b, 0, 0)),
        out_shape=jax.ShapeDtypeStruct((b, length, c), BF16),
        scratch_shapes=[pltpu.VMEM((length, c), BF16), pltpu.VMEM((length, c), F32), pltpu.VMEM((length, c), F32)],
        compiler_params=_cp("arbitrary", "arbitrary"),
        name="hyena_%d" % length,
    )(*args)


def _rw_prep_kernel(p_ref, pprev_ref, pnext_ref, mu_ref, w0_ref, wup_ref, a0_ref, aup_ref, gup_ref, kk_ref, ka_ref, rk_ref,
                    r_o, v_o, kap_o, lwf_o, lwb_o, kf_o, kb_o, bf_o, bb_o, g_o, bonus_o, *, n_lat_tiles):
    t = pl.program_id(1)
    nt = pl.num_programs(1)
    p = p_ref[0]
    first = jnp.logical_or(t == 0, t == n_lat_tiles)
    last = jnp.logical_or(t == n_lat_tiles - 1, t == nt - 1)
    prev_row = jnp.where(first, 0.0, pprev_ref[0, 7:8, :])
    next_row = jnp.where(last, 0.0, pnext_ref[0, 0:1, :])
    pm, pp = _shift_rows(p, prev_row, next_row)
    p = p + mu_ref[0:1] * (pm - p) + mu_ref[1:2] * (pp - p)
    c = MIX_W
    r, k, v = p[:, 0:c], p[:, c:2 * c], p[:, 2 * c:3 * c]
    wd, ad, gd = p[:, 3 * c:3 * c + 128], p[:, 3 * c + 128:3 * c + 256], p[:, 3 * c + 256:3 * c + 384]
    bd = _head_blockdiag(c, RW_HEAD_DIM)
    g = _mm(_sigmoid(gd), gup_ref[...])
    kk = k * kk_ref[...]
    nrm = jnp.sqrt(_x_exact01(kk * kk, bd))
    kap = kk / jnp.maximum(nrm, 1e-12)
    logw = -RW_DECAY_SCALE * _sigmoid(w0_ref[...] + _mm_hp(jnp.tanh(wd), wup_ref[...]))
    a = _sigmoid(a0_ref[...] + _mm_hp(ad, aup_ref[...]))
    bonus_s = 0.0
    outs_k, outs_b = (kf_o, kb_o), (bf_o, bb_o)
    for d in range(2):
        a_d = a[:, d * c:(d + 1) * c]
        k_d = k * (1.0 + (a_d - 1.0) * ka_ref[...])
        outs_k[d][0] = k_d.astype(BF16)
        outs_b[d][0] = (a_d * kap).astype(BF16)
        bonus_s = bonus_s + _x_exact01(r * k_d * rk_ref[...], bd)
    r_o[0] = r.astype(BF16)
    v_o[0] = v.astype(BF16)
    kap_o[0] = kap.astype(BF16)
    lwf_o[0] = logw[:, 0:c]
    lwb_o[0] = logw[:, c:2 * c]
    g_o[0] = g
    bonus_o[0] = bonus_s * v


def _blockdiag2(a, b):
    z1 = jnp.zeros((a.shape[0], b.shape[1]), a.dtype)
    z2 = jnp.zeros((b.shape[0], a.shape[1]), a.dtype)
    return jnp.concatenate([jnp.concatenate([a, z1], 1), jnp.concatenate([z2, b], 1)], 0)


def _rw_prep(p_rw, n_lat_tiles, mu, w0, w_up, a0, a_up, g_up, k_k, k_a, r_k):
    b, t, n = p_rw.shape
    nt = t // TOKEN_TILE
    c = MIX_W
    sub = TOKEN_TILE // 8
    n8 = t // 8
    consts = [mu, w0.reshape(1, 2 * c), _blockdiag2(w_up[0], w_up[1]), a0.reshape(1, 2 * c), _blockdiag2(a_up[0], a_up[1]),
              g_up, k_k.reshape(1, c), k_a.reshape(1, c), r_k.reshape(1, c)]
    cspecs = [pl.BlockSpec(x.shape, lambda b, t: (0, 0)) for x in consts]
    tile = lambda w: pl.BlockSpec((1, TOKEN_TILE, w), lambda b, t: (b, t, 0))
    return pl.pallas_call(
        functools.partial(_rw_prep_kernel, n_lat_tiles=n_lat_tiles),
        grid=(b, nt),
        in_specs=[tile(n),
                  pl.BlockSpec((1, 8, n), lambda b, t: (b, jnp.maximum(t * sub - 1, 0), 0)),
                  pl.BlockSpec((1, 8, n), lambda b, t: (b, jnp.minimum((t + 1) * sub, n8 - 1), 0))] + cspecs,
        out_specs=[tile(c)] * 11,
        out_shape=[jax.ShapeDtypeStruct((b, t, c), dt) for dt in (BF16, BF16, BF16, F32, F32, BF16, BF16, BF16, BF16, F32, F32)],
        compiler_params=_cp("arbitrary", "arbitrary"),
        name="rw_prep",
    )(p_rw, p_rw, p_rw, *consts)


def _rw_scan_kernel(rf, vf, kapf, lwf, kf, bbf, rb, vb, kapb, lwb, kb, bbb, of_ref, ob_ref, s_scr):
    @pl.when(pl.program_id(1) == 0)
    def _():
        s_scr[...] = jnp.zeros_like(s_scr)

    c = rf.shape[1]
    pw_ = 2 * RW_HEAD_DIM
    n_pairs = MIX_W // pw_
    lane_a = lax.broadcasted_iota(jnp.int32, (c, pw_), 1) < RW_HEAD_DIM
    lane_a2 = lax.broadcasted_iota(jnp.int32, (2 * c, pw_), 1) < RW_HEAD_DIM
    blk = ((lax.broadcasted_iota(jnp.int32, (pw_, pw_), 0) < RW_HEAD_DIM)
           == (lax.broadcasted_iota(jnp.int32, (pw_, pw_), 1) < RW_HEAD_DIM))

    def sel(x):
        return jnp.where(lane_a, x[:c], x[c:])

    def bf(x):
        return x.astype(BF16)

    nb = rf.shape[0]
    chains = []
    for e in range(nb):
        for d, (r, v, kap, lw, k, b) in enumerate(((rf, vf, kapf, lwf, kf, bbf), (rb, vb, kapb, lwb, kb, bbb))):
            rev = d == 1
            incl = _tri(c, rev, False)
            strict = _tri(c, rev, True)
            logw = lw[e]
            cum = _mm_exact01(incl.astype(F32), logw)
            tot = cum[0:1] if rev else cum[c - 1:c]
            ginv = jnp.exp(-cum)
            gt = jnp.exp(tot - cum)
            rt = r[e] * jnp.exp(cum)
            kt = kap[e] * jnp.exp(cum - logw)
            kd, bd = k[e] * ginv, b[e] * ginv
            kg, bg = k[e] * gt, b[e] * gt
            dec = jnp.exp(tot)
            vv = v[e]
            for p in range(n_pairs):
                sl = slice(p * pw_, (p + 1) * pw_)
                lhs2 = jnp.concatenate([rt[:, sl], kt[:, sl]], 0)
                chains.append(dict(
                    e=e, d=d, p=p, idx=(e * 2 + d) * n_pairs + p, incl=incl, strict=strict,
                    lhs2=bf(lhs2),
                    lhs4=bf(jnp.concatenate([jnp.where(lane_a2, lhs2, 0.0), jnp.where(lane_a2, 0.0, lhs2)], 0)),
                    kd=bf(kd[:, sl]), bd=bf(bd[:, sl]), v=bf(vv[:, sl]),
                    kgbg=bf(jnp.concatenate([kg[:, sl], bg[:, sl]], 0)), dec=dec[:, sl]))

    dot = functools.partial(lax.dot_general, preferred_element_type=F32)
    nn = (((1,), (0,)), ((), ()))
    for ch in chains:
        ch["s"] = s_scr[ch["idx"]]
    for ch in chains:
        ch["a1k"] = dot(ch["lhs4"], ch["kd"], NT_DIMS)
        ch["a1b"] = dot(ch["lhs4"], ch["bd"], NT_DIMS)
        ch["a2"] = dot(ch["lhs2"], bf(ch["s"]), NT_DIMS)
    for ch in chains:
        a1k, a1b, st = ch["a1k"], ch["a1b"], ch["strict"]
        lk = jnp.where(jnp.concatenate([st, st], 0), jnp.concatenate([a1k[c:2 * c], a1k[3 * c:]], 0), 0.0)
        ch["lb_a"] = bf(jnp.where(st, a1b[c:2 * c], 0.0))
        ch["lb_b"] = bf(jnp.where(st, a1b[3 * c:], 0.0))
        ch["u"] = ch["a2"][c:] + sel(dot(bf(lk), ch["v"], nn))
    for ch in chains:
        ch["u"] = ch["u"] - sel(dot(jnp.concatenate([ch["lb_a"], ch["lb_b"]], 0), bf(ch["u"]), nn))
        ch["p_a"] = bf(dot(ch["lb_a"], ch["lb_a"], nn))
        ch["p_b"] = bf(dot(ch["lb_b"], ch["lb_b"], nn))
    n = int(math.log2(c)) - 1
    for i in range(n):
        for ch in chains:
            ch["u"] = ch["u"] + sel(dot(jnp.concatenate([ch["p_a"], ch["p_b"]], 0), bf(ch["u"]), nn))
            if i < n - 1:
                ch["p_a"] = bf(dot(ch["p_a"], ch["p_a"], nn))
                ch["p_b"] = bf(dot(ch["p_b"], ch["p_b"], nn))
    outs = [[[None] * n_pairs, [None] * n_pairs] for _ in range(nb)]
    for ch in chains:
        a1k, a1b, inc = ch["a1k"], ch["a1b"], ch["incl"]
        inc2 = jnp.concatenate([inc, inc], 0)
        rk = jnp.where(inc2, jnp.concatenate([a1k[:c], a1k[2 * c:3 * c]], 0), 0.0)
        rb_ = jnp.where(inc2, jnp.concatenate([a1b[:c], a1b[2 * c:3 * c]], 0), 0.0)
        ub = bf(ch["u"])
        o = ch["a2"][:c] + sel(dot(bf(rk), ch["v"], nn)) - sel(dot(bf(rb_), ub, nn))
        outs[ch["e"]][ch["d"]][ch["p"]] = o
        upd = dot(jnp.concatenate([ch["v"], -ub], 0), ch["kgbg"], TN_DIMS)
        s_scr[ch["idx"]] = ch["s"] * ch["dec"] + jnp.where(blk, upd, 0.0)
    for e in range(nb):
        of_ref[e] = jnp.concatenate(outs[e][0], axis=1)
        ob_ref[e] = jnp.concatenate(outs[e][1], axis=1)


def _chunk_orders(n_lat, n_ctx):
    n = n_lat + n_ctx
    fwd = lambda b, s: (b, (s + n_lat) % n, 0)
    rev = lambda b, s: (b, n - 1 - s, 0)
    return fwd, rev


def _rw_scan(r, v, kap, lwf, lwb, kf, kb, bf, bb, n_lat_rows):
    b, t, c = r.shape
    ch = SEQ_CHUNK
    n = t // ch
    fwd, rev = _chunk_orders(n_lat_rows // ch, (t - n_lat_rows) // ch)
    nb = SCAN_BATCH
    blk = lambda im: pl.BlockSpec((nb, ch, c), im)
    return pl.pallas_call(
        _rw_scan_kernel,
        grid=(b // nb, n),
        in_specs=[blk(fwd)] * 6 + [blk(rev)] * 6,
        out_specs=[blk(fwd), blk(rev)],
        out_shape=[jax.ShapeDtypeStruct((b, t, c), F32)] * 2,
        scratch_shapes=[pltpu.VMEM((nb * RW_HEADS, 2 * RW_HEAD_DIM, 2 * RW_HEAD_DIM), F32)],
        compiler_params=_cp("arbitrary", "arbitrary"),
        name="rw_scan",
    )(r, v, kap, lwf, kf, bf, r, v, kap, lwb, kb, bb)


def _rw_readout(of, ob, bonus, g, ln_g, ln_b):
    o = of + ob
    bd = _head_blockdiag(MIX_W, RW_HEAD_DIM)
    mean = _x_exact01(o, bd) * (1.0 / RW_HEAD_DIM)
    oc = o - mean
    var = _x_exact01(oc * oc, bd) * (1.0 / RW_HEAD_DIM)
    on = oc * lax.rsqrt(var + RW_GN_EPS) * ln_g + ln_b
    return (on + bonus) * g


def _s5_matrices(a_re, a_im, log_dt, b_re, b_im, c_re, c_im):
    tc, g, n, p = S5_CHUNK, S5_GROUPS, S5_STATE, S5_GROUP
    hp = lax.Precision.HIGHEST
    dt = jnp.exp(log_dt)[:, :, None]
    lam_re, lam_im = a_re * dt, a_im * dt
    mag1 = jnp.exp(lam_re)
    ab_re, ab_im = mag1 * jnp.cos(lam_im), mag1 * jnp.sin(lam_im)
    nr, ni = ab_re - 1.0, ab_im
    den = a_re * a_re + a_im * a_im
    f_re, f_im = (nr * a_re + ni * a_im) / den, (ni * a_re - nr * a_im) / den
    tau = jnp.arange(tc + 1, dtype=F32)[:, None, None, None]
    pw_mag = jnp.exp(lam_re[None] * tau)
    pw_re, pw_im = pw_mag * jnp.cos(lam_im[None] * tau), pw_mag * jnp.sin(lam_im[None] * tau)
    fb_re = f_re[..., None] * b_re[None] - f_im[..., None] * b_im[None]
    fb_im = f_re[..., None] * b_im[None] + f_im[..., None] * b_re[None]
    e_re = pw_re[..., None] * fb_re[None] - pw_im[..., None] * fb_im[None]
    e_im = pw_re[..., None] * fb_im[None] + pw_im[..., None] * fb_re[None]
    m = (jnp.einsum('gpn,tdgnq->tdgpq', c_re, e_re[:tc], precision=hp)
         - jnp.einsum('gpn,tdgnq->tdgpq', c_im, e_im[:tc], precision=hp))
    i = jnp.arange(tc)
    lag = i[None, :] - i[:, None]
    mf = jnp.where((lag >= 0)[:, :, None, None, None], m[jnp.clip(lag, 0, tc - 1), 0], 0.0)
    mb = jnp.where((lag <= 0)[:, :, None, None, None], m[jnp.clip(-lag, 0, tc - 1), 1], 0.0)
    w = jnp.transpose(mf + mb, (2, 0, 4, 1, 3)).reshape(g, tc * p, tc * p)
    idx_f = tc - 1 - i
    pf_re, pf_im = e_re[idx_f, 0], e_im[idx_f, 0]
    pb_re, pb_im = e_re[i, 1], e_im[i, 1]
    pm = jnp.concatenate([pf_re, pb_re, pf_im, pb_im], axis=2)
    pm = jnp.transpose(pm, (1, 0, 3, 2)).reshape(g, tc * p, 4 * n)
    cf_re = c_re[None, :, :, :] * pw_re[i + 1, 0][:, :, None, :] - c_im[None] * pw_im[i + 1, 0][:, :, None, :]
    cf_im = c_re[None] * pw_im[i + 1, 0][:, :, None, :] + c_im[None] * pw_re[i + 1, 0][:, :, None, :]
    cb_re = c_re[None] * pw_re[tc - i, 1][:, :, None, :] - c_im[None] * pw_im[tc - i, 1][:, :, None, :]
    cb_im = c_re[None] * pw_im[tc - i, 1][:, :, None, :] + c_im[None] * pw_re[tc - i, 1][:, :, None, :]
    q = jnp.concatenate([cf_re, cb_re, -cf_im, -cb_im], axis=3)
    q = jnp.transpose(q, (1, 3, 0, 2)).reshape(g, 4 * n, tc * p)
    dec_re = jnp.concatenate([pw_re[tc, 0], pw_re[tc, 1]], axis=1)
    dec_im = jnp.concatenate([pw_im[tc, 0], pw_im[tc, 1]], axis=1)
    dec = jnp.stack([dec_re, dec_im], axis=1)
    return w.astype(BF16), pm.astype(BF16), q.astype(BF16), dec


def _s5_kernel(u_ref, w_ref, p_ref, q_ref, dec_ref, y_ref, inc_re, inc_im, sin_re, sin_im, *, n_lat, n_ctx, bsz):
    u = u_ref[0].astype(BF16)
    inc = jnp.dot(u, p_ref[0], preferred_element_type=F32)
    n2 = 2 * S5_STATE
    inc_re[...] = inc[:, :n2]
    inc_im[...] = inc[:, n2:]
    n = n_lat + n_ctx
    d_re = dec_ref[0, 0:1, :]
    d_im = dec_ref[0, 1:2, :]
    is_f = lax.broadcasted_iota(jnp.int32, (bsz, n2), 1) < S5_STATE

    def body(s, carry):
        s_re, s_im = carry
        cf = (s + n_lat) % n
        cb = n - 1 - s
        rf = pl.ds(pl.multiple_of(cf * bsz, bsz), bsz)
        rb = pl.ds(pl.multiple_of(cb * bsz, bsz), bsz)
        sin_re[rf, :] = jnp.where(is_f, s_re, sin_re[rf, :])
        sin_im[rf, :] = jnp.where(is_f, s_im, sin_im[rf, :])
        sin_re[rb, :] = jnp.where(is_f, sin_re[rb, :], s_re)
        sin_im[rb, :] = jnp.where(is_f, sin_im[rb, :], s_im)
        i_re = jnp.where(is_f, inc_re[rf, :], inc_re[rb, :])
        i_im = jnp.where(is_f, inc_im[rf, :], inc_im[rb, :])
        return (d_re * s_re - d_im * s_im + i_re, d_re * s_im + d_im * s_re + i_im)

    zero = jnp.zeros((bsz, n2), F32)
    sin_re[...] = jnp.zeros_like(sin_re)
    sin_im[...] = jnp.zeros_like(sin_im)
    lax.fori_loop(0, n, body, (zero, zero))
    s_all = jnp.concatenate([sin_re[...], sin_im[...]], axis=1).astype(BF16)
    y_ref[0] = jnp.dot(u, w_ref[0], preferred_element_type=F32) + jnp.dot(s_all, q_ref[0], preferred_element_type=F32)


def _s5_core(p_s5, n_lat_rows, mats):
    b, t, c = p_s5.shape
    tc, g, p = S5_CHUNK, S5_GROUPS, S5_GROUP
    nch = t // tc
    w, pm, q, dec = mats
    u = p_s5.reshape(b, nch, tc, g, p).transpose(3, 1, 0, 2, 4).reshape(g, nch * b, tc * p)
    rows = nch * b
    k = tc * p
    y = pl.pallas_call(
        functools.partial(_s5_kernel, n_lat=n_lat_rows // tc, n_ctx=(t - n_lat_rows) // tc, bsz=b),
        grid=(g,),
        in_specs=[pl.BlockSpec((1, rows, k), lambda i: (i, 0, 0)),
                  pl.BlockSpec((1, k, k), lambda i: (i, 0, 0)),
                  pl.BlockSpec((1, k, 4 * S5_STATE), lambda i: (i, 0, 0)),
                  pl.BlockSpec((1, 4 * S5_STATE, k), lambda i: (i, 0, 0)),
                  pl.BlockSpec((1, 2, 2 * S5_STATE), lambda i: (i, 0, 0))],
        out_specs=pl.BlockSpec((1, rows, k), lambda i: (i, 0, 0)),
        out_shape=jax.ShapeDtypeStruct((g, rows, k), F32),
        scratch_shapes=[pltpu.VMEM((rows, 2 * S5_STATE), F32)] * 4,
        compiler_params=_cp("arbitrary"),
        name="s5_core",
    )(u, w, pm, q, dec)
    return y.reshape(g, nch, b, tc, p).transpose(2, 1, 3, 0, 4).reshape(b, t, c)


def _s5_output(y, u, d_skip, glu_w, glu_b):
    z = y + d_skip * u
    z = 0.5 * z * (1.0 + jnp.tanh(math.sqrt(2.0 / math.pi) * (z + 0.044715 * (z * z * z))))
    return z * _sigmoid(_mm(z, glu_w) + glu_b)


def _gla_scan_kernel(pf_ref, pb_ref, gup_ref, gb_ref, of_ref, ob_ref, s_scr):
    @pl.when(pl.program_id(1) == 0)
    def _():
        s_scr[...] = jnp.zeros_like(s_scr)

    c = pf_ref.shape[1]
    pw_ = 2 * GLA_HK
    n_pairs = GLA_DK // pw_
    lane_a = lax.broadcasted_iota(jnp.int32, (c, pw_), 1) < GLA_HK
    lane_av = lax.broadcasted_iota(jnp.int32, (GLA_HV, pw_), 1) < GLA_HK
    dot = functools.partial(lax.dot_general, preferred_element_type=F32)
    nn = (((1,), (0,)), ((), ()))
    nb = pf_ref.shape[0]
    chains = []
    for e in range(nb):
        for d, p_ref in enumerate((pf_ref, pb_ref)):
            rev = d == 1
            incl = _tri(c, rev, False)
            gd = p_ref[e, :, 2 * GLA_DK + 2 * GLA_DV:]
            x = _mm_hp(gd, gup_ref[...])[:, d * GLA_DK:(d + 1) * GLA_DK] + gb_ref[0:1, d * GLA_DK:(d + 1) * GLA_DK]
            logg = (jnp.minimum(x, 0.0) - jnp.log(1.0 + jnp.exp(-jnp.abs(x)))) * (1.0 / GLA_GATE_NORM)
            cum = _mm_exact01(incl.astype(F32), logg)
            mid = c // 2
            ref = cum[mid:mid + 1] if rev else cum[mid - 1:mid]
            tot = cum[0:1] if rev else cum[c - 1:c]
            q = p_ref[e, :, 0:GLA_DK] * (GLA_HK ** -0.5)
            k = p_ref[e, :, GLA_DK:2 * GLA_DK]
            qs = q * jnp.exp(cum - ref)
            ks = k * jnp.exp(ref - cum)
            qc = q * jnp.exp(cum)
            kg = k * jnp.exp(tot - cum)
            dec = jnp.exp(tot)
            for p in range(n_pairs):
                sl = slice(p * pw_, (p + 1) * pw_)
                mask2 = lambda a: jnp.concatenate([jnp.where(lane_a, a, 0.0), jnp.where(lane_a, 0.0, a)], 0).astype(BF16)
                chains.append(dict(
                    e=e, d=d, p=p, idx=(e * 2 + d) * n_pairs + p, incl=incl, qs4=mask2(qs[:, sl]), qc4=mask2(qc[:, sl]),
                    ks=ks[:, sl].astype(BF16), kg=kg[:, sl].astype(BF16), dec=dec[:, sl],
                    va=p_ref[e, :, 2 * GLA_DK + 2 * p * GLA_HV:2 * GLA_DK + (2 * p + 1) * GLA_HV].astype(BF16),
                    vb=p_ref[e, :, 2 * GLA_DK + (2 * p + 1) * GLA_HV:2 * GLA_DK + (2 * p + 2) * GLA_HV].astype(BF16)))
    for ch in chains:
        ch["s"] = s_scr[ch["idx"]]
    for ch in chains:
        ch["sc"] = dot(ch["qs4"], ch["ks"], NT_DIMS)
        ch["st"] = dot(ch["qc4"], ch["s"].astype(BF16), NT_DIMS)
    outs = [[[None] * GLA_HEADS, [None] * GLA_HEADS] for _ in range(nb)]
    for ch in chains:
        e, d, p = ch["e"], ch["d"], ch["p"]
        sa = jnp.where(ch["incl"], ch["sc"][:c], 0.0).astype(BF16)
        sb = jnp.where(ch["incl"], ch["sc"][c:], 0.0).astype(BF16)
        outs[e][d][2 * p] = ch["st"][:c] + dot(sa, ch["va"], nn)
        outs[e][d][2 * p + 1] = ch["st"][c:] + dot(sb, ch["vb"], nn)
        upd = jnp.where(lane_av, dot(ch["va"], ch["kg"], TN_DIMS), dot(ch["vb"], ch["kg"], TN_DIMS))
        s_scr[ch["idx"]] = ch["s"] * ch["dec"] + upd
    for e in range(nb):
        of_ref[e] = jnp.concatenate(outs[e][0], axis=1)
        ob_ref[e] = jnp.concatenate(outs[e][1], axis=1)


def _gla_scan(p_gla, n_lat_rows, gate_up, gate_b):
    b, t, n = p_gla.shape
    ch = SEQ_CHUNK
    fwd, rev = _chunk_orders(n_lat_rows // ch, (t - n_lat_rows) // ch)
    gup = jnp.zeros((128, 2 * GLA_DK), F32)
    gup = gup.at[0:GLA_GATE_RANK, 0:GLA_DK].set(gate_up[0]).at[GLA_GATE_RANK:2 * GLA_GATE_RANK, GLA_DK:].set(gate_up[1])
    nb = SCAN_BATCH
    return pl.pallas_call(
        _gla_scan_kernel,
        grid=(b // nb, t // ch),
        in_specs=[pl.BlockSpec((nb, ch, n), fwd), pl.BlockSpec((nb, ch, n), rev),
                  pl.BlockSpec((128, 2 * GLA_DK), lambda b, s: (0, 0)),
                  pl.BlockSpec((1, 2 * GLA_DK), lambda b, s: (0, 0))],
        out_specs=[pl.BlockSpec((nb, ch, GLA_DV), fwd), pl.BlockSpec((nb, ch, GLA_DV), rev)],
        out_shape=[jax.ShapeDtypeStruct((b, t, GLA_DV), F32)] * 2,
        scratch_shapes=[pltpu.VMEM((nb * GLA_HEADS, GLA_HV, 2 * GLA_HK), F32)],
        compiler_params=_cp("arbitrary", "arbitrary"),
        name="gla_scan",
    )(p_gla, p_gla, gup, gate_b.reshape(1, 2 * GLA_DK))


def _gla_readout_kernel(of_ref, ob_ref, g_ref, ng_ref, ylat_ref, yctx_ref, *, n_lat_tiles, cols):
    t = pl.program_id(1)
    o = of_ref[0] + ob_ref[0]
    bd = _head_blockdiag(GLA_DV, GLA_HV)
    ms = _x_exact01(o * o, bd) * (1.0 / GLA_HV)
    y = (o * lax.rsqrt(ms + RMS_EPS) * ng_ref[...] * _silu(g_ref[0])).astype(ylat_ref.dtype)
    rows = y.shape[0] // cols

    @pl.when(t < n_lat_tiles)
    def _():
        ylat_ref[0] = jnp.concatenate([y[j * rows:(j + 1) * rows] for j in range(cols)], axis=1)

    @pl.when(t >= n_lat_tiles)
    def _():
        yctx_ref[0] = y


def _gla_readout(of, ob, p_gla, norm_g, n_lat_rows):
    b, t, c = of.shape
    n_lat_tiles = n_lat_rows // TOKEN_TILE
    grid_rows = n_lat_rows // GRID_W
    cols = _grid_cols_per_tile(n_lat_rows)
    tile = pl.BlockSpec((1, TOKEN_TILE, c), lambda b, t: (b, t, 0))
    y_lat, y_ctx = pl.pallas_call(
        functools.partial(_gla_readout_kernel, n_lat_tiles=n_lat_tiles, cols=cols),
        grid=(b, t // TOKEN_TILE),
        in_specs=[tile, tile, pl.BlockSpec((1, TOKEN_TILE, c), lambda b, t: (b, t, (2 * GLA_DK + GLA_DV) // c)),
                  pl.BlockSpec((1, c), lambda b, t: (0, 0))],
        out_specs=[pl.BlockSpec((1, grid_rows, cols * c), lambda b, t: (b, 0, jnp.minimum(t, n_lat_tiles - 1))),
                   pl.BlockSpec((1, TOKEN_TILE, c), lambda b, t: (b, 0, 0))],
        out_shape=[jax.ShapeDtypeStruct((b, grid_rows, GRID_W * c), BF16), jax.ShapeDtypeStruct((b, t - n_lat_rows, c), BF16)],
        compiler_params=_cp("arbitrary", "arbitrary"),
        name="gla_readout",
    )(of, ob, p_gla, jnp.tile(norm_g, GLA_HEADS).reshape(1, c))
    return y_lat.reshape(b, n_lat_rows, c), y_ctx


def _merge_kernel(h_ref, mod_ref, gate_ref, yhyl_ref, yhyc_ref, rof_ref, rob_ref, rbonus_ref, rg_ref, ys5_ref, us5_ref,
                  yglal_ref, yglac_ref, lng_ref, lnb_ref, dskip_ref, gluw_ref, glub_ref, wbr_ref, wout_ref, o_ref,
                  *, n_lat_tiles):
    d = D_MODEL
    is_lat = pl.program_id(1) < n_lat_tiles
    y_hy = jnp.where(is_lat, yhyl_ref[0], yhyc_ref[0])
    y_gla = jnp.where(is_lat, yglal_ref[0], yglac_ref[0])
    y_rw = _rw_readout(rof_ref[0], rob_ref[0], rbonus_ref[0], rg_ref[0], lng_ref[...], lnb_ref[...])
    y_s5 = _s5_output(ys5_ref[0], us5_ref[0], dskip_ref[...], gluw_ref[...], glub_ref[...])
    acc = jnp.zeros((h_ref.shape[1], d), F32)
    for m, y in enumerate((y_hy, y_rw, y_s5, y_gla)):
        acc = acc + (_sigmoid(gate_ref[0, :, m * d:(m + 1) * d].astype(F32))
                     * jnp.dot(y.astype(BF16), wbr_ref[m], preferred_element_type=F32))
    out = jnp.dot(acc.astype(BF16), wout_ref[...], preferred_element_type=F32)
    o_ref[0] = h_ref[0] + mod_ref[0, 0, 2:3, :] * out


def _merge(h_all, modsel, n_tiles, n_lat_tiles, gate, y_hy_lat, y_hy_ctx, rof, rob, rbonus, rg, y_s5, u_s5, y_gla_lat, y_gla_ctx,
           ln_g, ln_b, d_skip, glu_w, glu_b, w_branch, w_out):
    b, t, d = h_all.shape
    c = MIX_W
    tile = lambda w: pl.BlockSpec((1, TOKEN_TILE, w), lambda b, t: (b, t, 0))
    lat = pl.BlockSpec((1, TOKEN_TILE, c), lambda b, t: (b, jnp.minimum(t, n_lat_tiles - 1), 0))
    ctx = pl.BlockSpec((1, TOKEN_TILE, c), lambda b, t: (b, jnp.maximum(t - n_lat_tiles, 0), 0))
    vec = lambda x: x.reshape(1, -1)
    consts = [vec(ln_g), vec(ln_b), vec(d_skip), glu_w.astype(BF16), vec(glu_b), w_branch.astype(BF16), w_out.astype(BF16)]
    cspecs = [pl.BlockSpec(x.shape, (lambda n: lambda b, t: (0,) * n)(x.ndim)) for x in consts]
    return pl.pallas_call(
        functools.partial(_merge_kernel, n_lat_tiles=n_lat_tiles),
        grid=(b, n_tiles),
        in_specs=[tile(d), pl.BlockSpec((1, 1, 6, d), _mod_index(n_lat_tiles)), tile(4 * d), lat, ctx] + [tile(c)] * 6
                 + [lat, ctx] + cspecs,
        out_specs=tile(d),
        out_shape=jax.ShapeDtypeStruct((b, t, d), F32),
        input_output_aliases={0: 0},
        compiler_params=_cp("arbitrary", "arbitrary"),
        name="merge",
    )(h_all, modsel, gate, y_hy_lat, y_hy_ctx, rof, rob, rbonus, rg, y_s5, u_s5, y_gla_lat, y_gla_ctx, *consts)


def _ffn_kernel(h_ref, g_ref, mod_ref, w1_ref, w3_ref, w2_ref, o_ref):
    h = h_ref[0]
    xm = _norm_mod(h, g_ref[...], mod_ref[0, 0], 3).astype(BF16)
    a = jnp.dot(xm, w1_ref[...], preferred_element_type=F32)
    b = jnp.dot(xm, w3_ref[...], preferred_element_type=F32)
    y = jnp.dot((_silu(a) * b).astype(BF16), w2_ref[...], preferred_element_type=F32)
    o_ref[0] = h + mod_ref[0, 0, 5:6, :] * y


def _ffn_dense(h_all, g, modsel, n_lat_tiles, w1, w3, w2):
    b, t, d = h_all.shape
    tile = pl.BlockSpec((1, TOKEN_TILE, d), lambda b, t: (b, t, 0))
    ws = [w1.astype(BF16), w3.astype(BF16), w2.astype(BF16)]
    wspec = [pl.BlockSpec(w.shape, lambda b, t: (0, 0), pipeline_mode=pl.Buffered(1)) for w in ws]
    return pl.pallas_call(
        _ffn_kernel,
        grid=(b, t // TOKEN_TILE),
        in_specs=[tile, pl.BlockSpec((1, d), lambda b, t: (0, 0)), pl.BlockSpec((1, 1, 6, d), _mod_index(n_lat_tiles))] + wspec,
        out_specs=tile,
        out_shape=jax.ShapeDtypeStruct((b, t, d), F32),
        input_output_aliases={0: 0},
        compiler_params=_cp("arbitrary", "arbitrary"),
        name="ffn_dense",
    )(h_all, g.reshape(1, d), modsel, *ws)


def _router_kernel(h_ref, g_ref, mod_ref, rw_ref, rb_ref, x_ref, logit_ref):
    xm = _norm_mod(h_ref[0], g_ref[...], mod_ref[0, 0], 3)
    x_ref[0] = xm.astype(BF16)
    logit_ref[0] = _mm_hp(xm, rw_ref[...]) + rb_ref[...]


def _router(h_all, g, modsel, n_lat_tiles, router_w, router_b):
    b, t, d = h_all.shape
    n_lat = n_lat_tiles * TOKEN_TILE
    rw = jnp.pad(router_w, ((0, 0), (0, 128 - N_EXPERTS)))
    rb = jnp.pad(router_b, (0, 128 - N_EXPERTS), constant_values=-1e30).reshape(1, 128)
    tile = lambda w: pl.BlockSpec((1, TOKEN_TILE, w), lambda b, t: (b, t, 0))
    return pl.pallas_call(
        _router_kernel,
        grid=(b, n_lat_tiles),
        in_specs=[tile(d), pl.BlockSpec((1, d), lambda b, t: (0, 0)), pl.BlockSpec((1, 1, 6, d), _mod_index(n_lat_tiles)),
                  pl.BlockSpec((d, 128), lambda b, t: (0, 0)), pl.BlockSpec((1, 128), lambda b, t: (0, 0))],
        out_specs=[tile(d), tile(128)],
        out_shape=[jax.ShapeDtypeStruct((b, n_lat, d), BF16), jax.ShapeDtypeStruct((b, n_lat, 128), F32)],
        compiler_params=_cp("arbitrary", "arbitrary"),
        name="moe_router",
    )(h_all, g.reshape(1, d), modsel, rw, rb)


def _moe_expert_kernel(be_ref, x_ref, w1_ref, w3_ref, w2_ref, o_ref):
    x = x_ref[...]
    a = jnp.dot(x, w1_ref[0], preferred_element_type=F32)
    b = jnp.dot(x, w3_ref[0], preferred_element_type=F32)
    o_ref[...] = jnp.dot((_silu(a) * b).astype(BF16), w2_ref[0], preferred_element_type=F32)


def _moe_experts(xb, block_exp, w1, w3, w2):
    rows, d = xb.shape
    nb = rows // MOE_BLOCK_ROWS
    f = w1.shape[2]
    grid_spec = pltpu.PrefetchScalarGridSpec(
        num_scalar_prefetch=1,
        grid=(nb,),
        in_specs=[pl.BlockSpec((MOE_BLOCK_ROWS, d), lambda i, be: (i, 0)),
                  pl.BlockSpec((1, d, f), lambda i, be: (be[i], 0, 0), pipeline_mode=pl.Buffered(1)),
                  pl.BlockSpec((1, d, f), lambda i, be: (be[i], 0, 0), pipeline_mode=pl.Buffered(1)),
                  pl.BlockSpec((1, f, d), lambda i, be: (be[i], 0, 0), pipeline_mode=pl.Buffered(1))],
        out_specs=pl.BlockSpec((MOE_BLOCK_ROWS, d), lambda i, be: (i, 0)),
    )
    return pl.pallas_call(
        _moe_expert_kernel,
        grid_spec=grid_spec,
        out_shape=jax.ShapeDtypeStruct((rows, d), F32),
        compiler_params=_cp("arbitrary"),
        name="moe_experts",
    )(block_exp, xb, w1, w3, w2)


def _moe_combine_kernel(h_ref, mod_ref, y_ref, gate_ref, fg_ref, o_ref):
    d = h_ref.shape[2]
    y = y_ref[0, :, :d] * gate_ref[0, :, 0:1] + y_ref[0, :, d:] * gate_ref[0, :, 1:2]
    h = h_ref[0] + mod_ref[0, 0, 5:6, :] * y
    o_ref[0] = h * lax.rsqrt(jnp.mean(h * h, axis=-1, keepdims=True) + RMS_EPS) * fg_ref[...]


def _moe_combine(h_all, modsel, n_lat_tiles, y2, gates, final_g):
    b, t, d = h_all.shape
    n_lat = n_lat_tiles * TOKEN_TILE
    tile = lambda w: pl.BlockSpec((1, TOKEN_TILE, w), lambda b, t: (b, t, 0))
    return pl.pallas_call(
        _moe_combine_kernel,
        grid=(b, n_lat_tiles),
        in_specs=[tile(d), pl.BlockSpec((1, 1, 6, d), _mod_index(n_lat_tiles)),
                  tile(TOP_K * d),
                  tile(128), pl.BlockSpec((1, d), lambda b, t: (0, 0))],
        out_specs=tile(d),
        out_shape=jax.ShapeDtypeStruct((b, n_lat, d), F32),
        compiler_params=_cp("arbitrary", "arbitrary"),
        name="moe_combine",
    )(h_all, modsel, y2, gates, final_g.reshape(1, d))


def _moe_ffn_and_final_norm(h_all, g, modsel, n_lat_tiles, router_w, router_b, w1, w3, w2, final_g):
    b, t, d = h_all.shape
    n_lat = n_lat_tiles * TOKEN_TILE
    n_tok = b * n_lat
    xn, logits = _router(h_all, g, modsel, n_lat_tiles, router_w, router_b)
    logits = logits.reshape(n_tok, 128)[:, :N_EXPERTS]
    top_v, top_i = lax.top_k(logits, TOP_K)
    gates = jax.nn.softmax(top_v, axis=-1)
    n_assign = n_tok * TOP_K
    e_flat = top_i.reshape(-1).astype(jnp.int32)
    order = jnp.argsort(e_flat).astype(jnp.int32)
    rank = jnp.argsort(order).astype(jnp.int32)
    counts = jnp.sum((e_flat[:, None] == jnp.arange(N_EXPERTS, dtype=jnp.int32)[None, :]).astype(jnp.int32), axis=0)
    padded = (counts + MOE_BLOCK_ROWS - 1) // MOE_BLOCK_ROWS * MOE_BLOCK_ROWS
    start = jnp.cumsum(counts) - counts
    pend = jnp.cumsum(padded)
    pstart = pend - padded
    n_blocks = n_assign // MOE_BLOCK_ROWS + N_EXPERTS
    n_rows = n_blocks * MOE_BLOCK_ROWS
    row_ids = jnp.arange(n_rows, dtype=jnp.int32)
    row_exp = jnp.minimum(jnp.searchsorted(pend, row_ids, side='right'), N_EXPERTS - 1).astype(jnp.int32)
    row_off = row_ids - pstart[row_exp]
    row_slot = jnp.clip(start[row_exp] + row_off, 0, n_assign - 1)
    row_tok = jnp.where(row_off < counts[row_exp], order[row_slot] // TOP_K, n_tok).astype(jnp.int32)
    block_exp = row_exp[::MOE_BLOCK_ROWS]
    tok_pad = jnp.concatenate([xn.reshape(n_tok, d), jnp.zeros((1, d), BF16)], axis=0)
    xb = tok_pad.at[row_tok].get(mode="promise_in_bounds")
    yb = _moe_experts(xb, block_exp, w1.astype(BF16), w3.astype(BF16), w2.astype(BF16))
    dest_orig = (pstart[e_flat] + rank - start[e_flat]).astype(jnp.int32)
    y2 = yb.at[dest_orig].get(mode="promise_in_bounds").reshape(b, n_lat, TOP_K * d)
    gates_pad = jnp.pad(gates, ((0, 0), (0, 128 - TOP_K))).reshape(b, n_lat, 128)
    return _moe_combine(h_all, modsel, n_lat_tiles, y2, gates_pad, final_g)


def _pad_cols(w, n):
    return jnp.pad(w, ((0, 0), (0, n - w.shape[1])))


def kernel(x, c, ctx, c_ctx, ada_w, ada_b, norm_mix_g, norm_ffn_g, w_in, hy_conv_w, hy_conv_b, hy_f_w1, hy_f_b1, hy_f_w2, hy_f_b2, hy_f_w3, hy_f_b3, hy_bias, rw_mu, rw_w0, rw_w_up, rw_a0, rw_a_up, rw_g_up, rw_k_k, rw_k_a, rw_r_k, rw_ln_g, rw_ln_b, s5_a_re, s5_a_im, s5_log_dt, s5_b_re, s5_b_im, s5_c_re, s5_c_im, s5_d, s5_glu_w, s5_glu_b, gla_gate_up, gla_gate_b, gla_norm_g, w_branch, w_out, ffn_w1, ffn_w3, ffn_w2, moe_router_w, moe_router_b, moe_w1, moe_w3, moe_w2, final_norm_g):
    bsz, n_lat, d = x.shape
    n_ctx = ctx.shape[1]
    depth = w_in.shape[0]
    assert depth == 2 and d == D_MODEL
    assert n_lat % TOKEN_TILE == 0 and n_ctx % TOKEN_TILE == 0 and n_lat % GRID_W == 0
    t_all = n_lat + n_ctx
    n_lat_tiles = n_lat // TOKEN_TILE
    n_tiles = t_all // TOKEN_TILE
    assert n_tiles <= 2 * n_lat_tiles

    rows = (bsz + 1 + 7) // 8 * 8
    c_all = jnp.zeros((rows, d), F32).at[:bsz].set(c.astype(F32)).at[bsz].set(c_ctx.astype(F32))
    mod = _ada_mod(c_all, ada_w, ada_b)
    h_all = jnp.concatenate([x.astype(F32), ctx.astype(F32)], axis=1)

    dft_lat = _dft_tables(n_lat)
    dft_ctx = _dft_tables(n_ctx)
    seg = np.cumsum([0, 3 * MIX_W, RW_IN, MIX_W, 2 * GLA_DK + 2 * GLA_DV + 2 * GLA_GATE_RANK, 4 * d])

    out = None
    for l in range(depth):
        last = l == depth - 1
        mod_l = mod[l, :bsz].reshape(bsz, 6, d)
        mod_c = jnp.broadcast_to(mod[l, bsz].reshape(1, 6, d), (bsz, 6, d))
        modsel = jnp.stack([mod_l, mod_c], axis=1)

        wl = w_in[l].astype(BF16)
        ws = [wl[:, seg[0]:seg[1]], wl[:, seg[1]:seg[2]], wl[:, seg[2]:seg[3]], wl[:, seg[4]:seg[5]]]
        p_hy, p_rw, p_s5, gate = _in_proj(h_all, norm_mix_g[l], modsel, ws, n_lat_tiles)
        p_gla = _in_proj_gla(h_all, norm_mix_g[l], modsel, _pad_cols(wl[:, seg[3]:seg[4]], GLA_IN_PAD), n_lat)

        filt_args = (hy_f_w1[l], hy_f_b1[l], hy_f_w2[l], hy_f_b2[l], hy_f_w3[l], hy_f_b3[l])
        spec_lat = _hy_spectrum(_hy_filter(n_lat, *filt_args), dft_lat[0], dft_lat[1])
        y_hy_lat = _hyena(p_hy, 0, n_lat, hy_conv_w[l], hy_conv_b[l], hy_bias[l], dft_lat, spec_lat)
        y_hy_ctx = y_hy_lat
        if not last:
            spec_ctx = _hy_spectrum(_hy_filter(n_ctx, *filt_args), dft_ctx[0], dft_ctx[1])
            y_hy_ctx = _hyena(p_hy, n_lat // n_ctx, n_ctx, hy_conv_w[l], hy_conv_b[l], hy_bias[l], dft_ctx, spec_ctx)

        r, v, kap, lwf, lwb, kf, kb, bf, bb, rg, rbonus = _rw_prep(
            p_rw, n_lat_tiles, rw_mu[l], rw_w0[l], rw_w_up[l], rw_a0[l], rw_a_up[l], rw_g_up[l], rw_k_k[l], rw_k_a[l], rw_r_k[l])
        rof, rob = _rw_scan(r, v, kap, lwf, lwb, kf, kb, bf, bb, n_lat)

        y_s5 = _s5_core(p_s5, n_lat, _s5_matrices(s5_a_re[l], s5_a_im[l], s5_log_dt[l], s5_b_re[l], s5_b_im[l], s5_c_re[l], s5_c_im[l]))

        gof, gob = _gla_scan(p_gla, n_lat, gla_gate_up[l], gla_gate_b[l])
        y_gla_lat, y_gla_ctx = _gla_readout(gof, gob, p_gla, gla_norm_g[l], n_lat)

        nt_l = n_lat_tiles if last else n_tiles
        h_all = _merge(h_all, modsel, nt_l, n_lat_tiles, gate, y_hy_lat, y_hy_ctx, rof, rob, rbonus, rg, y_s5, p_s5,
                       y_gla_lat, y_gla_ctx, rw_ln_g[l], rw_ln_b[l], s5_d[l], s5_glu_w[l], s5_glu_b[l], w_branch[l], w_out[l])

        i = l // 2
        if l % 2 == 0:
            h_all = _ffn_dense(h_all, norm_ffn_g[l], modsel, n_lat_tiles, ffn_w1[i], ffn_w3[i], ffn_w2[i])
        else:
            out = _moe_ffn_and_final_norm(h_all, norm_ffn_g[l], modsel, n_lat_tiles, moe_router_w[i], moe_router_b[i],
                                          moe_w1[i], moe_w3[i], moe_w2[i], final_norm_g)
    return out.astype(x.dtype)
```

```python
import functools
import math

import numpy as np
import jax
import jax.numpy as jnp
from jax import lax
from jax.experimental import pallas as pl
from jax.experimental.pallas import tpu as pltpu

F32 = jnp.float32
BF16 = jnp.bfloat16

D_MODEL = 1024
GRID_W = 64
MIX_W = 512
RMS_EPS = 1e-6
HY_BANDS = 16
HY_DECAY_TARGET = 1e-2
HY_SHORT_PCT = 0.3
HY_LONG_PCT = 1.5
RW_HEADS = 8
RW_HEAD_DIM = 64
RW_DECAY_SCALE = 0.606531
RW_GN_EPS = 64e-5
RW_IN = 1920
S5_GROUP = 16
S5_GROUPS = 32
S5_STATE = 64
S5_CHUNK = 16
GLA_HEADS = 4
GLA_HK = 64
GLA_HV = 128
GLA_DK = 256
GLA_DV = 512
GLA_GATE_RANK = 16
GLA_GATE_NORM = 16.0
GLA_IN_PAD = 1664
N_EXPERTS = 8
TOP_K = 2
MOE_BLOCK_ROWS = 256
TOKEN_TILE = 256
SEQ_CHUNK = 64
SCAN_BATCH = 2
VMEM_LIMIT = 56 * 1024 * 1024

NT_DIMS = (((1,), (1,)), ((), ()))
TN_DIMS = (((0,), (0,)), ((), ()))


def _cp(*sem):
    return pltpu.CompilerParams(dimension_semantics=sem, vmem_limit_bytes=VMEM_LIMIT)


def _mm(a, b, dims=None):
    a = a.astype(BF16)
    b = b.astype(BF16)
    if dims is None:
        return jnp.dot(a, b, preferred_element_type=F32)
    return lax.dot_general(a, b, dims, preferred_element_type=F32)


def _split_bf16(x):
    hi = x.astype(BF16)
    lo = (x - hi.astype(F32)).astype(BF16)
    return hi, lo


def _mm_hp(a, b):
    ah, al = _split_bf16(a)
    bh, bl = _split_bf16(b)
    d = functools.partial(jnp.dot, preferred_element_type=F32)
    return d(ah, bh) + (d(ah, bl) + d(al, bh))


def _mm_exact01(m01, x):
    m = m01.astype(BF16)
    hi = x.astype(BF16)
    r1 = x - hi.astype(F32)
    mid = r1.astype(BF16)
    lo = (r1 - mid.astype(F32)).astype(BF16)
    d = functools.partial(jnp.dot, preferred_element_type=F32)
    return d(m, hi) + (d(m, mid) + d(m, lo))


def _x_exact01(x, m01):
    m = m01.astype(BF16)
    hi = x.astype(BF16)
    mid = (x - hi.astype(F32)).astype(BF16)
    d = functools.partial(jnp.dot, preferred_element_type=F32)
    return d(hi, m) + d(mid, m)


def _sigmoid(x):
    return 1.0 / (1.0 + jnp.exp(-x))


def _silu(x):
    return x * _sigmoid(x)


def _tri(c, rev, strict):
    i = lax.broadcasted_iota(jnp.int32, (c, c), 0)
    j = lax.broadcasted_iota(jnp.int32, (c, c), 1)
    if rev:
        return (j > i) if strict else (j >= i)
    return (j < i) if strict else (j <= i)


def _head_blockdiag(n, head):
    i = lax.broadcasted_iota(jnp.int32, (n, n), 0) // head
    j = lax.broadcasted_iota(jnp.int32, (n, n), 1) // head
    return (i == j).astype(F32)


def _shift_rows(x, prev_row, next_row):
    n = x.shape[0]
    row = lax.broadcasted_iota(jnp.int32, x.shape, 0)
    xm = jnp.where(row == 0, prev_row, pltpu.roll(x, 1, 0))
    xp = jnp.where(row == n - 1, next_row, pltpu.roll(x, n - 1, 0))
    return xm, xp


def _ada_kernel(c_ref, w_ref, b_ref, o_ref):
    o_ref[0] = _mm_hp(_silu(c_ref[...]), w_ref[0]) + b_ref[0]


def _ada_mod(c_all, ada_w, ada_b):
    depth, d, n = ada_w.shape
    rows = c_all.shape[0]
    return pl.pallas_call(
        _ada_kernel,
        grid=(depth, n // d),
        in_specs=[pl.BlockSpec((rows, d), lambda l, j: (0, 0)),
                  pl.BlockSpec((1, d, d), lambda l, j: (l, 0, j)),
                  pl.BlockSpec((1, 1, d), lambda l, j: (l, 0, j))],
        out_specs=pl.BlockSpec((1, rows, d), lambda l, j: (l, 0, j)),
        out_shape=jax.ShapeDtypeStruct((depth, rows, n), F32),
        compiler_params=_cp("arbitrary", "arbitrary"),
        name="ada_mod",
    )(c_all, ada_w, ada_b.reshape(depth, 1, n))


def _norm_mod(x, g, mod, sh_row):
    xn = x * lax.rsqrt(jnp.mean(x * x, axis=-1, keepdims=True) + RMS_EPS) * g
    return xn * (1.0 + mod[sh_row + 1:sh_row + 2]) + mod[sh_row:sh_row + 1]


def _in_proj_kernel(x_ref, g_ref, mod_ref, w_hy, w_rw, w_s5, w_gla, w_gate, o_hy, o_rw, o_s5, o_gla, o_gate):
    xm = _norm_mod(x_ref[0], g_ref[...], mod_ref[0, 0], 0).astype(BF16)
    for w, o in ((w_hy, o_hy), (w_rw, o_rw), (w_s5, o_s5), (w_gla, o_gla), (w_gate, o_gate)):
        o[0] = jnp.dot(xm, w[...], preferred_element_type=F32).astype(o.dtype)


def _mod_index(n_lat_tiles):
    return lambda b, t: (b, t // n_lat_tiles, 0, 0)


def _in_proj(h_all, g, modsel, ws, n_lat_tiles):
    b, t, d = h_all.shape
    nt = t // TOKEN_TILE
    wspec = [pl.BlockSpec(w.shape, lambda b, t: (0, 0), pipeline_mode=pl.Buffered(1)) for w in ws]
    dtypes = [F32, F32, F32, F32, BF16]
    return pl.pallas_call(
        _in_proj_kernel,
        grid=(b, nt),
        in_specs=[pl.BlockSpec((1, TOKEN_TILE, d), lambda b, t: (b, t, 0)),
                  pl.BlockSpec((1, d), lambda b, t: (0, 0)),
                  pl.BlockSpec((1, 1, 6, d), _mod_index(n_lat_tiles))] + wspec,
        out_specs=[pl.BlockSpec((1, TOKEN_TILE, w.shape[1]), lambda b, t: (b, t, 0)) for w in ws],
        out_shape=[jax.ShapeDtypeStruct((b, t, w.shape[1]), dt) for w, dt in zip(ws, dtypes)],
        compiler_params=_cp("arbitrary", "arbitrary"),
        name="in_proj",
    )(h_all, g.reshape(1, d), modsel, *ws)


def _grid_cols(n_lat_rows, rows):
    grid_rows = n_lat_rows // GRID_W
    assert grid_rows % 8 == 0 and rows % grid_rows == 0 and GRID_W % (rows // grid_rows) == 0
    return rows // grid_rows


def _col_major_rows(view_ref, e, cols, width, lo, hi):
    return jnp.concatenate([view_ref[e, :, j * width + lo:j * width + hi] for j in range(cols)], axis=0)


def _hy_filter_kernel(z_ref, w1, b1, w2, b2, w3, b3, dec_ref, o_ref):
    h = jnp.sin(_mm_hp(z_ref[...], w1[...]) + b1[...])
    h = jnp.sin(_mm_hp(h, w2[...]) + b2[...])
    o_ref[...] = (_mm_hp(h, w3[...]) + b3[...]) * dec_ref[...]


def _hy_features(length):
    pos = jnp.arange(length, dtype=F32)
    t = pos / max(length - 1, 1)
    freqs = jnp.linspace(1e-4, HY_BANDS - 1, HY_BANDS, dtype=F32)
    ang = (2.0 * math.pi / length) * pos[:, None] * freqs[None, :]
    z = jnp.concatenate([t[:, None], jnp.cos(ang), -jnp.sin(ang)], axis=-1)
    rates = jnp.abs(jnp.linspace(math.log(HY_DECAY_TARGET) / HY_LONG_PCT,
                                 math.log(HY_DECAY_TARGET) / HY_SHORT_PCT, MIX_W, dtype=F32))
    rates = jnp.concatenate([rates, rates])
    decay = jnp.exp(-t[:, None] * rates[None, :])
    return jnp.pad(z, ((0, 0), (0, 128 - z.shape[1]))), decay


def _hy_filter(length, fw1, fb1, fw2, fb2, fw3, fb3):
    z, decay = _hy_features(length)
    w1 = jnp.pad(fw1, ((0, 128 - fw1.shape[0]), (0, 128 - fw1.shape[1])))
    b1 = jnp.pad(fb1, (0, 128 - fb1.shape[0])).reshape(1, 128)
    w2 = jnp.pad(fw2, ((0, 128 - fw2.shape[0]), (0, 128 - fw2.shape[1])))
    b2 = jnp.pad(fb2, (0, 128 - fb2.shape[0])).reshape(1, 128)
    w3 = jnp.pad(fw3, ((0, 128 - fw3.shape[0]), (0, 0)))
    n = fw3.shape[1]
    rows = min(length, 512)
    full = lambda s: pl.BlockSpec(s, lambda i: (0, 0))
    return pl.pallas_call(
        _hy_filter_kernel,
        grid=(length // rows,),
        in_specs=[pl.BlockSpec((rows, 128), lambda i: (i, 0)), full((128, 128)), full((1, 128)), full((128, 128)),
                  full((1, 128)), full((128, n)), full((1, n)), pl.BlockSpec((rows, n), lambda i: (i, 0))],
        out_specs=pl.BlockSpec((rows, n), lambda i: (i, 0)),
        out_shape=jax.ShapeDtypeStruct((length, n), F32),
        compiler_params=_cp("arbitrary"),
        name="hy_filter",
    )(z, w1, b1, w2, b2, w3, fb3.reshape(1, n), decay)


def _dft_tables(length):
    n = 2 * length
    k = jnp.arange(length, dtype=jnp.int32)
    m = (k[:, None] * k[None, :]) % n
    ang = m.astype(F32) * (2.0 * math.pi / n)
    wc = jnp.cos(ang)
    ws = jnp.sin(ang)
    nyq = jnp.where(k % 2 == 0, 1.0, -1.0).astype(F32)
    ws = ws.at[0].set(nyq)
    return wc.astype(BF16), ws.astype(BF16), wc.T.astype(BF16), ws.T.astype(BF16)


def _dft_fwd_kernel(wc_ref, ws_ref, xh_ref, xl_ref, oc_ref, os_ref):
    d = functools.partial(jnp.dot, preferred_element_type=F32)
    oc_ref[...] = d(wc_ref[...], xh_ref[...]) + d(wc_ref[...], xl_ref[...])
    os_ref[...] = d(ws_ref[...], xh_ref[...]) + d(ws_ref[...], xl_ref[...])


def _hy_spectrum(filt, wc, ws):
    length, c2 = filt.shape
    c = c2 // 2
    n = 2 * length
    hb = filt[:, c:].at[0].set(0.0)
    x = jnp.concatenate([filt[:, :c], hb], axis=1)
    xh, xl = _split_bf16(x)
    fr = min(length, 512)
    fc, fs = pl.pallas_call(
        _dft_fwd_kernel,
        grid=(length // fr,),
        in_specs=[pl.BlockSpec((fr, length), lambda i: (i, 0)), pl.BlockSpec((fr, length), lambda i: (i, 0)),
                  pl.BlockSpec((length, c2), lambda i: (0, 0)), pl.BlockSpec((length, c2), lambda i: (0, 0))],
        out_specs=[pl.BlockSpec((fr, c2), lambda i: (i, 0)), pl.BlockSpec((fr, c2), lambda i: (i, 0))],
        out_shape=[jax.ShapeDtypeStruct((length, c2), F32)] * 2,
        compiler_params=_cp("arbitrary"),
        name="hy_spectrum",
    )(wc, ws, xh, xl)
    k_re = fc[:, :c] + fc[:, c:]
    k_im = fs[:, c:] - fs[:, :c]
    k_nyq = fs[0, :c] + fs[0, c:]
    ka = (k_re * (2.0 / n)).at[0].set(k_re[0] / n)
    kb = (k_im * (2.0 / n)).at[0].set(0.0)
    ka2 = ka.at[0].set(k_nyq / n)
    return ka, ka2, kb


def _hyena_kernel(p_ref, cw_ref, cb_ref, bias_ref, wc_ref, ws_ref, wct_ref, wst_ref, ka_ref, ka2_ref, kb_ref,
                  o_ref, z_scr, acc_scr, x0_scr):
    f = pl.program_id(1)
    c = MIX_W

    def conv3(lo):
        p = p_ref[0, :, lo:lo + c]
        zero = jnp.zeros((1, c), F32)
        pm, pp = _shift_rows(p, zero, zero)
        return pm * cw_ref[0:1, lo:lo + c] + p * cw_ref[1:2, lo:lo + c] + pp * cw_ref[2:3, lo:lo + c] + cb_ref[0:1, lo:lo + c]

    @pl.when(f == 0)
    def _():
        zf = conv3(2 * c) * conv3(c)
        z_scr[...] = zf.astype(BF16)
        acc_scr[...] = zf * bias_ref[...]
        x0_scr[...] = conv3(0)

    z = z_scr[...]
    zc = jnp.dot(wc_ref[...], z, preferred_element_type=F32)
    zs = jnp.dot(ws_ref[...], z, preferred_element_type=F32)
    a = zc * ka_ref[...] + zs * kb_ref[...]
    bv = zs * ka2_ref[...] - zc * kb_ref[...]
    acc_scr[...] += (jnp.dot(wct_ref[...], a.astype(BF16), preferred_element_type=F32)
                     + jnp.dot(wst_ref[...], bv.astype(BF16), preferred_element_type=F32))

    @pl.when(f == pl.num_programs(1) - 1)
    def _():
        o_ref[0] = (x0_scr[...] * acc_scr[...]).astype(o_ref.dtype)


def _hyena(p_hy, row_block, length, conv_w, conv_b, bias, tables, spec):
    b, t, c3 = p_hy.shape
    wc, ws, wct, wst = tables
    ka, ka2, kb = spec
    fr = min(length, 256)
    nf = length // fr
    c = MIX_W
    args = [p_hy, conv_w, conv_b.reshape(1, c3), bias.reshape(1, c), wc, ws, wct, wst, ka, ka2, kb]
    in_specs = [pl.BlockSpec((1, length, c3), lambda b, f: (b, row_block, 0), pipeline_mode=pl.Buffered(1)),
                pl.BlockSpec((3, c3), lambda b, f: (0, 0)),
                pl.BlockSpec((1, c3), lambda b, f: (0, 0)),
                pl.BlockSpec((1, c), lambda b, f: (0, 0)),
                pl.BlockSpec((fr, length), lambda b, f: (f, 0)),
                pl.BlockSpec((fr, length), lambda b, f: (f, 0)),
                pl.BlockSpec((length, fr), lambda b, f: (0, f)),
                pl.BlockSpec((length, fr), lambda b, f: (0, f)),
                pl.BlockSpec((fr, c), lambda b, f: (f, 0)),
                pl.BlockSpec((fr, c), lambda b, f: (f, 0)),
                pl.BlockSpec((fr, c), lambda b, f: (f, 0))]
    return pl.pallas_call(
        _hyena_kernel,
        grid=(b, nf),
        in_specs=in_specs,
        out_specs=pl.BlockSpec((1, length, c), lambda b, f: (b, 0, 0)),
        out_shape=jax.ShapeDtypeStruct((b, length, c), BF16),
        scratch_shapes=[pltpu.VMEM((length, c), BF16), pltpu.VMEM((length, c), F32), pltpu.VMEM((length, c), F32)],
        compiler_params=_cp("arbitrary", "arbitrary"),
        name="hyena_%d" % length,
    )(*args)


def _rw_prep_kernel(p_ref, pprev_ref, pnext_ref, mu_ref, w0_ref, wup_ref, a0_ref, aup_ref, gup_ref, kk_ref, ka_ref, rk_ref,
                    r_o, v_o, kap_o, lwf_o, lwb_o, kf_o, kb_o, bf_o, bb_o, g_o, bonus_o, *, n_lat_tiles):
    t = pl.program_id(1)
    nt = pl.num_programs(1)
    p = p_ref[0]
    first = jnp.logical_or(t == 0, t == n_lat_tiles)
    last = jnp.logical_or(t == n_lat_tiles - 1, t == nt - 1)
    prev_row = jnp.where(first, 0.0, pprev_ref[0, 7:8, :])
    next_row = jnp.where(last, 0.0, pnext_ref[0, 0:1, :])
    pm, pp = _shift_rows(p, prev_row, next_row)
    p = p + mu_ref[0:1] * (pm - p) + mu_ref[1:2] * (pp - p)
    c = MIX_W
    r, k, v = p[:, 0:c], p[:, c:2 * c], p[:, 2 * c:3 * c]
    wd, ad, gd = p[:, 3 * c:3 * c + 128], p[:, 3 * c + 128:3 * c + 256], p[:, 3 * c + 256:3 * c + 384]
    bd = _head_blockdiag(c, RW_HEAD_DIM)
    g = _mm(_sigmoid(gd), gup_ref[...])
    kk = k * kk_ref[...]
    nrm = jnp.sqrt(_x_exact01(kk * kk, bd))
    kap = kk / jnp.maximum(nrm, 1e-12)
    logw = -RW_DECAY_SCALE * _sigmoid(w0_ref[...] + _mm_hp(jnp.tanh(wd), wup_ref[...]))
    a = _sigmoid(a0_ref[...] + _mm_hp(ad, aup_ref[...]))
    bonus_s = 0.0
    outs_k, outs_b = (kf_o, kb_o), (bf_o, bb_o)
    for d in range(2):
        a_d = a[:, d * c:(d + 1) * c]
        k_d = k * (1.0 + (a_d - 1.0) * ka_ref[...])
        outs_k[d][0] = k_d.astype(BF16)
        outs_b[d][0] = (a_d * kap).astype(BF16)
        bonus_s = bonus_s + _x_exact01(r * k_d * rk_ref[...], bd)
    r_o[0] = r.astype(BF16)
    v_o[0] = v.astype(BF16)
    kap_o[0] = kap.astype(BF16)
    lwf_o[0] = logw[:, 0:c]
    lwb_o[0] = logw[:, c:2 * c]
    g_o[0] = g
    bonus_o[0] = bonus_s * v


def _blockdiag2(a, b):
    z1 = jnp.zeros((a.shape[0], b.shape[1]), a.dtype)
    z2 = jnp.zeros((b.shape[0], a.shape[1]), a.dtype)
    return jnp.concatenate([jnp.concatenate([a, z1], 1), jnp.concatenate([z2, b], 1)], 0)


def _rw_prep(p_rw, n_lat_tiles, mu, w0, w_up, a0, a_up, g_up, k_k, k_a, r_k):
    b, t, n = p_rw.shape
    nt = t // TOKEN_TILE
    c = MIX_W
    sub = TOKEN_TILE // 8
    n8 = t // 8
    consts = [mu, w0.reshape(1, 2 * c), _blockdiag2(w_up[0], w_up[1]), a0.reshape(1, 2 * c), _blockdiag2(a_up[0], a_up[1]),
              g_up, k_k.reshape(1, c), k_a.reshape(1, c), r_k.reshape(1, c)]
    cspecs = [pl.BlockSpec(x.shape, lambda b, t: (0, 0)) for x in consts]
    tile = lambda w: pl.BlockSpec((1, TOKEN_TILE, w), lambda b, t: (b, t, 0))
    return pl.pallas_call(
        functools.partial(_rw_prep_kernel, n_lat_tiles=n_lat_tiles),
        grid=(b, nt),
        in_specs=[tile(n),
                  pl.BlockSpec((1, 8, n), lambda b, t: (b, jnp.maximum(t * sub - 1, 0), 0)),
                  pl.BlockSpec((1, 8, n), lambda b, t: (b, jnp.minimum((t + 1) * sub, n8 - 1), 0))] + cspecs,
        out_specs=[tile(c)] * 11,
        out_shape=[jax.ShapeDtypeStruct((b, t, c), dt) for dt in (BF16, BF16, BF16, F32, F32, BF16, BF16, BF16, BF16, F32, F32)],
        compiler_params=_cp("arbitrary", "arbitrary"),
        name="rw_prep",
    )(p_rw, p_rw, p_rw, *consts)


def _rw_scan_kernel(rf, vf, kapf, lwf, kf, bbf, rb, vb, kapb, lwb, kb, bbb, of_ref, ob_ref, s_scr):
    @pl.when(pl.program_id(1) == 0)
    def _():
        s_scr[...] = jnp.zeros_like(s_scr)

    c = rf.shape[1]
    pw_ = 2 * RW_HEAD_DIM
    n_pairs = MIX_W // pw_
    lane_a = lax.broadcasted_iota(jnp.int32, (c, pw_), 1) < RW_HEAD_DIM
    lane_a2 = lax.broadcasted_iota(jnp.int32, (2 * c, pw_), 1) < RW_HEAD_DIM
    blk = ((lax.broadcasted_iota(jnp.int32, (pw_, pw_), 0) < RW_HEAD_DIM)
           == (lax.broadcasted_iota(jnp.int32, (pw_, pw_), 1) < RW_HEAD_DIM))

    def sel(x):
        return jnp.where(lane_a, x[:c], x[c:])

    def bf(x):
        return x.astype(BF16)

    nb = rf.shape[0]
    chains = []
    for e in range(nb):
        for d, (r, v, kap, lw, k, b) in enumerate(((rf, vf, kapf, lwf, kf, bbf), (rb, vb, kapb, lwb, kb, bbb))):
            rev = d == 1
            incl = _tri(c, rev, False)
            strict = _tri(c, rev, True)
            logw = lw[e]
            cum = _mm_exact01(incl.astype(F32), logw)
            tot = cum[0:1] if rev else cum[c - 1:c]
            ginv = jnp.exp(-cum)
            gt = jnp.exp(tot - cum)
            rt = r[e] * jnp.exp(cum)
            kt = kap[e] * jnp.exp(cum - logw)
            kd, bd = k[e] * ginv, b[e] * ginv
            kg, bg = k[e] * gt, b[e] * gt
            dec = jnp.exp(tot)
            vv = v[e]
            for p in range(n_pairs):
                sl = slice(p * pw_, (p + 1) * pw_)
                lhs2 = jnp.concatenate([rt[:, sl], kt[:, sl]], 0)
                chains.append(dict(
                    e=e, d=d, p=p, idx=(e * 2 + d) * n_pairs + p, incl=incl, strict=strict,
                    lhs2=bf(lhs2),
                    lhs4=bf(jnp.concatenate([jnp.where(lane_a2, lhs2, 0.0), jnp.where(lane_a2, 0.0, lhs2)], 0)),
                    kd=bf(kd[:, sl]), bd=bf(bd[:, sl]), v=bf(vv[:, sl]),
                    kgbg=bf(jnp.concatenate([kg[:, sl], bg[:, sl]], 0)), dec=dec[:, sl]))

    dot = functools.partial(lax.dot_general, preferred_element_type=F32)
    nn = (((1,), (0,)), ((), ()))
    for ch in chains:
        ch["s"] = s_scr[ch["idx"]]
    for ch in chains:
        ch["a1k"] = dot(ch["lhs4"], ch["kd"], NT_DIMS)
        ch["a1b"] = dot(ch["lhs4"], ch["bd"], NT_DIMS)
        ch["a2"] = dot(ch["lhs2"], bf(ch["s"]), NT_DIMS)
    for ch in chains:
        a1k, a1b, st = ch["a1k"], ch["a1b"], ch["strict"]
        lk = jnp.where(jnp.concatenate([st, st], 0), jnp.concatenate([a1k[c:2 * c], a1k[3 * c:]], 0), 0.0)
        ch["lb_a"] = bf(jnp.where(st, a1b[c:2 * c], 0.0))
        ch["lb_b"] = bf(jnp.where(st, a1b[3 * c:], 0.0))
        ch["u"] = ch["a2"][c:] + sel(dot(bf(lk), ch["v"], nn))
    for ch in chains:
        ch["u"] = ch["u"] - sel(dot(jnp.concatenate([ch["lb_a"], ch["lb_b"]], 0), bf(ch["u"]), nn))
        ch["p_a"] = bf(dot(ch["lb_a"], ch["lb_a"], nn))
        ch["p_b"] = bf(dot(ch["lb_b"], ch["lb_b"], nn))
    n = int(math.log2(c)) - 1
    for i in range(n):
        for ch in chains:
            ch["u"] = ch["u"] + sel(dot(jnp.concatenate([ch["p_a"], ch["p_b"]], 0), bf(ch["u"]), nn))
            if i < n - 1:
                ch["p_a"] = bf(dot(ch["p_a"], ch["p_a"], nn))
                ch["p_b"] = bf(dot(ch["p_b"], ch["p_b"], nn))
    outs = [[[None] * n_pairs, [None] * n_pairs] for _ in range(nb)]
    for ch in chains:
        a1k, a1b, inc = ch["a1k"], ch["a1b"], ch["incl"]
        inc2 = jnp.concatenate([inc, inc], 0)
        rk = jnp.where(inc2, jnp.concatenate([a1k[:c], a1k[2 * c:3 * c]], 0), 0.0)
        rb_ = jnp.where(inc2, jnp.concatenate([a1b[:c], a1b[2 * c:3 * c]], 0), 0.0)
        ub = bf(ch["u"])
        o = ch["a2"][:c] + sel(dot(bf(rk), ch["v"], nn)) - sel(dot(bf(rb_), ub, nn))
        outs[ch["e"]][ch["d"]][ch["p"]] = o
        upd = dot(jnp.concatenate([ch["v"], -ub], 0), ch["kgbg"], TN_DIMS)
        s_scr[ch["idx"]] = ch["s"] * ch["dec"] + jnp.where(blk, upd, 0.0)
    for e in range(nb):
        of_ref[e] = jnp.concatenate(outs[e][0], axis=1)
        ob_ref[e] = jnp.concatenate(outs[e][1], axis=1)


def _chunk_orders(n_lat, n_ctx):
    n = n_lat + n_ctx
    fwd = lambda b, s: (b, (s + n_lat) % n, 0)
    rev = lambda b, s: (b, n - 1 - s, 0)
    return fwd, rev


def _rw_scan(r, v, kap, lwf, lwb, kf, kb, bf, bb, n_lat_rows):
    b, t, c = r.shape
    ch = SEQ_CHUNK
    n = t // ch
    fwd, rev = _chunk_orders(n_lat_rows // ch, (t - n_lat_rows) // ch)
    nb = SCAN_BATCH
    blk = lambda im: pl.BlockSpec((nb, ch, c), im)
    return pl.pallas_call(
        _rw_scan_kernel,
        grid=(b // nb, n),
        in_specs=[blk(fwd)] * 6 + [blk(rev)] * 6,
        out_specs=[blk(fwd), blk(rev)],
        out_shape=[jax.ShapeDtypeStruct((b, t, c), F32)] * 2,
        scratch_shapes=[pltpu.VMEM((nb * RW_HEADS, 2 * RW_HEAD_DIM, 2 * RW_HEAD_DIM), F32)],
        compiler_params=_cp("arbitrary", "arbitrary"),
        name="rw_scan",
    )(r, v, kap, lwf, kf, bf, r, v, kap, lwb, kb, bb)


def _rw_readout(of, ob, bonus, g, ln_g, ln_b):
    o = of + ob
    bd = _head_blockdiag(MIX_W, RW_HEAD_DIM)
    mean = _x_exact01(o, bd) * (1.0 / RW_HEAD_DIM)
    oc = o - mean
    var = _x_exact01(oc * oc, bd) * (1.0 / RW_HEAD_DIM)
    on = oc * lax.rsqrt(var + RW_GN_EPS) * ln_g + ln_b
    return (on + bonus) * g


def _s5_matrices(a_re, a_im, log_dt, b_re, b_im, c_re, c_im):
    tc, g, n, p = S5_CHUNK, S5_GROUPS, S5_STATE, S5_GROUP
    hp = lax.Precision.HIGHEST
    dt = jnp.exp(log_dt)[:, :, None]
    lam_re, lam_im = a_re * dt, a_im * dt
    mag1 = jnp.exp(lam_re)
    ab_re, ab_im = mag1 * jnp.cos(lam_im), mag1 * jnp.sin(lam_im)
    nr, ni = ab_re - 1.0, ab_im
    den = a_re * a_re + a_im * a_im
    f_re, f_im = (nr * a_re + ni * a_im) / den, (ni * a_re - nr * a_im) / den
    tau = jnp.arange(tc + 1, dtype=F32)[:, None, None, None]
    pw_mag = jnp.exp(lam_re[None] * tau)
    pw_re, pw_im = pw_mag * jnp.cos(lam_im[None] * tau), pw_mag * jnp.sin(lam_im[None] * tau)
    fb_re = f_re[..., None] * b_re[None] - f_im[..., None] * b_im[None]
    fb_im = f_re[..., None] * b_im[None] + f_im[..., None] * b_re[None]
    e_re = pw_re[..., None] * fb_re[None] - pw_im[..., None] * fb_im[None]
    e_im = pw_re[..., None] * fb_im[None] + pw_im[..., None] * fb_re[None]
    m = (jnp.einsum('gpn,tdgnq->tdgpq', c_re, e_re[:tc], precision=hp)
         - jnp.einsum('gpn,tdgnq->tdgpq', c_im, e_im[:tc], precision=hp))
    i = jnp.arange(tc)
    lag = i[None, :] - i[:, None]
    mf = jnp.where((lag >= 0)[:, :, None, None, None], m[jnp.clip(lag, 0, tc - 1), 0], 0.0)
    mb = jnp.where((lag <= 0)[:, :, None, None, None], m[jnp.clip(-lag, 0, tc - 1), 1], 0.0)
    w = jnp.transpose(mf + mb, (2, 0, 4, 1, 3)).reshape(g, tc * p, tc * p)
    idx_f = tc - 1 - i
    pf_re, pf_im = e_re[idx_f, 0], e_im[idx_f, 0]
    pb_re, pb_im = e_re[i, 1], e_im[i, 1]
    pm = jnp.concatenate([pf_re, pb_re, pf_im, pb_im], axis=2)
    pm = jnp.transpose(pm, (1, 0, 3, 2)).reshape(g, tc * p, 4 * n)
    cf_re = c_re[None, :, :, :] * pw_re[i + 1, 0][:, :, None, :] - c_im[None] * pw_im[i + 1, 0][:, :, None, :]
    cf_im = c_re[None] * pw_im[i + 1, 0][:, :, None, :] + c_im[None] * pw_re[i + 1, 0][:, :, None, :]
    cb_re = c_re[None] * pw_re[tc - i, 1][:, :, None, :] - c_im[None] * pw_im[tc - i, 1][:, :, None, :]
    cb_im = c_re[None] * pw_im[tc - i, 1][:, :, None, :] + c_im[None] * pw_re[tc - i, 1][:, :, None, :]
    q = jnp.concatenate([cf_re, cb_re, -cf_im, -cb_im], axis=3)
    q = jnp.transpose(q, (1, 3, 0, 2)).reshape(g, 4 * n, tc * p)
    dec_re = jnp.concatenate([pw_re[tc, 0], pw_re[tc, 1]], axis=1)
    dec_im = jnp.concatenate([pw_im[tc, 0], pw_im[tc, 1]], axis=1)
    dec = jnp.stack([dec_re, dec_im], axis=1)
    return w.astype(BF16), pm.astype(BF16), q.astype(BF16), dec


S5_PACK_BATCH = 2
S5_LANE_GROUPS = 128 // S5_GROUP


def _s5_pack_kernel(x_ref, o_ref):
    rows = o_ref.shape[0]
    grp = lax.broadcasted_iota(jnp.int32, (rows, 128), 1) // S5_GROUP
    tiles = [jnp.zeros((rows, 128), F32) for _ in range(2 * S5_LANE_GROUPS)]
    for j in range(S5_CHUNK):
        xj = x_ref[pl.ds(j, rows, stride=S5_CHUNK), :]
        jl, jh = j % S5_LANE_GROUPS, j // S5_LANE_GROUPS
        for g8 in range(S5_LANE_GROUPS):
            sh = ((jl - g8) * S5_GROUP) % 128
            moved = pltpu.roll(xj, sh, 1) if sh else xj
            tiles[2 * g8 + jh] = jnp.where(grp == jl, moved, tiles[2 * g8 + jh])
    o_ref[...] = jnp.concatenate(tiles, axis=1).astype(o_ref.dtype)


def _s5_unpack_kernel(y_ref, o_ref):
    rows = y_ref.shape[0]
    grp = lax.broadcasted_iota(jnp.int32, (rows, 128), 1) // S5_GROUP
    for i in range(S5_CHUNK):
        il, ih = i % S5_LANE_GROUPS, i // S5_LANE_GROUPS
        out = jnp.zeros((rows, 128), F32)
        for g8 in range(S5_LANE_GROUPS):
            src = y_ref[:, (2 * g8 + ih) * 128:(2 * g8 + ih + 1) * 128]
            sh = ((g8 - il) * S5_GROUP) % 128
            moved = pltpu.roll(src, sh, 1) if sh else src
            out = jnp.where(grp == g8, moved, out)
        o_ref[pl.ds(i, rows, stride=S5_CHUNK), :] = out


def _s5_kernel(u_ref, w_ref, p_ref, q_ref, dec_ref, y_ref, inc_re, inc_im, sf_re, sf_im, sb_re, sb_im, *, n_lat, n_ctx, bsz):
    u = u_ref[...]
    inc = jnp.dot(u, p_ref[0], preferred_element_type=F32)
    n2 = 2 * S5_STATE
    inc_re[...] = inc[:, :n2]
    inc_im[...] = inc[:, n2:]
    n = n_lat + n_ctx
    d_re = dec_ref[0, 0:1, :]
    d_im = dec_ref[0, 1:2, :]
    is_f = lax.broadcasted_iota(jnp.int32, (bsz, n2), 1) < S5_STATE

    def body(s, carry):
        s_re, s_im = carry
        rf = pl.ds((s + n_lat) % n, bsz, stride=n)
        rb = pl.ds(n - 1 - s, bsz, stride=n)
        sf_re[rf, :] = s_re
        sf_im[rf, :] = s_im
        sb_re[rb, :] = s_re
        sb_im[rb, :] = s_im
        i_re = jnp.where(is_f, inc_re[rf, :], inc_re[rb, :])
        i_im = jnp.where(is_f, inc_im[rf, :], inc_im[rb, :])
        return (d_re * s_re - d_im * s_im + i_re, d_re * s_im + d_im * s_re + i_im)

    zero = jnp.zeros((bsz, n2), F32)
    lax.fori_loop(0, n, body, (zero, zero))
    is_f_all = lax.broadcasted_iota(jnp.int32, sf_re.shape, 1) < S5_STATE
    s_all = jnp.concatenate([jnp.where(is_f_all, sf_re[...], sb_re[...]),
                             jnp.where(is_f_all, sf_im[...], sb_im[...])], axis=1).astype(BF16)
    y_ref[...] = jnp.dot(u, w_ref[0], preferred_element_type=F32) + jnp.dot(s_all, q_ref[0], preferred_element_type=F32)


def _s5_core(p_s5, n_lat_rows, mats):
    b, t, c = p_s5.shape
    tc, g = S5_CHUNK, S5_GROUPS
    nch = t // tc
    w, pm, q, dec = mats
    k = tc * S5_GROUP
    nbb = S5_PACK_BATCH
    n_tiles = c // 128
    tok_blk = pl.BlockSpec((nbb * t, 128), lambda i, o: (i, o))
    chk_blk = pl.BlockSpec((nbb * nch, S5_LANE_GROUPS * k), lambda i, o: (i, o))
    u = pl.pallas_call(
        _s5_pack_kernel,
        grid=(b // nbb, n_tiles),
        in_specs=[tok_blk],
        out_specs=chk_blk,
        out_shape=jax.ShapeDtypeStruct((b * nch, g * k), BF16),
        compiler_params=_cp("arbitrary", "arbitrary"),
        name="s5_pack",
    )(p_s5.reshape(b * t, c))
    rows = nch * b
    y = pl.pallas_call(
        functools.partial(_s5_kernel, n_lat=n_lat_rows // tc, n_ctx=(t - n_lat_rows) // tc, bsz=b),
        grid=(g,),
        in_specs=[pl.BlockSpec((rows, k), lambda i: (0, i)),
                  pl.BlockSpec((1, k, k), lambda i: (i, 0, 0)),
                  pl.BlockSpec((1, k, 4 * S5_STATE), lambda i: (i, 0, 0)),
                  pl.BlockSpec((1, 4 * S5_STATE, k), lambda i: (i, 0, 0)),
                  pl.BlockSpec((1, 2, 2 * S5_STATE), lambda i: (i, 0, 0))],
        out_specs=pl.BlockSpec((rows, k), lambda i: (0, i)),
        out_shape=jax.ShapeDtypeStruct((rows, g * k), F32),
        scratch_shapes=[pltpu.VMEM((rows, 2 * S5_STATE), F32)] * 6,
        compiler_params=_cp("arbitrary"),
        name="s5_core",
    )(u, w, pm, q, dec)
    out = pl.pallas_call(
        _s5_unpack_kernel,
        grid=(b // nbb, n_tiles),
        in_specs=[chk_blk],
        out_specs=tok_blk,
        out_shape=jax.ShapeDtypeStruct((b * t, c), F32),
        compiler_params=_cp("arbitrary", "arbitrary"),
        name="s5_unpack",
    )(y)
    return out.reshape(b, t, c)


def _s5_output(y, u, d_skip, glu_w, glu_b):
    z = y + d_skip * u
    z = 0.5 * z * (1.0 + jnp.tanh(math.sqrt(2.0 / math.pi) * (z + 0.044715 * (z * z * z))))
    return z * _sigmoid(_mm(z, glu_w) + glu_b)


def _gla_scan_kernel(pfv_ref, pfc_ref, pbv_ref, pbc_ref, gup_ref, gb_ref, of_ref, ob_ref, s_scr, *, n_lat, n_ctx, cols):
    step = pl.program_id(1)

    @pl.when(step == 0)
    def _():
        s_scr[...] = jnp.zeros_like(s_scr)

    c = pfc_ref.shape[1]
    width = pfc_ref.shape[2]
    n = n_lat + n_ctx
    chunk_ids = ((step + n_lat) % n, n - 1 - step)
    pw_ = 2 * GLA_HK
    n_pairs = GLA_DK // pw_
    lane_a = lax.broadcasted_iota(jnp.int32, (c, pw_), 1) < GLA_HK
    lane_av = lax.broadcasted_iota(jnp.int32, (GLA_HV, pw_), 1) < GLA_HK
    dot = functools.partial(lax.dot_general, preferred_element_type=F32)
    nn = (((1,), (0,)), ((), ()))
    nb = pfc_ref.shape[0]
    chains = []
    for e in range(nb):
        for d, (v_ref, c_ref) in enumerate(((pfv_ref, pfc_ref), (pbv_ref, pbc_ref))):
            rev = d == 1
            incl = _tri(c, rev, False)
            pe = jnp.where(chunk_ids[d] < n_lat, _col_major_rows(v_ref, e, cols, width, 0, width), c_ref[e])
            gd = pe[:, 2 * GLA_DK + 2 * GLA_DV:]
            x = _mm_hp(gd, gup_ref[...])[:, d * GLA_DK:(d + 1) * GLA_DK] + gb_ref[0:1, d * GLA_DK:(d + 1) * GLA_DK]
            logg = (jnp.minimum(x, 0.0) - jnp.log(1.0 + jnp.exp(-jnp.abs(x)))) * (1.0 / GLA_GATE_NORM)
            cum = _mm_exact01(incl.astype(F32), logg)
            mid = c // 2
            ref = cum[mid:mid + 1] if rev else cum[mid - 1:mid]
            tot = cum[0:1] if rev else cum[c - 1:c]
            q = pe[:, 0:GLA_DK] * (GLA_HK ** -0.5)
            k = pe[:, GLA_DK:2 * GLA_DK]
            qs = q * jnp.exp(cum - ref)
            ks = k * jnp.exp(ref - cum)
            qc = q * jnp.exp(cum)
            kg = k * jnp.exp(tot - cum)
            dec = jnp.exp(tot)
            for p in range(n_pairs):
                sl = slice(p * pw_, (p + 1) * pw_)
                mask2 = lambda a: jnp.concatenate([jnp.where(lane_a, a, 0.0), jnp.where(lane_a, 0.0, a)], 0).astype(BF16)
                chains.append(dict(
                    e=e, d=d, p=p, idx=(e * 2 + d) * n_pairs + p, incl=incl, qs4=mask2(qs[:, sl]), qc4=mask2(qc[:, sl]),
                    ks=ks[:, sl].astype(BF16), kg=kg[:, sl].astype(BF16), dec=dec[:, sl],
                    va=pe[:, 2 * GLA_DK + 2 * p * GLA_HV:2 * GLA_DK + (2 * p + 1) * GLA_HV].astype(BF16),
                    vb=pe[:, 2 * GLA_DK + (2 * p + 1) * GLA_HV:2 * GLA_DK + (2 * p + 2) * GLA_HV].astype(BF16)))
    for ch in chains:
        ch["s"] = s_scr[ch["idx"]]
    for ch in chains:
        ch["sc"] = dot(ch["qs4"], ch["ks"], NT_DIMS)
        ch["st"] = dot(ch["qc4"], ch["s"].astype(BF16), NT_DIMS)
    outs = [[[None] * GLA_HEADS, [None] * GLA_HEADS] for _ in range(nb)]
    for ch in chains:
        e, d, p = ch["e"], ch["d"], ch["p"]
        sa = jnp.where(ch["incl"], ch["sc"][:c], 0.0).astype(BF16)
        sb = jnp.where(ch["incl"], ch["sc"][c:], 0.0).astype(BF16)
        outs[e][d][2 * p] = ch["st"][:c] + dot(sa, ch["va"], nn)
        outs[e][d][2 * p + 1] = ch["st"][c:] + dot(sb, ch["vb"], nn)
        upd = jnp.where(lane_av, dot(ch["va"], ch["kg"], TN_DIMS), dot(ch["vb"], ch["kg"], TN_DIMS))
        s_scr[ch["idx"]] = ch["s"] * ch["dec"] + upd
    for e in range(nb):
        of_ref[e] = jnp.concatenate(outs[e][0], axis=1)
        ob_ref[e] = jnp.concatenate(outs[e][1], axis=1)


def _gla_scan(p_gla, n_lat_rows, gate_up, gate_b):
    b, t, n = p_gla.shape
    ch = SEQ_CHUNK
    n_lat, n_ctx = n_lat_rows // ch, (t - n_lat_rows) // ch
    n_ch = n_lat + n_ctx
    fwd, rev = _chunk_orders(n_lat, n_ctx)
    cols = _grid_cols(n_lat_rows, ch)
    grid_rows = n_lat_rows // GRID_W
    gup = jnp.zeros((128, 2 * GLA_DK), F32)
    gup = gup.at[0:GLA_GATE_RANK, 0:GLA_DK].set(gate_up[0]).at[GLA_GATE_RANK:2 * GLA_GATE_RANK, GLA_DK:].set(gate_up[1])
    nb = SCAN_BATCH
    p_view = p_gla.reshape(b, t // GRID_W, GRID_W * n)
    view_blk = lambda cid: pl.BlockSpec((nb, grid_rows, cols * n), lambda b, s: (b, 0, jnp.minimum(cid(s), n_lat - 1)))
    ctx_blk = lambda cid: pl.BlockSpec((nb, ch, n), lambda b, s: (b, jnp.maximum(cid(s), n_lat), 0))
    cid_f = lambda s: (s + n_lat) % n_ch
    cid_b = lambda s: n_ch - 1 - s
    return pl.pallas_call(
        functools.partial(_gla_scan_kernel, n_lat=n_lat, n_ctx=n_ctx, cols=cols),
        grid=(b // nb, n_ch),
        in_specs=[view_blk(cid_f), ctx_blk(cid_f), view_blk(cid_b), ctx_blk(cid_b),
                  pl.BlockSpec((128, 2 * GLA_DK), lambda b, s: (0, 0)),
                  pl.BlockSpec((1, 2 * GLA_DK), lambda b, s: (0, 0))],
        out_specs=[pl.BlockSpec((nb, ch, GLA_DV), fwd), pl.BlockSpec((nb, ch, GLA_DV), rev)],
        out_shape=[jax.ShapeDtypeStruct((b, t, GLA_DV), F32)] * 2,
        scratch_shapes=[pltpu.VMEM((nb * GLA_HEADS, GLA_HV, 2 * GLA_HK), F32)],
        compiler_params=_cp("arbitrary", "arbitrary"),
        name="gla_scan",
    )(p_view, p_gla, p_view, p_gla, gup, gate_b.reshape(1, 2 * GLA_DK))


def _gla_readout_kernel(of_ref, ob_ref, gv_ref, gc_ref, ng_ref, ylat_ref, yctx_ref, *, n_lat_tiles, cols, width):
    t = pl.program_id(1)
    o = of_ref[0] + ob_ref[0]
    bd = _head_blockdiag(GLA_DV, GLA_HV)
    ms = _x_exact01(o * o, bd) * (1.0 / GLA_HV)
    g_lo = 2 * GLA_DK + GLA_DV
    g = jnp.where(t < n_lat_tiles, _col_major_rows(gv_ref, 0, cols, width, g_lo, g_lo + GLA_DV), gc_ref[0])
    y = (o * lax.rsqrt(ms + RMS_EPS) * ng_ref[...] * _silu(g)).astype(ylat_ref.dtype)
    rows = y.shape[0] // cols

    @pl.when(t < n_lat_tiles)
    def _():
        ylat_ref[0] = jnp.concatenate([y[j * rows:(j + 1) * rows] for j in range(cols)], axis=1)

    @pl.when(t >= n_lat_tiles)
    def _():
        yctx_ref[0] = y


def _gla_readout(of, ob, p_gla, norm_g, n_lat_rows):
    b, t, c = of.shape
    n = p_gla.shape[2]
    n_lat_tiles = n_lat_rows // TOKEN_TILE
    grid_rows = n_lat_rows // GRID_W
    cols = _grid_cols(n_lat_rows, TOKEN_TILE)
    tile = pl.BlockSpec((1, TOKEN_TILE, c), lambda b, t: (b, t, 0))
    p_view = p_gla.reshape(b, t // GRID_W, GRID_W * n)
    y_lat, y_ctx = pl.pallas_call(
        functools.partial(_gla_readout_kernel, n_lat_tiles=n_lat_tiles, cols=cols, width=n),
        grid=(b, t // TOKEN_TILE),
        in_specs=[tile, tile,
                  pl.BlockSpec((1, grid_rows, cols * n), lambda b, t: (b, 0, jnp.minimum(t, n_lat_tiles - 1))),
                  pl.BlockSpec((1, TOKEN_TILE, c), lambda b, t: (b, jnp.maximum(t, n_lat_tiles), (2 * GLA_DK + GLA_DV) // c)),
                  pl.BlockSpec((1, c), lambda b, t: (0, 0))],
        out_specs=[pl.BlockSpec((1, grid_rows, cols * c), lambda b, t: (b, 0, jnp.minimum(t, n_lat_tiles - 1))),
                   pl.BlockSpec((1, TOKEN_TILE, c), lambda b, t: (b, 0, 0))],
        out_shape=[jax.ShapeDtypeStruct((b, grid_rows, GRID_W * c), BF16), jax.ShapeDtypeStruct((b, t - n_lat_rows, c), BF16)],
        compiler_params=_cp("arbitrary", "arbitrary"),
        name="gla_readout",
    )(of, ob, p_view, p_gla, jnp.tile(norm_g, GLA_HEADS).reshape(1, c))
    return y_lat.reshape(b, n_lat_rows, c), y_ctx


def _merge_kernel(h_ref, mod_ref, gate_ref, yhyl_ref, yhyc_ref, rof_ref, rob_ref, rbonus_ref, rg_ref, ys5_ref, us5_ref,
                  yglal_ref, yglac_ref, lng_ref, lnb_ref, dskip_ref, gluw_ref, glub_ref, wbr_ref, wout_ref, o_ref,
                  *, n_lat_tiles):
    d = D_MODEL
    is_lat = pl.program_id(1) < n_lat_tiles
    y_hy = jnp.where(is_lat, yhyl_ref[0], yhyc_ref[0])
    y_gla = jnp.where(is_lat, yglal_ref[0], yglac_ref[0])
    y_rw = _rw_readout(rof_ref[0], rob_ref[0], rbonus_ref[0], rg_ref[0], lng_ref[...], lnb_ref[...])
    y_s5 = _s5_output(ys5_ref[0], us5_ref[0], dskip_ref[...], gluw_ref[...], glub_ref[...])
    acc = jnp.zeros((h_ref.shape[1], d), F32)
    for m, y in enumerate((y_hy, y_rw, y_s5, y_gla)):
        acc = acc + (_sigmoid(gate_ref[0, :, m * d:(m + 1) * d].astype(F32))
                     * jnp.dot(y.astype(BF16), wbr_ref[m], preferred_element_type=F32))
    out = jnp.dot(acc.astype(BF16), wout_ref[...], preferred_element_type=F32)
    o_ref[0] = h_ref[0] + mod_ref[0, 0, 2:3, :] * out


def _merge(h_all, modsel, n_tiles, n_lat_tiles, gate, y_hy_lat, y_hy_ctx, rof, rob, rbonus, rg, y_s5, u_s5, y_gla_lat, y_gla_ctx,
           ln_g, ln_b, d_skip, glu_w, glu_b, w_branch, w_out):
    b, t, d = h_all.shape
    c = MIX_W
    tile = lambda w: pl.BlockSpec((1, TOKEN_TILE, w), lambda b, t: (b, t, 0))
    lat = pl.BlockSpec((1, TOKEN_TILE, c), lambda b, t: (b, jnp.minimum(t, n_lat_tiles - 1), 0))
    ctx = pl.BlockSpec((1, TOKEN_TILE, c), lambda b, t: (b, jnp.maximum(t - n_lat_tiles, 0), 0))
    vec = lambda x: x.reshape(1, -1)
    consts = [vec(ln_g), vec(ln_b), vec(d_skip), glu_w.astype(BF16), vec(glu_b), w_branch.astype(BF16), w_out.astype(BF16)]
    cspecs = [pl.BlockSpec(x.shape, (lambda n: lambda b, t: (0,) * n)(x.ndim)) for x in consts]
    return pl.pallas_call(
        functools.partial(_merge_kernel, n_lat_tiles=n_lat_tiles),
        grid=(b, n_tiles),
        in_specs=[tile(d), pl.BlockSpec((1, 1, 6, d), _mod_index(n_lat_tiles)), tile(4 * d), lat, ctx] + [tile(c)] * 6
                 + [lat, ctx] + cspecs,
        out_specs=tile(d),
        out_shape=jax.ShapeDtypeStruct((b, t, d), F32),
        input_output_aliases={0: 0},
        compiler_params=_cp("arbitrary", "arbitrary"),
        name="merge",
    )(h_all, modsel, gate, y_hy_lat, y_hy_ctx, rof, rob, rbonus, rg, y_s5, u_s5, y_gla_lat, y_gla_ctx, *consts)


def _ffn_kernel(h_ref, g_ref, mod_ref, w1_ref, w3_ref, w2_ref, o_ref):
    h = h_ref[0]
    xm = _norm_mod(h, g_ref[...], mod_ref[0, 0], 3).astype(BF16)
    a = jnp.dot(xm, w1_ref[...], preferred_element_type=F32)
    b = jnp.dot(xm, w3_ref[...], preferred_element_type=F32)
    y = jnp.dot((_silu(a) * b).astype(BF16), w2_ref[...], preferred_element_type=F32)
    o_ref[0] = h + mod_ref[0, 0, 5:6, :] * y


def _ffn_dense(h_all, g, modsel, n_lat_tiles, w1, w3, w2):
    b, t, d = h_all.shape
    tile = pl.BlockSpec((1, TOKEN_TILE, d), lambda b, t: (b, t, 0))
    ws = [w1.astype(BF16), w3.astype(BF16), w2.astype(BF16)]
    wspec = [pl.BlockSpec(w.shape, lambda b, t: (0, 0), pipeline_mode=pl.Buffered(1)) for w in ws]
    return pl.pallas_call(
        _ffn_kernel,
        grid=(b, t // TOKEN_TILE),
        in_specs=[tile, pl.BlockSpec((1, d), lambda b, t: (0, 0)), pl.BlockSpec((1, 1, 6, d), _mod_index(n_lat_tiles))] + wspec,
        out_specs=tile,
        out_shape=jax.ShapeDtypeStruct((b, t, d), F32),
        input_output_aliases={0: 0},
        compiler_params=_cp("arbitrary", "arbitrary"),
        name="ffn_dense",
    )(h_all, g.reshape(1, d), modsel, *ws)


def _router_kernel(h_ref, g_ref, mod_ref, rw_ref, rb_ref, x_ref, logit_ref):
    xm = _norm_mod(h_ref[0], g_ref[...], mod_ref[0, 0], 3)
    x_ref[0] = xm.astype(BF16)
    logit_ref[0] = _mm_hp(xm, rw_ref[...]) + rb_ref[...]


def _router(h_all, g, modsel, n_lat_tiles, router_w, router_b):
    b, t, d = h_all.shape
    n_lat = n_lat_tiles * TOKEN_TILE
    rw = jnp.pad(router_w, ((0, 0), (0, 128 - N_EXPERTS)))
    rb = jnp.pad(router_b, (0, 128 - N_EXPERTS), constant_values=-1e30).reshape(1, 128)
    tile = lambda w: pl.BlockSpec((1, TOKEN_TILE, w), lambda b, t: (b, t, 0))
    return pl.pallas_call(
        _router_kernel,
        grid=(b, n_lat_tiles),
        in_specs=[tile(d), pl.BlockSpec((1, d), lambda b, t: (0, 0)), pl.BlockSpec((1, 1, 6, d), _mod_index(n_lat_tiles)),
                  pl.BlockSpec((d, 128), lambda b, t: (0, 0)), pl.BlockSpec((1, 128), lambda b, t: (0, 0))],
        out_specs=[tile(d), tile(128)],
        out_shape=[jax.ShapeDtypeStruct((b, n_lat, d), BF16), jax.ShapeDtypeStruct((b, n_lat, 128), F32)],
        compiler_params=_cp("arbitrary", "arbitrary"),
        name="moe_router",
    )(h_all, g.reshape(1, d), modsel, rw, rb)


def _moe_expert_kernel(be_ref, x_ref, w1_ref, w3_ref, w2_ref, o_ref):
    x = x_ref[...]
    a = jnp.dot(x, w1_ref[0], preferred_element_type=F32)
    b = jnp.dot(x, w3_ref[0], preferred_element_type=F32)
    o_ref[...] = jnp.dot((_silu(a) * b).astype(BF16), w2_ref[0], preferred_element_type=F32).astype(o_ref.dtype)


def _moe_experts(xb, block_exp, w1, w3, w2):
    rows, d = xb.shape
    nb = rows // MOE_BLOCK_ROWS
    f = w1.shape[2]
    grid_spec = pltpu.PrefetchScalarGridSpec(
        num_scalar_prefetch=1,
        grid=(nb,),
        in_specs=[pl.BlockSpec((MOE_BLOCK_ROWS, d), lambda i, be: (i, 0)),
                  pl.BlockSpec((1, d, f), lambda i, be: (be[i], 0, 0), pipeline_mode=pl.Buffered(1)),
                  pl.BlockSpec((1, d, f), lambda i, be: (be[i], 0, 0), pipeline_mode=pl.Buffered(1)),
                  pl.BlockSpec((1, f, d), lambda i, be: (be[i], 0, 0), pipeline_mode=pl.Buffered(1))],
        out_specs=pl.BlockSpec((MOE_BLOCK_ROWS, d), lambda i, be: (i, 0)),
    )
    return pl.pallas_call(
        _moe_expert_kernel,
        grid_spec=grid_spec,
        out_shape=jax.ShapeDtypeStruct((rows, d), BF16),
        compiler_params=_cp("arbitrary"),
        name="moe_experts",
    )(block_exp, xb, w1, w3, w2)


def _moe_combine_kernel(h_ref, mod_ref, y_ref, gate_ref, fg_ref, o_ref):
    d = h_ref.shape[2]
    y = y_ref[0, :, :d].astype(F32) * gate_ref[0, :, 0:1] + y_ref[0, :, d:].astype(F32) * gate_ref[0, :, 1:2]
    h = h_ref[0] + mod_ref[0, 0, 5:6, :] * y
    o_ref[0] = h * lax.rsqrt(jnp.mean(h * h, axis=-1, keepdims=True) + RMS_EPS) * fg_ref[...]


def _moe_combine(h_all, modsel, n_lat_tiles, y2, gates, final_g):
    b, t, d = h_all.shape
    n_lat = n_lat_tiles * TOKEN_TILE
    tile = lambda w: pl.BlockSpec((1, TOKEN_TILE, w), lambda b, t: (b, t, 0))
    return pl.pallas_call(
        _moe_combine_kernel,
        grid=(b, n_lat_tiles),
        in_specs=[tile(d), pl.BlockSpec((1, 1, 6, d), _mod_index(n_lat_tiles)),
                  tile(TOP_K * d),
                  tile(128), pl.BlockSpec((1, d), lambda b, t: (0, 0))],
        out_specs=tile(d),
        out_shape=jax.ShapeDtypeStruct((b, n_lat, d), F32),
        compiler_params=_cp("arbitrary", "arbitrary"),
        name="moe_combine",
    )(h_all, modsel, y2, gates, final_g.reshape(1, d))


def _moe_ffn_and_final_norm(h_all, g, modsel, n_lat_tiles, router_w, router_b, w1, w3, w2, final_g):
    b, t, d = h_all.shape
    n_lat = n_lat_tiles * TOKEN_TILE
    n_tok = b * n_lat
    xn, logits = _router(h_all, g, modsel, n_lat_tiles, router_w, router_b)
    logits = logits.reshape(n_tok, 128)[:, :N_EXPERTS]
    top_v, top_i = lax.top_k(logits, TOP_K)
    gates = jax.nn.softmax(top_v, axis=-1)
    n_assign = n_tok * TOP_K
    e_flat = top_i.reshape(-1).astype(jnp.int32)
    order = jnp.argsort(e_flat).astype(jnp.int32)
    rank = jnp.argsort(order).astype(jnp.int32)
    counts = jnp.sum((e_flat[:, None] == jnp.arange(N_EXPERTS, dtype=jnp.int32)[None, :]).astype(jnp.int32), axis=0)
    padded = (counts + MOE_BLOCK_ROWS - 1) // MOE_BLOCK_ROWS * MOE_BLOCK_ROWS
    start = jnp.cumsum(counts) - counts
    pend = jnp.cumsum(padded)
    pstart = pend - padded
    n_blocks = n_assign // MOE_BLOCK_ROWS + N_EXPERTS
    n_rows = n_blocks * MOE_BLOCK_ROWS
    row_ids = jnp.arange(n_rows, dtype=jnp.int32)
    row_exp = jnp.minimum(jnp.searchsorted(pend, row_ids, side='right'), N_EXPERTS - 1).astype(jnp.int32)
    row_off = row_ids - pstart[row_exp]
    row_slot = jnp.clip(start[row_exp] + row_off, 0, n_assign - 1)
    row_tok = jnp.where(row_off < counts[row_exp], order[row_slot] // TOP_K, n_tok).astype(jnp.int32)
    block_exp = row_exp[::MOE_BLOCK_ROWS]
    tok_pad = jnp.concatenate([xn.reshape(n_tok, d), jnp.zeros((1, d), BF16)], axis=0)
    xb = tok_pad.at[row_tok].get(mode="promise_in_bounds")
    yb = _moe_experts(xb, block_exp, w1.astype(BF16), w3.astype(BF16), w2.astype(BF16))
    dest_orig = (pstart[e_flat] + rank - start[e_flat]).astype(jnp.int32)
    y2 = yb.at[dest_orig].get(mode="promise_in_bounds").reshape(b, n_lat, TOP_K * d)
    gates_pad = jnp.pad(gates, ((0, 0), (0, 128 - TOP_K))).reshape(b, n_lat, 128)
    return _moe_combine(h_all, modsel, n_lat_tiles, y2, gates_pad, final_g)


def _pad_cols(w, n):
    return jnp.pad(w, ((0, 0), (0, n - w.shape[1])))


def kernel(x, c, ctx, c_ctx, ada_w, ada_b, norm_mix_g, norm_ffn_g, w_in, hy_conv_w, hy_conv_b, hy_f_w1, hy_f_b1, hy_f_w2, hy_f_b2, hy_f_w3, hy_f_b3, hy_bias, rw_mu, rw_w0, rw_w_up, rw_a0, rw_a_up, rw_g_up, rw_k_k, rw_k_a, rw_r_k, rw_ln_g, rw_ln_b, s5_a_re, s5_a_im, s5_log_dt, s5_b_re, s5_b_im, s5_c_re, s5_c_im, s5_d, s5_glu_w, s5_glu_b, gla_gate_up, gla_gate_b, gla_norm_g, w_branch, w_out, ffn_w1, ffn_w3, ffn_w2, moe_router_w, moe_router_b, moe_w1, moe_w3, moe_w2, final_norm_g):
    bsz, n_lat, d = x.shape
    n_ctx = ctx.shape[1]
    depth = w_in.shape[0]
    assert depth == 2 and d == D_MODEL
    assert n_lat % TOKEN_TILE == 0 and n_ctx % TOKEN_TILE == 0 and n_lat % GRID_W == 0
    t_all = n_lat + n_ctx
    n_lat_tiles = n_lat // TOKEN_TILE
    n_tiles = t_all // TOKEN_TILE
    assert n_tiles <= 2 * n_lat_tiles

    rows = (bsz + 1 + 7) // 8 * 8
    c_all = jnp.zeros((rows, d), F32).at[:bsz].set(c.astype(F32)).at[bsz].set(c_ctx.astype(F32))
    mod = _ada_mod(c_all, ada_w, ada_b)
    h_all = jnp.concatenate([x.astype(F32), ctx.astype(F32)], axis=1)

    dft_lat = _dft_tables(n_lat)
    dft_ctx = _dft_tables(n_ctx)
    seg = np.cumsum([0, 3 * MIX_W, RW_IN, MIX_W, 2 * GLA_DK + 2 * GLA_DV + 2 * GLA_GATE_RANK, 4 * d])

    out = None
    for l in range(depth):
        last = l == depth - 1
        mod_l = mod[l, :bsz].reshape(bsz, 6, d)
        mod_c = jnp.broadcast_to(mod[l, bsz].reshape(1, 6, d), (bsz, 6, d))
        modsel = jnp.stack([mod_l, mod_c], axis=1)

        wl = w_in[l].astype(BF16)
        ws = [wl[:, seg[0]:seg[1]], wl[:, seg[1]:seg[2]], wl[:, seg[2]:seg[3]],
              _pad_cols(wl[:, seg[3]:seg[4]], GLA_IN_PAD), wl[:, seg[4]:seg[5]]]
        p_hy, p_rw, p_s5, p_gla, gate = _in_proj(h_all, norm_mix_g[l], modsel, ws, n_lat_tiles)

        filt_args = (hy_f_w1[l], hy_f_b1[l], hy_f_w2[l], hy_f_b2[l], hy_f_w3[l], hy_f_b3[l])
        spec_lat = _hy_spectrum(_hy_filter(n_lat, *filt_args), dft_lat[0], dft_lat[1])
        y_hy_lat = _hyena(p_hy, 0, n_lat, hy_conv_w[l], hy_conv_b[l], hy_bias[l], dft_lat, spec_lat)
        y_hy_ctx = y_hy_lat
        if not last:
            spec_ctx = _hy_spectrum(_hy_filter(n_ctx, *filt_args), dft_ctx[0], dft_ctx[1])
            y_hy_ctx = _hyena(p_hy, n_lat // n_ctx, n_ctx, hy_conv_w[l], hy_conv_b[l], hy_bias[l], dft_ctx, spec_ctx)

        r, v, kap, lwf, lwb, kf, kb, bf, bb, rg, rbonus = _rw_prep(
            p_rw, n_lat_tiles, rw_mu[l], rw_w0[l], rw_w_up[l], rw_a0[l], rw_a_up[l], rw_g_up[l], rw_k_k[l], rw_k_a[l], rw_r_k[l])
        rof, rob = _rw_scan(r, v, kap, lwf, lwb, kf, kb, bf, bb, n_lat)

        y_s5 = _s5_core(p_s5, n_lat, _s5_matrices(s5_a_re[l], s5_a_im[l], s5_log_dt[l], s5_b_re[l], s5_b_im[l], s5_c_re[l], s5_c_im[l]))

        gof, gob = _gla_scan(p_gla, n_lat, gla_gate_up[l], gla_gate_b[l])
        y_gla_lat, y_gla_ctx = _gla_readout(gof, gob, p_gla, gla_norm_g[l], n_lat)

        nt_l = n_lat_tiles if last else n_tiles
        h_all = _merge(h_all, modsel, nt_l, n_lat_tiles, gate, y_hy_lat, y_hy_ctx, rof, rob, rbonus, rg, y_s5, p_s5,
                       y_gla_lat, y_gla_ctx, rw_ln_g[l], rw_ln_b[l], s5_d[l], s5_glu_w[l], s5_glu_b[l], w_branch[l], w_out[l])

        i = l // 2
        if l % 2 == 0:
            h_all = _ffn_dense(h_all, norm_ffn_g[l], modsel, n_lat_tiles, ffn_w1[i], ffn_w3[i], ffn_w2[i])
        else:
            out = _moe_ffn_and_final_norm(h_all, norm_ffn_g[l], modsel, n_lat_tiles, moe_router_w[i], moe_router_b[i],
                                          moe_w1[i], moe_w3[i], moe_w2[i], final_norm_g)
    return out.astype(x.dtype)
```

```python
import functools
import math

import numpy as np
import jax
import jax.numpy as jnp
from jax import lax
from jax.experimental import pallas as pl
from jax.experimental.pallas import tpu as pltpu

F32 = jnp.float32
BF16 = jnp.bfloat16

D_MODEL = 1024
GRID_W = 64
MIX_W = 512
RMS_EPS = 1e-6
HY_BANDS = 16
HY_DECAY_TARGET = 1e-2
HY_SHORT_PCT = 0.3
HY_LONG_PCT = 1.5
RW_HEADS = 8
RW_HEAD_DIM = 64
RW_DECAY_SCALE = 0.606531
RW_GN_EPS = 64e-5
RW_IN = 1920
S5_GROUP = 16
S5_GROUPS = 32
S5_STATE = 64
S5_CHUNK = 16
GLA_HEADS = 4
GLA_HK = 64
GLA_HV = 128
GLA_DK = 256
GLA_DV = 512
GLA_GATE_RANK = 16
GLA_GATE_NORM = 16.0
GLA_IN_PAD = 1664
N_EXPERTS = 8
TOP_K = 2
MOE_BLOCK_ROWS = 256
TOKEN_TILE = 256
SEQ_CHUNK = 64
SCAN_BATCH = 2
VMEM_LIMIT = 56 * 1024 * 1024

NT_DIMS = (((1,), (1,)), ((), ()))
TN_DIMS = (((0,), (0,)), ((), ()))


def _cp(*sem):
    return pltpu.CompilerParams(dimension_semantics=sem, vmem_limit_bytes=VMEM_LIMIT)


def _mm(a, b, dims=None):
    a = a.astype(BF16)
    b = b.astype(BF16)
    if dims is None:
        return jnp.dot(a, b, preferred_element_type=F32)
    return lax.dot_general(a, b, dims, preferred_element_type=F32)


def _split_bf16(x):
    hi = x.astype(BF16)
    lo = (x - hi.astype(F32)).astype(BF16)
    return hi, lo


def _mm_hp(a, b):
    ah, al = _split_bf16(a)
    bh, bl = _split_bf16(b)
    d = functools.partial(jnp.dot, preferred_element_type=F32)
    return d(ah, bh) + (d(ah, bl) + d(al, bh))


def _mm_exact01(m01, x):
    m = m01.astype(BF16)
    hi = x.astype(BF16)
    r1 = x - hi.astype(F32)
    mid = r1.astype(BF16)
    lo = (r1 - mid.astype(F32)).astype(BF16)
    d = functools.partial(jnp.dot, preferred_element_type=F32)
    return d(m, hi) + (d(m, mid) + d(m, lo))


def _x_exact01(x, m01):
    m = m01.astype(BF16)
    hi = x.astype(BF16)
    mid = (x - hi.astype(F32)).astype(BF16)
    d = functools.partial(jnp.dot, preferred_element_type=F32)
    return d(hi, m) + d(mid, m)


def _sigmoid(x):
    return 0.5 * jnp.tanh(0.5 * x) + 0.5


def _silu(x):
    return x * _sigmoid(x)


def _tri(c, rev, strict):
    i = lax.broadcasted_iota(jnp.int32, (c, c), 0)
    j = lax.broadcasted_iota(jnp.int32, (c, c), 1)
    if rev:
        return (j > i) if strict else (j >= i)
    return (j < i) if strict else (j <= i)


def _head_blockdiag(n, head):
    i = lax.broadcasted_iota(jnp.int32, (n, n), 0) // head
    j = lax.broadcasted_iota(jnp.int32, (n, n), 1) // head
    return (i == j).astype(F32)


def _shift_rows(x, prev_row, next_row):
    n = x.shape[0]
    row = lax.broadcasted_iota(jnp.int32, x.shape, 0)
    xm = jnp.where(row == 0, prev_row, pltpu.roll(x, 1, 0))
    xp = jnp.where(row == n - 1, next_row, pltpu.roll(x, n - 1, 0))
    return xm, xp


def _ada_kernel(c_ref, w_ref, b_ref, o_ref):
    o_ref[0] = _mm_hp(_silu(c_ref[...]), w_ref[0]) + b_ref[0]


def _ada_mod(c_all, ada_w, ada_b):
    depth, d, n = ada_w.shape
    rows = c_all.shape[0]
    return pl.pallas_call(
        _ada_kernel,
        grid=(depth, n // d),
        in_specs=[pl.BlockSpec((rows, d), lambda l, j: (0, 0)),
                  pl.BlockSpec((1, d, d), lambda l, j: (l, 0, j)),
                  pl.BlockSpec((1, 1, d), lambda l, j: (l, 0, j))],
        out_specs=pl.BlockSpec((1, rows, d), lambda l, j: (l, 0, j)),
        out_shape=jax.ShapeDtypeStruct((depth, rows, n), F32),
        compiler_params=_cp("arbitrary", "arbitrary"),
        name="ada_mod",
    )(c_all, ada_w, ada_b.reshape(depth, 1, n))


def _norm_mod(x, g, mod, sh_row):
    xn = x * lax.rsqrt(jnp.mean(x * x, axis=-1, keepdims=True) + RMS_EPS) * g
    return xn * (1.0 + mod[sh_row + 1:sh_row + 2]) + mod[sh_row:sh_row + 1]


def _seq_edges(t, n_lat_tiles, n_tiles):
    first = jnp.logical_or(t == 0, t == n_lat_tiles)
    last = jnp.logical_or(t == n_lat_tiles - 1, t == n_tiles - 1)
    return first, last


def _in_proj_kernel(x_ref, xprev_ref, xnext_ref, g_ref, mod_ref, cw_ref, cb_ref, w_hy, w_rw, w_s5, w_gla, w_gate,
                    o_x0, o_z, o_rw, o_s5, o_gla, o_gate, *, n_lat_tiles):
    t = pl.program_id(1)
    first, last = _seq_edges(t, n_lat_tiles, pl.num_programs(1))
    mod = mod_ref[0, 0]
    xm = _norm_mod(x_ref[0], g_ref[...], mod, 0).astype(BF16)
    for w, o in ((w_rw, o_rw), (w_s5, o_s5), (w_gla, o_gla), (w_gate, o_gate)):
        o[0] = jnp.dot(xm, w[...], preferred_element_type=F32).astype(o.dtype)
    p = jnp.dot(xm, w_hy[...], preferred_element_type=F32)
    halo = jnp.concatenate([xprev_ref[0, 7:8, :], xnext_ref[0, 0:1, :], jnp.zeros((6, x_ref.shape[2]), F32)], axis=0)
    ph = jnp.dot(_norm_mod(halo, g_ref[...], mod, 0).astype(BF16), w_hy[...], preferred_element_type=F32)
    pm, pp = _shift_rows(p, jnp.where(first, 0.0, ph[0:1]), jnp.where(last, 0.0, ph[1:2]))
    u = pm * cw_ref[0:1] + p * cw_ref[1:2] + pp * cw_ref[2:3] + cb_ref[...]
    c = MIX_W
    o_x0[0] = u[:, 0:c].astype(o_x0.dtype)
    o_z[0] = (u[:, 2 * c:3 * c] * u[:, c:2 * c]).astype(o_z.dtype)


def _mod_index(n_lat_tiles):
    return lambda b, t: (b, t // n_lat_tiles, 0, 0)


def _halo_specs(width, n_rows):
    sub = TOKEN_TILE // 8
    return [pl.BlockSpec((1, 8, width), lambda b, t: (b, jnp.maximum(t * sub - 1, 0), 0)),
            pl.BlockSpec((1, 8, width), lambda b, t: (b, jnp.minimum((t + 1) * sub, n_rows // 8 - 1), 0))]


def _in_proj(h_all, g, modsel, ws, conv_w, conv_b, n_lat_tiles):
    b, t, d = h_all.shape
    nt = t // TOKEN_TILE
    c = MIX_W
    wspec = [pl.BlockSpec(w.shape, lambda b, t: (0, 0), pipeline_mode=pl.Buffered(1)) for w in ws]
    widths = [c, c] + [w.shape[1] for w in ws[1:]]
    dtypes = [BF16, BF16, F32, F32, F32, BF16]
    return pl.pallas_call(
        functools.partial(_in_proj_kernel, n_lat_tiles=n_lat_tiles),
        grid=(b, nt),
        in_specs=[pl.BlockSpec((1, TOKEN_TILE, d), lambda b, t: (b, t, 0))] + _halo_specs(d, t)
                 + [pl.BlockSpec((1, d), lambda b, t: (0, 0)),
                    pl.BlockSpec((1, 1, 6, d), _mod_index(n_lat_tiles)),
                    pl.BlockSpec((3, 3 * c), lambda b, t: (0, 0)),
                    pl.BlockSpec((1, 3 * c), lambda b, t: (0, 0))] + wspec,
        out_specs=[pl.BlockSpec((1, TOKEN_TILE, w), lambda b, t: (b, t, 0)) for w in widths],
        out_shape=[jax.ShapeDtypeStruct((b, t, w), dt) for w, dt in zip(widths, dtypes)],
        compiler_params=_cp("arbitrary", "arbitrary"),
        name="in_proj",
    )(h_all, h_all, h_all, g.reshape(1, d), modsel, conv_w, conv_b.reshape(1, 3 * c), *ws)


def _grid_cols(n_lat_rows, rows):
    grid_rows = n_lat_rows // GRID_W
    assert grid_rows % 8 == 0 and rows % grid_rows == 0 and GRID_W % (rows // grid_rows) == 0
    return rows // grid_rows


GRID_COL_BLOCK = 8


def _col_major_rows(view_ref, e, j0, cols, lo, hi):
    return jnp.concatenate([view_ref[e, :, pl.ds(j0 + j, 1), lo:hi][:, 0, :] for j in range(cols)], axis=0)


def _hy_filter_kernel(z_ref, w1, b1, w2, b2, w3, b3, dec_ref, o_ref):
    h = jnp.sin(_mm_hp(z_ref[...], w1[...]) + b1[...])
    h = jnp.sin(_mm_hp(h, w2[...]) + b2[...])
    o_ref[...] = (_mm_hp(h, w3[...]) + b3[...]) * dec_ref[...]


def _hy_features(length):
    pos = jnp.arange(length, dtype=F32)
    t = pos / max(length - 1, 1)
    freqs = jnp.linspace(1e-4, HY_BANDS - 1, HY_BANDS, dtype=F32)
    ang = (2.0 * math.pi / length) * pos[:, None] * freqs[None, :]
    z = jnp.concatenate([t[:, None], jnp.cos(ang), -jnp.sin(ang)], axis=-1)
    rates = jnp.abs(jnp.linspace(math.log(HY_DECAY_TARGET) / HY_LONG_PCT,
                                 math.log(HY_DECAY_TARGET) / HY_SHORT_PCT, MIX_W, dtype=F32))
    rates = jnp.concatenate([rates, rates])
    decay = jnp.exp(-t[:, None] * rates[None, :])
    return jnp.pad(z, ((0, 0), (0, 128 - z.shape[1]))), decay


def _hy_filter(length, fw1, fb1, fw2, fb2, fw3, fb3):
    z, decay = _hy_features(length)
    w1 = jnp.pad(fw1, ((0, 128 - fw1.shape[0]), (0, 128 - fw1.shape[1])))
    b1 = jnp.pad(fb1, (0, 128 - fb1.shape[0])).reshape(1, 128)
    w2 = jnp.pad(fw2, ((0, 128 - fw2.shape[0]), (0, 128 - fw2.shape[1])))
    b2 = jnp.pad(fb2, (0, 128 - fb2.shape[0])).reshape(1, 128)
    w3 = jnp.pad(fw3, ((0, 128 - fw3.shape[0]), (0, 0)))
    n = fw3.shape[1]
    rows = min(length, 512)
    full = lambda s: pl.BlockSpec(s, lambda i: (0, 0))
    return pl.pallas_call(
        _hy_filter_kernel,
        grid=(length // rows,),
        in_specs=[pl.BlockSpec((rows, 128), lambda i: (i, 0)), full((128, 128)), full((1, 128)), full((128, 128)),
                  full((1, 128)), full((128, n)), full((1, n)), pl.BlockSpec((rows, n), lambda i: (i, 0))],
        out_specs=pl.BlockSpec((rows, n), lambda i: (i, 0)),
        out_shape=jax.ShapeDtypeStruct((length, n), F32),
        compiler_params=_cp("arbitrary"),
        name="hy_filter",
    )(z, w1, b1, w2, b2, w3, fb3.reshape(1, n), decay)


def _dft_tables(length):
    n = 2 * length
    k = jnp.arange(length, dtype=jnp.int32)
    m = (k[:, None] * k[None, :]) % n
    ang = m.astype(F32) * (2.0 * math.pi / n)
    wc = jnp.cos(ang)
    ws = jnp.sin(ang)
    nyq = jnp.where(k % 2 == 0, 1.0, -1.0).astype(F32)
    ws = ws.at[0].set(nyq)
    return wc.astype(BF16), ws.astype(BF16), wc.T.astype(BF16), ws.T.astype(BF16)


def _dft_fwd_kernel(wc_ref, ws_ref, xh_ref, xl_ref, oc_ref, os_ref):
    d = functools.partial(jnp.dot, preferred_element_type=F32)
    oc_ref[...] = d(wc_ref[...], xh_ref[...]) + d(wc_ref[...], xl_ref[...])
    os_ref[...] = d(ws_ref[...], xh_ref[...]) + d(ws_ref[...], xl_ref[...])


def _hy_spectrum(filt, wc, ws):
    length, c2 = filt.shape
    c = c2 // 2
    n = 2 * length
    hb = filt[:, c:].at[0].set(0.0)
    x = jnp.concatenate([filt[:, :c], hb], axis=1)
    xh, xl = _split_bf16(x)
    fr = min(length, 512)
    fc, fs = pl.pallas_call(
        _dft_fwd_kernel,
        grid=(length // fr,),
        in_specs=[pl.BlockSpec((fr, length), lambda i: (i, 0)), pl.BlockSpec((fr, length), lambda i: (i, 0)),
                  pl.BlockSpec((length, c2), lambda i: (0, 0)), pl.BlockSpec((length, c2), lambda i: (0, 0))],
        out_specs=[pl.BlockSpec((fr, c2), lambda i: (i, 0)), pl.BlockSpec((fr, c2), lambda i: (i, 0))],
        out_shape=[jax.ShapeDtypeStruct((length, c2), F32)] * 2,
        compiler_params=_cp("arbitrary"),
        name="hy_spectrum",
    )(wc, ws, xh, xl)
    k_re = fc[:, :c] + fc[:, c:]
    k_im = fs[:, c:] - fs[:, :c]
    k_nyq = fs[0, :c] + fs[0, c:]
    ka = (k_re * (2.0 / n)).at[0].set(k_re[0] / n)
    kb = (k_im * (2.0 / n)).at[0].set(0.0)
    ka2 = ka.at[0].set(k_nyq / n)
    return ka, ka2, kb


HYENA_BATCH = 2


def _hyena_kernel(z_ref, x0_ref, bias_ref, wc_ref, ws_ref, wct_ref, wst_ref, ka_ref, ka2_ref, kb_ref, o_ref, acc_scr):
    f = pl.program_id(1)
    nb = z_ref.shape[0]
    c = z_ref.shape[2]
    z = jnp.concatenate([z_ref[e] for e in range(nb)], axis=1)
    tile = lambda ref: jnp.concatenate([ref[...]] * nb, axis=1)

    @pl.when(f == 0)
    def _():
        acc_scr[...] = z.astype(F32) * tile(bias_ref)

    zc = jnp.dot(wc_ref[...], z, preferred_element_type=F32)
    zs = jnp.dot(ws_ref[...], z, preferred_element_type=F32)
    ka, ka2, kb = tile(ka_ref), tile(ka2_ref), tile(kb_ref)
    a = zc * ka + zs * kb
    bv = zs * ka2 - zc * kb
    acc_scr[...] += (jnp.dot(wct_ref[...], a.astype(BF16), preferred_element_type=F32)
                     + jnp.dot(wst_ref[...], bv.astype(BF16), preferred_element_type=F32))

    @pl.when(f == pl.num_programs(1) - 1)
    def _():
        for e in range(nb):
            o_ref[e] = (x0_ref[e].astype(F32) * acc_scr[:, e * c:(e + 1) * c]).astype(o_ref.dtype)


def _hyena(z, x0, row_block, length, bias, tables, spec):
    b, t, c = z.shape
    wc, ws, wct, wst = tables
    ka, ka2, kb = spec
    fr = min(length, 256)
    nf = length // fr
    nb = HYENA_BATCH
    seq = pl.BlockSpec((nb, length, c), lambda b, f: (b, row_block, 0))
    in_specs = [seq, seq,
                pl.BlockSpec((1, c), lambda b, f: (0, 0)),
                pl.BlockSpec((fr, length), lambda b, f: (f, 0)),
                pl.BlockSpec((fr, length), lambda b, f: (f, 0)),
                pl.BlockSpec((length, fr), lambda b, f: (0, f)),
                pl.BlockSpec((length, fr), lambda b, f: (0, f)),
                pl.BlockSpec((fr, c), lambda b, f: (f, 0)),
                pl.BlockSpec((fr, c), lambda b, f: (f, 0)),
                pl.BlockSpec((fr, c), lambda b, f: (f, 0))]
    return pl.pallas_call(
        _hyena_kernel,
        grid=(b // nb, nf),
        in_specs=in_specs,
        out_specs=pl.BlockSpec((nb, length, c), lambda b, f: (b, 0, 0)),
        out_shape=jax.ShapeDtypeStruct((b, length, c), BF16),
        scratch_shapes=[pltpu.VMEM((length, nb * c), F32)],
        compiler_params=_cp("arbitrary", "arbitrary"),
        name="hyena_%d" % length,
    )(z, x0, bias.reshape(1, c), wc, ws, wct, wst, ka, ka2, kb)


def _rw_prep_kernel(p_ref, pprev_ref, pnext_ref, mu_ref, w0_ref, wup_ref, a0_ref, aup_ref, gup_ref, kk_ref, ka_ref, rk_ref,
                    r_o, v_o, kap_o, lwf_o, lwb_o, kf_o, kb_o, bf_o, bb_o, g_o, bonus_o, *, n_lat_tiles):
    t = pl.program_id(1)
    nt = pl.num_programs(1)
    p = p_ref[0]
    first = jnp.logical_or(t == 0, t == n_lat_tiles)
    last = jnp.logical_or(t == n_lat_tiles - 1, t == nt - 1)
    prev_row = jnp.where(first, 0.0, pprev_ref[0, 7:8, :])
    next_row = jnp.where(last, 0.0, pnext_ref[0, 0:1, :])
    pm, pp = _shift_rows(p, prev_row, next_row)
    p = p + mu_ref[0:1] * (pm - p) + mu_ref[1:2] * (pp - p)
    c = MIX_W
    r, k, v = p[:, 0:c], p[:, c:2 * c], p[:, 2 * c:3 * c]
    wd, ad, gd = p[:, 3 * c:3 * c + 128], p[:, 3 * c + 128:3 * c + 256], p[:, 3 * c + 256:3 * c + 384]
    bd = _head_blockdiag(c, RW_HEAD_DIM)
    g = _mm(_sigmoid(gd), gup_ref[...])
    kk = k * kk_ref[...]
    nrm = jnp.sqrt(_x_exact01(kk * kk, bd))
    kap = kk / jnp.maximum(nrm, 1e-12)
    logw = -RW_DECAY_SCALE * _sigmoid(w0_ref[...] + _mm_hp(jnp.tanh(wd), wup_ref[...]))
    a = _sigmoid(a0_ref[...] + _mm_hp(ad, aup_ref[...]))
    bonus_s = 0.0
    outs_k, outs_b = (kf_o, kb_o), (bf_o, bb_o)
    for d in range(2):
        a_d = a[:, d * c:(d + 1) * c]
        k_d = k * (1.0 + (a_d - 1.0) * ka_ref[...])
        outs_k[d][0] = k_d.astype(BF16)
        outs_b[d][0] = (a_d * kap).astype(BF16)
        bonus_s = bonus_s + _x_exact01(r * k_d * rk_ref[...], bd)
    r_o[0] = r.astype(BF16)
    v_o[0] = v.astype(BF16)
    kap_o[0] = kap.astype(BF16)
    lwf_o[0] = logw[:, 0:c]
    lwb_o[0] = logw[:, c:2 * c]
    g_o[0] = g
    bonus_o[0] = bonus_s * v


def _blockdiag2(a, b):
    z1 = jnp.zeros((a.shape[0], b.shape[1]), a.dtype)
    z2 = jnp.zeros((b.shape[0], a.shape[1]), a.dtype)
    return jnp.concatenate([jnp.concatenate([a, z1], 1), jnp.concatenate([z2, b], 1)], 0)


def _rw_prep(p_rw, n_lat_tiles, mu, w0, w_up, a0, a_up, g_up, k_k, k_a, r_k):
    b, t, n = p_rw.shape
    nt = t // TOKEN_TILE
    c = MIX_W
    sub = TOKEN_TILE // 8
    n8 = t // 8
    consts = [mu, w0.reshape(1, 2 * c), _blockdiag2(w_up[0], w_up[1]), a0.reshape(1, 2 * c), _blockdiag2(a_up[0], a_up[1]),
              g_up, k_k.reshape(1, c), k_a.reshape(1, c), r_k.reshape(1, c)]
    cspecs = [pl.BlockSpec(x.shape, lambda b, t: (0, 0)) for x in consts]
    tile = lambda w: pl.BlockSpec((1, TOKEN_TILE, w), lambda b, t: (b, t, 0))
    return pl.pallas_call(
        functools.partial(_rw_prep_kernel, n_lat_tiles=n_lat_tiles),
        grid=(b, nt),
        in_specs=[tile(n),
                  pl.BlockSpec((1, 8, n), lambda b, t: (b, jnp.maximum(t * sub - 1, 0), 0)),
                  pl.BlockSpec((1, 8, n), lambda b, t: (b, jnp.minimum((t + 1) * sub, n8 - 1), 0))] + cspecs,
        out_specs=[tile(c)] * 11,
        out_shape=[jax.ShapeDtypeStruct((b, t, c), dt) for dt in (BF16, BF16, BF16, F32, F32, BF16, BF16, BF16, BF16, F32, F32)],
        compiler_params=_cp("arbitrary", "arbitrary"),
        name="rw_prep",
    )(p_rw, p_rw, p_rw, *consts)


def _rw_scan_kernel(rf, vf, kapf, lwf, kf, bbf, rb, vb, kapb, lwb, kb, bbb, of_ref, ob_ref, s_scr):
    @pl.when(pl.program_id(1) == 0)
    def _():
        s_scr[...] = jnp.zeros_like(s_scr)

    c = rf.shape[1]
    pw_ = 2 * RW_HEAD_DIM
    n_pairs = MIX_W // pw_
    lane_a = lax.broadcasted_iota(jnp.int32, (c, pw_), 1) < RW_HEAD_DIM
    lane_a2 = lax.broadcasted_iota(jnp.int32, (2 * c, pw_), 1) < RW_HEAD_DIM
    blk = ((lax.broadcasted_iota(jnp.int32, (pw_, pw_), 0) < RW_HEAD_DIM)
           == (lax.broadcasted_iota(jnp.int32, (pw_, pw_), 1) < RW_HEAD_DIM))

    def sel(x):
        return jnp.where(lane_a, x[:c], x[c:])

    def bf(x):
        return x.astype(BF16)

    nb = rf.shape[0]
    chains = []
    for e in range(nb):
        for d, (r, v, kap, lw, k, b) in enumerate(((rf, vf, kapf, lwf, kf, bbf), (rb, vb, kapb, lwb, kb, bbb))):
            rev = d == 1
            incl = _tri(c, rev, False)
            strict = _tri(c, rev, True)
            logw = lw[e]
            cum = _mm_exact01(incl.astype(F32), logw)
            tot = cum[0:1] if rev else cum[c - 1:c]
            ginv = jnp.exp(-cum)
            gt = jnp.exp(tot - cum)
            rt = r[e] * jnp.exp(cum)
            kt = kap[e] * jnp.exp(cum - logw)
            kd, bd = k[e] * ginv, b[e] * ginv
            kg, bg = k[e] * gt, b[e] * gt
            dec = jnp.exp(tot)
            vv = v[e]
            for p in range(n_pairs):
                sl = slice(p * pw_, (p + 1) * pw_)
                lhs2 = jnp.concatenate([rt[:, sl], kt[:, sl]], 0)
                chains.append(dict(
                    e=e, d=d, p=p, idx=(e * 2 + d) * n_pairs + p, incl=incl, strict=strict,
                    lhs2=bf(lhs2),
                    lhs4=bf(jnp.concatenate([jnp.where(lane_a2, lhs2, 0.0), jnp.where(lane_a2, 0.0, lhs2)], 0)),
                    kd=bf(kd[:, sl]), bd=bf(bd[:, sl]), v=bf(vv[:, sl]),
                    kgbg=bf(jnp.concatenate([kg[:, sl], bg[:, sl]], 0)), dec=dec[:, sl]))

    dot = functools.partial(lax.dot_general, preferred_element_type=F32)
    nn = (((1,), (0,)), ((), ()))
    for ch in chains:
        ch["s"] = s_scr[ch["idx"]]
    for ch in chains:
        ch["a1k"] = dot(ch["lhs4"], ch["kd"], NT_DIMS)
        ch["a1b"] = dot(ch["lhs4"], ch["bd"], NT_DIMS)
        ch["a2"] = dot(ch["lhs2"], bf(ch["s"]), NT_DIMS)
    for ch in chains:
        a1k, a1b, st = ch["a1k"], ch["a1b"], ch["strict"]
        lk = jnp.where(jnp.concatenate([st, st], 0), jnp.concatenate([a1k[c:2 * c], a1k[3 * c:]], 0), 0.0)
        ch["lb_a"] = bf(jnp.where(st, a1b[c:2 * c], 0.0))
        ch["lb_b"] = bf(jnp.where(st, a1b[3 * c:], 0.0))
        ch["u"] = ch["a2"][c:] + sel(dot(bf(lk), ch["v"], nn))
    for ch in chains:
        ch["u"] = ch["u"] - sel(dot(jnp.concatenate([ch["lb_a"], ch["lb_b"]], 0), bf(ch["u"]), nn))
        ch["p_a"] = bf(dot(ch["lb_a"], ch["lb_a"], nn))
        ch["p_b"] = bf(dot(ch["lb_b"], ch["lb_b"], nn))
    n = int(math.log2(c)) - 1
    for i in range(n):
        for ch in chains:
            ch["u"] = ch["u"] + sel(dot(jnp.concatenate([ch["p_a"], ch["p_b"]], 0), bf(ch["u"]), nn))
            if i < n - 1:
                ch["p_a"] = bf(dot(ch["p_a"], ch["p_a"], nn))
                ch["p_b"] = bf(dot(ch["p_b"], ch["p_b"], nn))
    outs = [[[None] * n_pairs, [None] * n_pairs] for _ in range(nb)]
    for ch in chains:
        a1k, a1b, inc = ch["a1k"], ch["a1b"], ch["incl"]
        inc2 = jnp.concatenate([inc, inc], 0)
        rk = jnp.where(inc2, jnp.concatenate([a1k[:c], a1k[2 * c:3 * c]], 0), 0.0)
        rb_ = jnp.where(inc2, jnp.concatenate([a1b[:c], a1b[2 * c:3 * c]], 0), 0.0)
        ub = bf(ch["u"])
        o = ch["a2"][:c] + sel(dot(bf(rk), ch["v"], nn)) - sel(dot(bf(rb_), ub, nn))
        outs[ch["e"]][ch["d"]][ch["p"]] = o
        upd = dot(jnp.concatenate([ch["v"], -ub], 0), ch["kgbg"], TN_DIMS)
        s_scr[ch["idx"]] = ch["s"] * ch["dec"] + jnp.where(blk, upd, 0.0)
    for e in range(nb):
        of_ref[e] = jnp.concatenate(outs[e][0], axis=1)
        ob_ref[e] = jnp.concatenate(outs[e][1], axis=1)


def _chunk_orders(n_lat, n_ctx):
    n = n_lat + n_ctx
    fwd = lambda b, s: (b, (s + n_lat) % n, 0)
    rev = lambda b, s: (b, n - 1 - s, 0)
    return fwd, rev


def _rw_scan(r, v, kap, lwf, lwb, kf, kb, bf, bb, n_lat_rows):
    b, t, c = r.shape
    ch = SEQ_CHUNK
    n = t // ch
    fwd, rev = _chunk_orders(n_lat_rows // ch, (t - n_lat_rows) // ch)
    nb = SCAN_BATCH
    blk = lambda im: pl.BlockSpec((nb, ch, c), im)
    return pl.pallas_call(
        _rw_scan_kernel,
        grid=(b // nb, n),
        in_specs=[blk(fwd)] * 6 + [blk(rev)] * 6,
        out_specs=[blk(fwd), blk(rev)],
        out_shape=[jax.ShapeDtypeStruct((b, t, c), F32)] * 2,
        scratch_shapes=[pltpu.VMEM((nb * RW_HEADS, 2 * RW_HEAD_DIM, 2 * RW_HEAD_DIM), F32)],
        compiler_params=_cp("arbitrary", "arbitrary"),
        name="rw_scan",
    )(r, v, kap, lwf, kf, bf, r, v, kap, lwb, kb, bb)


def _rw_readout(of, ob, bonus, g, ln_g, ln_b):
    o = of + ob
    bd = _head_blockdiag(MIX_W, RW_HEAD_DIM)
    mean = _x_exact01(o, bd) * (1.0 / RW_HEAD_DIM)
    oc = o - mean
    var = _x_exact01(oc * oc, bd) * (1.0 / RW_HEAD_DIM)
    on = oc * lax.rsqrt(var + RW_GN_EPS) * ln_g + ln_b
    return (on + bonus) * g


def _s5_matrices(a_re, a_im, log_dt, b_re, b_im, c_re, c_im):
    tc, g, n, p = S5_CHUNK, S5_GROUPS, S5_STATE, S5_GROUP
    hp = lax.Precision.HIGHEST
    dt = jnp.exp(log_dt)[:, :, None]
    lam_re, lam_im = a_re * dt, a_im * dt
    mag1 = jnp.exp(lam_re)
    ab_re, ab_im = mag1 * jnp.cos(lam_im), mag1 * jnp.sin(lam_im)
    nr, ni = ab_re - 1.0, ab_im
    den = a_re * a_re + a_im * a_im
    f_re, f_im = (nr * a_re + ni * a_im) / den, (ni * a_re - nr * a_im) / den
    tau = jnp.arange(tc + 1, dtype=F32)[:, None, None, None]
    pw_mag = jnp.exp(lam_re[None] * tau)
    pw_re, pw_im = pw_mag * jnp.cos(lam_im[None] * tau), pw_mag * jnp.sin(lam_im[None] * tau)
    fb_re = f_re[..., None] * b_re[None] - f_im[..., None] * b_im[None]
    fb_im = f_re[..., None] * b_im[None] + f_im[..., None] * b_re[None]
    e_re = pw_re[..., None] * fb_re[None] - pw_im[..., None] * fb_im[None]
    e_im = pw_re[..., None] * fb_im[None] + pw_im[..., None] * fb_re[None]
    m = (jnp.einsum('gpn,tdgnq->tdgpq', c_re, e_re[:tc], precision=hp)
         - jnp.einsum('gpn,tdgnq->tdgpq', c_im, e_im[:tc], precision=hp))
    i = jnp.arange(tc)
    lag = i[None, :] - i[:, None]
    mf = jnp.where((lag >= 0)[:, :, None, None, None], m[jnp.clip(lag, 0, tc - 1), 0], 0.0)
    mb = jnp.where((lag <= 0)[:, :, None, None, None], m[jnp.clip(-lag, 0, tc - 1), 1], 0.0)
    w = jnp.transpose(mf + mb, (2, 0, 4, 1, 3)).reshape(g, tc * p, tc * p)
    idx_f = tc - 1 - i
    pf_re, pf_im = e_re[idx_f, 0], e_im[idx_f, 0]
    pb_re, pb_im = e_re[i, 1], e_im[i, 1]
    pm = jnp.concatenate([pf_re, pb_re, pf_im, pb_im], axis=2)
    pm = jnp.transpose(pm, (1, 0, 3, 2)).reshape(g, tc * p, 4 * n)
    cf_re = c_re[None, :, :, :] * pw_re[i + 1, 0][:, :, None, :] - c_im[None] * pw_im[i + 1, 0][:, :, None, :]
    cf_im = c_re[None] * pw_im[i + 1, 0][:, :, None, :] + c_im[None] * pw_re[i + 1, 0][:, :, None, :]
    cb_re = c_re[None] * pw_re[tc - i, 1][:, :, None, :] - c_im[None] * pw_im[tc - i, 1][:, :, None, :]
    cb_im = c_re[None] * pw_im[tc - i, 1][:, :, None, :] + c_im[None] * pw_re[tc - i, 1][:, :, None, :]
    q = jnp.concatenate([cf_re, cb_re, -cf_im, -cb_im], axis=3)
    q = jnp.transpose(q, (1, 3, 0, 2)).reshape(g, 4 * n, tc * p)
    dec_re = jnp.concatenate([pw_re[tc, 0], pw_re[tc, 1]], axis=1)
    dec_im = jnp.concatenate([pw_im[tc, 0], pw_im[tc, 1]], axis=1)
    dec = jnp.stack([dec_re, dec_im], axis=1)
    return w.astype(BF16), pm.astype(BF16), q.astype(BF16), dec


S5_PACK_BATCH = 2
S5_LANE_GROUPS = 128 // S5_GROUP


def _s5_pack_kernel(x_ref, o_ref):
    rows = o_ref.shape[0]
    grp = lax.broadcasted_iota(jnp.int32, (rows, 128), 1) // S5_GROUP
    tiles = [jnp.zeros((rows, 128), F32) for _ in range(2 * S5_LANE_GROUPS)]
    for j in range(S5_CHUNK):
        xj = x_ref[pl.ds(j, rows, stride=S5_CHUNK), :]
        jl, jh = j % S5_LANE_GROUPS, j // S5_LANE_GROUPS
        for g8 in range(S5_LANE_GROUPS):
            sh = ((jl - g8) * S5_GROUP) % 128
            moved = pltpu.roll(xj, sh, 1) if sh else xj
            tiles[2 * g8 + jh] = jnp.where(grp == jl, moved, tiles[2 * g8 + jh])
    o_ref[...] = jnp.concatenate(tiles, axis=1).astype(o_ref.dtype)


def _s5_unpack_kernel(y_ref, o_ref):
    rows = y_ref.shape[0]
    grp = lax.broadcasted_iota(jnp.int32, (rows, 128), 1) // S5_GROUP
    for i in range(S5_CHUNK):
        il, ih = i % S5_LANE_GROUPS, i // S5_LANE_GROUPS
        out = jnp.zeros((rows, 128), F32)
        for g8 in range(S5_LANE_GROUPS):
            src = y_ref[:, (2 * g8 + ih) * 128:(2 * g8 + ih + 1) * 128]
            sh = ((g8 - il) * S5_GROUP) % 128
            moved = pltpu.roll(src, sh, 1) if sh else src
            out = jnp.where(grp == g8, moved, out)
        o_ref[pl.ds(i, rows, stride=S5_CHUNK), :] = out


def _s5_kernel(u_ref, w_ref, p_ref, q_ref, dec_ref, y_ref, inc_re, inc_im, sf_re, sf_im, sb_re, sb_im, *, n_lat, n_ctx, bsz):
    u = u_ref[...]
    inc = jnp.dot(u, p_ref[0], preferred_element_type=F32)
    n2 = 2 * S5_STATE
    inc_re[...] = inc[:, :n2]
    inc_im[...] = inc[:, n2:]
    n = n_lat + n_ctx
    d_re = dec_ref[0, 0:1, :]
    d_im = dec_ref[0, 1:2, :]
    is_f = lax.broadcasted_iota(jnp.int32, (bsz, n2), 1) < S5_STATE

    def body(s, carry):
        s_re, s_im = carry
        rf = pl.ds((s + n_lat) % n, bsz, stride=n)
        rb = pl.ds(n - 1 - s, bsz, stride=n)
        sf_re[rf, :] = s_re
        sf_im[rf, :] = s_im
        sb_re[rb, :] = s_re
        sb_im[rb, :] = s_im
        i_re = jnp.where(is_f, inc_re[rf, :], inc_re[rb, :])
        i_im = jnp.where(is_f, inc_im[rf, :], inc_im[rb, :])
        return (d_re * s_re - d_im * s_im + i_re, d_re * s_im + d_im * s_re + i_im)

    zero = jnp.zeros((bsz, n2), F32)
    lax.fori_loop(0, n, body, (zero, zero))
    is_f_all = lax.broadcasted_iota(jnp.int32, sf_re.shape, 1) < S5_STATE
    s_all = jnp.concatenate([jnp.where(is_f_all, sf_re[...], sb_re[...]),
                             jnp.where(is_f_all, sf_im[...], sb_im[...])], axis=1).astype(BF16)
    y_ref[...] = jnp.dot(u, w_ref[0], preferred_element_type=F32) + jnp.dot(s_all, q_ref[0], preferred_element_type=F32)


def _s5_core(p_s5, n_lat_rows, mats):
    b, t, c = p_s5.shape
    tc, g = S5_CHUNK, S5_GROUPS
    nch = t // tc
    w, pm, q, dec = mats
    k = tc * S5_GROUP
    nbb = S5_PACK_BATCH
    n_tiles = c // 128
    tok_blk = pl.BlockSpec((nbb * t, 128), lambda i, o: (i, o))
    chk_blk = pl.BlockSpec((nbb * nch, S5_LANE_GROUPS * k), lambda i, o: (i, o))
    u = pl.pallas_call(
        _s5_pack_kernel,
        grid=(b // nbb, n_tiles),
        in_specs=[tok_blk],
        out_specs=chk_blk,
        out_shape=jax.ShapeDtypeStruct((b * nch, g * k), BF16),
        compiler_params=_cp("arbitrary", "arbitrary"),
        name="s5_pack",
    )(p_s5.reshape(b * t, c))
    rows = nch * b
    y = pl.pallas_call(
        functools.partial(_s5_kernel, n_lat=n_lat_rows // tc, n_ctx=(t - n_lat_rows) // tc, bsz=b),
        grid=(g,),
        in_specs=[pl.BlockSpec((rows, k), lambda i: (0, i)),
                  pl.BlockSpec((1, k, k), lambda i: (i, 0, 0)),
                  pl.BlockSpec((1, k, 4 * S5_STATE), lambda i: (i, 0, 0)),
                  pl.BlockSpec((1, 4 * S5_STATE, k), lambda i: (i, 0, 0)),
                  pl.BlockSpec((1, 2, 2 * S5_STATE), lambda i: (i, 0, 0))],
        out_specs=pl.BlockSpec((rows, k), lambda i: (0, i)),
        out_shape=jax.ShapeDtypeStruct((rows, g * k), F32),
        scratch_shapes=[pltpu.VMEM((rows, 2 * S5_STATE), F32)] * 6,
        compiler_params=_cp("arbitrary"),
        name="s5_core",
    )(u, w, pm, q, dec)
    out = pl.pallas_call(
        _s5_unpack_kernel,
        grid=(b // nbb, n_tiles),
        in_specs=[chk_blk],
        out_specs=tok_blk,
        out_shape=jax.ShapeDtypeStruct((b * t, c), F32),
        compiler_params=_cp("arbitrary", "arbitrary"),
        name="s5_unpack",
    )(y)
    return out.reshape(b, t, c)


def _s5_output(y, u, d_skip, glu_w, glu_b):
    z = y + d_skip * u
    z = 0.5 * z * (1.0 + jnp.tanh(math.sqrt(2.0 / math.pi) * (z + 0.044715 * (z * z * z))))
    return z * _sigmoid(_mm(z, glu_w) + glu_b)


def _gla_scan_kernel(pfv_ref, pfc_ref, pbv_ref, pbc_ref, gup_ref, gb_ref, of_ref, ob_ref, s_scr, *, n_lat, n_ctx, cols):
    step = pl.program_id(1)

    @pl.when(step == 0)
    def _():
        s_scr[...] = jnp.zeros_like(s_scr)

    c = pfc_ref.shape[1]
    width = pfc_ref.shape[2]
    n = n_lat + n_ctx
    chunk_ids = ((step + n_lat) % n, n - 1 - step)
    pw_ = 2 * GLA_HK
    n_pairs = GLA_DK // pw_
    lane_a = lax.broadcasted_iota(jnp.int32, (c, pw_), 1) < GLA_HK
    lane_av = lax.broadcasted_iota(jnp.int32, (GLA_HV, pw_), 1) < GLA_HK
    dot = functools.partial(lax.dot_general, preferred_element_type=F32)
    nn = (((1,), (0,)), ((), ()))
    nb = pfc_ref.shape[0]
    chains = []
    for e in range(nb):
        for d, (v_ref, c_ref) in enumerate(((pfv_ref, pfc_ref), (pbv_ref, pbc_ref))):
            rev = d == 1
            incl = _tri(c, rev, False)
            j0 = (jnp.minimum(chunk_ids[d], n_lat - 1) * cols) % GRID_COL_BLOCK
            pe = jnp.where(chunk_ids[d] < n_lat, _col_major_rows(v_ref, e, j0, cols, 0, width), c_ref[e])
            gd = pe[:, 2 * GLA_DK + 2 * GLA_DV:]
            x = _mm_hp(gd, gup_ref[...])[:, d * GLA_DK:(d + 1) * GLA_DK] + gb_ref[0:1, d * GLA_DK:(d + 1) * GLA_DK]
            logg = (jnp.minimum(x, 0.0) - jnp.log(1.0 + jnp.exp(-jnp.abs(x)))) * (1.0 / GLA_GATE_NORM)
            cum = _mm_exact01(incl.astype(F32), logg)
            mid = c // 2
            ref = cum[mid:mid + 1] if rev else cum[mid - 1:mid]
            tot = cum[0:1] if rev else cum[c - 1:c]
            q = pe[:, 0:GLA_DK] * (GLA_HK ** -0.5)
            k = pe[:, GLA_DK:2 * GLA_DK]
            qs = q * jnp.exp(cum - ref)
            ks = k * jnp.exp(ref - cum)
            qc = q * jnp.exp(cum)
            kg = k * jnp.exp(tot - cum)
            dec = jnp.exp(tot)
            for p in range(n_pairs):
                sl = slice(p * pw_, (p + 1) * pw_)
                mask2 = lambda a: jnp.concatenate([jnp.where(lane_a, a, 0.0), jnp.where(lane_a, 0.0, a)], 0).astype(BF16)
                chains.append(dict(
                    e=e, d=d, p=p, idx=(e * 2 + d) * n_pairs + p, incl=incl, qs4=mask2(qs[:, sl]), qc4=mask2(qc[:, sl]),
                    ks=ks[:, sl].astype(BF16), kg=kg[:, sl].astype(BF16), dec=dec[:, sl],
                    va=pe[:, 2 * GLA_DK + 2 * p * GLA_HV:2 * GLA_DK + (2 * p + 1) * GLA_HV].astype(BF16),
                    vb=pe[:, 2 * GLA_DK + (2 * p + 1) * GLA_HV:2 * GLA_DK + (2 * p + 2) * GLA_HV].astype(BF16)))
    for ch in chains:
        ch["s"] = s_scr[ch["idx"]]
    for ch in chains:
        ch["sc"] = dot(ch["qs4"], ch["ks"], NT_DIMS)
        ch["st"] = dot(ch["qc4"], ch["s"].astype(BF16), NT_DIMS)
    outs = [[[None] * GLA_HEADS, [None] * GLA_HEADS] for _ in range(nb)]
    for ch in chains:
        e, d, p = ch["e"], ch["d"], ch["p"]
        sa = jnp.where(ch["incl"], ch["sc"][:c], 0.0).astype(BF16)
        sb = jnp.where(ch["incl"], ch["sc"][c:], 0.0).astype(BF16)
        outs[e][d][2 * p] = ch["st"][:c] + dot(sa, ch["va"], nn)
        outs[e][d][2 * p + 1] = ch["st"][c:] + dot(sb, ch["vb"], nn)
        upd = jnp.where(lane_av, dot(ch["va"], ch["kg"], TN_DIMS), dot(ch["vb"], ch["kg"], TN_DIMS))
        s_scr[ch["idx"]] = ch["s"] * ch["dec"] + upd
    for e in range(nb):
        of_ref[e] = jnp.concatenate(outs[e][0], axis=1)
        ob_ref[e] = jnp.concatenate(outs[e][1], axis=1)


def _gla_scan(p_gla, n_lat_rows, gate_up, gate_b):
    b, t, n = p_gla.shape
    ch = SEQ_CHUNK
    n_lat, n_ctx = n_lat_rows // ch, (t - n_lat_rows) // ch
    n_ch = n_lat + n_ctx
    fwd, rev = _chunk_orders(n_lat, n_ctx)
    cols = _grid_cols(n_lat_rows, ch)
    grid_rows = n_lat_rows // GRID_W
    gup = jnp.zeros((128, 2 * GLA_DK), F32)
    gup = gup.at[0:GLA_GATE_RANK, 0:GLA_DK].set(gate_up[0]).at[GLA_GATE_RANK:2 * GLA_GATE_RANK, GLA_DK:].set(gate_up[1])
    nb = SCAN_BATCH
    assert GRID_COL_BLOCK % cols == 0
    p_view = p_gla.reshape(b, t // GRID_W, GRID_W, n)
    view_blk = lambda cid: pl.BlockSpec((nb, grid_rows, GRID_COL_BLOCK, n),
                                        lambda b, s: (b, 0, jnp.minimum(cid(s), n_lat - 1) * cols // GRID_COL_BLOCK, 0))
    ctx_blk = lambda cid: pl.BlockSpec((nb, ch, n), lambda b, s: (b, jnp.maximum(cid(s), n_lat), 0))
    cid_f = lambda s: (s + n_lat) % n_ch
    cid_b = lambda s: n_ch - 1 - s
    return pl.pallas_call(
        functools.partial(_gla_scan_kernel, n_lat=n_lat, n_ctx=n_ctx, cols=cols),
        grid=(b // nb, n_ch),
        in_specs=[view_blk(cid_f), ctx_blk(cid_f), view_blk(cid_b), ctx_blk(cid_b),
                  pl.BlockSpec((128, 2 * GLA_DK), lambda b, s: (0, 0)),
                  pl.BlockSpec((1, 2 * GLA_DK), lambda b, s: (0, 0))],
        out_specs=[pl.BlockSpec((nb, ch, GLA_DV), fwd), pl.BlockSpec((nb, ch, GLA_DV), rev)],
        out_shape=[jax.ShapeDtypeStruct((b, t, GLA_DV), F32)] * 2,
        scratch_shapes=[pltpu.VMEM((nb * GLA_HEADS, GLA_HV, 2 * GLA_HK), F32)],
        compiler_params=_cp("arbitrary", "arbitrary"),
        name="gla_scan",
    )(p_view, p_gla, p_view, p_gla, gup, gate_b.reshape(1, 2 * GLA_DK))


def _gla_readout_kernel(of_ref, ob_ref, gv_ref, gc_ref, ng_ref, ylat_ref, yctx_ref, *, n_lat_tiles, cols):
    t = pl.program_id(1)
    o = of_ref[0] + ob_ref[0]
    bd = _head_blockdiag(GLA_DV, GLA_HV)
    ms = _x_exact01(o * o, bd) * (1.0 / GLA_HV)
    g_lo = 2 * GLA_DK + GLA_DV
    g = jnp.where(t < n_lat_tiles, _col_major_rows(gv_ref, 0, 0, cols, g_lo, g_lo + GLA_DV), gc_ref[0])
    y = (o * lax.rsqrt(ms + RMS_EPS) * ng_ref[...] * _silu(g)).astype(ylat_ref.dtype)
    rows = y.shape[0] // cols

    @pl.when(t < n_lat_tiles)
    def _():
        for j in range(cols):
            ylat_ref[0, :, j, :] = y[j * rows:(j + 1) * rows]

    @pl.when(t >= n_lat_tiles)
    def _():
        yctx_ref[0] = y


def _gla_readout(of, ob, p_gla, norm_g, n_lat_rows):
    b, t, c = of.shape
    n = p_gla.shape[2]
    n_lat_tiles = n_lat_rows // TOKEN_TILE
    grid_rows = n_lat_rows // GRID_W
    cols = _grid_cols(n_lat_rows, TOKEN_TILE)
    assert cols == GRID_COL_BLOCK
    tile = pl.BlockSpec((1, TOKEN_TILE, c), lambda b, t: (b, t, 0))
    p_view = p_gla.reshape(b, t // GRID_W, GRID_W, n)
    y_lat, y_ctx = pl.pallas_call(
        functools.partial(_gla_readout_kernel, n_lat_tiles=n_lat_tiles, cols=cols),
        grid=(b, t // TOKEN_TILE),
        in_specs=[tile, tile,
                  pl.BlockSpec((1, grid_rows, cols, n), lambda b, t: (b, 0, jnp.minimum(t, n_lat_tiles - 1), 0)),
                  pl.BlockSpec((1, TOKEN_TILE, c), lambda b, t: (b, jnp.maximum(t, n_lat_tiles), (2 * GLA_DK + GLA_DV) // c)),
                  pl.BlockSpec((1, c), lambda b, t: (0, 0))],
        out_specs=[pl.BlockSpec((1, grid_rows, cols, c), lambda b, t: (b, 0, jnp.minimum(t, n_lat_tiles - 1), 0)),
                   pl.BlockSpec((1, TOKEN_TILE, c), lambda b, t: (b, 0, 0))],
        out_shape=[jax.ShapeDtypeStruct((b, grid_rows, GRID_W, c), BF16), jax.ShapeDtypeStruct((b, t - n_lat_rows, c), BF16)],
        compiler_params=_cp("arbitrary", "arbitrary"),
        name="gla_readout",
    )(of, ob, p_view, p_gla, jnp.tile(norm_g, GLA_HEADS).reshape(1, c))
    return y_lat.reshape(b, n_lat_rows, c), y_ctx


def _merge_kernel(h_ref, mod_ref, gate_ref, yhyl_ref, yhyc_ref, rof_ref, rob_ref, rbonus_ref, rg_ref, ys5_ref, us5_ref,
                  yglal_ref, yglac_ref, lng_ref, lnb_ref, dskip_ref, gluw_ref, glub_ref, wbr_ref, wout_ref, o_ref,
                  *, n_lat_tiles):
    d = D_MODEL
    is_lat = pl.program_id(1) < n_lat_tiles
    y_hy = jnp.where(is_lat, yhyl_ref[0], yhyc_ref[0])
    y_gla = jnp.where(is_lat, yglal_ref[0], yglac_ref[0])
    y_rw = _rw_readout(rof_ref[0], rob_ref[0], rbonus_ref[0], rg_ref[0], lng_ref[...], lnb_ref[...])
    y_s5 = _s5_output(ys5_ref[0], us5_ref[0], dskip_ref[...], gluw_ref[...], glub_ref[...])
    acc = jnp.zeros((h_ref.shape[1], d), F32)
    for m, y in enumerate((y_hy, y_rw, y_s5, y_gla)):
        acc = acc + (_sigmoid(gate_ref[0, :, m * d:(m + 1) * d].astype(F32))
                     * jnp.dot(y.astype(BF16), wbr_ref[m], preferred_element_type=F32))
    out = jnp.dot(acc.astype(BF16), wout_ref[...], preferred_element_type=F32)
    o_ref[0] = h_ref[0] + mod_ref[0, 0, 2:3, :] * out


def _merge(h_all, modsel, n_tiles, n_lat_tiles, gate, y_hy_lat, y_hy_ctx, rof, rob, rbonus, rg, y_s5, u_s5, y_gla_lat, y_gla_ctx,
           ln_g, ln_b, d_skip, glu_w, glu_b, w_branch, w_out):
    b, t, d = h_all.shape
    c = MIX_W
    tile = lambda w: pl.BlockSpec((1, TOKEN_TILE, w), lambda b, t: (b, t, 0))
    lat = pl.BlockSpec((1, TOKEN_TILE, c), lambda b, t: (b, jnp.minimum(t, n_lat_tiles - 1), 0))
    ctx = pl.BlockSpec((1, TOKEN_TILE, c), lambda b, t: (b, jnp.maximum(t - n_lat_tiles, 0), 0))
    vec = lambda x: x.reshape(1, -1)
    consts = [vec(ln_g), vec(ln_b), vec(d_skip), glu_w.astype(BF16), vec(glu_b), w_branch.astype(BF16), w_out.astype(BF16)]
    cspecs = [pl.BlockSpec(x.shape, (lambda n: lambda b, t: (0,) * n)(x.ndim)) for x in consts]
    return pl.pallas_call(
        functools.partial(_merge_kernel, n_lat_tiles=n_lat_tiles),
        grid=(b, n_tiles),
        in_specs=[tile(d), pl.BlockSpec((1, 1, 6, d), _mod_index(n_lat_tiles)), tile(4 * d), lat, ctx] + [tile(c)] * 6
                 + [lat, ctx] + cspecs,
        out_specs=tile(d),
        out_shape=jax.ShapeDtypeStruct((b, t, d), F32),
        input_output_aliases={0: 0},
        compiler_params=_cp("arbitrary", "arbitrary"),
        name="merge",
    )(h_all, modsel, gate, y_hy_lat, y_hy_ctx, rof, rob, rbonus, rg, y_s5, u_s5, y_gla_lat, y_gla_ctx, *consts)


def _ffn_kernel(h_ref, g_ref, mod_ref, w1_ref, w3_ref, w2_ref, o_ref):
    h = h_ref[0]
    xm = _norm_mod(h, g_ref[...], mod_ref[0, 0], 3).astype(BF16)
    a = jnp.dot(xm, w1_ref[...], preferred_element_type=F32)
    b = jnp.dot(xm, w3_ref[...], preferred_element_type=F32)
    y = jnp.dot((_silu(a) * b).astype(BF16), w2_ref[...], preferred_element_type=F32)
    o_ref[0] = h + mod_ref[0, 0, 5:6, :] * y


def _ffn_dense(h_all, g, modsel, n_lat_tiles, w1, w3, w2):
    b, t, d = h_all.shape
    tile = pl.BlockSpec((1, TOKEN_TILE, d), lambda b, t: (b, t, 0))
    ws = [w1.astype(BF16), w3.astype(BF16), w2.astype(BF16)]
    wspec = [pl.BlockSpec(w.shape, lambda b, t: (0, 0), pipeline_mode=pl.Buffered(1)) for w in ws]
    return pl.pallas_call(
        _ffn_kernel,
        grid=(b, t // TOKEN_TILE),
        in_specs=[tile, pl.BlockSpec((1, d), lambda b, t: (0, 0)), pl.BlockSpec((1, 1, 6, d), _mod_index(n_lat_tiles))] + wspec,
        out_specs=tile,
        out_shape=jax.ShapeDtypeStruct((b, t, d), F32),
        input_output_aliases={0: 0},
        compiler_params=_cp("arbitrary", "arbitrary"),
        name="ffn_dense",
    )(h_all, g.reshape(1, d), modsel, *ws)


def _router_kernel(h_ref, g_ref, mod_ref, rw_ref, rb_ref, x_ref, logit_ref):
    xm = _norm_mod(h_ref[0], g_ref[...], mod_ref[0, 0], 3)
    x_ref[0] = xm.astype(BF16)
    logit_ref[0] = _mm_hp(xm, rw_ref[...]) + rb_ref[...]


def _router(h_all, g, modsel, n_lat_tiles, router_w, router_b):
    b, t, d = h_all.shape
    n_lat = n_lat_tiles * TOKEN_TILE
    rw = jnp.pad(router_w, ((0, 0), (0, 128 - N_EXPERTS)))
    rb = jnp.pad(router_b, (0, 128 - N_EXPERTS), constant_values=-1e30).reshape(1, 128)
    tile = lambda w: pl.BlockSpec((1, TOKEN_TILE, w), lambda b, t: (b, t, 0))
    return pl.pallas_call(
        _router_kernel,
        grid=(b, n_lat_tiles),
        in_specs=[tile(d), pl.BlockSpec((1, d), lambda b, t: (0, 0)), pl.BlockSpec((1, 1, 6, d), _mod_index(n_lat_tiles)),
                  pl.BlockSpec((d, 128), lambda b, t: (0, 0)), pl.BlockSpec((1, 128), lambda b, t: (0, 0))],
        out_specs=[tile(d), tile(128)],
        out_shape=[jax.ShapeDtypeStruct((b, n_lat, d), BF16), jax.ShapeDtypeStruct((b, n_lat, 128), F32)],
        compiler_params=_cp("arbitrary", "arbitrary"),
        name="moe_router",
    )(h_all, g.reshape(1, d), modsel, rw, rb)


def _moe_expert_kernel(be_ref, x_ref, w1_ref, w3_ref, w2_ref, o_ref):
    x = x_ref[...]
    a = jnp.dot(x, w1_ref[0], preferred_element_type=F32)
    b = jnp.dot(x, w3_ref[0], preferred_element_type=F32)
    o_ref[...] = jnp.dot((_silu(a) * b).astype(BF16), w2_ref[0], preferred_element_type=F32).astype(o_ref.dtype)


def _moe_experts(xb, block_exp, w1, w3, w2):
    rows, d = xb.shape
    nb = rows // MOE_BLOCK_ROWS
    f = w1.shape[2]
    grid_spec = pltpu.PrefetchScalarGridSpec(
        num_scalar_prefetch=1,
        grid=(nb,),
        in_specs=[pl.BlockSpec((MOE_BLOCK_ROWS, d), lambda i, be: (i, 0)),
                  pl.BlockSpec((1, d, f), lambda i, be: (be[i], 0, 0), pipeline_mode=pl.Buffered(1)),
                  pl.BlockSpec((1, d, f), lambda i, be: (be[i], 0, 0), pipeline_mode=pl.Buffered(1)),
                  pl.BlockSpec((1, f, d), lambda i, be: (be[i], 0, 0), pipeline_mode=pl.Buffered(1))],
        out_specs=pl.BlockSpec((MOE_BLOCK_ROWS, d), lambda i, be: (i, 0)),
    )
    return pl.pallas_call(
        _moe_expert_kernel,
        grid_spec=grid_spec,
        out_shape=jax.ShapeDtypeStruct((rows, d), BF16),
        compiler_params=_cp("arbitrary"),
        name="moe_experts",
    )(block_exp, xb, w1, w3, w2)


def _moe_combine_kernel(h_ref, mod_ref, y_ref, gate_ref, fg_ref, o_ref):
    d = h_ref.shape[2]
    y = y_ref[0, :, :d].astype(F32) * gate_ref[0, :, 0:1] + y_ref[0, :, d:].astype(F32) * gate_ref[0, :, 1:2]
    h = h_ref[0] + mod_ref[0, 0, 5:6, :] * y
    o_ref[0] = h * lax.rsqrt(jnp.mean(h * h, axis=-1, keepdims=True) + RMS_EPS) * fg_ref[...]


def _moe_combine(h_all, modsel, n_lat_tiles, y2, gates, final_g):
    b, t, d = h_all.shape
    n_lat = n_lat_tiles * TOKEN_TILE
    tile = lambda w: pl.BlockSpec((1, TOKEN_TILE, w), lambda b, t: (b, t, 0))
    return pl.pallas_call(
        _moe_combine_kernel,
        grid=(b, n_lat_tiles),
        in_specs=[tile(d), pl.BlockSpec((1, 1, 6, d), _mod_index(n_lat_tiles)),
                  tile(TOP_K * d),
                  tile(128), pl.BlockSpec((1, d), lambda b, t: (0, 0))],
        out_specs=tile(d),
        out_shape=jax.ShapeDtypeStruct((b, n_lat, d), F32),
        compiler_params=_cp("arbitrary", "arbitrary"),
        name="moe_combine",
    )(h_all, modsel, y2, gates, final_g.reshape(1, d))


def _moe_ffn_and_final_norm(h_all, g, modsel, n_lat_tiles, router_w, router_b, w1, w3, w2, final_g):
    b, t, d = h_all.shape
    n_lat = n_lat_tiles * TOKEN_TILE
    n_tok = b * n_lat
    xn, logits = _router(h_all, g, modsel, n_lat_tiles, router_w, router_b)
    logits = logits.reshape(n_tok, 128)[:, :N_EXPERTS]
    top_v, top_i = lax.top_k(logits, TOP_K)
    gates = jax.nn.softmax(top_v, axis=-1)
    n_assign = n_tok * TOP_K
    e_flat = top_i.reshape(-1).astype(jnp.int32)
    order = jnp.argsort(e_flat).astype(jnp.int32)
    rank = jnp.argsort(order).astype(jnp.int32)
    counts = jnp.sum((e_flat[:, None] == jnp.arange(N_EXPERTS, dtype=jnp.int32)[None, :]).astype(jnp.int32), axis=0)
    padded = (counts + MOE_BLOCK_ROWS - 1) // MOE_BLOCK_ROWS * MOE_BLOCK_ROWS
    start = jnp.cumsum(counts) - counts
    pend = jnp.cumsum(padded)
    pstart = pend - padded
    n_blocks = n_assign // MOE_BLOCK_ROWS + N_EXPERTS
    n_rows = n_blocks * MOE_BLOCK_ROWS
    row_ids = jnp.arange(n_rows, dtype=jnp.int32)
    row_exp = jnp.minimum(jnp.searchsorted(pend, row_ids, side='right'), N_EXPERTS - 1).astype(jnp.int32)
    row_off = row_ids - pstart[row_exp]
    row_slot = jnp.clip(start[row_exp] + row_off, 0, n_assign - 1)
    row_tok = jnp.where(row_off < counts[row_exp], order[row_slot] // TOP_K, n_tok).astype(jnp.int32)
    block_exp = row_exp[::MOE_BLOCK_ROWS]
    tok_pad = jnp.concatenate([xn.reshape(n_tok, d), jnp.zeros((1, d), BF16)], axis=0)
    xb = tok_pad.at[row_tok].get(mode="promise_in_bounds")
    yb = _moe_experts(xb, block_exp, w1.astype(BF16), w3.astype(BF16), w2.astype(BF16))
    dest_orig = (pstart[e_flat] + rank - start[e_flat]).astype(jnp.int32)
    y2 = yb.at[dest_orig].get(mode="promise_in_bounds").reshape(b, n_lat, TOP_K * d)
    gates_pad = jnp.pad(gates, ((0, 0), (0, 128 - TOP_K))).reshape(b, n_lat, 128)
    return _moe_combine(h_all, modsel, n_lat_tiles, y2, gates_pad, final_g)


def _pad_cols(w, n):
    return jnp.pad(w, ((0, 0), (0, n - w.shape[1])))


def kernel(x, c, ctx, c_ctx, ada_w, ada_b, norm_mix_g, norm_ffn_g, w_in, hy_conv_w, hy_conv_b, hy_f_w1, hy_f_b1, hy_f_w2, hy_f_b2, hy_f_w3, hy_f_b3, hy_bias, rw_mu, rw_w0, rw_w_up, rw_a0, rw_a_up, rw_g_up, rw_k_k, rw_k_a, rw_r_k, rw_ln_g, rw_ln_b, s5_a_re, s5_a_im, s5_log_dt, s5_b_re, s5_b_im, s5_c_re, s5_c_im, s5_d, s5_glu_w, s5_glu_b, gla_gate_up, gla_gate_b, gla_norm_g, w_branch, w_out, ffn_w1, ffn_w3, ffn_w2, moe_router_w, moe_router_b, moe_w1, moe_w3, moe_w2, final_norm_g):
    bsz, n_lat, d = x.shape
    n_ctx = ctx.shape[1]
    depth = w_in.shape[0]
    assert depth == 2 and d == D_MODEL
    assert n_lat % TOKEN_TILE == 0 and n_ctx % TOKEN_TILE == 0 and n_lat % GRID_W == 0
    t_all = n_lat + n_ctx
    n_lat_tiles = n_lat // TOKEN_TILE
    n_tiles = t_all // TOKEN_TILE
    assert n_tiles <= 2 * n_lat_tiles

    rows = (bsz + 1 + 7) // 8 * 8
    c_all = jnp.zeros((rows, d), F32).at[:bsz].set(c.astype(F32)).at[bsz].set(c_ctx.astype(F32))
    mod = _ada_mod(c_all, ada_w, ada_b)
    h_all = jnp.concatenate([x.astype(F32), ctx.astype(F32)], axis=1)

    dft_lat = _dft_tables(n_lat)
    dft_ctx = _dft_tables(n_ctx)
    seg = np.cumsum([0, 3 * MIX_W, RW_IN, MIX_W, 2 * GLA_DK + 2 * GLA_DV + 2 * GLA_GATE_RANK, 4 * d])

    out = None
    for l in range(depth):
        last = l == depth - 1
        mod_l = mod[l, :bsz].reshape(bsz, 6, d)
        mod_c = jnp.broadcast_to(mod[l, bsz].reshape(1, 6, d), (bsz, 6, d))
        modsel = jnp.stack([mod_l, mod_c], axis=1)

        wl = w_in[l].astype(BF16)
        ws = [wl[:, seg[0]:seg[1]], wl[:, seg[1]:seg[2]], wl[:, seg[2]:seg[3]],
              _pad_cols(wl[:, seg[3]:seg[4]], GLA_IN_PAD), wl[:, seg[4]:seg[5]]]
        hy_x0, hy_z, p_rw, p_s5, p_gla, gate = _in_proj(h_all, norm_mix_g[l], modsel, ws, hy_conv_w[l], hy_conv_b[l], n_lat_tiles)

        filt_args = (hy_f_w1[l], hy_f_b1[l], hy_f_w2[l], hy_f_b2[l], hy_f_w3[l], hy_f_b3[l])
        spec_lat = _hy_spectrum(_hy_filter(n_lat, *filt_args), dft_lat[0], dft_lat[1])
        y_hy_lat = _hyena(hy_z, hy_x0, 0, n_lat, hy_bias[l], dft_lat, spec_lat)
        y_hy_ctx = y_hy_lat
        if not last:
            spec_ctx = _hy_spectrum(_hy_filter(n_ctx, *filt_args), dft_ctx[0], dft_ctx[1])
            y_hy_ctx = _hyena(hy_z, hy_x0, n_lat // n_ctx, n_ctx, hy_bias[l], dft_ctx, spec_ctx)

        r, v, kap, lwf, lwb, kf, kb, bf, bb, rg, rbonus = _rw_prep(
            p_rw, n_lat_tiles, rw_mu[l], rw_w0[l], rw_w_up[l], rw_a0[l], rw_a_up[l], rw_g_up[l], rw_k_k[l], rw_k_a[l], rw_r_k[l])
        rof, rob = _rw_scan(r, v, kap, lwf, lwb, kf, kb, bf, bb, n_lat)

        y_s5 = _s5_core(p_s5, n_lat, _s5_matrices(s5_a_re[l], s5_a_im[l], s5_log_dt[l], s5_b_re[l], s5_b_im[l], s5_c_re[l], s5_c_im[l]))

        gof, gob = _gla_scan(p_gla, n_lat, gla_gate_up[l], gla_gate_b[l])
        y_gla_lat, y_gla_ctx = _gla_readout(gof, gob, p_gla, gla_norm_g[l], n_lat)

        nt_l = n_lat_tiles if last else n_tiles
        h_all = _merge(h_all, modsel, nt_l, n_lat_tiles, gate, y_hy_lat, y_hy_ctx, rof, rob, rbonus, rg, y_s5, p_s5,
                       y_gla_lat, y_gla_ctx, rw_ln_g[l], rw_ln_b[l], s5_d[l], s5_glu_w[l], s5_glu_b[l], w_branch[l], w_out[l])

        i = l // 2
        if l % 2 == 0:
            h_all = _ffn_dense(h_all, norm_ffn_g[l], modsel, n_lat_tiles, ffn_w1[i], ffn_w3[i], ffn_w2[i])
        else:
            out = _moe_ffn_and_final_norm(h_all, norm_ffn_g[l], modsel, n_lat_tiles, moe_router_w[i], moe_router_b[i],
                                          moe_w1[i], moe_w3[i], moe_w2[i], final_norm_g)
    return out.astype(x.dtype)
```

```python
import functools
import math

import numpy as np
import jax
import jax.numpy as jnp
from jax import lax
from jax.experimental import pallas as pl
from jax.experimental.pallas import tpu as pltpu

F32 = jnp.float32
BF16 = jnp.bfloat16

D_MODEL = 1024
GRID_W = 64
MIX_W = 512
RMS_EPS = 1e-6
HY_BANDS = 16
HY_DECAY_TARGET = 1e-2
HY_SHORT_PCT = 0.3
HY_LONG_PCT = 1.5
RW_HEADS = 8
RW_HEAD_DIM = 64
RW_DECAY_SCALE = 0.606531
RW_GN_EPS = 64e-5
RW_IN = 1920
S5_GROUP = 16
S5_GROUPS = 32
S5_STATE = 64
S5_CHUNK = 16
GLA_HEADS = 4
GLA_HK = 64
GLA_HV = 128
GLA_DK = 256
GLA_DV = 512
GLA_GATE_RANK = 16
GLA_GATE_NORM = 16.0
GLA_IN_PAD = 1664
N_EXPERTS = 8
TOP_K = 2
MOE_BLOCK_ROWS = 256
TOKEN_TILE = 256
SEQ_CHUNK = 64
SCAN_BATCH = 2
VMEM_LIMIT = 56 * 1024 * 1024

NT_DIMS = (((1,), (1,)), ((), ()))
TN_DIMS = (((0,), (0,)), ((), ()))


def _cp(*sem):
    return pltpu.CompilerParams(dimension_semantics=sem, vmem_limit_bytes=VMEM_LIMIT)


def _mm(a, b, dims=None):
    a = a.astype(BF16)
    b = b.astype(BF16)
    if dims is None:
        return jnp.dot(a, b, preferred_element_type=F32)
    return lax.dot_general(a, b, dims, preferred_element_type=F32)


def _split_bf16(x):
    hi = x.astype(BF16)
    lo = (x - hi.astype(F32)).astype(BF16)
    return hi, lo


def _mm_hp(a, b):
    ah, al = _split_bf16(a)
    bh, bl = _split_bf16(b)
    d = functools.partial(jnp.dot, preferred_element_type=F32)
    return d(ah, bh) + (d(ah, bl) + d(al, bh))


def _mm_exact01(m01, x):
    m = m01.astype(BF16)
    hi = x.astype(BF16)
    r1 = x - hi.astype(F32)
    mid = r1.astype(BF16)
    lo = (r1 - mid.astype(F32)).astype(BF16)
    d = functools.partial(jnp.dot, preferred_element_type=F32)
    return d(m, hi) + (d(m, mid) + d(m, lo))


def _x_exact01(x, m01):
    m = m01.astype(BF16)
    hi = x.astype(BF16)
    mid = (x - hi.astype(F32)).astype(BF16)
    d = functools.partial(jnp.dot, preferred_element_type=F32)
    return d(hi, m) + d(mid, m)


def _sigmoid(x):
    return 0.5 * jnp.tanh(0.5 * x) + 0.5


def _silu(x):
    return x * _sigmoid(x)


def _tri(c, rev, strict):
    i = lax.broadcasted_iota(jnp.int32, (c, c), 0)
    j = lax.broadcasted_iota(jnp.int32, (c, c), 1)
    if rev:
        return (j > i) if strict else (j >= i)
    return (j < i) if strict else (j <= i)


def _head_blockdiag(n, head):
    i = lax.broadcasted_iota(jnp.int32, (n, n), 0) // head
    j = lax.broadcasted_iota(jnp.int32, (n, n), 1) // head
    return (i == j).astype(F32)


def _shift_rows(x, prev_row, next_row):
    n = x.shape[0]
    row = lax.broadcasted_iota(jnp.int32, x.shape, 0)
    xm = jnp.where(row == 0, prev_row, pltpu.roll(x, 1, 0))
    xp = jnp.where(row == n - 1, next_row, pltpu.roll(x, n - 1, 0))
    return xm, xp


def _ada_kernel(c_ref, w_ref, b_ref, o_ref):
    o_ref[0] = _mm_hp(_silu(c_ref[...]), w_ref[0]) + b_ref[0]


def _ada_mod(c_all, ada_w, ada_b):
    depth, d, n = ada_w.shape
    rows = c_all.shape[0]
    return pl.pallas_call(
        _ada_kernel,
        grid=(depth, n // d),
        in_specs=[pl.BlockSpec((rows, d), lambda l, j: (0, 0)),
                  pl.BlockSpec((1, d, d), lambda l, j: (l, 0, j)),
                  pl.BlockSpec((1, 1, d), lambda l, j: (l, 0, j))],
        out_specs=pl.BlockSpec((1, rows, d), lambda l, j: (l, 0, j)),
        out_shape=jax.ShapeDtypeStruct((depth, rows, n), F32),
        compiler_params=_cp("arbitrary", "arbitrary"),
        name="ada_mod",
    )(c_all, ada_w, ada_b.reshape(depth, 1, n))


def _norm_mod(x, g, mod, sh_row):
    xn = x * lax.rsqrt(jnp.mean(x * x, axis=-1, keepdims=True) + RMS_EPS) * g
    return xn * (1.0 + mod[sh_row + 1:sh_row + 2]) + mod[sh_row:sh_row + 1]


def _seq_edges(t, n_lat_tiles, n_tiles):
    first = jnp.logical_or(t == 0, t == n_lat_tiles)
    last = jnp.logical_or(t == n_lat_tiles - 1, t == n_tiles - 1)
    return first, last


def _in_proj_kernel(x_ref, xprev_ref, xnext_ref, g_ref, mod_ref, cw_ref, cb_ref, w_hy, w_rw, w_s5, w_gla, w_gate,
                    o_x0, o_z, o_rw, o_s5, o_gla, o_gate, *, n_lat_tiles):
    t = pl.program_id(1)
    first, last = _seq_edges(t, n_lat_tiles, pl.num_programs(1))
    mod = mod_ref[0, 0]
    xm = _norm_mod(x_ref[0], g_ref[...], mod, 0).astype(BF16)
    for w, o in ((w_rw, o_rw), (w_s5, o_s5), (w_gla, o_gla), (w_gate, o_gate)):
        o[0] = jnp.dot(xm, w[...], preferred_element_type=F32).astype(o.dtype)
    p = jnp.dot(xm, w_hy[...], preferred_element_type=F32)
    halo = jnp.concatenate([xprev_ref[0, 7:8, :], xnext_ref[0, 0:1, :], jnp.zeros((6, x_ref.shape[2]), F32)], axis=0)
    ph = jnp.dot(_norm_mod(halo, g_ref[...], mod, 0).astype(BF16), w_hy[...], preferred_element_type=F32)
    pm, pp = _shift_rows(p, jnp.where(first, 0.0, ph[0:1]), jnp.where(last, 0.0, ph[1:2]))
    u = pm * cw_ref[0:1] + p * cw_ref[1:2] + pp * cw_ref[2:3] + cb_ref[...]
    c = MIX_W
    o_x0[0] = u[:, 0:c].astype(o_x0.dtype)
    o_z[0] = (u[:, 2 * c:3 * c] * u[:, c:2 * c]).astype(o_z.dtype)


def _mod_index(n_lat_tiles):
    return lambda b, t: (b, t // n_lat_tiles, 0, 0)


def _halo_specs(width, n_rows):
    sub = TOKEN_TILE // 8
    return [pl.BlockSpec((1, 8, width), lambda b, t: (b, jnp.maximum(t * sub - 1, 0), 0)),
            pl.BlockSpec((1, 8, width), lambda b, t: (b, jnp.minimum((t + 1) * sub, n_rows // 8 - 1), 0))]


def _in_proj(h_all, g, modsel, ws, conv_w, conv_b, n_lat_tiles):
    b, t, d = h_all.shape
    nt = t // TOKEN_TILE
    c = MIX_W
    wspec = [pl.BlockSpec(w.shape, lambda b, t: (0, 0), pipeline_mode=pl.Buffered(1)) for w in ws]
    widths = [c, c] + [w.shape[1] for w in ws[1:]]
    dtypes = [BF16, BF16, F32, F32, F32, BF16]
    return pl.pallas_call(
        functools.partial(_in_proj_kernel, n_lat_tiles=n_lat_tiles),
        grid=(b, nt),
        in_specs=[pl.BlockSpec((1, TOKEN_TILE, d), lambda b, t: (b, t, 0))] + _halo_specs(d, t)
                 + [pl.BlockSpec((1, d), lambda b, t: (0, 0)),
                    pl.BlockSpec((1, 1, 6, d), _mod_index(n_lat_tiles)),
                    pl.BlockSpec((3, 3 * c), lambda b, t: (0, 0)),
                    pl.BlockSpec((1, 3 * c), lambda b, t: (0, 0))] + wspec,
        out_specs=[pl.BlockSpec((1, TOKEN_TILE, w), lambda b, t: (b, t, 0)) for w in widths],
        out_shape=[jax.ShapeDtypeStruct((b, t, w), dt) for w, dt in zip(widths, dtypes)],
        compiler_params=_cp("arbitrary", "arbitrary"),
        name="in_proj",
    )(h_all, h_all, h_all, g.reshape(1, d), modsel, conv_w, conv_b.reshape(1, 3 * c), *ws)


def _grid_cols(n_lat_rows, rows):
    grid_rows = n_lat_rows // GRID_W
    assert grid_rows % 8 == 0 and rows % grid_rows == 0 and GRID_W % (rows // grid_rows) == 0
    return rows // grid_rows


GRID_COL_BLOCK = 8


def _col_major_rows(view_ref, e, j0, cols, lo, hi):
    return jnp.concatenate([view_ref[e, :, pl.ds(j0 + j, 1), lo:hi][:, 0, :] for j in range(cols)], axis=0)


def _hy_filter_kernel(z_ref, w1, b1, w2, b2, w3, b3, dec_ref, o_ref):
    h = jnp.sin(_mm_hp(z_ref[...], w1[...]) + b1[...])
    h = jnp.sin(_mm_hp(h, w2[...]) + b2[...])
    o_ref[...] = (_mm_hp(h, w3[...]) + b3[...]) * dec_ref[...]


def _hy_features(length):
    pos = jnp.arange(length, dtype=F32)
    t = pos / max(length - 1, 1)
    freqs = jnp.linspace(1e-4, HY_BANDS - 1, HY_BANDS, dtype=F32)
    ang = (2.0 * math.pi / length) * pos[:, None] * freqs[None, :]
    z = jnp.concatenate([t[:, None], jnp.cos(ang), -jnp.sin(ang)], axis=-1)
    rates = jnp.abs(jnp.linspace(math.log(HY_DECAY_TARGET) / HY_LONG_PCT,
                                 math.log(HY_DECAY_TARGET) / HY_SHORT_PCT, MIX_W, dtype=F32))
    rates = jnp.concatenate([rates, rates])
    decay = jnp.exp(-t[:, None] * rates[None, :])
    return jnp.pad(z, ((0, 0), (0, 128 - z.shape[1]))), decay


def _hy_filter(length, fw1, fb1, fw2, fb2, fw3, fb3):
    z, decay = _hy_features(length)
    w1 = jnp.pad(fw1, ((0, 128 - fw1.shape[0]), (0, 128 - fw1.shape[1])))
    b1 = jnp.pad(fb1, (0, 128 - fb1.shape[0])).reshape(1, 128)
    w2 = jnp.pad(fw2, ((0, 128 - fw2.shape[0]), (0, 128 - fw2.shape[1])))
    b2 = jnp.pad(fb2, (0, 128 - fb2.shape[0])).reshape(1, 128)
    w3 = jnp.pad(fw3, ((0, 128 - fw3.shape[0]), (0, 0)))
    n = fw3.shape[1]
    rows = min(length, 512)
    full = lambda s: pl.BlockSpec(s, lambda i: (0, 0))
    return pl.pallas_call(
        _hy_filter_kernel,
        grid=(length // rows,),
        in_specs=[pl.BlockSpec((rows, 128), lambda i: (i, 0)), full((128, 128)), full((1, 128)), full((128, 128)),
                  full((1, 128)), full((128, n)), full((1, n)), pl.BlockSpec((rows, n), lambda i: (i, 0))],
        out_specs=pl.BlockSpec((rows, n), lambda i: (i, 0)),
        out_shape=jax.ShapeDtypeStruct((length, n), F32),
        compiler_params=_cp("arbitrary"),
        name="hy_filter",
    )(z, w1, b1, w2, b2, w3, fb3.reshape(1, n), decay)


def _dft_tables(length):
    n = 2 * length
    k = jnp.arange(length, dtype=jnp.int32)
    m = (k[:, None] * k[None, :]) % n
    ang = m.astype(F32) * (2.0 * math.pi / n)
    wc = jnp.cos(ang).astype(BF16)
    sin = jnp.sin(ang)
    nyq = jnp.where(k % 2 == 0, 1.0, -1.0).astype(F32)
    return wc, sin.at[0].set(nyq).astype(BF16), wc, sin.at[:, 0].set(nyq).astype(BF16)


def _dft_fwd_kernel(wc_ref, ws_ref, xh_ref, xl_ref, oc_ref, os_ref):
    d = functools.partial(jnp.dot, preferred_element_type=F32)
    oc_ref[...] = d(wc_ref[...], xh_ref[...]) + d(wc_ref[...], xl_ref[...])
    os_ref[...] = d(ws_ref[...], xh_ref[...]) + d(ws_ref[...], xl_ref[...])


def _hy_spectrum(filt, wc, ws):
    length, c2 = filt.shape
    c = c2 // 2
    n = 2 * length
    hb = filt[:, c:].at[0].set(0.0)
    x = jnp.concatenate([filt[:, :c], hb], axis=1)
    xh, xl = _split_bf16(x)
    fr = min(length, 512)
    fc, fs = pl.pallas_call(
        _dft_fwd_kernel,
        grid=(length // fr,),
        in_specs=[pl.BlockSpec((fr, length), lambda i: (i, 0)), pl.BlockSpec((fr, length), lambda i: (i, 0)),
                  pl.BlockSpec((length, c2), lambda i: (0, 0)), pl.BlockSpec((length, c2), lambda i: (0, 0))],
        out_specs=[pl.BlockSpec((fr, c2), lambda i: (i, 0)), pl.BlockSpec((fr, c2), lambda i: (i, 0))],
        out_shape=[jax.ShapeDtypeStruct((length, c2), F32)] * 2,
        compiler_params=_cp("arbitrary"),
        name="hy_spectrum",
    )(wc, ws, xh, xl)
    k_re = fc[:, :c] + fc[:, c:]
    k_im = fs[:, c:] - fs[:, :c]
    k_nyq = fs[0, :c] + fs[0, c:]
    ka = (k_re * (2.0 / n)).at[0].set(k_re[0] / n)
    kb = (k_im * (2.0 / n)).at[0].set(0.0)
    ka2 = ka.at[0].set(k_nyq / n)
    return ka, ka2, kb


HYENA_BATCH = 2


def _hyena_kernel(z_ref, x0_ref, bias_ref, wc_ref, ws_ref, wct_ref, wst_ref, ka_ref, ka2_ref, kb_ref, o_ref, acc_scr):
    f = pl.program_id(1)
    nb = z_ref.shape[0]
    c = z_ref.shape[2]
    z = jnp.concatenate([z_ref[e] for e in range(nb)], axis=1)
    tile = lambda ref: jnp.concatenate([ref[...]] * nb, axis=1)

    @pl.when(f == 0)
    def _():
        acc_scr[...] = z.astype(F32) * tile(bias_ref)

    zc = jnp.dot(wc_ref[...], z, preferred_element_type=F32)
    zs = jnp.dot(ws_ref[...], z, preferred_element_type=F32)
    ka, ka2, kb = tile(ka_ref), tile(ka2_ref), tile(kb_ref)
    a = zc * ka + zs * kb
    bv = zs * ka2 - zc * kb
    acc_scr[...] += (jnp.dot(wct_ref[...], a.astype(BF16), preferred_element_type=F32)
                     + jnp.dot(wst_ref[...], bv.astype(BF16), preferred_element_type=F32))

    @pl.when(f == pl.num_programs(1) - 1)
    def _():
        for e in range(nb):
            o_ref[e] = (x0_ref[e].astype(F32) * acc_scr[:, e * c:(e + 1) * c]).astype(o_ref.dtype)


def _hyena(z, x0, row_block, length, bias, tables, spec):
    b, t, c = z.shape
    wc, ws, wct, wst = tables
    ka, ka2, kb = spec
    fr = min(length, 256)
    nf = length // fr
    nb = HYENA_BATCH
    seq = pl.BlockSpec((nb, length, c), lambda b, f: (b, row_block, 0))
    in_specs = [seq, seq,
                pl.BlockSpec((1, c), lambda b, f: (0, 0)),
                pl.BlockSpec((fr, length), lambda b, f: (f, 0)),
                pl.BlockSpec((fr, length), lambda b, f: (f, 0)),
                pl.BlockSpec((length, fr), lambda b, f: (0, f)),
                pl.BlockSpec((length, fr), lambda b, f: (0, f)),
                pl.BlockSpec((fr, c), lambda b, f: (f, 0)),
                pl.BlockSpec((fr, c), lambda b, f: (f, 0)),
                pl.BlockSpec((fr, c), lambda b, f: (f, 0))]
    return pl.pallas_call(
        _hyena_kernel,
        grid=(b // nb, nf),
        in_specs=in_specs,
        out_specs=pl.BlockSpec((nb, length, c), lambda b, f: (b, 0, 0)),
        out_shape=jax.ShapeDtypeStruct((b, length, c), BF16),
        scratch_shapes=[pltpu.VMEM((length, nb * c), F32)],
        compiler_params=_cp("arbitrary", "arbitrary"),
        name="hyena_%d" % length,
    )(z, x0, bias.reshape(1, c), wc, ws, wct, wst, ka, ka2, kb)


def _rw_prep_kernel(p_ref, pprev_ref, pnext_ref, mu_ref, w0_ref, wup_ref, a0_ref, aup_ref, gup_ref, kk_ref, ka_ref, rk_ref,
                    r_o, v_o, kap_o, lwf_o, lwb_o, kf_o, kb_o, bf_o, bb_o, g_o, bonus_o, *, n_lat_tiles):
    t = pl.program_id(1)
    nt = pl.num_programs(1)
    p = p_ref[0]
    first = jnp.logical_or(t == 0, t == n_lat_tiles)
    last = jnp.logical_or(t == n_lat_tiles - 1, t == nt - 1)
    prev_row = jnp.where(first, 0.0, pprev_ref[0, 7:8, :])
    next_row = jnp.where(last, 0.0, pnext_ref[0, 0:1, :])
    pm, pp = _shift_rows(p, prev_row, next_row)
    p = p + mu_ref[0:1] * (pm - p) + mu_ref[1:2] * (pp - p)
    c = MIX_W
    r, k, v = p[:, 0:c], p[:, c:2 * c], p[:, 2 * c:3 * c]
    wd, ad, gd = p[:, 3 * c:3 * c + 128], p[:, 3 * c + 128:3 * c + 256], p[:, 3 * c + 256:3 * c + 384]
    bd = _head_blockdiag(c, RW_HEAD_DIM)
    g = _mm(_sigmoid(gd), gup_ref[...])
    kk = k * kk_ref[...]
    kap = kk * lax.rsqrt(jnp.maximum(_x_exact01(kk * kk, bd), 1e-24))
    logw = -RW_DECAY_SCALE * _sigmoid(w0_ref[...] + _mm_hp(jnp.tanh(wd), wup_ref[...]))
    a = _sigmoid(a0_ref[...] + _mm_hp(ad, aup_ref[...]))
    bonus_s = 0.0
    outs_k, outs_b = (kf_o, kb_o), (bf_o, bb_o)
    for d in range(2):
        a_d = a[:, d * c:(d + 1) * c]
        k_d = k * (1.0 + (a_d - 1.0) * ka_ref[...])
        outs_k[d][0] = k_d.astype(BF16)
        outs_b[d][0] = (a_d * kap).astype(BF16)
        bonus_s = bonus_s + _x_exact01(r * k_d * rk_ref[...], bd)
    r_o[0] = r.astype(BF16)
    v_o[0] = v.astype(BF16)
    kap_o[0] = kap.astype(BF16)
    lwf_o[0] = logw[:, 0:c]
    lwb_o[0] = logw[:, c:2 * c]
    g_o[0] = g
    bonus_o[0] = bonus_s * v


def _blockdiag2(a, b):
    z1 = jnp.zeros((a.shape[0], b.shape[1]), a.dtype)
    z2 = jnp.zeros((b.shape[0], a.shape[1]), a.dtype)
    return jnp.concatenate([jnp.concatenate([a, z1], 1), jnp.concatenate([z2, b], 1)], 0)


def _rw_prep(p_rw, n_lat_tiles, mu, w0, w_up, a0, a_up, g_up, k_k, k_a, r_k):
    b, t, n = p_rw.shape
    nt = t // TOKEN_TILE
    c = MIX_W
    sub = TOKEN_TILE // 8
    n8 = t // 8
    consts = [mu, w0.reshape(1, 2 * c), _blockdiag2(w_up[0], w_up[1]), a0.reshape(1, 2 * c), _blockdiag2(a_up[0], a_up[1]),
              g_up, k_k.reshape(1, c), k_a.reshape(1, c), r_k.reshape(1, c)]
    cspecs = [pl.BlockSpec(x.shape, lambda b, t: (0, 0)) for x in consts]
    tile = lambda w: pl.BlockSpec((1, TOKEN_TILE, w), lambda b, t: (b, t, 0))
    return pl.pallas_call(
        functools.partial(_rw_prep_kernel, n_lat_tiles=n_lat_tiles),
        grid=(b, nt),
        in_specs=[tile(n),
                  pl.BlockSpec((1, 8, n), lambda b, t: (b, jnp.maximum(t * sub - 1, 0), 0)),
                  pl.BlockSpec((1, 8, n), lambda b, t: (b, jnp.minimum((t + 1) * sub, n8 - 1), 0))] + cspecs,
        out_specs=[tile(c)] * 11,
        out_shape=[jax.ShapeDtypeStruct((b, t, c), dt) for dt in (BF16, BF16, BF16, F32, F32, BF16, BF16, BF16, BF16, F32, F32)],
        compiler_params=_cp("arbitrary", "arbitrary"),
        name="rw_prep",
    )(p_rw, p_rw, p_rw, *consts)


def _rw_scan_kernel(rf, vf, kapf, lwf, kf, bbf, rb, vb, kapb, lwb, kb, bbb, of_ref, ob_ref, s_scr):
    @pl.when(pl.program_id(1) == 0)
    def _():
        s_scr[...] = jnp.zeros_like(s_scr)

    c = rf.shape[1]
    pw_ = 2 * RW_HEAD_DIM
    n_pairs = MIX_W // pw_
    lane_a = lax.broadcasted_iota(jnp.int32, (c, pw_), 1) < RW_HEAD_DIM
    lane_a2 = lax.broadcasted_iota(jnp.int32, (2 * c, pw_), 1) < RW_HEAD_DIM
    blk = ((lax.broadcasted_iota(jnp.int32, (pw_, pw_), 0) < RW_HEAD_DIM)
           == (lax.broadcasted_iota(jnp.int32, (pw_, pw_), 1) < RW_HEAD_DIM))

    def sel(x):
        return jnp.where(lane_a, x[:c], x[c:])

    def bf(x):
        return x.astype(BF16)

    nb = rf.shape[0]
    chains = []
    for e in range(nb):
        for d, (r, v, kap, lw, k, b) in enumerate(((rf, vf, kapf, lwf, kf, bbf), (rb, vb, kapb, lwb, kb, bbb))):
            rev = d == 1
            incl = _tri(c, rev, False)
            strict = _tri(c, rev, True)
            logw = lw[e]
            cum = _mm_exact01(incl.astype(F32), logw)
            tot = cum[0:1] if rev else cum[c - 1:c]
            ginv = jnp.exp(-cum)
            gt = jnp.exp(tot - cum)
            rt = r[e] * jnp.exp(cum)
            kt = kap[e] * jnp.exp(cum - logw)
            kd, bd = k[e] * ginv, b[e] * ginv
            kg, bg = k[e] * gt, b[e] * gt
            dec = jnp.exp(tot)
            vv = v[e]
            for p in range(n_pairs):
                sl = slice(p * pw_, (p + 1) * pw_)
                lhs2 = jnp.concatenate([rt[:, sl], kt[:, sl]], 0)
                chains.append(dict(
                    e=e, d=d, p=p, idx=(e * 2 + d) * n_pairs + p, incl=incl, strict=strict,
                    lhs2=bf(lhs2),
                    lhs4=bf(jnp.concatenate([jnp.where(lane_a2, lhs2, 0.0), jnp.where(lane_a2, 0.0, lhs2)], 0)),
                    kd=bf(kd[:, sl]), bd=bf(bd[:, sl]), v=bf(vv[:, sl]),
                    kgbg=bf(jnp.concatenate([kg[:, sl], bg[:, sl]], 0)), dec=dec[:, sl]))

    dot = functools.partial(lax.dot_general, preferred_element_type=F32)
    nn = (((1,), (0,)), ((), ()))
    for ch in chains:
        ch["s"] = s_scr[ch["idx"]]
    for ch in chains:
        ch["a1k"] = dot(ch["lhs4"], ch["kd"], NT_DIMS)
        ch["a1b"] = dot(ch["lhs4"], ch["bd"], NT_DIMS)
        ch["a2"] = dot(ch["lhs2"], bf(ch["s"]), NT_DIMS)
    for ch in chains:
        a1k, a1b, st = ch["a1k"], ch["a1b"], ch["strict"]
        lk = jnp.where(jnp.concatenate([st, st], 0), jnp.concatenate([a1k[c:2 * c], a1k[3 * c:]], 0), 0.0)
        ch["lb_a"] = bf(jnp.where(st, a1b[c:2 * c], 0.0))
        ch["lb_b"] = bf(jnp.where(st, a1b[3 * c:], 0.0))
        ch["u"] = ch["a2"][c:] + sel(dot(bf(lk), ch["v"], nn))
    for ch in chains:
        ch["u"] = ch["u"] - sel(dot(jnp.concatenate([ch["lb_a"], ch["lb_b"]], 0), bf(ch["u"]), nn))
        ch["p_a"] = bf(dot(ch["lb_a"], ch["lb_a"], nn))
        ch["p_b"] = bf(dot(ch["lb_b"], ch["lb_b"], nn))
    n = int(math.log2(c)) - 1
    for i in range(n):
        for ch in chains:
            ch["u"] = ch["u"] + sel(dot(jnp.concatenate([ch["p_a"], ch["p_b"]], 0), bf(ch["u"]), nn))
            if i < n - 1:
                ch["p_a"] = bf(dot(ch["p_a"], ch["p_a"], nn))
                ch["p_b"] = bf(dot(ch["p_b"], ch["p_b"], nn))
    outs = [[[None] * n_pairs, [None] * n_pairs] for _ in range(nb)]
    for ch in chains:
        a1k, a1b, inc = ch["a1k"], ch["a1b"], ch["incl"]
        inc2 = jnp.concatenate([inc, inc], 0)
        rk = jnp.where(inc2, jnp.concatenate([a1k[:c], a1k[2 * c:3 * c]], 0), 0.0)
        rb_ = jnp.where(inc2, jnp.concatenate([a1b[:c], a1b[2 * c:3 * c]], 0), 0.0)
        ub = bf(ch["u"])
        o = ch["a2"][:c] + sel(dot(bf(rk), ch["v"], nn)) - sel(dot(bf(rb_), ub, nn))
        outs[ch["e"]][ch["d"]][ch["p"]] = o
        upd = dot(jnp.concatenate([ch["v"], -ub], 0), ch["kgbg"], TN_DIMS)
        s_scr[ch["idx"]] = ch["s"] * ch["dec"] + jnp.where(blk, upd, 0.0)
    for e in range(nb):
        of_ref[e] = jnp.concatenate(outs[e][0], axis=1)
        ob_ref[e] = jnp.concatenate(outs[e][1], axis=1)


def _chunk_orders(n_lat, n_ctx):
    n = n_lat + n_ctx
    fwd = lambda b, s: (b, (s + n_lat) % n, 0)
    rev = lambda b, s: (b, n - 1 - s, 0)
    return fwd, rev


def _rw_scan(r, v, kap, lwf, lwb, kf, kb, bf, bb, n_lat_rows):
    b, t, c = r.shape
    ch = SEQ_CHUNK
    n = t // ch
    fwd, rev = _chunk_orders(n_lat_rows // ch, (t - n_lat_rows) // ch)
    nb = SCAN_BATCH
    blk = lambda im: pl.BlockSpec((nb, ch, c), im)
    return pl.pallas_call(
        _rw_scan_kernel,
        grid=(b // nb, n),
        in_specs=[blk(fwd)] * 6 + [blk(rev)] * 6,
        out_specs=[blk(fwd), blk(rev)],
        out_shape=[jax.ShapeDtypeStruct((b, t, c), F32)] * 2,
        scratch_shapes=[pltpu.VMEM((nb * RW_HEADS, 2 * RW_HEAD_DIM, 2 * RW_HEAD_DIM), F32)],
        compiler_params=_cp("arbitrary", "arbitrary"),
        name="rw_scan",
    )(r, v, kap, lwf, kf, bf, r, v, kap, lwb, kb, bb)


def _rw_readout(of, ob, bonus, g, ln_g, ln_b):
    o = of + ob
    bd = _head_blockdiag(MIX_W, RW_HEAD_DIM)
    mean = _x_exact01(o, bd) * (1.0 / RW_HEAD_DIM)
    oc = o - mean
    var = _x_exact01(oc * oc, bd) * (1.0 / RW_HEAD_DIM)
    on = oc * lax.rsqrt(var + RW_GN_EPS) * ln_g + ln_b
    return (on + bonus) * g


def _s5_matrices(a_re, a_im, log_dt, b_re, b_im, c_re, c_im):
    tc, g, n, p = S5_CHUNK, S5_GROUPS, S5_STATE, S5_GROUP
    hp = lax.Precision.HIGHEST
    dt = jnp.exp(log_dt)[:, :, None]
    lam_re, lam_im = a_re * dt, a_im * dt
    mag1 = jnp.exp(lam_re)
    ab_re, ab_im = mag1 * jnp.cos(lam_im), mag1 * jnp.sin(lam_im)
    nr, ni = ab_re - 1.0, ab_im
    den = a_re * a_re + a_im * a_im
    f_re, f_im = (nr * a_re + ni * a_im) / den, (ni * a_re - nr * a_im) / den
    tau = jnp.arange(tc + 1, dtype=F32)[:, None, None, None]
    pw_mag = jnp.exp(lam_re[None] * tau)
    pw_re, pw_im = pw_mag * jnp.cos(lam_im[None] * tau), pw_mag * jnp.sin(lam_im[None] * tau)
    fb_re = f_re[..., None] * b_re[None] - f_im[..., None] * b_im[None]
    fb_im = f_re[..., None] * b_im[None] + f_im[..., None] * b_re[None]
    e_re = pw_re[..., None] * fb_re[None] - pw_im[..., None] * fb_im[None]
    e_im = pw_re[..., None] * fb_im[None] + pw_im[..., None] * fb_re[None]
    m = (jnp.einsum('gpn,tdgnq->tdgpq', c_re, e_re[:tc], precision=hp)
         - jnp.einsum('gpn,tdgnq->tdgpq', c_im, e_im[:tc], precision=hp))
    i = jnp.arange(tc)
    lag = i[None, :] - i[:, None]
    mf = jnp.where((lag >= 0)[:, :, None, None, None], m[jnp.clip(lag, 0, tc - 1), 0], 0.0)
    mb = jnp.where((lag <= 0)[:, :, None, None, None], m[jnp.clip(-lag, 0, tc - 1), 1], 0.0)
    w = jnp.transpose(mf + mb, (2, 0, 4, 1, 3)).reshape(g, tc * p, tc * p)
    idx_f = tc - 1 - i
    pf_re, pf_im = e_re[idx_f, 0], e_im[idx_f, 0]
    pb_re, pb_im = e_re[i, 1], e_im[i, 1]
    pm = jnp.concatenate([pf_re, pb_re, pf_im, pb_im], axis=2)
    pm = jnp.transpose(pm, (1, 0, 3, 2)).reshape(g, tc * p, 4 * n)
    cf_re = c_re[None, :, :, :] * pw_re[i + 1, 0][:, :, None, :] - c_im[None] * pw_im[i + 1, 0][:, :, None, :]
    cf_im = c_re[None] * pw_im[i + 1, 0][:, :, None, :] + c_im[None] * pw_re[i + 1, 0][:, :, None, :]
    cb_re = c_re[None] * pw_re[tc - i, 1][:, :, None, :] - c_im[None] * pw_im[tc - i, 1][:, :, None, :]
    cb_im = c_re[None] * pw_im[tc - i, 1][:, :, None, :] + c_im[None] * pw_re[tc - i, 1][:, :, None, :]
    q = jnp.concatenate([cf_re, cb_re, -cf_im, -cb_im], axis=3)
    q = jnp.transpose(q, (1, 3, 0, 2)).reshape(g, 4 * n, tc * p)
    dec_re = jnp.concatenate([pw_re[tc, 0], pw_re[tc, 1]], axis=1)
    dec_im = jnp.concatenate([pw_im[tc, 0], pw_im[tc, 1]], axis=1)
    dec = jnp.stack([dec_re, dec_im], axis=1)
    return w.astype(BF16), pm.astype(BF16), q.astype(BF16), dec


S5_PACK_BATCH = 2
S5_LANE_GROUPS = 128 // S5_GROUP


def _s5_pack_kernel(x_ref, o_ref):
    rows = o_ref.shape[0]
    grp = lax.broadcasted_iota(jnp.int32, (rows, 128), 1) // S5_GROUP
    tiles = [jnp.zeros((rows, 128), F32) for _ in range(2 * S5_LANE_GROUPS)]
    for j in range(S5_CHUNK):
        xj = x_ref[pl.ds(j, rows, stride=S5_CHUNK), :]
        jl, jh = j % S5_LANE_GROUPS, j // S5_LANE_GROUPS
        for g8 in range(S5_LANE_GROUPS):
            sh = ((jl - g8) * S5_GROUP) % 128
            moved = pltpu.roll(xj, sh, 1) if sh else xj
            tiles[2 * g8 + jh] = jnp.where(grp == jl, moved, tiles[2 * g8 + jh])
    o_ref[...] = jnp.concatenate(tiles, axis=1).astype(o_ref.dtype)


def _s5_unpack_kernel(y_ref, o_ref):
    rows = y_ref.shape[0]
    grp = lax.broadcasted_iota(jnp.int32, (rows, 128), 1) // S5_GROUP
    for i in range(S5_CHUNK):
        il, ih = i % S5_LANE_GROUPS, i // S5_LANE_GROUPS
        out = jnp.zeros((rows, 128), F32)
        for g8 in range(S5_LANE_GROUPS):
            src = y_ref[:, (2 * g8 + ih) * 128:(2 * g8 + ih + 1) * 128]
            sh = ((g8 - il) * S5_GROUP) % 128
            moved = pltpu.roll(src, sh, 1) if sh else src
            out = jnp.where(grp == g8, moved, out)
        o_ref[pl.ds(i, rows, stride=S5_CHUNK), :] = out


def _s5_kernel(u_ref, w_ref, p_ref, q_ref, dec_ref, y_ref, inc_re, inc_im, sf_re, sf_im, sb_re, sb_im, *, n_lat, n_ctx, bsz):
    u = u_ref[...]
    inc = jnp.dot(u, p_ref[0], preferred_element_type=F32)
    n2 = 2 * S5_STATE
    inc_re[...] = inc[:, :n2]
    inc_im[...] = inc[:, n2:]
    n = n_lat + n_ctx
    d_re = dec_ref[0, 0:1, :]
    d_im = dec_ref[0, 1:2, :]
    is_f = lax.broadcasted_iota(jnp.int32, (bsz, n2), 1) < S5_STATE

    def body(s, carry):
        s_re, s_im = carry
        rf = pl.ds((s + n_lat) % n, bsz, stride=n)
        rb = pl.ds(n - 1 - s, bsz, stride=n)
        sf_re[rf, :] = s_re
        sf_im[rf, :] = s_im
        sb_re[rb, :] = s_re
        sb_im[rb, :] = s_im
        i_re = jnp.where(is_f, inc_re[rf, :], inc_re[rb, :])
        i_im = jnp.where(is_f, inc_im[rf, :], inc_im[rb, :])
        return (d_re * s_re - d_im * s_im + i_re, d_re * s_im + d_im * s_re + i_im)

    zero = jnp.zeros((bsz, n2), F32)
    lax.fori_loop(0, n, body, (zero, zero))
    is_f_all = lax.broadcasted_iota(jnp.int32, sf_re.shape, 1) < S5_STATE
    s_all = jnp.concatenate([jnp.where(is_f_all, sf_re[...], sb_re[...]),
                             jnp.where(is_f_all, sf_im[...], sb_im[...])], axis=1).astype(BF16)
    y_ref[...] = jnp.dot(u, w_ref[0], preferred_element_type=F32) + jnp.dot(s_all, q_ref[0], preferred_element_type=F32)


def _s5_core(p_s5, n_lat_rows, mats):
    b, t, c = p_s5.shape
    tc, g = S5_CHUNK, S5_GROUPS
    nch = t // tc
    w, pm, q, dec = mats
    k = tc * S5_GROUP
    nbb = S5_PACK_BATCH
    n_tiles = c // 128
    tok_blk = pl.BlockSpec((nbb * t, 128), lambda i, o: (i, o))
    chk_blk = pl.BlockSpec((nbb * nch, S5_LANE_GROUPS * k), lambda i, o: (i, o))
    u = pl.pallas_call(
        _s5_pack_kernel,
        grid=(b // nbb, n_tiles),
        in_specs=[tok_blk],
        out_specs=chk_blk,
        out_shape=jax.ShapeDtypeStruct((b * nch, g * k), BF16),
        compiler_params=_cp("arbitrary", "arbitrary"),
        name="s5_pack",
    )(p_s5.reshape(b * t, c))
    rows = nch * b
    y = pl.pallas_call(
        functools.partial(_s5_kernel, n_lat=n_lat_rows // tc, n_ctx=(t - n_lat_rows) // tc, bsz=b),
        grid=(g,),
        in_specs=[pl.BlockSpec((rows, k), lambda i: (0, i)),
                  pl.BlockSpec((1, k, k), lambda i: (i, 0, 0)),
                  pl.BlockSpec((1, k, 4 * S5_STATE), lambda i: (i, 0, 0)),
                  pl.BlockSpec((1, 4 * S5_STATE, k), lambda i: (i, 0, 0)),
                  pl.BlockSpec((1, 2, 2 * S5_STATE), lambda i: (i, 0, 0))],
        out_specs=pl.BlockSpec((rows, k), lambda i: (0, i)),
        out_shape=jax.ShapeDtypeStruct((rows, g * k), F32),
        scratch_shapes=[pltpu.VMEM((rows, 2 * S5_STATE), F32)] * 6,
        compiler_params=_cp("arbitrary"),
        name="s5_core",
    )(u, w, pm, q, dec)
    out = pl.pallas_call(
        _s5_unpack_kernel,
        grid=(b // nbb, n_tiles),
        in_specs=[chk_blk],
        out_specs=tok_blk,
        out_shape=jax.ShapeDtypeStruct((b * t, c), F32),
        compiler_params=_cp("arbitrary", "arbitrary"),
        name="s5_unpack",
    )(y)
    return out.reshape(b, t, c)


def _s5_output(y, u, d_skip, glu_w, glu_b):
    z = y + d_skip * u
    z = 0.5 * z * (1.0 + jnp.tanh(math.sqrt(2.0 / math.pi) * (z + 0.044715 * (z * z * z))))
    return z * _sigmoid(_mm(z, glu_w) + glu_b)


def _gla_scan_kernel(pfv_ref, pfc_ref, pbv_ref, pbc_ref, gup_ref, gb_ref, of_ref, ob_ref, s_scr, *, n_lat, n_ctx, cols, c):
    step = pl.program_id(1)

    @pl.when(step == 0)
    def _():
        s_scr[...] = jnp.zeros_like(s_scr)

    subs = pfc_ref.shape[1] // c
    n = n_lat + n_ctx
    block_ids = ((step + n_lat) % n, n - 1 - step)
    pw_ = 2 * GLA_HK
    n_pairs = GLA_DK // pw_
    lane_a = lax.broadcasted_iota(jnp.int32, (c, pw_), 1) < GLA_HK
    lane_av = lax.broadcasted_iota(jnp.int32, (GLA_HV, pw_), 1) < GLA_HK
    dot = functools.partial(lax.dot_general, preferred_element_type=F32)
    nn = (((1,), (0,)), ((), ()))
    nb = pfc_ref.shape[0]
    mask2 = lambda a: jnp.concatenate([jnp.where(lane_a, a, 0.0), jnp.where(lane_a, 0.0, a)], 0).astype(BF16)
    for sub in range(subs):
        chains = []
        for e in range(nb):
            for d, (v_ref, c_ref) in enumerate(((pfv_ref, pfc_ref), (pbv_ref, pbc_ref))):
                rev = d == 1
                cs = subs - 1 - sub if rev else sub
                incl = _tri(c, rev, False)
                lat_rows = jnp.concatenate([v_ref[e, :, cs * cols + j, :] for j in range(cols)], axis=0)
                pe = jnp.where(block_ids[d] < n_lat, lat_rows, c_ref[e, cs * c:(cs + 1) * c, :])
                gd = pe[:, 2 * GLA_DK + 2 * GLA_DV:]
                x = _mm_hp(gd, gup_ref[...])[:, d * GLA_DK:(d + 1) * GLA_DK] + gb_ref[0:1, d * GLA_DK:(d + 1) * GLA_DK]
                logg = (jnp.minimum(x, 0.0) - jnp.log(1.0 + jnp.exp(-jnp.abs(x)))) * (1.0 / GLA_GATE_NORM)
                cum = _mm_exact01(incl.astype(F32), logg)
                mid = c // 2
                ref = cum[mid:mid + 1] if rev else cum[mid - 1:mid]
                tot = cum[0:1] if rev else cum[c - 1:c]
                q = pe[:, 0:GLA_DK] * (GLA_HK ** -0.5)
                k = pe[:, GLA_DK:2 * GLA_DK]
                qs = q * jnp.exp(cum - ref)
                ks = k * jnp.exp(ref - cum)
                qc = q * jnp.exp(cum)
                kg = k * jnp.exp(tot - cum)
                dec = jnp.exp(tot)
                for p in range(n_pairs):
                    sl = slice(p * pw_, (p + 1) * pw_)
                    chains.append(dict(
                        e=e, d=d, p=p, idx=(e * 2 + d) * n_pairs + p, incl=incl, qs4=mask2(qs[:, sl]), qc4=mask2(qc[:, sl]),
                        ks=ks[:, sl].astype(BF16), kg=kg[:, sl].astype(BF16), dec=dec[:, sl],
                        va=pe[:, 2 * GLA_DK + 2 * p * GLA_HV:2 * GLA_DK + (2 * p + 1) * GLA_HV].astype(BF16),
                        vb=pe[:, 2 * GLA_DK + (2 * p + 1) * GLA_HV:2 * GLA_DK + (2 * p + 2) * GLA_HV].astype(BF16)))
        for ch in chains:
            ch["s"] = s_scr[ch["idx"]]
        for ch in chains:
            ch["sc"] = dot(ch["qs4"], ch["ks"], NT_DIMS)
            ch["st"] = dot(ch["qc4"], ch["s"].astype(BF16), NT_DIMS)
        outs = [[[None] * GLA_HEADS, [None] * GLA_HEADS] for _ in range(nb)]
        for ch in chains:
            e, d, p = ch["e"], ch["d"], ch["p"]
            sa = jnp.where(ch["incl"], ch["sc"][:c], 0.0).astype(BF16)
            sb = jnp.where(ch["incl"], ch["sc"][c:], 0.0).astype(BF16)
            outs[e][d][2 * p] = ch["st"][:c] + dot(sa, ch["va"], nn)
            outs[e][d][2 * p + 1] = ch["st"][c:] + dot(sb, ch["vb"], nn)
            upd = jnp.where(lane_av, dot(ch["va"], ch["kg"], TN_DIMS), dot(ch["vb"], ch["kg"], TN_DIMS))
            s_scr[ch["idx"]] = ch["s"] * ch["dec"] + upd
        for e in range(nb):
            of_ref[e, sub * c:(sub + 1) * c, :] = jnp.concatenate(outs[e][0], axis=1)
            ob_ref[e, (subs - 1 - sub) * c:(subs - sub) * c, :] = jnp.concatenate(outs[e][1], axis=1)


def _gla_scan(p_gla, n_lat_rows, gate_up, gate_b):
    b, t, n = p_gla.shape
    ch = SEQ_CHUNK
    grid_rows = n_lat_rows // GRID_W
    cols = _grid_cols(n_lat_rows, ch)
    blk_rows = GRID_COL_BLOCK * grid_rows
    assert GRID_COL_BLOCK % cols == 0 and blk_rows % ch == 0 and (t - n_lat_rows) % blk_rows == 0
    n_lat, n_ctx = n_lat_rows // blk_rows, (t - n_lat_rows) // blk_rows
    n_blk = n_lat + n_ctx
    fwd, rev = _chunk_orders(n_lat, n_ctx)
    gup = jnp.zeros((128, 2 * GLA_DK), F32)
    gup = gup.at[0:GLA_GATE_RANK, 0:GLA_DK].set(gate_up[0]).at[GLA_GATE_RANK:2 * GLA_GATE_RANK, GLA_DK:].set(gate_up[1])
    nb = SCAN_BATCH
    p_view = p_gla.reshape(b, t // GRID_W, GRID_W, n)
    view_blk = lambda bid: pl.BlockSpec((nb, grid_rows, GRID_COL_BLOCK, n), lambda b, s: (b, 0, jnp.minimum(bid(s), n_lat - 1), 0))
    ctx_blk = lambda bid: pl.BlockSpec((nb, blk_rows, n), lambda b, s: (b, jnp.maximum(bid(s), n_lat), 0))
    bid_f = lambda s: (s + n_lat) % n_blk
    bid_b = lambda s: n_blk - 1 - s
    return pl.pallas_call(
        functools.partial(_gla_scan_kernel, n_lat=n_lat, n_ctx=n_ctx, cols=cols, c=ch),
        grid=(b // nb, n_blk),
        in_specs=[view_blk(bid_f), ctx_blk(bid_f), view_blk(bid_b), ctx_blk(bid_b),
                  pl.BlockSpec((128, 2 * GLA_DK), lambda b, s: (0, 0)),
                  pl.BlockSpec((1, 2 * GLA_DK), lambda b, s: (0, 0))],
        out_specs=[pl.BlockSpec((nb, blk_rows, GLA_DV), fwd), pl.BlockSpec((nb, blk_rows, GLA_DV), rev)],
        out_shape=[jax.ShapeDtypeStruct((b, t, GLA_DV), F32)] * 2,
        scratch_shapes=[pltpu.VMEM((nb * GLA_HEADS, GLA_HV, 2 * GLA_HK), F32)],
        compiler_params=_cp("arbitrary", "arbitrary"),
        name="gla_scan",
    )(p_view, p_gla, p_view, p_gla, gup, gate_b.reshape(1, 2 * GLA_DK))


def _gla_readout_kernel(of_ref, ob_ref, gv_ref, gc_ref, ng_ref, ylat_ref, yctx_ref, *, n_lat_tiles, cols):
    t = pl.program_id(1)
    o = of_ref[0] + ob_ref[0]
    bd = _head_blockdiag(GLA_DV, GLA_HV)
    ms = _x_exact01(o * o, bd) * (1.0 / GLA_HV)
    g_lo = 2 * GLA_DK + GLA_DV
    g = jnp.where(t < n_lat_tiles, _col_major_rows(gv_ref, 0, 0, cols, g_lo, g_lo + GLA_DV), gc_ref[0])
    y = (o * lax.rsqrt(ms + RMS_EPS) * ng_ref[...] * _silu(g)).astype(ylat_ref.dtype)
    rows = y.shape[0] // cols

    @pl.when(t < n_lat_tiles)
    def _():
        for j in range(cols):
            ylat_ref[0, :, j, :] = y[j * rows:(j + 1) * rows]

    @pl.when(t >= n_lat_tiles)
    def _():
        yctx_ref[0] = y


def _gla_readout(of, ob, p_gla, norm_g, n_lat_rows):
    b, t, c = of.shape
    n = p_gla.shape[2]
    n_lat_tiles = n_lat_rows // TOKEN_TILE
    grid_rows = n_lat_rows // GRID_W
    cols = _grid_cols(n_lat_rows, TOKEN_TILE)
    assert cols == GRID_COL_BLOCK
    tile = pl.BlockSpec((1, TOKEN_TILE, c), lambda b, t: (b, t, 0))
    p_view = p_gla.reshape(b, t // GRID_W, GRID_W, n)
    y_lat, y_ctx = pl.pallas_call(
        functools.partial(_gla_readout_kernel, n_lat_tiles=n_lat_tiles, cols=cols),
        grid=(b, t // TOKEN_TILE),
        in_specs=[tile, tile,
                  pl.BlockSpec((1, grid_rows, cols, n), lambda b, t: (b, 0, jnp.minimum(t, n_lat_tiles - 1), 0)),
                  pl.BlockSpec((1, TOKEN_TILE, c), lambda b, t: (b, jnp.maximum(t, n_lat_tiles), (2 * GLA_DK + GLA_DV) // c)),
                  pl.BlockSpec((1, c), lambda b, t: (0, 0))],
        out_specs=[pl.BlockSpec((1, grid_rows, cols, c), lambda b, t: (b, 0, jnp.minimum(t, n_lat_tiles - 1), 0)),
                   pl.BlockSpec((1, TOKEN_TILE, c), lambda b, t: (b, 0, 0))],
        out_shape=[jax.ShapeDtypeStruct((b, grid_rows, GRID_W, c), BF16), jax.ShapeDtypeStruct((b, t - n_lat_rows, c), BF16)],
        compiler_params=_cp("arbitrary", "arbitrary"),
        name="gla_readout",
    )(of, ob, p_view, p_gla, jnp.tile(norm_g, GLA_HEADS).reshape(1, c))
    return y_lat.reshape(b, n_lat_rows, c), y_ctx


def _merge_kernel(h_ref, mod_ref, gate_ref, yhyl_ref, yhyc_ref, rof_ref, rob_ref, rbonus_ref, rg_ref, ys5_ref, us5_ref,
                  yglal_ref, yglac_ref, lng_ref, lnb_ref, dskip_ref, gluw_ref, glub_ref, wbr_ref, wout_ref, o_ref,
                  *, n_lat_tiles):
    d = D_MODEL
    is_lat = pl.program_id(1) < n_lat_tiles
    y_hy = jnp.where(is_lat, yhyl_ref[0], yhyc_ref[0])
    y_gla = jnp.where(is_lat, yglal_ref[0], yglac_ref[0])
    y_rw = _rw_readout(rof_ref[0], rob_ref[0], rbonus_ref[0], rg_ref[0], lng_ref[...], lnb_ref[...])
    y_s5 = _s5_output(ys5_ref[0], us5_ref[0], dskip_ref[...], gluw_ref[...], glub_ref[...])
    acc = jnp.zeros((h_ref.shape[1], d), F32)
    for m, y in enumerate((y_hy, y_rw, y_s5, y_gla)):
        acc = acc + (_sigmoid(gate_ref[0, :, m * d:(m + 1) * d].astype(F32))
                     * jnp.dot(y.astype(BF16), wbr_ref[m], preferred_element_type=F32))
    out = jnp.dot(acc.astype(BF16), wout_ref[...], preferred_element_type=F32)
    o_ref[0] = h_ref[0] + mod_ref[0, 0, 2:3, :] * out


def _merge(h_all, modsel, n_tiles, n_lat_tiles, gate, y_hy_lat, y_hy_ctx, rof, rob, rbonus, rg, y_s5, u_s5, y_gla_lat, y_gla_ctx,
           ln_g, ln_b, d_skip, glu_w, glu_b, w_branch, w_out):
    b, t, d = h_all.shape
    c = MIX_W
    tile = lambda w: pl.BlockSpec((1, TOKEN_TILE, w), lambda b, t: (b, t, 0))
    lat = pl.BlockSpec((1, TOKEN_TILE, c), lambda b, t: (b, jnp.minimum(t, n_lat_tiles - 1), 0))
    ctx = pl.BlockSpec((1, TOKEN_TILE, c), lambda b, t: (b, jnp.maximum(t - n_lat_tiles, 0), 0))
    vec = lambda x: x.reshape(1, -1)
    consts = [vec(ln_g), vec(ln_b), vec(d_skip), glu_w.astype(BF16), vec(glu_b), w_branch.astype(BF16), w_out.astype(BF16)]
    cspecs = [pl.BlockSpec(x.shape, (lambda n: lambda b, t: (0,) * n)(x.ndim)) for x in consts]
    return pl.pallas_call(
        functools.partial(_merge_kernel, n_lat_tiles=n_lat_tiles),
        grid=(b, n_tiles),
        in_specs=[tile(d), pl.BlockSpec((1, 1, 6, d), _mod_index(n_lat_tiles)), tile(4 * d), lat, ctx] + [tile(c)] * 6
                 + [lat, ctx] + cspecs,
        out_specs=tile(d),
        out_shape=jax.ShapeDtypeStruct((b, t, d), F32),
        input_output_aliases={0: 0},
        compiler_params=_cp("arbitrary", "arbitrary"),
        name="merge",
    )(h_all, modsel, gate, y_hy_lat, y_hy_ctx, rof, rob, rbonus, rg, y_s5, u_s5, y_gla_lat, y_gla_ctx, *consts)


def _ffn_kernel(h_ref, g_ref, mod_ref, w1_ref, w3_ref, w2_ref, o_ref):
    h = h_ref[0]
    xm = _norm_mod(h, g_ref[...], mod_ref[0, 0], 3).astype(BF16)
    a = jnp.dot(xm, w1_ref[...], preferred_element_type=F32)
    b = jnp.dot(xm, w3_ref[...], preferred_element_type=F32)
    y = jnp.dot((_silu(a) * b).astype(BF16), w2_ref[...], preferred_element_type=F32)
    o_ref[0] = h + mod_ref[0, 0, 5:6, :] * y


def _ffn_dense(h_all, g, modsel, n_lat_tiles, w1, w3, w2):
    b, t, d = h_all.shape
    tile = pl.BlockSpec((1, TOKEN_TILE, d), lambda b, t: (b, t, 0))
    ws = [w1.astype(BF16), w3.astype(BF16), w2.astype(BF16)]
    wspec = [pl.BlockSpec(w.shape, lambda b, t: (0, 0), pipeline_mode=pl.Buffered(1)) for w in ws]
    return pl.pallas_call(
        _ffn_kernel,
        grid=(b, t // TOKEN_TILE),
        in_specs=[tile, pl.BlockSpec((1, d), lambda b, t: (0, 0)), pl.BlockSpec((1, 1, 6, d), _mod_index(n_lat_tiles))] + wspec,
        out_specs=tile,
        out_shape=jax.ShapeDtypeStruct((b, t, d), F32),
        input_output_aliases={0: 0},
        compiler_params=_cp("arbitrary", "arbitrary"),
        name="ffn_dense",
    )(h_all, g.reshape(1, d), modsel, *ws)


def _router_kernel(h_ref, g_ref, mod_ref, rw_ref, rb_ref, x_ref, logit_ref):
    xm = _norm_mod(h_ref[0], g_ref[...], mod_ref[0, 0], 3)
    x_ref[0] = xm.astype(BF16)
    logit_ref[0] = _mm_hp(xm, rw_ref[...]) + rb_ref[...]


def _router(h_all, g, modsel, n_lat_tiles, router_w, router_b):
    b, t, d = h_all.shape
    n_lat = n_lat_tiles * TOKEN_TILE
    rw = jnp.pad(router_w, ((0, 0), (0, 128 - N_EXPERTS)))
    rb = jnp.pad(router_b, (0, 128 - N_EXPERTS), constant_values=-1e30).reshape(1, 128)
    tile = lambda w: pl.BlockSpec((1, TOKEN_TILE, w), lambda b, t: (b, t, 0))
    return pl.pallas_call(
        _router_kernel,
        grid=(b, n_lat_tiles),
        in_specs=[tile(d), pl.BlockSpec((1, d), lambda b, t: (0, 0)), pl.BlockSpec((1, 1, 6, d), _mod_index(n_lat_tiles)),
                  pl.BlockSpec((d, 128), lambda b, t: (0, 0)), pl.BlockSpec((1, 128), lambda b, t: (0, 0))],
        out_specs=[tile(d), tile(128)],
        out_shape=[jax.ShapeDtypeStruct((b, n_lat, d), BF16), jax.ShapeDtypeStruct((b, n_lat, 128), F32)],
        compiler_params=_cp("arbitrary", "arbitrary"),
        name="moe_router",
    )(h_all, g.reshape(1, d), modsel, rw, rb)


def _moe_expert_kernel(be_ref, x_ref, w1_ref, w3_ref, w2_ref, o_ref):
    x = x_ref[...]
    a = jnp.dot(x, w1_ref[0], preferred_element_type=F32)
    b = jnp.dot(x, w3_ref[0], preferred_element_type=F32)
    o_ref[...] = jnp.dot((_silu(a) * b).astype(BF16), w2_ref[0], preferred_element_type=F32).astype(o_ref.dtype)


def _moe_experts(xb, block_exp, w1, w3, w2):
    rows, d = xb.shape
    nb = rows // MOE_BLOCK_ROWS
    f = w1.shape[2]
    grid_spec = pltpu.PrefetchScalarGridSpec(
        num_scalar_prefetch=1,
        grid=(nb,),
        in_specs=[pl.BlockSpec((MOE_BLOCK_ROWS, d), lambda i, be: (i, 0)),
                  pl.BlockSpec((1, d, f), lambda i, be: (be[i], 0, 0), pipeline_mode=pl.Buffered(1)),
                  pl.BlockSpec((1, d, f), lambda i, be: (be[i], 0, 0), pipeline_mode=pl.Buffered(1)),
                  pl.BlockSpec((1, f, d), lambda i, be: (be[i], 0, 0), pipeline_mode=pl.Buffered(1))],
        out_specs=pl.BlockSpec((MOE_BLOCK_ROWS, d), lambda i, be: (i, 0)),
    )
    return pl.pallas_call(
        _moe_expert_kernel,
        grid_spec=grid_spec,
        out_shape=jax.ShapeDtypeStruct((rows, d), BF16),
        compiler_params=_cp("arbitrary"),
        name="moe_experts",
    )(block_exp, xb, w1, w3, w2)


def _moe_combine_kernel(h_ref, mod_ref, y_ref, gate_ref, fg_ref, o_ref):
    d = h_ref.shape[2]
    y = y_ref[0, :, :d].astype(F32) * gate_ref[0, :, 0:1] + y_ref[0, :, d:].astype(F32) * gate_ref[0, :, 1:2]
    h = h_ref[0] + mod_ref[0, 0, 5:6, :] * y
    o_ref[0] = h * lax.rsqrt(jnp.mean(h * h, axis=-1, keepdims=True) + RMS_EPS) * fg_ref[...]


def _moe_combine(h_all, modsel, n_lat_tiles, y2, gates, final_g):
    b, t, d = h_all.shape
    n_lat = n_lat_tiles * TOKEN_TILE
    tile = lambda w: pl.BlockSpec((1, TOKEN_TILE, w), lambda b, t: (b, t, 0))
    return pl.pallas_call(
        _moe_combine_kernel,
        grid=(b, n_lat_tiles),
        in_specs=[tile(d), pl.BlockSpec((1, 1, 6, d), _mod_index(n_lat_tiles)),
                  tile(TOP_K * d),
                  tile(128), pl.BlockSpec((1, d), lambda b, t: (0, 0))],
        out_specs=tile(d),
        out_shape=jax.ShapeDtypeStruct((b, n_lat, d), F32),
        compiler_params=_cp("arbitrary", "arbitrary"),
        name="moe_combine",
    )(h_all, modsel, y2, gates, final_g.reshape(1, d))


def _moe_ffn_and_final_norm(h_all, g, modsel, n_lat_tiles, router_w, router_b, w1, w3, w2, final_g):
    b, t, d = h_all.shape
    n_lat = n_lat_tiles * TOKEN_TILE
    n_tok = b * n_lat
    xn, logits = _router(h_all, g, modsel, n_lat_tiles, router_w, router_b)
    logits = logits.reshape(n_tok, 128)[:, :N_EXPERTS]
    top_v, top_i = lax.top_k(logits, TOP_K)
    gates = jax.nn.softmax(top_v, axis=-1)
    n_assign = n_tok * TOP_K
    e_flat = top_i.reshape(-1).astype(jnp.int32)
    order = jnp.argsort(e_flat).astype(jnp.int32)
    rank = jnp.argsort(order).astype(jnp.int32)
    counts = jnp.sum((e_flat[:, None] == jnp.arange(N_EXPERTS, dtype=jnp.int32)[None, :]).astype(jnp.int32), axis=0)
    padded = (counts + MOE_BLOCK_ROWS - 1) // MOE_BLOCK_ROWS * MOE_BLOCK_ROWS
    start = jnp.cumsum(counts) - counts
    pend = jnp.cumsum(padded)
    pstart = pend - padded
    n_blocks = n_assign // MOE_BLOCK_ROWS + N_EXPERTS
    n_rows = n_blocks * MOE_BLOCK_ROWS
    row_ids = jnp.arange(n_rows, dtype=jnp.int32)
    row_exp = jnp.minimum(jnp.searchsorted(pend, row_ids, side='right'), N_EXPERTS - 1).astype(jnp.int32)
    row_off = row_ids - pstart[row_exp]
    row_slot = jnp.clip(start[row_exp] + row_off, 0, n_assign - 1)
    row_tok = jnp.where(row_off < counts[row_exp], order[row_slot] // TOP_K, n_tok).astype(jnp.int32)
    block_exp = row_exp[::MOE_BLOCK_ROWS]
    tok_pad = jnp.concatenate([xn.reshape(n_tok, d), jnp.zeros((1, d), BF16)], axis=0)
    xb = tok_pad.at[row_tok].get(mode="promise_in_bounds")
    yb = _moe_experts(xb, block_exp, w1.astype(BF16), w3.astype(BF16), w2.astype(BF16))
    dest_orig = (pstart[e_flat] + rank - start[e_flat]).astype(jnp.int32)
    y2 = yb.at[dest_orig].get(mode="promise_in_bounds").reshape(b, n_lat, TOP_K * d)
    gates_pad = jnp.pad(gates, ((0, 0), (0, 128 - TOP_K))).reshape(b, n_lat, 128)
    return _moe_combine(h_all, modsel, n_lat_tiles, y2, gates_pad, final_g)


def _pad_cols(w, n):
    return jnp.pad(w, ((0, 0), (0, n - w.shape[1])))


def kernel(x, c, ctx, c_ctx, ada_w, ada_b, norm_mix_g, norm_ffn_g, w_in, hy_conv_w, hy_conv_b, hy_f_w1, hy_f_b1, hy_f_w2, hy_f_b2, hy_f_w3, hy_f_b3, hy_bias, rw_mu, rw_w0, rw_w_up, rw_a0, rw_a_up, rw_g_up, rw_k_k, rw_k_a, rw_r_k, rw_ln_g, rw_ln_b, s5_a_re, s5_a_im, s5_log_dt, s5_b_re, s5_b_im, s5_c_re, s5_c_im, s5_d, s5_glu_w, s5_glu_b, gla_gate_up, gla_gate_b, gla_norm_g, w_branch, w_out, ffn_w1, ffn_w3, ffn_w2, moe_router_w, moe_router_b, moe_w1, moe_w3, moe_w2, final_norm_g):
    bsz, n_lat, d = x.shape
    n_ctx = ctx.shape[1]
    depth = w_in.shape[0]
    assert depth == 2 and d == D_MODEL
    assert n_lat % TOKEN_TILE == 0 and n_ctx % TOKEN_TILE == 0 and n_lat % GRID_W == 0
    t_all = n_lat + n_ctx
    n_lat_tiles = n_lat // TOKEN_TILE
    n_tiles = t_all // TOKEN_TILE
    assert n_tiles <= 2 * n_lat_tiles

    rows = (bsz + 1 + 7) // 8 * 8
    c_all = jnp.zeros((rows, d), F32).at[:bsz].set(c.astype(F32)).at[bsz].set(c_ctx.astype(F32))
    mod = _ada_mod(c_all, ada_w, ada_b)
    h_all = jnp.concatenate([x.astype(F32), ctx.astype(F32)], axis=1)

    dft_lat = _dft_tables(n_lat)
    dft_ctx = _dft_tables(n_ctx)
    seg = np.cumsum([0, 3 * MIX_W, RW_IN, MIX_W, 2 * GLA_DK + 2 * GLA_DV + 2 * GLA_GATE_RANK, 4 * d])

    out = None
    for l in range(depth):
        last = l == depth - 1
        mod_l = mod[l, :bsz].reshape(bsz, 6, d)
        mod_c = jnp.broadcast_to(mod[l, bsz].reshape(1, 6, d), (bsz, 6, d))
        modsel = jnp.stack([mod_l, mod_c], axis=1)

        wl = w_in[l].astype(BF16)
        ws = [wl[:, seg[0]:seg[1]], wl[:, seg[1]:seg[2]], wl[:, seg[2]:seg[3]],
              _pad_cols(wl[:, seg[3]:seg[4]], GLA_IN_PAD), wl[:, seg[4]:seg[5]]]
        hy_x0, hy_z, p_rw, p_s5, p_gla, gate = _in_proj(h_all, norm_mix_g[l], modsel, ws, hy_conv_w[l], hy_conv_b[l], n_lat_tiles)

        filt_args = (hy_f_w1[l], hy_f_b1[l], hy_f_w2[l], hy_f_b2[l], hy_f_w3[l], hy_f_b3[l])
        spec_lat = _hy_spectrum(_hy_filter(n_lat, *filt_args), dft_lat[0], dft_lat[1])
        y_hy_lat = _hyena(hy_z, hy_x0, 0, n_lat, hy_bias[l], dft_lat, spec_lat)
        y_hy_ctx = y_hy_lat
        if not last:
            spec_ctx = _hy_spectrum(_hy_filter(n_ctx, *filt_args), dft_ctx[0], dft_ctx[1])
            y_hy_ctx = _hyena(hy_z, hy_x0, n_lat // n_ctx, n_ctx, hy_bias[l], dft_ctx, spec_ctx)

        r, v, kap, lwf, lwb, kf, kb, bf, bb, rg, rbonus = _rw_prep(
            p_rw, n_lat_tiles, rw_mu[l], rw_w0[l], rw_w_up[l], rw_a0[l], rw_a_up[l], rw_g_up[l], rw_k_k[l], rw_k_a[l], rw_r_k[l])
        rof, rob = _rw_scan(r, v, kap, lwf, lwb, kf, kb, bf, bb, n_lat)

        y_s5 = _s5_core(p_s5, n_lat, _s5_matrices(s5_a_re[l], s5_a_im[l], s5_log_dt[l], s5_b_re[l], s5_b_im[l], s5_c_re[l], s5_c_im[l]))

        gof, gob = _gla_scan(p_gla, n_lat, gla_gate_up[l], gla_gate_b[l])
        y_gla_lat, y_gla_ctx = _gla_readout(gof, gob, p_gla, gla_norm_g[l], n_lat)

        nt_l = n_lat_tiles if last else n_tiles
        h_all = _merge(h_all, modsel, nt_l, n_lat_tiles, gate, y_hy_lat, y_hy_ctx, rof, rob, rbonus, rg, y_s5, p_s5,
                       y_gla_lat, y_gla_ctx, rw_ln_g[l], rw_ln_b[l], s5_d[l], s5_glu_w[l], s5_glu_b[l], w_branch[l], w_out[l])

        i = l // 2
        if l % 2 == 0:
            h_all = _ffn_dense(h_all, norm_ffn_g[l], modsel, n_lat_tiles, ffn_w1[i], ffn_w3[i], ffn_w2[i])
        else:
            out = _moe_ffn_and_final_norm(h_all, norm_ffn_g[l], modsel, n_lat_tiles, moe_router_w[i], moe_router_b[i],
                                          moe_w1[i], moe_w3[i], moe_w2[i], final_norm_g)
    return out.astype(x.dtype)
```

```python
import functools
import math

import numpy as np
import jax
import jax.numpy as jnp
from jax import lax
from jax.experimental import pallas as pl
from jax.experimental.pallas import tpu as pltpu

F32 = jnp.float32
BF16 = jnp.bfloat16

D_MODEL = 1024
GRID_W = 64
MIX_W = 512
RMS_EPS = 1e-6
HY_BANDS = 16
HY_DECAY_TARGET = 1e-2
HY_SHORT_PCT = 0.3
HY_LONG_PCT = 1.5
RW_HEADS = 8
RW_HEAD_DIM = 64
RW_DECAY_SCALE = 0.606531
RW_GN_EPS = 64e-5
RW_IN = 1920
S5_GROUP = 16
S5_GROUPS = 32
S5_STATE = 64
S5_CHUNK = 16
GLA_HEADS = 4
GLA_HK = 64
GLA_HV = 128
GLA_DK = 256
GLA_DV = 512
GLA_GATE_RANK = 16
GLA_GATE_NORM = 16.0
GLA_IN_PAD = 1664
N_EXPERTS = 8
TOP_K = 2
MOE_BLOCK_ROWS = 256
TOKEN_TILE = 256
SEQ_CHUNK = 64
RW_SCAN_BATCH = 4
GLA_SCAN_BATCH = 2
VMEM_LIMIT = 56 * 1024 * 1024

NT_DIMS = (((1,), (1,)), ((), ()))
TN_DIMS = (((0,), (0,)), ((), ()))


def _cp(*sem):
    return pltpu.CompilerParams(dimension_semantics=sem, vmem_limit_bytes=VMEM_LIMIT)


def _mm(a, b, dims=None):
    a = a.astype(BF16)
    b = b.astype(BF16)
    if dims is None:
        return jnp.dot(a, b, preferred_element_type=F32)
    return lax.dot_general(a, b, dims, preferred_element_type=F32)


def _split_bf16(x):
    hi = x.astype(BF16)
    lo = (x - hi.astype(F32)).astype(BF16)
    return hi, lo


def _mm_hp(a, b):
    ah, al = _split_bf16(a)
    bh, bl = _split_bf16(b)
    d = functools.partial(jnp.dot, preferred_element_type=F32)
    return d(ah, bh) + (d(ah, bl) + d(al, bh))


def _mm_exact01(m01, x):
    m = m01.astype(BF16)
    hi = x.astype(BF16)
    r1 = x - hi.astype(F32)
    mid = r1.astype(BF16)
    lo = (r1 - mid.astype(F32)).astype(BF16)
    d = functools.partial(jnp.dot, preferred_element_type=F32)
    return d(m, hi) + (d(m, mid) + d(m, lo))


def _x_exact01(x, m01):
    m = m01.astype(BF16)
    hi = x.astype(BF16)
    mid = (x - hi.astype(F32)).astype(BF16)
    d = functools.partial(jnp.dot, preferred_element_type=F32)
    return d(hi, m) + d(mid, m)


def _sigmoid(x):
    return 0.5 * jnp.tanh(0.5 * x) + 0.5


def _silu(x):
    return x * _sigmoid(x)


def _tri(c, rev, strict):
    i = lax.broadcasted_iota(jnp.int32, (c, c), 0)
    j = lax.broadcasted_iota(jnp.int32, (c, c), 1)
    if rev:
        return (j > i) if strict else (j >= i)
    return (j < i) if strict else (j <= i)


def _head_blockdiag(n, head):
    i = lax.broadcasted_iota(jnp.int32, (n, n), 0) // head
    j = lax.broadcasted_iota(jnp.int32, (n, n), 1) // head
    return (i == j).astype(F32)


def _shift_rows(x, prev_row, next_row):
    n = x.shape[0]
    row = lax.broadcasted_iota(jnp.int32, x.shape, 0)
    xm = jnp.where(row == 0, prev_row, pltpu.roll(x, 1, 0))
    xp = jnp.where(row == n - 1, next_row, pltpu.roll(x, n - 1, 0))
    return xm, xp


def _ada_kernel(c_ref, w_ref, b_ref, o_ref):
    o_ref[0] = _mm_hp(_silu(c_ref[...]), w_ref[0]) + b_ref[0]


def _ada_mod(c_all, ada_w, ada_b):
    depth, d, n = ada_w.shape
    rows = c_all.shape[0]
    return pl.pallas_call(
        _ada_kernel,
        grid=(depth, n // d),
        in_specs=[pl.BlockSpec((rows, d), lambda l, j: (0, 0)),
                  pl.BlockSpec((1, d, d), lambda l, j: (l, 0, j)),
                  pl.BlockSpec((1, 1, d), lambda l, j: (l, 0, j))],
        out_specs=pl.BlockSpec((1, rows, d), lambda l, j: (l, 0, j)),
        out_shape=jax.ShapeDtypeStruct((depth, rows, n), F32),
        compiler_params=_cp("arbitrary", "arbitrary"),
        name="ada_mod",
    )(c_all, ada_w, ada_b.reshape(depth, 1, n))


def _norm_mod(x, g, mod, sh_row):
    xn = x * lax.rsqrt(jnp.mean(x * x, axis=-1, keepdims=True) + RMS_EPS) * g
    return xn * (1.0 + mod[sh_row + 1:sh_row + 2]) + mod[sh_row:sh_row + 1]


def _seq_edges(t, n_lat_tiles, n_tiles):
    first = jnp.logical_or(t == 0, t == n_lat_tiles)
    last = jnp.logical_or(t == n_lat_tiles - 1, t == n_tiles - 1)
    return first, last


def _in_proj_kernel(x_ref, xprev_ref, xnext_ref, g_ref, mod_ref, cw_ref, cb_ref, w_hy, w_rw, w_s5, w_gla, w_gate,
                    o_x0, o_z, o_rw, o_s5, o_gla, o_gate, *, n_lat_tiles):
    t = pl.program_id(1)
    first, last = _seq_edges(t, n_lat_tiles, pl.num_programs(1))
    mod = mod_ref[0, 0]
    xm = _norm_mod(x_ref[0], g_ref[...], mod, 0).astype(BF16)
    for w, o in ((w_rw, o_rw), (w_s5, o_s5), (w_gla, o_gla), (w_gate, o_gate)):
        o[0] = jnp.dot(xm, w[...], preferred_element_type=F32).astype(o.dtype)
    p = jnp.dot(xm, w_hy[...], preferred_element_type=F32)
    halo = jnp.concatenate([xprev_ref[0, 7:8, :], xnext_ref[0, 0:1, :], jnp.zeros((6, x_ref.shape[2]), F32)], axis=0)
    ph = jnp.dot(_norm_mod(halo, g_ref[...], mod, 0).astype(BF16), w_hy[...], preferred_element_type=F32)
    pm, pp = _shift_rows(p, jnp.where(first, 0.0, ph[0:1]), jnp.where(last, 0.0, ph[1:2]))
    u = pm * cw_ref[0:1] + p * cw_ref[1:2] + pp * cw_ref[2:3] + cb_ref[...]
    c = MIX_W
    o_x0[0] = u[:, 0:c].astype(o_x0.dtype)
    o_z[0] = (u[:, 2 * c:3 * c] * u[:, c:2 * c]).astype(o_z.dtype)


def _mod_index(n_lat_tiles):
    return lambda b, t: (b, t // n_lat_tiles, 0, 0)


def _halo_specs(width, n_rows):
    sub = TOKEN_TILE // 8
    return [pl.BlockSpec((1, 8, width), lambda b, t: (b, jnp.maximum(t * sub - 1, 0), 0)),
            pl.BlockSpec((1, 8, width), lambda b, t: (b, jnp.minimum((t + 1) * sub, n_rows // 8 - 1), 0))]


def _in_proj(h_all, g, modsel, ws, conv_w, conv_b, n_lat_tiles):
    b, t, d = h_all.shape
    nt = t // TOKEN_TILE
    c = MIX_W
    wspec = [pl.BlockSpec(w.shape, lambda b, t: (0, 0), pipeline_mode=pl.Buffered(1)) for w in ws]
    widths = [c, c] + [w.shape[1] for w in ws[1:]]
    dtypes = [BF16, BF16, F32, F32, F32, BF16]
    return pl.pallas_call(
        functools.partial(_in_proj_kernel, n_lat_tiles=n_lat_tiles),
        grid=(b, nt),
        in_specs=[pl.BlockSpec((1, TOKEN_TILE, d), lambda b, t: (b, t, 0))] + _halo_specs(d, t)
                 + [pl.BlockSpec((1, d), lambda b, t: (0, 0)),
                    pl.BlockSpec((1, 1, 6, d), _mod_index(n_lat_tiles)),
                    pl.BlockSpec((3, 3 * c), lambda b, t: (0, 0)),
                    pl.BlockSpec((1, 3 * c), lambda b, t: (0, 0))] + wspec,
        out_specs=[pl.BlockSpec((1, TOKEN_TILE, w), lambda b, t: (b, t, 0)) for w in widths],
        out_shape=[jax.ShapeDtypeStruct((b, t, w), dt) for w, dt in zip(widths, dtypes)],
        compiler_params=_cp("arbitrary", "arbitrary"),
        name="in_proj",
    )(h_all, h_all, h_all, g.reshape(1, d), modsel, conv_w, conv_b.reshape(1, 3 * c), *ws)


def _grid_cols(n_lat_rows, rows):
    grid_rows = n_lat_rows // GRID_W
    assert grid_rows % 8 == 0 and rows % grid_rows == 0 and GRID_W % (rows // grid_rows) == 0
    return rows // grid_rows


GRID_COL_BLOCK = 8


def _col_major_rows(view_ref, e, j0, cols, lo, hi):
    return jnp.concatenate([view_ref[e, :, pl.ds(j0 + j, 1), lo:hi][:, 0, :] for j in range(cols)], axis=0)


def _hy_filter_kernel(z_ref, w1, b1, w2, b2, w3, b3, dec_ref, o_ref):
    h = jnp.sin(_mm_hp(z_ref[...], w1[...]) + b1[...])
    h = jnp.sin(_mm_hp(h, w2[...]) + b2[...])
    o_ref[...] = (_mm_hp(h, w3[...]) + b3[...]) * dec_ref[...]


def _hy_features(length):
    pos = jnp.arange(length, dtype=F32)
    t = pos / max(length - 1, 1)
    freqs = jnp.linspace(1e-4, HY_BANDS - 1, HY_BANDS, dtype=F32)
    ang = (2.0 * math.pi / length) * pos[:, None] * freqs[None, :]
    z = jnp.concatenate([t[:, None], jnp.cos(ang), -jnp.sin(ang)], axis=-1)
    rates = jnp.abs(jnp.linspace(math.log(HY_DECAY_TARGET) / HY_LONG_PCT,
                                 math.log(HY_DECAY_TARGET) / HY_SHORT_PCT, MIX_W, dtype=F32))
    rates = jnp.concatenate([rates, rates])
    decay = jnp.exp(-t[:, None] * rates[None, :])
    return jnp.pad(z, ((0, 0), (0, 128 - z.shape[1]))), decay


def _hy_filter(length, fw1, fb1, fw2, fb2, fw3, fb3):
    z, decay = _hy_features(length)
    w1 = jnp.pad(fw1, ((0, 128 - fw1.shape[0]), (0, 128 - fw1.shape[1])))
    b1 = jnp.pad(fb1, (0, 128 - fb1.shape[0])).reshape(1, 128)
    w2 = jnp.pad(fw2, ((0, 128 - fw2.shape[0]), (0, 128 - fw2.shape[1])))
    b2 = jnp.pad(fb2, (0, 128 - fb2.shape[0])).reshape(1, 128)
    w3 = jnp.pad(fw3, ((0, 128 - fw3.shape[0]), (0, 0)))
    n = fw3.shape[1]
    rows = min(length, 512)
    full = lambda s: pl.BlockSpec(s, lambda i: (0, 0))
    return pl.pallas_call(
        _hy_filter_kernel,
        grid=(length // rows,),
        in_specs=[pl.BlockSpec((rows, 128), lambda i: (i, 0)), full((128, 128)), full((1, 128)), full((128, 128)),
                  full((1, 128)), full((128, n)), full((1, n)), pl.BlockSpec((rows, n), lambda i: (i, 0))],
        out_specs=pl.BlockSpec((rows, n), lambda i: (i, 0)),
        out_shape=jax.ShapeDtypeStruct((length, n), F32),
        compiler_params=_cp("arbitrary"),
        name="hy_filter",
    )(z, w1, b1, w2, b2, w3, fb3.reshape(1, n), decay)


def _dft_tables(length):
    n = 2 * length
    k = jnp.arange(length, dtype=jnp.int32)
    m = (k[:, None] * k[None, :]) % n
    ang = m.astype(F32) * (2.0 * math.pi / n)
    wc = jnp.cos(ang).astype(BF16)
    sin = jnp.sin(ang)
    nyq = jnp.where(k % 2 == 0, 1.0, -1.0).astype(F32)
    return wc, sin.at[0].set(nyq).astype(BF16), wc, sin.at[:, 0].set(nyq).astype(BF16)


def _dft_fwd_kernel(wc_ref, ws_ref, xh_ref, xl_ref, oc_ref, os_ref):
    d = functools.partial(jnp.dot, preferred_element_type=F32)
    oc_ref[...] = d(wc_ref[...], xh_ref[...]) + d(wc_ref[...], xl_ref[...])
    os_ref[...] = d(ws_ref[...], xh_ref[...]) + d(ws_ref[...], xl_ref[...])


def _hy_spectrum(filt, wc, ws):
    length, c2 = filt.shape
    c = c2 // 2
    n = 2 * length
    hb = filt[:, c:].at[0].set(0.0)
    x = jnp.concatenate([filt[:, :c], hb], axis=1)
    xh, xl = _split_bf16(x)
    fr = min(length, 512)
    fc, fs = pl.pallas_call(
        _dft_fwd_kernel,
        grid=(length // fr,),
        in_specs=[pl.BlockSpec((fr, length), lambda i: (i, 0)), pl.BlockSpec((fr, length), lambda i: (i, 0)),
                  pl.BlockSpec((length, c2), lambda i: (0, 0)), pl.BlockSpec((length, c2), lambda i: (0, 0))],
        out_specs=[pl.BlockSpec((fr, c2), lambda i: (i, 0)), pl.BlockSpec((fr, c2), lambda i: (i, 0))],
        out_shape=[jax.ShapeDtypeStruct((length, c2), F32)] * 2,
        compiler_params=_cp("arbitrary"),
        name="hy_spectrum",
    )(wc, ws, xh, xl)
    k_re = fc[:, :c] + fc[:, c:]
    k_im = fs[:, c:] - fs[:, :c]
    k_nyq = fs[0, :c] + fs[0, c:]
    ka = (k_re * (2.0 / n)).at[0].set(k_re[0] / n)
    kb = (k_im * (2.0 / n)).at[0].set(0.0)
    ka2 = ka.at[0].set(k_nyq / n)
    return ka, ka2, kb


HYENA_BATCH = 2


def _hyena_kernel(z_ref, x0_ref, bias_ref, wc_ref, ws_ref, wct_ref, wst_ref, ka_ref, ka2_ref, kb_ref, o_ref, acc_scr):
    f = pl.program_id(1)
    nb = z_ref.shape[0]
    c = z_ref.shape[2]
    z = jnp.concatenate([z_ref[e] for e in range(nb)], axis=1)
    tile = lambda ref: jnp.concatenate([ref[...]] * nb, axis=1)

    @pl.when(f == 0)
    def _():
        acc_scr[...] = z.astype(F32) * tile(bias_ref)

    zc = jnp.dot(wc_ref[...], z, preferred_element_type=F32)
    zs = jnp.dot(ws_ref[...], z, preferred_element_type=F32)
    ka, ka2, kb = tile(ka_ref), tile(ka2_ref), tile(kb_ref)
    a = zc * ka + zs * kb
    bv = zs * ka2 - zc * kb
    acc_scr[...] += (jnp.dot(wct_ref[...], a.astype(BF16), preferred_element_type=F32)
                     + jnp.dot(wst_ref[...], bv.astype(BF16), preferred_element_type=F32))

    @pl.when(f == pl.num_programs(1) - 1)
    def _():
        for e in range(nb):
            o_ref[e] = (x0_ref[e].astype(F32) * acc_scr[:, e * c:(e + 1) * c]).astype(o_ref.dtype)


def _hyena(z, x0, row_block, length, bias, tables, spec):
    b, t, c = z.shape
    wc, ws, wct, wst = tables
    ka, ka2, kb = spec
    fr = min(length, 256)
    nf = length // fr
    nb = HYENA_BATCH
    seq = pl.BlockSpec((nb, length, c), lambda b, f: (b, row_block, 0))
    in_specs = [seq, seq,
                pl.BlockSpec((1, c), lambda b, f: (0, 0)),
                pl.BlockSpec((fr, length), lambda b, f: (f, 0)),
                pl.BlockSpec((fr, length), lambda b, f: (f, 0)),
                pl.BlockSpec((length, fr), lambda b, f: (0, f)),
                pl.BlockSpec((length, fr), lambda b, f: (0, f)),
                pl.BlockSpec((fr, c), lambda b, f: (f, 0)),
                pl.BlockSpec((fr, c), lambda b, f: (f, 0)),
                pl.BlockSpec((fr, c), lambda b, f: (f, 0))]
    return pl.pallas_call(
        _hyena_kernel,
        grid=(b // nb, nf),
        in_specs=in_specs,
        out_specs=pl.BlockSpec((nb, length, c), lambda b, f: (b, 0, 0)),
        out_shape=jax.ShapeDtypeStruct((b, length, c), BF16),
        scratch_shapes=[pltpu.VMEM((length, nb * c), F32)],
        compiler_params=_cp("arbitrary", "arbitrary"),
        name="hyena_%d" % length,
    )(z, x0, bias.reshape(1, c), wc, ws, wct, wst, ka, ka2, kb)


def _rw_prep_kernel(p_ref, pprev_ref, pnext_ref, mu_ref, w0_ref, wup_ref, a0_ref, aup_ref, gup_ref, kk_ref, ka_ref, rk_ref,
                    r_o, v_o, kap_o, lwf_o, lwb_o, kf_o, kb_o, bf_o, bb_o, g_o, bonus_o, *, n_lat_tiles):
    t = pl.program_id(1)
    nt = pl.num_programs(1)
    p = p_ref[0]
    first = jnp.logical_or(t == 0, t == n_lat_tiles)
    last = jnp.logical_or(t == n_lat_tiles - 1, t == nt - 1)
    prev_row = jnp.where(first, 0.0, pprev_ref[0, 7:8, :])
    next_row = jnp.where(last, 0.0, pnext_ref[0, 0:1, :])
    pm, pp = _shift_rows(p, prev_row, next_row)
    p = p + mu_ref[0:1] * (pm - p) + mu_ref[1:2] * (pp - p)
    c = MIX_W
    r, k, v = p[:, 0:c], p[:, c:2 * c], p[:, 2 * c:3 * c]
    wd, ad, gd = p[:, 3 * c:3 * c + 128], p[:, 3 * c + 128:3 * c + 256], p[:, 3 * c + 256:3 * c + 384]
    bd = _head_blockdiag(c, RW_HEAD_DIM)
    g = _mm(_sigmoid(gd), gup_ref[...])
    kk = k * kk_ref[...]
    kap = kk * lax.rsqrt(jnp.maximum(_x_exact01(kk * kk, bd), 1e-24))
    logw = -RW_DECAY_SCALE * _sigmoid(w0_ref[...] + _mm(jnp.tanh(wd), wup_ref[...]))
    a = _sigmoid(a0_ref[...] + _mm(ad, aup_ref[...]))
    k_sum = 0.0
    outs_k, outs_b = (kf_o, kb_o), (bf_o, bb_o)
    for d in range(2):
        a_d = a[:, d * c:(d + 1) * c]
        k_d = k * (1.0 + (a_d - 1.0) * ka_ref[...])
        outs_k[d][0] = k_d.astype(BF16)
        outs_b[d][0] = (a_d * kap).astype(BF16)
        k_sum = k_sum + k_d
    r_o[0] = r.astype(BF16)
    v_o[0] = v.astype(BF16)
    kap_o[0] = kap.astype(BF16)
    lwf_o[0] = logw[:, 0:c]
    lwb_o[0] = logw[:, c:2 * c]
    g_o[0] = g
    bonus_o[0] = _x_exact01(r * k_sum * rk_ref[...], bd) * v


def _blockdiag2(a, b):
    z1 = jnp.zeros((a.shape[0], b.shape[1]), a.dtype)
    z2 = jnp.zeros((b.shape[0], a.shape[1]), a.dtype)
    return jnp.concatenate([jnp.concatenate([a, z1], 1), jnp.concatenate([z2, b], 1)], 0)


def _rw_prep(p_rw, n_lat_tiles, mu, w0, w_up, a0, a_up, g_up, k_k, k_a, r_k):
    b, t, n = p_rw.shape
    nt = t // TOKEN_TILE
    c = MIX_W
    sub = TOKEN_TILE // 8
    n8 = t // 8
    consts = [mu, w0.reshape(1, 2 * c), _blockdiag2(w_up[0], w_up[1]), a0.reshape(1, 2 * c), _blockdiag2(a_up[0], a_up[1]),
              g_up, k_k.reshape(1, c), k_a.reshape(1, c), r_k.reshape(1, c)]
    cspecs = [pl.BlockSpec(x.shape, lambda b, t: (0, 0)) for x in consts]
    tile = lambda w: pl.BlockSpec((1, TOKEN_TILE, w), lambda b, t: (b, t, 0))
    return pl.pallas_call(
        functools.partial(_rw_prep_kernel, n_lat_tiles=n_lat_tiles),
        grid=(b, nt),
        in_specs=[tile(n),
                  pl.BlockSpec((1, 8, n), lambda b, t: (b, jnp.maximum(t * sub - 1, 0), 0)),
                  pl.BlockSpec((1, 8, n), lambda b, t: (b, jnp.minimum((t + 1) * sub, n8 - 1), 0))] + cspecs,
        out_specs=[tile(c)] * 11,
        out_shape=[jax.ShapeDtypeStruct((b, t, c), dt) for dt in (BF16, BF16, BF16, F32, F32, BF16, BF16, BF16, BF16, F32, F32)],
        compiler_params=_cp("arbitrary", "arbitrary"),
        name="rw_prep",
    )(p_rw, p_rw, p_rw, *consts)


def _rw_scan_kernel(rf, vf, kapf, lwf, kf, bbf, rb, vb, kapb, lwb, kb, bbb, of_ref, ob_ref, s_scr):
    @pl.when(pl.program_id(1) == 0)
    def _():
        s_scr[...] = jnp.zeros_like(s_scr)

    c = rf.shape[1]
    pw_ = 2 * RW_HEAD_DIM
    n_pairs = MIX_W // pw_
    lane_a = lax.broadcasted_iota(jnp.int32, (c, pw_), 1) < RW_HEAD_DIM
    lane_a2 = lax.broadcasted_iota(jnp.int32, (2 * c, pw_), 1) < RW_HEAD_DIM
    blk = ((lax.broadcasted_iota(jnp.int32, (pw_, pw_), 0) < RW_HEAD_DIM)
           == (lax.broadcasted_iota(jnp.int32, (pw_, pw_), 1) < RW_HEAD_DIM))

    def sel(x):
        return jnp.where(lane_a, x[:c], x[c:])

    def bf(x):
        return x.astype(BF16)

    nb = rf.shape[0]
    chains = []
    for e in range(nb):
        for d, (r, v, kap, lw, k, b) in enumerate(((rf, vf, kapf, lwf, kf, bbf), (rb, vb, kapb, lwb, kb, bbb))):
            rev = d == 1
            incl = _tri(c, rev, False)
            strict = _tri(c, rev, True)
            logw = lw[e]
            cum = _mm_exact01(incl.astype(F32), logw)
            tot = cum[0:1] if rev else cum[c - 1:c]
            ginv = jnp.exp(-cum)
            gt = jnp.exp(tot - cum)
            rt = r[e] * jnp.exp(cum)
            kt = kap[e] * jnp.exp(cum - logw)
            kd, bd = k[e] * ginv, b[e] * ginv
            kg, bg = k[e] * gt, b[e] * gt
            dec = jnp.exp(tot)
            vv = v[e]
            for p in range(n_pairs):
                sl = slice(p * pw_, (p + 1) * pw_)
                lhs2 = jnp.concatenate([rt[:, sl], kt[:, sl]], 0)
                chains.append(dict(
                    e=e, d=d, p=p, idx=(e * 2 + d) * n_pairs + p, incl=incl, strict=strict,
                    lhs2=bf(lhs2),
                    lhs4=bf(jnp.concatenate([jnp.where(lane_a2, lhs2, 0.0), jnp.where(lane_a2, 0.0, lhs2)], 0)),
                    kd=bf(kd[:, sl]), bd=bf(bd[:, sl]), v=bf(vv[:, sl]),
                    kgbg=bf(jnp.concatenate([kg[:, sl], bg[:, sl]], 0)), dec=dec[:, sl]))

    dot = functools.partial(lax.dot_general, preferred_element_type=F32)
    nn = (((1,), (0,)), ((), ()))
    for ch in chains:
        ch["s"] = s_scr[ch["idx"]]
    for ch in chains:
        ch["a1k"] = dot(ch["lhs4"], ch["kd"], NT_DIMS)
        ch["a1b"] = dot(ch["lhs4"], ch["bd"], NT_DIMS)
        ch["a2"] = dot(ch["lhs2"], bf(ch["s"]), NT_DIMS)
    for ch in chains:
        a1k, a1b, st = ch["a1k"], ch["a1b"], ch["strict"]
        lk = jnp.where(jnp.concatenate([st, st], 0), jnp.concatenate([a1k[c:2 * c], a1k[3 * c:]], 0), 0.0)
        ch["lb_a"] = bf(jnp.where(st, a1b[c:2 * c], 0.0))
        ch["lb_b"] = bf(jnp.where(st, a1b[3 * c:], 0.0))
        ch["u"] = ch["a2"][c:] + sel(dot(bf(lk), ch["v"], nn))
    for ch in chains:
        ch["u"] = ch["u"] - sel(dot(jnp.concatenate([ch["lb_a"], ch["lb_b"]], 0), bf(ch["u"]), nn))
        ch["p_a"] = bf(dot(ch["lb_a"], ch["lb_a"], nn))
        ch["p_b"] = bf(dot(ch["lb_b"], ch["lb_b"], nn))
    n = int(math.log2(c)) - 1
    for i in range(n):
        for ch in chains:
            ch["u"] = ch["u"] + sel(dot(jnp.concatenate([ch["p_a"], ch["p_b"]], 0), bf(ch["u"]), nn))
            if i < n - 1:
                ch["p_a"] = bf(dot(ch["p_a"], ch["p_a"], nn))
                ch["p_b"] = bf(dot(ch["p_b"], ch["p_b"], nn))
    outs = [[[None] * n_pairs, [None] * n_pairs] for _ in range(nb)]
    for ch in chains:
        a1k, a1b, inc = ch["a1k"], ch["a1b"], ch["incl"]
        inc2 = jnp.concatenate([inc, inc], 0)
        rk = jnp.where(inc2, jnp.concatenate([a1k[:c], a1k[2 * c:3 * c]], 0), 0.0)
        rb_ = jnp.where(inc2, jnp.concatenate([a1b[:c], a1b[2 * c:3 * c]], 0), 0.0)
        ub = bf(ch["u"])
        o = ch["a2"][:c] + sel(dot(bf(rk), ch["v"], nn)) - sel(dot(bf(rb_), ub, nn))
        outs[ch["e"]][ch["d"]][ch["p"]] = o
        upd = dot(jnp.concatenate([ch["v"], -ub], 0), ch["kgbg"], TN_DIMS)
        s_scr[ch["idx"]] = ch["s"] * ch["dec"] + jnp.where(blk, upd, 0.0)
    for e in range(nb):
        of_ref[e] = jnp.concatenate(outs[e][0], axis=1)
        ob_ref[e] = jnp.concatenate(outs[e][1], axis=1)


def _chunk_orders(n_lat, n_ctx):
    n = n_lat + n_ctx
    fwd = lambda b, s: (b, (s + n_lat) % n, 0)
    rev = lambda b, s: (b, n - 1 - s, 0)
    return fwd, rev


def _rw_scan(r, v, kap, lwf, lwb, kf, kb, bf, bb, n_lat_rows):
    b, t, c = r.shape
    ch = SEQ_CHUNK
    n = t // ch
    fwd, rev = _chunk_orders(n_lat_rows // ch, (t - n_lat_rows) // ch)
    nb = RW_SCAN_BATCH
    blk = lambda im: pl.BlockSpec((nb, ch, c), im)
    return pl.pallas_call(
        _rw_scan_kernel,
        grid=(b // nb, n),
        in_specs=[blk(fwd)] * 6 + [blk(rev)] * 6,
        out_specs=[blk(fwd), blk(rev)],
        out_shape=[jax.ShapeDtypeStruct((b, t, c), F32)] * 2,
        scratch_shapes=[pltpu.VMEM((nb * RW_HEADS, 2 * RW_HEAD_DIM, 2 * RW_HEAD_DIM), F32)],
        compiler_params=_cp("arbitrary", "arbitrary"),
        name="rw_scan",
    )(r, v, kap, lwf, kf, bf, r, v, kap, lwb, kb, bb)


def _rw_readout(of, ob, bonus, g, ln_g, ln_b):
    o = of + ob
    bd = _head_blockdiag(MIX_W, RW_HEAD_DIM)
    mean = _x_exact01(o, bd) * (1.0 / RW_HEAD_DIM)
    oc = o - mean
    var = _x_exact01(oc * oc, bd) * (1.0 / RW_HEAD_DIM)
    on = oc * lax.rsqrt(var + RW_GN_EPS) * ln_g + ln_b
    return (on + bonus) * g


def _s5_matrices(a_re, a_im, log_dt, b_re, b_im, c_re, c_im):
    tc, g, n, p = S5_CHUNK, S5_GROUPS, S5_STATE, S5_GROUP
    hp = lax.Precision.HIGHEST
    dt = jnp.exp(log_dt)[:, :, None]
    lam_re, lam_im = a_re * dt, a_im * dt
    mag1 = jnp.exp(lam_re)
    ab_re, ab_im = mag1 * jnp.cos(lam_im), mag1 * jnp.sin(lam_im)
    nr, ni = ab_re - 1.0, ab_im
    den = a_re * a_re + a_im * a_im
    f_re, f_im = (nr * a_re + ni * a_im) / den, (ni * a_re - nr * a_im) / den
    tau = jnp.arange(tc + 1, dtype=F32)[:, None, None, None]
    pw_mag = jnp.exp(lam_re[None] * tau)
    pw_re, pw_im = pw_mag * jnp.cos(lam_im[None] * tau), pw_mag * jnp.sin(lam_im[None] * tau)
    fb_re = f_re[..., None] * b_re[None] - f_im[..., None] * b_im[None]
    fb_im = f_re[..., None] * b_im[None] + f_im[..., None] * b_re[None]
    e_re = pw_re[..., None] * fb_re[None] - pw_im[..., None] * fb_im[None]
    e_im = pw_re[..., None] * fb_im[None] + pw_im[..., None] * fb_re[None]
    m = (jnp.einsum('gpn,tdgnq->tdgpq', c_re, e_re[:tc], precision=hp)
         - jnp.einsum('gpn,tdgnq->tdgpq', c_im, e_im[:tc], precision=hp))
    i = jnp.arange(tc)
    lag = i[None, :] - i[:, None]
    mf = jnp.where((lag >= 0)[:, :, None, None, None], m[jnp.clip(lag, 0, tc - 1), 0], 0.0)
    mb = jnp.where((lag <= 0)[:, :, None, None, None], m[jnp.clip(-lag, 0, tc - 1), 1], 0.0)
    w = jnp.transpose(mf + mb, (2, 0, 4, 1, 3)).reshape(g, tc * p, tc * p)
    idx_f = tc - 1 - i
    pf_re, pf_im = e_re[idx_f, 0], e_im[idx_f, 0]
    pb_re, pb_im = e_re[i, 1], e_im[i, 1]
    pm = jnp.concatenate([pf_re, pb_re, pf_im, pb_im], axis=2)
    pm = jnp.transpose(pm, (1, 0, 3, 2)).reshape(g, tc * p, 4 * n)
    cf_re = c_re[None, :, :, :] * pw_re[i + 1, 0][:, :, None, :] - c_im[None] * pw_im[i + 1, 0][:, :, None, :]
    cf_im = c_re[None] * pw_im[i + 1, 0][:, :, None, :] + c_im[None] * pw_re[i + 1, 0][:, :, None, :]
    cb_re = c_re[None] * pw_re[tc - i, 1][:, :, None, :] - c_im[None] * pw_im[tc - i, 1][:, :, None, :]
    cb_im = c_re[None] * pw_im[tc - i, 1][:, :, None, :] + c_im[None] * pw_re[tc - i, 1][:, :, None, :]
    q = jnp.concatenate([cf_re, cb_re, -cf_im, -cb_im], axis=3)
    q = jnp.transpose(q, (1, 3, 0, 2)).reshape(g, 4 * n, tc * p)
    dec_re = jnp.concatenate([pw_re[tc, 0], pw_re[tc, 1]], axis=1)
    dec_im = jnp.concatenate([pw_im[tc, 0], pw_im[tc, 1]], axis=1)
    dec = jnp.stack([dec_re, dec_im], axis=1)
    return w.astype(BF16), pm.astype(BF16), q.astype(BF16), dec


S5_PACK_BATCH = 2
S5_LANE_GROUPS = 128 // S5_GROUP


def _s5_pack_kernel(x_ref, o_ref):
    rows = o_ref.shape[0]
    grp = lax.broadcasted_iota(jnp.int32, (rows, 128), 1) // S5_GROUP
    tiles = [jnp.zeros((rows, 128), F32) for _ in range(2 * S5_LANE_GROUPS)]
    for j in range(S5_CHUNK):
        xj = x_ref[pl.ds(j, rows, stride=S5_CHUNK), :]
        jl, jh = j % S5_LANE_GROUPS, j // S5_LANE_GROUPS
        for g8 in range(S5_LANE_GROUPS):
            sh = ((jl - g8) * S5_GROUP) % 128
            moved = pltpu.roll(xj, sh, 1) if sh else xj
            tiles[2 * g8 + jh] = jnp.where(grp == jl, moved, tiles[2 * g8 + jh])
    o_ref[...] = jnp.concatenate(tiles, axis=1).astype(o_ref.dtype)


def _s5_unpack_kernel(y_ref, o_ref):
    rows = y_ref.shape[0]
    grp = lax.broadcasted_iota(jnp.int32, (rows, 128), 1) // S5_GROUP
    for i in range(S5_CHUNK):
        il, ih = i % S5_LANE_GROUPS, i // S5_LANE_GROUPS
        out = jnp.zeros((rows, 128), F32)
        for g8 in range(S5_LANE_GROUPS):
            src = y_ref[:, (2 * g8 + ih) * 128:(2 * g8 + ih + 1) * 128]
            sh = ((g8 - il) * S5_GROUP) % 128
            moved = pltpu.roll(src, sh, 1) if sh else src
            out = jnp.where(grp == g8, moved, out)
        o_ref[pl.ds(i, rows, stride=S5_CHUNK), :] = out


def _s5_kernel(u_ref, w_ref, p_ref, q_ref, dec_ref, y_ref, inc_re, inc_im, sf_re, sf_im, sb_re, sb_im, *, n_lat, n_ctx, bsz):
    u = u_ref[...]
    inc = jnp.dot(u, p_ref[0], preferred_element_type=F32)
    n2 = 2 * S5_STATE
    inc_re[...] = inc[:, :n2]
    inc_im[...] = inc[:, n2:]
    n = n_lat + n_ctx
    d_re = dec_ref[0, 0:1, :]
    d_im = dec_ref[0, 1:2, :]
    is_f = lax.broadcasted_iota(jnp.int32, (bsz, n2), 1) < S5_STATE

    def body(s, carry):
        s_re, s_im = carry
        rf = pl.ds((s + n_lat) % n, bsz, stride=n)
        rb = pl.ds(n - 1 - s, bsz, stride=n)
        sf_re[rf, :] = s_re
        sf_im[rf, :] = s_im
        sb_re[rb, :] = s_re
        sb_im[rb, :] = s_im
        i_re = jnp.where(is_f, inc_re[rf, :], inc_re[rb, :])
        i_im = jnp.where(is_f, inc_im[rf, :], inc_im[rb, :])
        return (d_re * s_re - d_im * s_im + i_re, d_re * s_im + d_im * s_re + i_im)

    zero = jnp.zeros((bsz, n2), F32)
    lax.fori_loop(0, n, body, (zero, zero))
    is_f_all = lax.broadcasted_iota(jnp.int32, sf_re.shape, 1) < S5_STATE
    s_all = jnp.concatenate([jnp.where(is_f_all, sf_re[...], sb_re[...]),
                             jnp.where(is_f_all, sf_im[...], sb_im[...])], axis=1).astype(BF16)
    y_ref[...] = jnp.dot(u, w_ref[0], preferred_element_type=F32) + jnp.dot(s_all, q_ref[0], preferred_element_type=F32)


def _s5_core(p_s5, n_lat_rows, mats):
    b, t, c = p_s5.shape
    tc, g = S5_CHUNK, S5_GROUPS
    nch = t // tc
    w, pm, q, dec = mats
    k = tc * S5_GROUP
    nbb = S5_PACK_BATCH
    n_tiles = c // 128
    tok_blk = pl.BlockSpec((nbb * t, 128), lambda i, o: (i, o))
    chk_blk = pl.BlockSpec((nbb * nch, S5_LANE_GROUPS * k), lambda i, o: (i, o))
    u = pl.pallas_call(
        _s5_pack_kernel,
        grid=(b // nbb, n_tiles),
        in_specs=[tok_blk],
        out_specs=chk_blk,
        out_shape=jax.ShapeDtypeStruct((b * nch, g * k), BF16),
        compiler_params=_cp("arbitrary", "arbitrary"),
        name="s5_pack",
    )(p_s5.reshape(b * t, c))
    rows = nch * b
    y = pl.pallas_call(
        functools.partial(_s5_kernel, n_lat=n_lat_rows // tc, n_ctx=(t - n_lat_rows) // tc, bsz=b),
        grid=(g,),
        in_specs=[pl.BlockSpec((rows, k), lambda i: (0, i)),
                  pl.BlockSpec((1, k, k), lambda i: (i, 0, 0)),
                  pl.BlockSpec((1, k, 4 * S5_STATE), lambda i: (i, 0, 0)),
                  pl.BlockSpec((1, 4 * S5_STATE, k), lambda i: (i, 0, 0)),
                  pl.BlockSpec((1, 2, 2 * S5_STATE), lambda i: (i, 0, 0))],
        out_specs=pl.BlockSpec((rows, k), lambda i: (0, i)),
        out_shape=jax.ShapeDtypeStruct((rows, g * k), F32),
        scratch_shapes=[pltpu.VMEM((rows, 2 * S5_STATE), F32)] * 6,
        compiler_params=_cp("arbitrary"),
        name="s5_core",
    )(u, w, pm, q, dec)
    out = pl.pallas_call(
        _s5_unpack_kernel,
        grid=(b // nbb, n_tiles),
        in_specs=[chk_blk],
        out_specs=tok_blk,
        out_shape=jax.ShapeDtypeStruct((b * t, c), F32),
        compiler_params=_cp("arbitrary", "arbitrary"),
        name="s5_unpack",
    )(y)
    return out.reshape(b, t, c)


def _s5_output(y, u, d_skip, glu_w, glu_b):
    z = y + d_skip * u
    z = 0.5 * z * (1.0 + jnp.tanh(math.sqrt(2.0 / math.pi) * (z + 0.044715 * (z * z * z))))
    return z * _sigmoid(_mm(z, glu_w) + glu_b)


def _gla_scan_kernel(pfv_ref, pfc_ref, pbv_ref, pbc_ref, gup_ref, gb_ref, of_ref, ob_ref, s_scr, *, n_lat, n_ctx, cols, c):
    step = pl.program_id(1)

    @pl.when(step == 0)
    def _():
        s_scr[...] = jnp.zeros_like(s_scr)

    subs = pfc_ref.shape[1] // c
    n = n_lat + n_ctx
    block_ids = ((step + n_lat) % n, n - 1 - step)
    pw_ = 2 * GLA_HK
    n_pairs = GLA_DK // pw_
    lane_a = lax.broadcasted_iota(jnp.int32, (c, pw_), 1) < GLA_HK
    lane_av = lax.broadcasted_iota(jnp.int32, (GLA_HV, pw_), 1) < GLA_HK
    dot = functools.partial(lax.dot_general, preferred_element_type=F32)
    nn = (((1,), (0,)), ((), ()))
    nb = pfc_ref.shape[0]
    mask2 = lambda a: jnp.concatenate([jnp.where(lane_a, a, 0.0), jnp.where(lane_a, 0.0, a)], 0).astype(BF16)
    for sub in range(subs):
        chains = []
        for e in range(nb):
            for d, (v_ref, c_ref) in enumerate(((pfv_ref, pfc_ref), (pbv_ref, pbc_ref))):
                rev = d == 1
                cs = subs - 1 - sub if rev else sub
                incl = _tri(c, rev, False)
                lat_rows = jnp.concatenate([v_ref[e, :, cs * cols + j, :] for j in range(cols)], axis=0)
                pe = jnp.where(block_ids[d] < n_lat, lat_rows, c_ref[e, cs * c:(cs + 1) * c, :])
                gd = pe[:, 2 * GLA_DK + 2 * GLA_DV:]
                x = _mm(gd, gup_ref[...])[:, d * GLA_DK:(d + 1) * GLA_DK] + gb_ref[0:1, d * GLA_DK:(d + 1) * GLA_DK]
                logg = (jnp.minimum(x, 0.0) - jnp.log(1.0 + jnp.exp(-jnp.abs(x)))) * (1.0 / GLA_GATE_NORM)
                cum = _mm_exact01(incl.astype(F32), logg)
                mid = c // 2
                ref = cum[mid:mid + 1] if rev else cum[mid - 1:mid]
                tot = cum[0:1] if rev else cum[c - 1:c]
                q = pe[:, 0:GLA_DK] * (GLA_HK ** -0.5)
                k = pe[:, GLA_DK:2 * GLA_DK]
                qs = q * jnp.exp(cum - ref)
                ks = k * jnp.exp(ref - cum)
                qc = q * jnp.exp(cum)
                kg = k * jnp.exp(tot - cum)
                dec = jnp.exp(tot)
                for p in range(n_pairs):
                    sl = slice(p * pw_, (p + 1) * pw_)
                    chains.append(dict(
                        e=e, d=d, p=p, idx=(e * 2 + d) * n_pairs + p, incl=incl, qs4=mask2(qs[:, sl]), qc4=mask2(qc[:, sl]),
                        ks=ks[:, sl].astype(BF16), kg=kg[:, sl].astype(BF16), dec=dec[:, sl],
                        va=pe[:, 2 * GLA_DK + 2 * p * GLA_HV:2 * GLA_DK + (2 * p + 1) * GLA_HV].astype(BF16),
                        vb=pe[:, 2 * GLA_DK + (2 * p + 1) * GLA_HV:2 * GLA_DK + (2 * p + 2) * GLA_HV].astype(BF16)))
        for ch in chains:
            ch["s"] = s_scr[ch["idx"]]
        for ch in chains:
            ch["sc"] = dot(ch["qs4"], ch["ks"], NT_DIMS)
            ch["st"] = dot(ch["qc4"], ch["s"].astype(BF16), NT_DIMS)
        outs = [[[None] * GLA_HEADS, [None] * GLA_HEADS] for _ in range(nb)]
        for ch in chains:
            e, d, p = ch["e"], ch["d"], ch["p"]
            sa = jnp.where(ch["incl"], ch["sc"][:c], 0.0).astype(BF16)
            sb = jnp.where(ch["incl"], ch["sc"][c:], 0.0).astype(BF16)
            outs[e][d][2 * p] = ch["st"][:c] + dot(sa, ch["va"], nn)
            outs[e][d][2 * p + 1] = ch["st"][c:] + dot(sb, ch["vb"], nn)
            upd = jnp.where(lane_av, dot(ch["va"], ch["kg"], TN_DIMS), dot(ch["vb"], ch["kg"], TN_DIMS))
            s_scr[ch["idx"]] = ch["s"] * ch["dec"] + upd
        for e in range(nb):
            of_ref[e, sub * c:(sub + 1) * c, :] = jnp.concatenate(outs[e][0], axis=1)
            ob_ref[e, (subs - 1 - sub) * c:(subs - sub) * c, :] = jnp.concatenate(outs[e][1], axis=1)


def _gla_scan(p_gla, n_lat_rows, gate_up, gate_b):
    b, t, n = p_gla.shape
    ch = SEQ_CHUNK
    grid_rows = n_lat_rows // GRID_W
    cols = _grid_cols(n_lat_rows, ch)
    blk_rows = GRID_COL_BLOCK * grid_rows
    assert GRID_COL_BLOCK % cols == 0 and blk_rows % ch == 0 and (t - n_lat_rows) % blk_rows == 0
    n_lat, n_ctx = n_lat_rows // blk_rows, (t - n_lat_rows) // blk_rows
    n_blk = n_lat + n_ctx
    fwd, rev = _chunk_orders(n_lat, n_ctx)
    gup = jnp.zeros((128, 2 * GLA_DK), F32)
    gup = gup.at[0:GLA_GATE_RANK, 0:GLA_DK].set(gate_up[0]).at[GLA_GATE_RANK:2 * GLA_GATE_RANK, GLA_DK:].set(gate_up[1])
    nb = GLA_SCAN_BATCH
    p_view = p_gla.reshape(b, t // GRID_W, GRID_W, n)
    view_blk = lambda bid: pl.BlockSpec((nb, grid_rows, GRID_COL_BLOCK, n), lambda b, s: (b, 0, jnp.minimum(bid(s), n_lat - 1), 0))
    ctx_blk = lambda bid: pl.BlockSpec((nb, blk_rows, n), lambda b, s: (b, jnp.maximum(bid(s), n_lat), 0))
    bid_f = lambda s: (s + n_lat) % n_blk
    bid_b = lambda s: n_blk - 1 - s
    return pl.pallas_call(
        functools.partial(_gla_scan_kernel, n_lat=n_lat, n_ctx=n_ctx, cols=cols, c=ch),
        grid=(b // nb, n_blk),
        in_specs=[view_blk(bid_f), ctx_blk(bid_f), view_blk(bid_b), ctx_blk(bid_b),
                  pl.BlockSpec((128, 2 * GLA_DK), lambda b, s: (0, 0)),
                  pl.BlockSpec((1, 2 * GLA_DK), lambda b, s: (0, 0))],
        out_specs=[pl.BlockSpec((nb, blk_rows, GLA_DV), fwd), pl.BlockSpec((nb, blk_rows, GLA_DV), rev)],
        out_shape=[jax.ShapeDtypeStruct((b, t, GLA_DV), F32)] * 2,
        scratch_shapes=[pltpu.VMEM((nb * GLA_HEADS, GLA_HV, 2 * GLA_HK), F32)],
        compiler_params=_cp("arbitrary", "arbitrary"),
        name="gla_scan",
    )(p_view, p_gla, p_view, p_gla, gup, gate_b.reshape(1, 2 * GLA_DK))


def _gla_readout_kernel(of_ref, ob_ref, gv_ref, gc_ref, ng_ref, ylat_ref, yctx_ref, *, n_lat_tiles, cols):
    t = pl.program_id(1)
    o = of_ref[0] + ob_ref[0]
    bd = _head_blockdiag(GLA_DV, GLA_HV)
    ms = _x_exact01(o * o, bd) * (1.0 / GLA_HV)
    g_lo = 2 * GLA_DK + GLA_DV
    g = jnp.where(t < n_lat_tiles, _col_major_rows(gv_ref, 0, 0, cols, g_lo, g_lo + GLA_DV), gc_ref[0])
    y = (o * lax.rsqrt(ms + RMS_EPS) * ng_ref[...] * _silu(g)).astype(ylat_ref.dtype)
    rows = y.shape[0] // cols

    @pl.when(t < n_lat_tiles)
    def _():
        for j in range(cols):
            ylat_ref[0, :, j, :] = y[j * rows:(j + 1) * rows]

    @pl.when(t >= n_lat_tiles)
    def _():
        yctx_ref[0] = y


def _gla_readout(of, ob, p_gla, norm_g, n_lat_rows):
    b, t, c = of.shape
    n = p_gla.shape[2]
    n_lat_tiles = n_lat_rows // TOKEN_TILE
    grid_rows = n_lat_rows // GRID_W
    cols = _grid_cols(n_lat_rows, TOKEN_TILE)
    assert cols == GRID_COL_BLOCK
    tile = pl.BlockSpec((1, TOKEN_TILE, c), lambda b, t: (b, t, 0))
    p_view = p_gla.reshape(b, t // GRID_W, GRID_W, n)
    y_lat, y_ctx = pl.pallas_call(
        functools.partial(_gla_readout_kernel, n_lat_tiles=n_lat_tiles, cols=cols),
        grid=(b, t // TOKEN_TILE),
        in_specs=[tile, tile,
                  pl.BlockSpec((1, grid_rows, cols, n), lambda b, t: (b, 0, jnp.minimum(t, n_lat_tiles - 1), 0)),
                  pl.BlockSpec((1, TOKEN_TILE, c), lambda b, t: (b, jnp.maximum(t, n_lat_tiles), (2 * GLA_DK + GLA_DV) // c)),
                  pl.BlockSpec((1, c), lambda b, t: (0, 0))],
        out_specs=[pl.BlockSpec((1, grid_rows, cols, c), lambda b, t: (b, 0, jnp.minimum(t, n_lat_tiles - 1), 0)),
                   pl.BlockSpec((1, TOKEN_TILE, c), lambda b, t: (b, 0, 0))],
        out_shape=[jax.ShapeDtypeStruct((b, grid_rows, GRID_W, c), BF16), jax.ShapeDtypeStruct((b, t - n_lat_rows, c), BF16)],
        compiler_params=_cp("arbitrary", "arbitrary"),
        name="gla_readout",
    )(of, ob, p_view, p_gla, jnp.tile(norm_g, GLA_HEADS).reshape(1, c))
    return y_lat.reshape(b, n_lat_rows, c), y_ctx


def _merge_kernel(h_ref, mod_ref, gate_ref, yhyl_ref, yhyc_ref, rof_ref, rob_ref, rbonus_ref, rg_ref, ys5_ref, us5_ref,
                  yglal_ref, yglac_ref, lng_ref, lnb_ref, dskip_ref, gluw_ref, glub_ref, wbr_ref, wout_ref, o_ref,
                  *, n_lat_tiles):
    d = D_MODEL
    is_lat = pl.program_id(1) < n_lat_tiles
    y_hy = jnp.where(is_lat, yhyl_ref[0], yhyc_ref[0])
    y_gla = jnp.where(is_lat, yglal_ref[0], yglac_ref[0])
    y_rw = _rw_readout(rof_ref[0], rob_ref[0], rbonus_ref[0], rg_ref[0], lng_ref[...], lnb_ref[...])
    y_s5 = _s5_output(ys5_ref[0], us5_ref[0], dskip_ref[...], gluw_ref[...], glub_ref[...])
    acc = jnp.zeros((h_ref.shape[1], d), F32)
    for m, y in enumerate((y_hy, y_rw, y_s5, y_gla)):
        acc = acc + (_sigmoid(gate_ref[0, :, m * d:(m + 1) * d].astype(F32))
                     * jnp.dot(y.astype(BF16), wbr_ref[m], preferred_element_type=F32))
    out = jnp.dot(acc.astype(BF16), wout_ref[...], preferred_element_type=F32)
    o_ref[0] = h_ref[0] + mod_ref[0, 0, 2:3, :] * out


def _merge(h_all, modsel, n_tiles, n_lat_tiles, gate, y_hy_lat, y_hy_ctx, rof, rob, rbonus, rg, y_s5, u_s5, y_gla_lat, y_gla_ctx,
           ln_g, ln_b, d_skip, glu_w, glu_b, w_branch, w_out):
    b, t, d = h_all.shape
    c = MIX_W
    tile = lambda w: pl.BlockSpec((1, TOKEN_TILE, w), lambda b, t: (b, t, 0))
    lat = pl.BlockSpec((1, TOKEN_TILE, c), lambda b, t: (b, jnp.minimum(t, n_lat_tiles - 1), 0))
    ctx = pl.BlockSpec((1, TOKEN_TILE, c), lambda b, t: (b, jnp.maximum(t - n_lat_tiles, 0), 0))
    vec = lambda x: x.reshape(1, -1)
    consts = [vec(ln_g), vec(ln_b), vec(d_skip), glu_w.astype(BF16), vec(glu_b), w_branch.astype(BF16), w_out.astype(BF16)]
    cspecs = [pl.BlockSpec(x.shape, (lambda n: lambda b, t: (0,) * n)(x.ndim)) for x in consts]
    return pl.pallas_call(
        functools.partial(_merge_kernel, n_lat_tiles=n_lat_tiles),
        grid=(b, n_tiles),
        in_specs=[tile(d), pl.BlockSpec((1, 1, 6, d), _mod_index(n_lat_tiles)), tile(4 * d), lat, ctx] + [tile(c)] * 6
                 + [lat, ctx] + cspecs,
        out_specs=tile(d),
        out_shape=jax.ShapeDtypeStruct((b, t, d), F32),
        input_output_aliases={0: 0},
        compiler_params=_cp("arbitrary", "arbitrary"),
        name="merge",
    )(h_all, modsel, gate, y_hy_lat, y_hy_ctx, rof, rob, rbonus, rg, y_s5, u_s5, y_gla_lat, y_gla_ctx, *consts)


def _ffn_kernel(h_ref, g_ref, mod_ref, w1_ref, w3_ref, w2_ref, o_ref):
    h = h_ref[0]
    xm = _norm_mod(h, g_ref[...], mod_ref[0, 0], 3).astype(BF16)
    a = jnp.dot(xm, w1_ref[...], preferred_element_type=F32)
    b = jnp.dot(xm, w3_ref[...], preferred_element_type=F32)
    y = jnp.dot((_silu(a) * b).astype(BF16), w2_ref[...], preferred_element_type=F32)
    o_ref[0] = h + mod_ref[0, 0, 5:6, :] * y


def _ffn_dense(h_all, g, modsel, n_lat_tiles, w1, w3, w2):
    b, t, d = h_all.shape
    tile = pl.BlockSpec((1, TOKEN_TILE, d), lambda b, t: (b, t, 0))
    ws = [w1.astype(BF16), w3.astype(BF16), w2.astype(BF16)]
    wspec = [pl.BlockSpec(w.shape, lambda b, t: (0, 0), pipeline_mode=pl.Buffered(1)) for w in ws]
    return pl.pallas_call(
        _ffn_kernel,
        grid=(b, t // TOKEN_TILE),
        in_specs=[tile, pl.BlockSpec((1, d), lambda b, t: (0, 0)), pl.BlockSpec((1, 1, 6, d), _mod_index(n_lat_tiles))] + wspec,
        out_specs=tile,
        out_shape=jax.ShapeDtypeStruct((b, t, d), F32),
        input_output_aliases={0: 0},
        compiler_params=_cp("arbitrary", "arbitrary"),
        name="ffn_dense",
    )(h_all, g.reshape(1, d), modsel, *ws)


def _router_kernel(h_ref, g_ref, mod_ref, rw_ref, rb_ref, x_ref, logit_ref):
    xm = _norm_mod(h_ref[0], g_ref[...], mod_ref[0, 0], 3)
    x_ref[0] = xm.astype(BF16)
    logit_ref[0] = _mm_hp(xm, rw_ref[...]) + rb_ref[...]


def _router(h_all, g, modsel, n_lat_tiles, router_w, router_b):
    b, t, d = h_all.shape
    n_lat = n_lat_tiles * TOKEN_TILE
    rw = jnp.pad(router_w, ((0, 0), (0, 128 - N_EXPERTS)))
    rb = jnp.pad(router_b, (0, 128 - N_EXPERTS), constant_values=-1e30).reshape(1, 128)
    tile = lambda w: pl.BlockSpec((1, TOKEN_TILE, w), lambda b, t: (b, t, 0))
    return pl.pallas_call(
        _router_kernel,
        grid=(b, n_lat_tiles),
        in_specs=[tile(d), pl.BlockSpec((1, d), lambda b, t: (0, 0)), pl.BlockSpec((1, 1, 6, d), _mod_index(n_lat_tiles)),
                  pl.BlockSpec((d, 128), lambda b, t: (0, 0)), pl.BlockSpec((1, 128), lambda b, t: (0, 0))],
        out_specs=[tile(d), tile(128)],
        out_shape=[jax.ShapeDtypeStruct((b, n_lat, d), BF16), jax.ShapeDtypeStruct((b, n_lat, 128), F32)],
        compiler_params=_cp("arbitrary", "arbitrary"),
        name="moe_router",
    )(h_all, g.reshape(1, d), modsel, rw, rb)


def _moe_expert_kernel(be_ref, x_ref, w1_ref, w3_ref, w2_ref, o_ref):
    x = x_ref[...]
    a = jnp.dot(x, w1_ref[0], preferred_element_type=F32)
    b = jnp.dot(x, w3_ref[0], preferred_element_type=F32)
    o_ref[...] = jnp.dot((_silu(a) * b).astype(BF16), w2_ref[0], preferred_element_type=F32).astype(o_ref.dtype)


def _moe_experts(xb, block_exp, w1, w3, w2):
    rows, d = xb.shape
    nb = rows // MOE_BLOCK_ROWS
    f = w1.shape[2]
    grid_spec = pltpu.PrefetchScalarGridSpec(
        num_scalar_prefetch=1,
        grid=(nb,),
        in_specs=[pl.BlockSpec((MOE_BLOCK_ROWS, d), lambda i, be: (i, 0)),
                  pl.BlockSpec((1, d, f), lambda i, be: (be[i], 0, 0), pipeline_mode=pl.Buffered(1)),
                  pl.BlockSpec((1, d, f), lambda i, be: (be[i], 0, 0), pipeline_mode=pl.Buffered(1)),
                  pl.BlockSpec((1, f, d), lambda i, be: (be[i], 0, 0), pipeline_mode=pl.Buffered(1))],
        out_specs=pl.BlockSpec((MOE_BLOCK_ROWS, d), lambda i, be: (i, 0)),
    )
    return pl.pallas_call(
        _moe_expert_kernel,
        grid_spec=grid_spec,
        out_shape=jax.ShapeDtypeStruct((rows, d), BF16),
        compiler_params=_cp("arbitrary"),
        name="moe_experts",
    )(block_exp, xb, w1, w3, w2)


def _moe_combine_kernel(h_ref, mod_ref, y0_ref, y1_ref, gate_ref, fg_ref, o_ref):
    y = y0_ref[0, 0].astype(F32) * gate_ref[0, :, 0:1] + y1_ref[0, 0].astype(F32) * gate_ref[0, :, 1:2]
    h = h_ref[0] + mod_ref[0, 0, 5:6, :] * y
    o_ref[0] = h * lax.rsqrt(jnp.mean(h * h, axis=-1, keepdims=True) + RMS_EPS) * fg_ref[...]


def _moe_combine(h_all, modsel, n_lat_tiles, y2, gates, final_g):
    b, t, d = h_all.shape
    n_lat = n_lat_tiles * TOKEN_TILE
    tile = lambda w: pl.BlockSpec((1, TOKEN_TILE, w), lambda b, t: (b, t, 0))
    pick = lambda k: pl.BlockSpec((1, 1, TOKEN_TILE, d), lambda b, t: (k, b, t, 0))
    return pl.pallas_call(
        _moe_combine_kernel,
        grid=(b, n_lat_tiles),
        in_specs=[tile(d), pl.BlockSpec((1, 1, 6, d), _mod_index(n_lat_tiles)), pick(0), pick(1),
                  tile(128), pl.BlockSpec((1, d), lambda b, t: (0, 0))],
        out_specs=tile(d),
        out_shape=jax.ShapeDtypeStruct((b, n_lat, d), F32),
        compiler_params=_cp("arbitrary", "arbitrary"),
        name="moe_combine",
    )(h_all, modsel, y2, y2, gates, final_g.reshape(1, d))


def _moe_ffn_and_final_norm(h_all, g, modsel, n_lat_tiles, router_w, router_b, w1, w3, w2, final_g):
    b, t, d = h_all.shape
    n_lat = n_lat_tiles * TOKEN_TILE
    n_tok = b * n_lat
    xn, logits = _router(h_all, g, modsel, n_lat_tiles, router_w, router_b)
    logits = logits.reshape(n_tok, 128)[:, :N_EXPERTS]
    top_v, top_i = lax.top_k(logits, TOP_K)
    gates = jax.nn.softmax(top_v, axis=-1)
    n_assign = n_tok * TOP_K
    e_flat = top_i.reshape(-1).astype(jnp.int32)
    order = jnp.argsort(e_flat).astype(jnp.int32)
    rank = jnp.argsort(order).astype(jnp.int32)
    counts = jnp.sum((e_flat[:, None] == jnp.arange(N_EXPERTS, dtype=jnp.int32)[None, :]).astype(jnp.int32), axis=0)
    padded = (counts + MOE_BLOCK_ROWS - 1) // MOE_BLOCK_ROWS * MOE_BLOCK_ROWS
    start = jnp.cumsum(counts) - counts
    pend = jnp.cumsum(padded)
    pstart = pend - padded
    n_blocks = n_assign // MOE_BLOCK_ROWS + N_EXPERTS
    n_rows = n_blocks * MOE_BLOCK_ROWS
    row_ids = jnp.arange(n_rows, dtype=jnp.int32)
    row_exp = jnp.minimum(jnp.searchsorted(pend, row_ids, side='right'), N_EXPERTS - 1).astype(jnp.int32)
    row_off = row_ids - pstart[row_exp]
    row_slot = jnp.clip(start[row_exp] + row_off, 0, n_assign - 1)
    row_tok = jnp.where(row_off < counts[row_exp], order[row_slot] // TOP_K, n_tok).astype(jnp.int32)
    block_exp = row_exp[::MOE_BLOCK_ROWS]
    tok_pad = jnp.concatenate([xn.reshape(n_tok, d), jnp.zeros((1, d), BF16)], axis=0)
    xb = tok_pad.at[row_tok].get(mode="promise_in_bounds")
    yb = _moe_experts(xb, block_exp, w1.astype(BF16), w3.astype(BF16), w2.astype(BF16))
    dest_orig = (pstart[e_flat] + rank - start[e_flat]).astype(jnp.int32)
    dest_by_k = dest_orig.reshape(n_tok, TOP_K).T.reshape(-1)
    y2 = yb.at[dest_by_k].get(mode="promise_in_bounds").reshape(TOP_K, b, n_lat, d)
    gates_pad = jnp.pad(gates, ((0, 0), (0, 128 - TOP_K))).reshape(b, n_lat, 128)
    return _moe_combine(h_all, modsel, n_lat_tiles, y2, gates_pad, final_g)


def _pad_cols(w, n):
    return jnp.pad(w, ((0, 0), (0, n - w.shape[1])))


def kernel(x, c, ctx, c_ctx, ada_w, ada_b, norm_mix_g, norm_ffn_g, w_in, hy_conv_w, hy_conv_b, hy_f_w1, hy_f_b1, hy_f_w2, hy_f_b2, hy_f_w3, hy_f_b3, hy_bias, rw_mu, rw_w0, rw_w_up, rw_a0, rw_a_up, rw_g_up, rw_k_k, rw_k_a, rw_r_k, rw_ln_g, rw_ln_b, s5_a_re, s5_a_im, s5_log_dt, s5_b_re, s5_b_im, s5_c_re, s5_c_im, s5_d, s5_glu_w, s5_glu_b, gla_gate_up, gla_gate_b, gla_norm_g, w_branch, w_out, ffn_w1, ffn_w3, ffn_w2, moe_router_w, moe_router_b, moe_w1, moe_w3, moe_w2, final_norm_g):
    bsz, n_lat, d = x.shape
    n_ctx = ctx.shape[1]
    depth = w_in.shape[0]
    assert depth == 2 and d == D_MODEL
    assert n_lat % TOKEN_TILE == 0 and n_ctx % TOKEN_TILE == 0 and n_lat % GRID_W == 0
    t_all = n_lat + n_ctx
    n_lat_tiles = n_lat // TOKEN_TILE
    n_tiles = t_all // TOKEN_TILE
    assert n_tiles <= 2 * n_lat_tiles

    rows = (bsz + 1 + 7) // 8 * 8
    c_all = jnp.zeros((rows, d), F32).at[:bsz].set(c.astype(F32)).at[bsz].set(c_ctx.astype(F32))
    mod = _ada_mod(c_all, ada_w, ada_b)
    h_all = jnp.concatenate([x.astype(F32), ctx.astype(F32)], axis=1)

    dft_lat = _dft_tables(n_lat)
    dft_ctx = _dft_tables(n_ctx)
    seg = np.cumsum([0, 3 * MIX_W, RW_IN, MIX_W, 2 * GLA_DK + 2 * GLA_DV + 2 * GLA_GATE_RANK, 4 * d])

    out = None
    for l in range(depth):
        last = l == depth - 1
        mod_l = mod[l, :bsz].reshape(bsz, 6, d)
        mod_c = jnp.broadcast_to(mod[l, bsz].reshape(1, 6, d), (bsz, 6, d))
        modsel = jnp.stack([mod_l, mod_c], axis=1)

        wl = w_in[l].astype(BF16)
        ws = [wl[:, seg[0]:seg[1]], wl[:, seg[1]:seg[2]], wl[:, seg[2]:seg[3]],
              _pad_cols(wl[:, seg[3]:seg[4]], GLA_IN_PAD), wl[:, seg[4]:seg[5]]]
        hy_x0, hy_z, p_rw, p_s5, p_gla, gate = _in_proj(h_all, norm_mix_g[l], modsel, ws, hy_conv_w[l], hy_conv_b[l], n_lat_tiles)

        filt_args = (hy_f_w1[l], hy_f_b1[l], hy_f_w2[l], hy_f_b2[l], hy_f_w3[l], hy_f_b3[l])
        spec_lat = _hy_spectrum(_hy_filter(n_lat, *filt_args), dft_lat[0], dft_lat[1])
        y_hy_lat = _hyena(hy_z, hy_x0, 0, n_lat, hy_bias[l], dft_lat, spec_lat)
        y_hy_ctx = y_hy_lat
        if not last:
            spec_ctx = _hy_spectrum(_hy_filter(n_ctx, *filt_args), dft_ctx[0], dft_ctx[1])
            y_hy_ctx = _hyena(hy_z, hy_x0, n_lat // n_ctx, n_ctx, hy_bias[l], dft_ctx, spec_ctx)

        r, v, kap, lwf, lwb, kf, kb, bf, bb, rg, rbonus = _rw_prep(
            p_rw, n_lat_tiles, rw_mu[l], rw_w0[l], rw_w_up[l], rw_a0[l], rw_a_up[l], rw_g_up[l], rw_k_k[l], rw_k_a[l], rw_r_k[l])
        rof, rob = _rw_scan(r, v, kap, lwf, lwb, kf, kb, bf, bb, n_lat)

        y_s5 = _s5_core(p_s5, n_lat, _s5_matrices(s5_a_re[l], s5_a_im[l], s5_log_dt[l], s5_b_re[l], s5_b_im[l], s5_c_re[l], s5_c_im[l]))

        gof, gob = _gla_scan(p_gla, n_lat, gla_gate_up[l], gla_gate_b[l])
        y_gla_lat, y_gla_ctx = _gla_readout(gof, gob, p_gla, gla_norm_g[l], n_lat)

        nt_l = n_lat_tiles if last else n_tiles
        h_all = _merge(h_all, modsel, nt_l, n_lat_tiles, gate, y_hy_lat, y_hy_ctx, rof, rob, rbonus, rg, y_s5, p_s5,
                       y_gla_lat, y_gla_ctx, rw_ln_g[l], rw_ln_b[l], s5_d[l], s5_glu_w[l], s5_glu_b[l], w_branch[l], w_out[l])

        i = l // 2
        if l % 2 == 0:
            h_all = _ffn_dense(h_all, norm_ffn_g[l], modsel, n_lat_tiles, ffn_w1[i], ffn_w3[i], ffn_w2[i])
        else:
            out = _moe_ffn_and_final_norm(h_all, norm_ffn_g[l], modsel, n_lat_tiles, moe_router_w[i], moe_router_b[i],
                                          moe_w1[i], moe_w3[i], moe_w2[i], final_norm_g)
    return out.astype(x.dtype)
```

```python
import functools
import math

import numpy as np
import jax
import jax.numpy as jnp
from jax import lax
from jax.experimental import pallas as pl
from jax.experimental.pallas import tpu as pltpu

F32 = jnp.float32
BF16 = jnp.bfloat16

D_MODEL = 1024
GRID_W = 64
MIX_W = 512
RMS_EPS = 1e-6
HY_BANDS = 16
HY_DECAY_TARGET = 1e-2
HY_SHORT_PCT = 0.3
HY_LONG_PCT = 1.5
RW_HEADS = 8
RW_HEAD_DIM = 64
RW_DECAY_SCALE = 0.606531
RW_GN_EPS = 64e-5
RW_IN = 1920
S5_GROUP = 16
S5_GROUPS = 32
S5_STATE = 64
S5_CHUNK = 16
GLA_HEADS = 4
GLA_HK = 64
GLA_HV = 128
GLA_DK = 256
GLA_DV = 512
GLA_GATE_RANK = 16
GLA_GATE_NORM = 16.0
GLA_IN_PAD = 1664
N_EXPERTS = 8
TOP_K = 2
MOE_BLOCK_ROWS = 256
TOKEN_TILE = 256
SEQ_CHUNK = 64
RW_SCAN_BATCH = 4
GLA_SCAN_BATCH = 2
VMEM_LIMIT = 56 * 1024 * 1024

NT_DIMS = (((1,), (1,)), ((), ()))
TN_DIMS = (((0,), (0,)), ((), ()))


def _cp(*sem):
    return pltpu.CompilerParams(dimension_semantics=sem, vmem_limit_bytes=VMEM_LIMIT)


def _mm(a, b, dims=None):
    a = a.astype(BF16)
    b = b.astype(BF16)
    if dims is None:
        return jnp.dot(a, b, preferred_element_type=F32)
    return lax.dot_general(a, b, dims, preferred_element_type=F32)


def _split_bf16(x):
    hi = x.astype(BF16)
    lo = (x - hi.astype(F32)).astype(BF16)
    return hi, lo


def _mm_hp(a, b):
    ah, al = _split_bf16(a)
    bh, bl = _split_bf16(b)
    d = functools.partial(jnp.dot, preferred_element_type=F32)
    return d(ah, bh) + (d(ah, bl) + d(al, bh))


def _mm_exact01(m01, x):
    m = m01.astype(BF16)
    hi = x.astype(BF16)
    r1 = x - hi.astype(F32)
    mid = r1.astype(BF16)
    lo = (r1 - mid.astype(F32)).astype(BF16)
    d = functools.partial(jnp.dot, preferred_element_type=F32)
    return d(m, hi) + (d(m, mid) + d(m, lo))


def _x_exact01(x, m01):
    m = m01.astype(BF16)
    hi = x.astype(BF16)
    mid = (x - hi.astype(F32)).astype(BF16)
    d = functools.partial(jnp.dot, preferred_element_type=F32)
    return d(hi, m) + d(mid, m)


def _sigmoid(x):
    return 0.5 * jnp.tanh(0.5 * x) + 0.5


def _silu(x):
    return x * _sigmoid(x)


def _tri(c, rev, strict):
    i = lax.broadcasted_iota(jnp.int32, (c, c), 0)
    j = lax.broadcasted_iota(jnp.int32, (c, c), 1)
    if rev:
        return (j > i) if strict else (j >= i)
    return (j < i) if strict else (j <= i)


def _head_blockdiag(n, head):
    i = lax.broadcasted_iota(jnp.int32, (n, n), 0) // head
    j = lax.broadcasted_iota(jnp.int32, (n, n), 1) // head
    return (i == j).astype(F32)


def _shift_rows(x, prev_row, next_row):
    n = x.shape[0]
    row = lax.broadcasted_iota(jnp.int32, x.shape, 0)
    xm = jnp.where(row == 0, prev_row, pltpu.roll(x, 1, 0))
    xp = jnp.where(row == n - 1, next_row, pltpu.roll(x, n - 1, 0))
    return xm, xp


def _ada_kernel(c_ref, w_ref, b_ref, o_ref):
    o_ref[0] = _mm_hp(_silu(c_ref[...]), w_ref[0]) + b_ref[0]


def _ada_mod(c_all, ada_w, ada_b):
    depth, d, n = ada_w.shape
    rows = c_all.shape[0]
    return pl.pallas_call(
        _ada_kernel,
        grid=(depth, n // d),
        in_specs=[pl.BlockSpec((rows, d), lambda l, j: (0, 0)),
                  pl.BlockSpec((1, d, d), lambda l, j: (l, 0, j)),
                  pl.BlockSpec((1, 1, d), lambda l, j: (l, 0, j))],
        out_specs=pl.BlockSpec((1, rows, d), lambda l, j: (l, 0, j)),
        out_shape=jax.ShapeDtypeStruct((depth, rows, n), F32),
        compiler_params=_cp("arbitrary", "arbitrary"),
        name="ada_mod",
    )(c_all, ada_w, ada_b.reshape(depth, 1, n))


def _norm_mod(x, g, mod, sh_row):
    xn = x * lax.rsqrt(jnp.mean(x * x, axis=-1, keepdims=True) + RMS_EPS) * g
    return xn * (1.0 + mod[sh_row + 1:sh_row + 2]) + mod[sh_row:sh_row + 1]


def _seq_edges(t, n_lat_tiles, n_tiles):
    first = jnp.logical_or(t == 0, t == n_lat_tiles)
    last = jnp.logical_or(t == n_lat_tiles - 1, t == n_tiles - 1)
    return first, last


def _in_proj_kernel(x_ref, xprev_ref, xnext_ref, g_ref, mod_ref, cw_ref, cb_ref, w_hy, w_rw, w_s5, w_gla, w_gate,
                    o_x0, o_z, o_rw, o_s5, o_gla, o_gate, *, n_lat_tiles):
    t = pl.program_id(1)
    first, last = _seq_edges(t, n_lat_tiles, pl.num_programs(1))
    mod = mod_ref[0, 0]
    xm = _norm_mod(x_ref[0], g_ref[...], mod, 0).astype(BF16)
    for w, o in ((w_rw, o_rw), (w_s5, o_s5), (w_gla, o_gla), (w_gate, o_gate)):
        o[0] = jnp.dot(xm, w[...], preferred_element_type=F32).astype(o.dtype)
    p = jnp.dot(xm, w_hy[...], preferred_element_type=F32)
    halo = jnp.concatenate([xprev_ref[0, 7:8, :], xnext_ref[0, 0:1, :], jnp.zeros((6, x_ref.shape[2]), F32)], axis=0)
    ph = jnp.dot(_norm_mod(halo, g_ref[...], mod, 0).astype(BF16), w_hy[...], preferred_element_type=F32)
    pm, pp = _shift_rows(p, jnp.where(first, 0.0, ph[0:1]), jnp.where(last, 0.0, ph[1:2]))
    u = pm * cw_ref[0:1] + p * cw_ref[1:2] + pp * cw_ref[2:3] + cb_ref[...]
    c = MIX_W
    o_x0[0] = u[:, 0:c].astype(o_x0.dtype)
    o_z[0] = (u[:, 2 * c:3 * c] * u[:, c:2 * c]).astype(o_z.dtype)


def _mod_index(n_lat_tiles):
    return lambda b, t: (b, t // n_lat_tiles, 0, 0)


def _halo_specs(width, n_rows):
    sub = TOKEN_TILE // 8
    return [pl.BlockSpec((1, 8, width), lambda b, t: (b, jnp.maximum(t * sub - 1, 0), 0)),
            pl.BlockSpec((1, 8, width), lambda b, t: (b, jnp.minimum((t + 1) * sub, n_rows // 8 - 1), 0))]


def _in_proj(h_all, g, modsel, ws, conv_w, conv_b, n_lat_tiles):
    b, t, d = h_all.shape
    nt = t // TOKEN_TILE
    c = MIX_W
    wspec = [pl.BlockSpec(w.shape, lambda b, t: (0, 0), pipeline_mode=pl.Buffered(1)) for w in ws]
    widths = [c, c] + [w.shape[1] for w in ws[1:]]
    dtypes = [BF16, BF16, F32, F32, F32, BF16]
    return pl.pallas_call(
        functools.partial(_in_proj_kernel, n_lat_tiles=n_lat_tiles),
        grid=(b, nt),
        in_specs=[pl.BlockSpec((1, TOKEN_TILE, d), lambda b, t: (b, t, 0))] + _halo_specs(d, t)
                 + [pl.BlockSpec((1, d), lambda b, t: (0, 0)),
                    pl.BlockSpec((1, 1, 6, d), _mod_index(n_lat_tiles)),
                    pl.BlockSpec((3, 3 * c), lambda b, t: (0, 0)),
                    pl.BlockSpec((1, 3 * c), lambda b, t: (0, 0))] + wspec,
        out_specs=[pl.BlockSpec((1, TOKEN_TILE, w), lambda b, t: (b, t, 0)) for w in widths],
        out_shape=[jax.ShapeDtypeStruct((b, t, w), dt) for w, dt in zip(widths, dtypes)],
        compiler_params=_cp("arbitrary", "arbitrary"),
        name="in_proj",
    )(h_all, h_all, h_all, g.reshape(1, d), modsel, conv_w, conv_b.reshape(1, 3 * c), *ws)


def _grid_cols(n_lat_rows, rows):
    grid_rows = n_lat_rows // GRID_W
    assert grid_rows % 8 == 0 and rows % grid_rows == 0 and GRID_W % (rows // grid_rows) == 0
    return rows // grid_rows


GRID_COL_BLOCK = 8


def _col_major_rows(view_ref, e, j0, cols, lo, hi):
    return jnp.concatenate([view_ref[e, :, pl.ds(j0 + j, 1), lo:hi][:, 0, :] for j in range(cols)], axis=0)


def _hy_filter_kernel(z_ref, w1, b1, w2, b2, w3, b3, dec_ref, o_ref):
    h = jnp.sin(_mm_hp(z_ref[...], w1[...]) + b1[...])
    h = jnp.sin(_mm_hp(h, w2[...]) + b2[...])
    o_ref[...] = (_mm_hp(h, w3[...]) + b3[...]) * dec_ref[...]


def _hy_features(length):
    pos = jnp.arange(length, dtype=F32)
    t = pos / max(length - 1, 1)
    freqs = jnp.linspace(1e-4, HY_BANDS - 1, HY_BANDS, dtype=F32)
    ang = (2.0 * math.pi / length) * pos[:, None] * freqs[None, :]
    z = jnp.concatenate([t[:, None], jnp.cos(ang), -jnp.sin(ang)], axis=-1)
    rates = jnp.abs(jnp.linspace(math.log(HY_DECAY_TARGET) / HY_LONG_PCT,
                                 math.log(HY_DECAY_TARGET) / HY_SHORT_PCT, MIX_W, dtype=F32))
    rates = jnp.concatenate([rates, rates])
    decay = jnp.exp(-t[:, None] * rates[None, :])
    return jnp.pad(z, ((0, 0), (0, 128 - z.shape[1]))), decay


def _hy_filter(length, fw1, fb1, fw2, fb2, fw3, fb3):
    z, decay = _hy_features(length)
    w1 = jnp.pad(fw1, ((0, 128 - fw1.shape[0]), (0, 128 - fw1.shape[1])))
    b1 = jnp.pad(fb1, (0, 128 - fb1.shape[0])).reshape(1, 128)
    w2 = jnp.pad(fw2, ((0, 128 - fw2.shape[0]), (0, 128 - fw2.shape[1])))
    b2 = jnp.pad(fb2, (0, 128 - fb2.shape[0])).reshape(1, 128)
    w3 = jnp.pad(fw3, ((0, 128 - fw3.shape[0]), (0, 0)))
    n = fw3.shape[1]
    rows = min(length, 512)
    full = lambda s: pl.BlockSpec(s, lambda i: (0, 0))
    return pl.pallas_call(
        _hy_filter_kernel,
        grid=(length // rows,),
        in_specs=[pl.BlockSpec((rows, 128), lambda i: (i, 0)), full((128, 128)), full((1, 128)), full((128, 128)),
                  full((1, 128)), full((128, n)), full((1, n)), pl.BlockSpec((rows, n), lambda i: (i, 0))],
        out_specs=pl.BlockSpec((rows, n), lambda i: (i, 0)),
        out_shape=jax.ShapeDtypeStruct((length, n), F32),
        compiler_params=_cp("arbitrary"),
        name="hy_filter",
    )(z, w1, b1, w2, b2, w3, fb3.reshape(1, n), decay)


def _dft_tables(length):
    n = 2 * length
    k = jnp.arange(length, dtype=jnp.int32)
    m = (k[:, None] * k[None, :]) % n
    ang = m.astype(F32) * (2.0 * math.pi / n)
    wc = jnp.cos(ang).astype(BF16)
    sin = jnp.sin(ang)
    nyq = jnp.where(k % 2 == 0, 1.0, -1.0).astype(F32)
    ws = jnp.where(k[:, None] == 0, nyq[None, :], sin).astype(BF16)
    wst = jnp.where(k[None, :] == 0, nyq[:, None], sin).astype(BF16)
    return wc, ws, wc, wst


def _dft_fwd_kernel(wc_ref, ws_ref, xh_ref, xl_ref, oc_ref, os_ref):
    d = functools.partial(jnp.dot, preferred_element_type=F32)
    oc_ref[...] = d(wc_ref[...], xh_ref[...]) + d(wc_ref[...], xl_ref[...])
    os_ref[...] = d(ws_ref[...], xh_ref[...]) + d(ws_ref[...], xl_ref[...])


def _hy_spectrum(filt, wc, ws):
    length, c2 = filt.shape
    c = c2 // 2
    n = 2 * length
    hb = filt[:, c:].at[0].set(0.0)
    x = jnp.concatenate([filt[:, :c], hb], axis=1)
    xh, xl = _split_bf16(x)
    fr = min(length, 512)
    fc, fs = pl.pallas_call(
        _dft_fwd_kernel,
        grid=(length // fr,),
        in_specs=[pl.BlockSpec((fr, length), lambda i: (i, 0)), pl.BlockSpec((fr, length), lambda i: (i, 0)),
                  pl.BlockSpec((length, c2), lambda i: (0, 0)), pl.BlockSpec((length, c2), lambda i: (0, 0))],
        out_specs=[pl.BlockSpec((fr, c2), lambda i: (i, 0)), pl.BlockSpec((fr, c2), lambda i: (i, 0))],
        out_shape=[jax.ShapeDtypeStruct((length, c2), F32)] * 2,
        compiler_params=_cp("arbitrary"),
        name="hy_spectrum",
    )(wc, ws, xh, xl)
    k_re = fc[:, :c] + fc[:, c:]
    k_im = fs[:, c:] - fs[:, :c]
    k_nyq = fs[0, :c] + fs[0, c:]
    ka = (k_re * (2.0 / n)).at[0].set(k_re[0] / n)
    kb = (k_im * (2.0 / n)).at[0].set(0.0)
    ka2 = ka.at[0].set(k_nyq / n)
    return ka, ka2, kb


HYENA_BATCH = 2


def _hyena_kernel(z_ref, x0_ref, bias_ref, wc_ref, ws_ref, wct_ref, wst_ref, ka_ref, ka2_ref, kb_ref, o_ref, acc_scr):
    f = pl.program_id(1)
    nb = z_ref.shape[0]
    c = z_ref.shape[2]
    z = jnp.concatenate([z_ref[e] for e in range(nb)], axis=1)
    tile = lambda ref: jnp.concatenate([ref[...]] * nb, axis=1)

    @pl.when(f == 0)
    def _():
        acc_scr[...] = z.astype(F32) * tile(bias_ref)

    zc = jnp.dot(wc_ref[...], z, preferred_element_type=F32)
    zs = jnp.dot(ws_ref[...], z, preferred_element_type=F32)
    ka, ka2, kb = tile(ka_ref), tile(ka2_ref), tile(kb_ref)
    a = zc * ka + zs * kb
    bv = zs * ka2 - zc * kb
    acc_scr[...] += (jnp.dot(wct_ref[...], a.astype(BF16), preferred_element_type=F32)
                     + jnp.dot(wst_ref[...], bv.astype(BF16), preferred_element_type=F32))

    @pl.when(f == pl.num_programs(1) - 1)
    def _():
        for e in range(nb):
            o_ref[e] = (x0_ref[e].astype(F32) * acc_scr[:, e * c:(e + 1) * c]).astype(o_ref.dtype)


def _hyena(z, x0, row_block, length, bias, tables, spec):
    b, t, c = z.shape
    wc, ws, wct, wst = tables
    ka, ka2, kb = spec
    fr = min(length, 256)
    nf = length // fr
    nb = HYENA_BATCH
    seq = pl.BlockSpec((nb, length, c), lambda b, f: (b, row_block, 0))
    in_specs = [seq, seq,
                pl.BlockSpec((1, c), lambda b, f: (0, 0)),
                pl.BlockSpec((fr, length), lambda b, f: (f, 0)),
                pl.BlockSpec((fr, length), lambda b, f: (f, 0)),
                pl.BlockSpec((length, fr), lambda b, f: (0, f)),
                pl.BlockSpec((length, fr), lambda b, f: (0, f)),
                pl.BlockSpec((fr, c), lambda b, f: (f, 0)),
                pl.BlockSpec((fr, c), lambda b, f: (f, 0)),
                pl.BlockSpec((fr, c), lambda b, f: (f, 0))]
    return pl.pallas_call(
        _hyena_kernel,
        grid=(b // nb, nf),
        in_specs=in_specs,
        out_specs=pl.BlockSpec((nb, length, c), lambda b, f: (b, 0, 0)),
        out_shape=jax.ShapeDtypeStruct((b, length, c), BF16),
        scratch_shapes=[pltpu.VMEM((length, nb * c), F32)],
        compiler_params=_cp("arbitrary", "arbitrary"),
        name="hyena_%d" % length,
    )(z, x0, bias.reshape(1, c), wc, ws, wct, wst, ka, ka2, kb)


def _rw_prep_kernel(p_ref, pprev_ref, pnext_ref, mu_ref, w0_ref, wup_ref, a0_ref, aup_ref, gup_ref, kk_ref, ka_ref, rk_ref,
                    r_o, v_o, kap_o, lwf_o, lwb_o, kf_o, kb_o, bf_o, bb_o, g_o, bonus_o, *, n_lat_tiles):
    t = pl.program_id(1)
    nt = pl.num_programs(1)
    p = p_ref[0]
    first = jnp.logical_or(t == 0, t == n_lat_tiles)
    last = jnp.logical_or(t == n_lat_tiles - 1, t == nt - 1)
    prev_row = jnp.where(first, 0.0, pprev_ref[0, 7:8, :])
    next_row = jnp.where(last, 0.0, pnext_ref[0, 0:1, :])
    pm, pp = _shift_rows(p, prev_row, next_row)
    p = p + mu_ref[0:1] * (pm - p) + mu_ref[1:2] * (pp - p)
    c = MIX_W
    r, k, v = p[:, 0:c], p[:, c:2 * c], p[:, 2 * c:3 * c]
    wd, ad, gd = p[:, 3 * c:3 * c + 128], p[:, 3 * c + 128:3 * c + 256], p[:, 3 * c + 256:3 * c + 384]
    bd = _head_blockdiag(c, RW_HEAD_DIM)
    g = _mm(_sigmoid(gd), gup_ref[...])
    kk = k * kk_ref[...]
    kap = kk * lax.rsqrt(jnp.maximum(_x_exact01(kk * kk, bd), 1e-24))
    logw = -RW_DECAY_SCALE * _sigmoid(w0_ref[...] + _mm(jnp.tanh(wd), wup_ref[...]))
    a = _sigmoid(a0_ref[...] + _mm(ad, aup_ref[...]))
    k_sum = 0.0
    outs_k, outs_b = (kf_o, kb_o), (bf_o, bb_o)
    for d in range(2):
        a_d = a[:, d * c:(d + 1) * c]
        k_d = k * (1.0 + (a_d - 1.0) * ka_ref[...])
        outs_k[d][0] = k_d.astype(BF16)
        outs_b[d][0] = (a_d * kap).astype(BF16)
        k_sum = k_sum + k_d
    r_o[0] = r.astype(BF16)
    v_o[0] = v.astype(BF16)
    kap_o[0] = kap.astype(BF16)
    lwf_o[0] = logw[:, 0:c]
    lwb_o[0] = logw[:, c:2 * c]
    g_o[0] = g
    bonus_o[0] = _x_exact01(r * k_sum * rk_ref[...], bd) * v


def _blockdiag2(a, b):
    z1 = jnp.zeros((a.shape[0], b.shape[1]), a.dtype)
    z2 = jnp.zeros((b.shape[0], a.shape[1]), a.dtype)
    return jnp.concatenate([jnp.concatenate([a, z1], 1), jnp.concatenate([z2, b], 1)], 0)


def _rw_prep(p_rw, n_lat_tiles, mu, w0, w_up, a0, a_up, g_up, k_k, k_a, r_k):
    b, t, n = p_rw.shape
    nt = t // TOKEN_TILE
    c = MIX_W
    sub = TOKEN_TILE // 8
    n8 = t // 8
    consts = [mu, w0.reshape(1, 2 * c), _blockdiag2(w_up[0], w_up[1]), a0.reshape(1, 2 * c), _blockdiag2(a_up[0], a_up[1]),
              g_up, k_k.reshape(1, c), k_a.reshape(1, c), r_k.reshape(1, c)]
    cspecs = [pl.BlockSpec(x.shape, lambda b, t: (0, 0)) for x in consts]
    tile = lambda w: pl.BlockSpec((1, TOKEN_TILE, w), lambda b, t: (b, t, 0))
    return pl.pallas_call(
        functools.partial(_rw_prep_kernel, n_lat_tiles=n_lat_tiles),
        grid=(b, nt),
        in_specs=[tile(n),
                  pl.BlockSpec((1, 8, n), lambda b, t: (b, jnp.maximum(t * sub - 1, 0), 0)),
                  pl.BlockSpec((1, 8, n), lambda b, t: (b, jnp.minimum((t + 1) * sub, n8 - 1), 0))] + cspecs,
        out_specs=[tile(c)] * 11,
        out_shape=[jax.ShapeDtypeStruct((b, t, c), dt) for dt in (BF16, BF16, BF16, F32, F32, BF16, BF16, BF16, BF16, F32, F32)],
        compiler_params=_cp("arbitrary", "arbitrary"),
        name="rw_prep",
    )(p_rw, p_rw, p_rw, *consts)


def _rw_scan_kernel(rf, vf, kapf, lwf, kf, bbf, rb, vb, kapb, lwb, kb, bbb, of_ref, ob_ref, s_scr):
    @pl.when(pl.program_id(1) == 0)
    def _():
        s_scr[...] = jnp.zeros_like(s_scr)

    c = rf.shape[1]
    pw_ = 2 * RW_HEAD_DIM
    n_pairs = MIX_W // pw_
    lane_a = lax.broadcasted_iota(jnp.int32, (c, pw_), 1) < RW_HEAD_DIM
    lane_a2 = lax.broadcasted_iota(jnp.int32, (2 * c, pw_), 1) < RW_HEAD_DIM
    blk = ((lax.broadcasted_iota(jnp.int32, (pw_, pw_), 0) < RW_HEAD_DIM)
           == (lax.broadcasted_iota(jnp.int32, (pw_, pw_), 1) < RW_HEAD_DIM))

    def sel(x):
        return jnp.where(lane_a, x[:c], x[c:])

    def bf(x):
        return x.astype(BF16)

    nb = rf.shape[0]
    chains = []
    for e in range(nb):
        for d, (r, v, kap, lw, k, b) in enumerate(((rf, vf, kapf, lwf, kf, bbf), (rb, vb, kapb, lwb, kb, bbb))):
            rev = d == 1
            incl = _tri(c, rev, False)
            strict = _tri(c, rev, True)
            logw = lw[e]
            cum = _mm_exact01(incl.astype(F32), logw)
            tot = cum[0:1] if rev else cum[c - 1:c]
            ginv = jnp.exp(-cum)
            gt = jnp.exp(tot - cum)
            rt = r[e] * jnp.exp(cum)
            kt = kap[e] * jnp.exp(cum - logw)
            kd, bd = k[e] * ginv, b[e] * ginv
            kg, bg = k[e] * gt, b[e] * gt
            dec = jnp.exp(tot)
            vv = v[e]
            for p in range(n_pairs):
                sl = slice(p * pw_, (p + 1) * pw_)
                lhs2 = jnp.concatenate([rt[:, sl], kt[:, sl]], 0)
                chains.append(dict(
                    e=e, d=d, p=p, idx=(e * 2 + d) * n_pairs + p, incl=incl, strict=strict,
                    lhs2=bf(lhs2),
                    lhs4=bf(jnp.concatenate([jnp.where(lane_a2, lhs2, 0.0), jnp.where(lane_a2, 0.0, lhs2)], 0)),
                    kd=bf(kd[:, sl]), bd=bf(bd[:, sl]), v=bf(vv[:, sl]),
                    kgbg=bf(jnp.concatenate([kg[:, sl], bg[:, sl]], 0)), dec=dec[:, sl]))

    dot = functools.partial(lax.dot_general, preferred_element_type=F32)
    nn = (((1,), (0,)), ((), ()))
    for ch in chains:
        ch["s"] = s_scr[ch["idx"]]
    for ch in chains:
        ch["a1k"] = dot(ch["lhs4"], ch["kd"], NT_DIMS)
        ch["a1b"] = dot(ch["lhs4"], ch["bd"], NT_DIMS)
        ch["a2"] = dot(ch["lhs2"], bf(ch["s"]), NT_DIMS)
    for ch in chains:
        a1k, a1b, st = ch["a1k"], ch["a1b"], ch["strict"]
        lk = jnp.where(jnp.concatenate([st, st], 0), jnp.concatenate([a1k[c:2 * c], a1k[3 * c:]], 0), 0.0)
        ch["lb_a"] = bf(jnp.where(st, a1b[c:2 * c], 0.0))
        ch["lb_b"] = bf(jnp.where(st, a1b[3 * c:], 0.0))
        ch["u"] = ch["a2"][c:] + sel(dot(bf(lk), ch["v"], nn))
    for ch in chains:
        ch["u"] = ch["u"] - sel(dot(jnp.concatenate([ch["lb_a"], ch["lb_b"]], 0), bf(ch["u"]), nn))
        ch["p_a"] = bf(dot(ch["lb_a"], ch["lb_a"], nn))
        ch["p_b"] = bf(dot(ch["lb_b"], ch["lb_b"], nn))
    n = int(math.log2(c)) - 1
    for i in range(n):
        for ch in chains:
            ch["u"] = ch["u"] + sel(dot(jnp.concatenate([ch["p_a"], ch["p_b"]], 0), bf(ch["u"]), nn))
            if i < n - 1:
                ch["p_a"] = bf(dot(ch["p_a"], ch["p_a"], nn))
                ch["p_b"] = bf(dot(ch["p_b"], ch["p_b"], nn))
    outs = [[[None] * n_pairs, [None] * n_pairs] for _ in range(nb)]
    for ch in chains:
        a1k, a1b, inc = ch["a1k"], ch["a1b"], ch["incl"]
        inc2 = jnp.concatenate([inc, inc], 0)
        rk = jnp.where(inc2, jnp.concatenate([a1k[:c], a1k[2 * c:3 * c]], 0), 0.0)
        rb_ = jnp.where(inc2, jnp.concatenate([a1b[:c], a1b[2 * c:3 * c]], 0), 0.0)
        ub = bf(ch["u"])
        o = ch["a2"][:c] + sel(dot(bf(rk), ch["v"], nn)) - sel(dot(bf(rb_), ub, nn))
        outs[ch["e"]][ch["d"]][ch["p"]] = o
        upd = dot(jnp.concatenate([ch["v"], -ub], 0), ch["kgbg"], TN_DIMS)
        s_scr[ch["idx"]] = ch["s"] * ch["dec"] + jnp.where(blk, upd, 0.0)
    for e in range(nb):
        of_ref[e] = jnp.concatenate(outs[e][0], axis=1)
        ob_ref[e] = jnp.concatenate(outs[e][1], axis=1)


def _chunk_orders(n_lat, n_ctx):
    n = n_lat + n_ctx
    fwd = lambda b, s: (b, (s + n_lat) % n, 0)
    rev = lambda b, s: (b, n - 1 - s, 0)
    return fwd, rev


def _rw_scan(r, v, kap, lwf, lwb, kf, kb, bf, bb, n_lat_rows):
    b, t, c = r.shape
    ch = SEQ_CHUNK
    n = t // ch
    fwd, rev = _chunk_orders(n_lat_rows // ch, (t - n_lat_rows) // ch)
    nb = RW_SCAN_BATCH
    blk = lambda im: pl.BlockSpec((nb, ch, c), im)
    return pl.pallas_call(
        _rw_scan_kernel,
        grid=(b // nb, n),
        in_specs=[blk(fwd)] * 6 + [blk(rev)] * 6,
        out_specs=[blk(fwd), blk(rev)],
        out_shape=[jax.ShapeDtypeStruct((b, t, c), F32)] * 2,
        scratch_shapes=[pltpu.VMEM((nb * RW_HEADS, 2 * RW_HEAD_DIM, 2 * RW_HEAD_DIM), F32)],
        compiler_params=_cp("arbitrary", "arbitrary"),
        name="rw_scan",
    )(r, v, kap, lwf, kf, bf, r, v, kap, lwb, kb, bb)


def _rw_readout(of, ob, bonus, g, ln_g, ln_b):
    o = of + ob
    bd = _head_blockdiag(MIX_W, RW_HEAD_DIM)
    mean = _x_exact01(o, bd) * (1.0 / RW_HEAD_DIM)
    oc = o - mean
    var = _x_exact01(oc * oc, bd) * (1.0 / RW_HEAD_DIM)
    on = oc * lax.rsqrt(var + RW_GN_EPS) * ln_g + ln_b
    return (on + bonus) * g


def _s5_matrices(a_re, a_im, log_dt, b_re, b_im, c_re, c_im):
    tc, g, n, p = S5_CHUNK, S5_GROUPS, S5_STATE, S5_GROUP
    hp = lax.Precision.HIGHEST
    dt = jnp.exp(log_dt)[:, :, None]
    lam_re, lam_im = a_re * dt, a_im * dt
    mag1 = jnp.exp(lam_re)
    ab_re, ab_im = mag1 * jnp.cos(lam_im), mag1 * jnp.sin(lam_im)
    nr, ni = ab_re - 1.0, ab_im
    den = a_re * a_re + a_im * a_im
    f_re, f_im = (nr * a_re + ni * a_im) / den, (ni * a_re - nr * a_im) / den
    tau = jnp.arange(tc + 1, dtype=F32)[:, None, None, None]
    pw_mag = jnp.exp(lam_re[None] * tau)
    pw_re, pw_im = pw_mag * jnp.cos(lam_im[None] * tau), pw_mag * jnp.sin(lam_im[None] * tau)
    fb_re = f_re[..., None] * b_re[None] - f_im[..., None] * b_im[None]
    fb_im = f_re[..., None] * b_im[None] + f_im[..., None] * b_re[None]
    e_re = pw_re[..., None] * fb_re[None] - pw_im[..., None] * fb_im[None]
    e_im = pw_re[..., None] * fb_im[None] + pw_im[..., None] * fb_re[None]
    m = (jnp.einsum('gpn,tdgnq->tdgpq', c_re, e_re[:tc], precision=hp)
         - jnp.einsum('gpn,tdgnq->tdgpq', c_im, e_im[:tc], precision=hp))
    i = jnp.arange(tc)
    lag = i[None, :] - i[:, None]
    mf = jnp.where((lag >= 0)[:, :, None, None, None], m[jnp.clip(lag, 0, tc - 1), 0], 0.0)
    mb = jnp.where((lag <= 0)[:, :, None, None, None], m[jnp.clip(-lag, 0, tc - 1), 1], 0.0)
    w = jnp.transpose(mf + mb, (2, 0, 4, 1, 3)).reshape(g, tc * p, tc * p)
    idx_f = tc - 1 - i
    pf_re, pf_im = e_re[idx_f, 0], e_im[idx_f, 0]
    pb_re, pb_im = e_re[i, 1], e_im[i, 1]
    pm = jnp.concatenate([pf_re, pb_re, pf_im, pb_im], axis=2)
    pm = jnp.transpose(pm, (1, 0, 3, 2)).reshape(g, tc * p, 4 * n)
    cf_re = c_re[None, :, :, :] * pw_re[i + 1, 0][:, :, None, :] - c_im[None] * pw_im[i + 1, 0][:, :, None, :]
    cf_im = c_re[None] * pw_im[i + 1, 0][:, :, None, :] + c_im[None] * pw_re[i + 1, 0][:, :, None, :]
    cb_re = c_re[None] * pw_re[tc - i, 1][:, :, None, :] - c_im[None] * pw_im[tc - i, 1][:, :, None, :]
    cb_im = c_re[None] * pw_im[tc - i, 1][:, :, None, :] + c_im[None] * pw_re[tc - i, 1][:, :, None, :]
    q = jnp.concatenate([cf_re, cb_re, -cf_im, -cb_im], axis=3)
    q = jnp.transpose(q, (1, 3, 0, 2)).reshape(g, 4 * n, tc * p)
    dec_re = jnp.concatenate([pw_re[tc, 0], pw_re[tc, 1]], axis=1)
    dec_im = jnp.concatenate([pw_im[tc, 0], pw_im[tc, 1]], axis=1)
    dec = jnp.stack([dec_re, dec_im], axis=1)
    return w.astype(BF16), pm.astype(BF16), q.astype(BF16), dec


S5_PACK_BATCH = 2
S5_CORE_GROUPS = 2
S5_LANE_GROUPS = 128 // S5_GROUP


def _s5_pack_kernel(x_ref, o_ref):
    rows = o_ref.shape[0]
    grp = lax.broadcasted_iota(jnp.int32, (rows, 128), 1) // S5_GROUP
    tiles = [jnp.zeros((rows, 128), F32) for _ in range(2 * S5_LANE_GROUPS)]
    for j in range(S5_CHUNK):
        xj = x_ref[pl.ds(j, rows, stride=S5_CHUNK), :]
        jl, jh = j % S5_LANE_GROUPS, j // S5_LANE_GROUPS
        for g8 in range(S5_LANE_GROUPS):
            sh = ((jl - g8) * S5_GROUP) % 128
            moved = pltpu.roll(xj, sh, 1) if sh else xj
            tiles[2 * g8 + jh] = jnp.where(grp == jl, moved, tiles[2 * g8 + jh])
    o_ref[...] = jnp.concatenate(tiles, axis=1).astype(o_ref.dtype)


def _s5_unpack_kernel(y_ref, o_ref):
    rows = y_ref.shape[0]
    grp = lax.broadcasted_iota(jnp.int32, (rows, 128), 1) // S5_GROUP
    for i in range(S5_CHUNK):
        il, ih = i % S5_LANE_GROUPS, i // S5_LANE_GROUPS
        out = jnp.zeros((rows, 128), F32)
        for g8 in range(S5_LANE_GROUPS):
            src = y_ref[:, (2 * g8 + ih) * 128:(2 * g8 + ih + 1) * 128]
            sh = ((g8 - il) * S5_GROUP) % 128
            moved = pltpu.roll(src, sh, 1) if sh else src
            out = jnp.where(grp == g8, moved, out)
        o_ref[pl.ds(i, rows, stride=S5_CHUNK), :] = out


def _s5_kernel(u_ref, w_ref, p_ref, q_ref, dec_ref, y_ref, inc_re, inc_im, sf_re, sf_im, sb_re, sb_im, *, n_lat, n_ctx, bsz):
    gs = w_ref.shape[0]
    k = w_ref.shape[1]
    n2 = 2 * S5_STATE
    us = [u_ref[:, g * k:(g + 1) * k] for g in range(gs)]
    for g in range(gs):
        inc = jnp.dot(us[g], p_ref[g], preferred_element_type=F32)
        inc_re[g] = inc[:, :n2]
        inc_im[g] = inc[:, n2:]
    n = n_lat + n_ctx
    is_f = lax.broadcasted_iota(jnp.int32, (bsz, n2), 1) < S5_STATE

    def body(s, carry):
        rf = pl.ds((s + n_lat) % n, bsz, stride=n)
        rb = pl.ds(n - 1 - s, bsz, stride=n)
        new = []
        for g in range(gs):
            s_re, s_im = carry[g]
            d_re, d_im = dec_ref[g, 0:1, :], dec_ref[g, 1:2, :]
            sf_re[g, rf, :] = s_re
            sf_im[g, rf, :] = s_im
            sb_re[g, rb, :] = s_re
            sb_im[g, rb, :] = s_im
            i_re = jnp.where(is_f, inc_re[g, rf, :], inc_re[g, rb, :])
            i_im = jnp.where(is_f, inc_im[g, rf, :], inc_im[g, rb, :])
            new.append((d_re * s_re - d_im * s_im + i_re, d_re * s_im + d_im * s_re + i_im))
        return tuple(new)

    zero = jnp.zeros((bsz, n2), F32)
    lax.fori_loop(0, n, body, tuple((zero, zero) for _ in range(gs)))
    is_f_all = lax.broadcasted_iota(jnp.int32, sf_re.shape[1:], 1) < S5_STATE
    for g in range(gs):
        s_all = jnp.concatenate([jnp.where(is_f_all, sf_re[g], sb_re[g]),
                                 jnp.where(is_f_all, sf_im[g], sb_im[g])], axis=1).astype(BF16)
        y_ref[:, g * k:(g + 1) * k] = (jnp.dot(us[g], w_ref[g], preferred_element_type=F32)
                                       + jnp.dot(s_all, q_ref[g], preferred_element_type=F32))


def _s5_core(p_s5, n_lat_rows, mats):
    b, t, c = p_s5.shape
    tc, g = S5_CHUNK, S5_GROUPS
    nch = t // tc
    w, pm, q, dec = mats
    k = tc * S5_GROUP
    nbb = S5_PACK_BATCH
    gs = S5_CORE_GROUPS
    n_tiles = c // 128
    tok_blk = pl.BlockSpec((nbb * t, 128), lambda i, o: (i, o))
    chk_blk = pl.BlockSpec((nbb * nch, S5_LANE_GROUPS * k), lambda i, o: (i, o))
    u = pl.pallas_call(
        _s5_pack_kernel,
        grid=(b // nbb, n_tiles),
        in_specs=[tok_blk],
        out_specs=chk_blk,
        out_shape=jax.ShapeDtypeStruct((b * nch, g * k), BF16),
        compiler_params=_cp("arbitrary", "arbitrary"),
        name="s5_pack",
    )(p_s5.reshape(b * t, c))
    rows = nch * b
    y = pl.pallas_call(
        functools.partial(_s5_kernel, n_lat=n_lat_rows // tc, n_ctx=(t - n_lat_rows) // tc, bsz=b),
        grid=(g // gs,),
        in_specs=[pl.BlockSpec((rows, gs * k), lambda i: (0, i)),
                  pl.BlockSpec((gs, k, k), lambda i: (i, 0, 0)),
                  pl.BlockSpec((gs, k, 4 * S5_STATE), lambda i: (i, 0, 0)),
                  pl.BlockSpec((gs, 4 * S5_STATE, k), lambda i: (i, 0, 0)),
                  pl.BlockSpec((gs, 2, 2 * S5_STATE), lambda i: (i, 0, 0))],
        out_specs=pl.BlockSpec((rows, gs * k), lambda i: (0, i)),
        out_shape=jax.ShapeDtypeStruct((rows, g * k), F32),
        scratch_shapes=[pltpu.VMEM((gs, rows, 2 * S5_STATE), F32)] * 6,
        compiler_params=_cp("arbitrary"),
        name="s5_core",
    )(u, w, pm, q, dec)
    out = pl.pallas_call(
        _s5_unpack_kernel,
        grid=(b // nbb, n_tiles),
        in_specs=[chk_blk],
        out_specs=tok_blk,
        out_shape=jax.ShapeDtypeStruct((b * t, c), F32),
        compiler_params=_cp("arbitrary", "arbitrary"),
        name="s5_unpack",
    )(y)
    return out.reshape(b, t, c)


def _s5_output(y, u, d_skip, glu_w, glu_b):
    z = y + d_skip * u
    z = 0.5 * z * (1.0 + jnp.tanh(math.sqrt(2.0 / math.pi) * (z + 0.044715 * (z * z * z))))
    return z * _sigmoid(_mm(z, glu_w) + glu_b)


def _gla_scan_kernel(pfv_ref, pfc_ref, pbv_ref, pbc_ref, gup_ref, gb_ref, of_ref, ob_ref, s_scr, *, n_lat, n_ctx, cols, c):
    step = pl.program_id(1)

    @pl.when(step == 0)
    def _():
        s_scr[...] = jnp.zeros_like(s_scr)

    subs = pfc_ref.shape[1] // c
    n = n_lat + n_ctx
    block_ids = ((step + n_lat) % n, n - 1 - step)
    pw_ = 2 * GLA_HK
    n_pairs = GLA_DK // pw_
    lane_a = lax.broadcasted_iota(jnp.int32, (c, pw_), 1) < GLA_HK
    lane_av = lax.broadcasted_iota(jnp.int32, (GLA_HV, pw_), 1) < GLA_HK
    dot = functools.partial(lax.dot_general, preferred_element_type=F32)
    nn = (((1,), (0,)), ((), ()))
    nb = pfc_ref.shape[0]
    mask2 = lambda a: jnp.concatenate([jnp.where(lane_a, a, 0.0), jnp.where(lane_a, 0.0, a)], 0).astype(BF16)
    for sub in range(subs):
        chains = []
        for e in range(nb):
            for d, (v_ref, c_ref) in enumerate(((pfv_ref, pfc_ref), (pbv_ref, pbc_ref))):
                rev = d == 1
                cs = subs - 1 - sub if rev else sub
                incl = _tri(c, rev, False)
                lat_rows = jnp.concatenate([v_ref[e, :, cs * cols + j, :] for j in range(cols)], axis=0)
                pe = jnp.where(block_ids[d] < n_lat, lat_rows, c_ref[e, cs * c:(cs + 1) * c, :])
                gd = pe[:, 2 * GLA_DK + 2 * GLA_DV:]
                x = _mm(gd, gup_ref[...])[:, d * GLA_DK:(d + 1) * GLA_DK] + gb_ref[0:1, d * GLA_DK:(d + 1) * GLA_DK]
                logg = (jnp.minimum(x, 0.0) - jnp.log(1.0 + jnp.exp(-jnp.abs(x)))) * (1.0 / GLA_GATE_NORM)
                cum = _mm_exact01(incl.astype(F32), logg)
                mid = c // 2
                ref = cum[mid:mid + 1] if rev else cum[mid - 1:mid]
                tot = cum[0:1] if rev else cum[c - 1:c]
                q = pe[:, 0:GLA_DK] * (GLA_HK ** -0.5)
                k = pe[:, GLA_DK:2 * GLA_DK]
                qs = q * jnp.exp(cum - ref)
                ks = k * jnp.exp(ref - cum)
                qc = q * jnp.exp(cum)
                kg = k * jnp.exp(tot - cum)
                dec = jnp.exp(tot)
                for p in range(n_pairs):
                    sl = slice(p * pw_, (p + 1) * pw_)
                    chains.append(dict(
                        e=e, d=d, p=p, idx=(e * 2 + d) * n_pairs + p, incl=incl, qs4=mask2(qs[:, sl]), qc4=mask2(qc[:, sl]),
                        ks=ks[:, sl].astype(BF16), kg=kg[:, sl].astype(BF16), dec=dec[:, sl],
                        va=pe[:, 2 * GLA_DK + 2 * p * GLA_HV:2 * GLA_DK + (2 * p + 1) * GLA_HV].astype(BF16),
                        vb=pe[:, 2 * GLA_DK + (2 * p + 1) * GLA_HV:2 * GLA_DK + (2 * p + 2) * GLA_HV].astype(BF16)))
        for ch in chains:
            ch["s"] = s_scr[ch["idx"]]
        for ch in chains:
            ch["sc"] = dot(ch["qs4"], ch["ks"], NT_DIMS)
            ch["st"] = dot(ch["qc4"], ch["s"].astype(BF16), NT_DIMS)
        outs = [[[None] * GLA_HEADS, [None] * GLA_HEADS] for _ in range(nb)]
        for ch in chains:
            e, d, p = ch["e"], ch["d"], ch["p"]
            sa = jnp.where(ch["incl"], ch["sc"][:c], 0.0).astype(BF16)
            sb = jnp.where(ch["incl"], ch["sc"][c:], 0.0).astype(BF16)
            outs[e][d][2 * p] = ch["st"][:c] + dot(sa, ch["va"], nn)
            outs[e][d][2 * p + 1] = ch["st"][c:] + dot(sb, ch["vb"], nn)
            upd = jnp.where(lane_av, dot(ch["va"], ch["kg"], TN_DIMS), dot(ch["vb"], ch["kg"], TN_DIMS))
            s_scr[ch["idx"]] = ch["s"] * ch["dec"] + upd
        for e in range(nb):
            of_ref[e, sub * c:(sub + 1) * c, :] = jnp.concatenate(outs[e][0], axis=1)
            ob_ref[e, (subs - 1 - sub) * c:(subs - sub) * c, :] = jnp.concatenate(outs[e][1], axis=1)


def _gla_scan(p_gla, n_lat_rows, gate_up, gate_b):
    b, t, n = p_gla.shape
    ch = SEQ_CHUNK
    grid_rows = n_lat_rows // GRID_W
    cols = _grid_cols(n_lat_rows, ch)
    blk_rows = GRID_COL_BLOCK * grid_rows
    assert GRID_COL_BLOCK % cols == 0 and blk_rows % ch == 0 and (t - n_lat_rows) % blk_rows == 0
    n_lat, n_ctx = n_lat_rows // blk_rows, (t - n_lat_rows) // blk_rows
    n_blk = n_lat + n_ctx
    fwd, rev = _chunk_orders(n_lat, n_ctx)
    gup = jnp.zeros((128, 2 * GLA_DK), F32)
    gup = gup.at[0:GLA_GATE_RANK, 0:GLA_DK].set(gate_up[0]).at[GLA_GATE_RANK:2 * GLA_GATE_RANK, GLA_DK:].set(gate_up[1])
    nb = GLA_SCAN_BATCH
    p_view = p_gla.reshape(b, t // GRID_W, GRID_W, n)
    view_blk = lambda bid: pl.BlockSpec((nb, grid_rows, GRID_COL_BLOCK, n), lambda b, s: (b, 0, jnp.minimum(bid(s), n_lat - 1), 0))
    ctx_blk = lambda bid: pl.BlockSpec((nb, blk_rows, n), lambda b, s: (b, jnp.maximum(bid(s), n_lat), 0))
    bid_f = lambda s: (s + n_lat) % n_blk
    bid_b = lambda s: n_blk - 1 - s
    return pl.pallas_call(
        functools.partial(_gla_scan_kernel, n_lat=n_lat, n_ctx=n_ctx, cols=cols, c=ch),
        grid=(b // nb, n_blk),
        in_specs=[view_blk(bid_f), ctx_blk(bid_f), view_blk(bid_b), ctx_blk(bid_b),
                  pl.BlockSpec((128, 2 * GLA_DK), lambda b, s: (0, 0)),
                  pl.BlockSpec((1, 2 * GLA_DK), lambda b, s: (0, 0))],
        out_specs=[pl.BlockSpec((nb, blk_rows, GLA_DV), fwd), pl.BlockSpec((nb, blk_rows, GLA_DV), rev)],
        out_shape=[jax.ShapeDtypeStruct((b, t, GLA_DV), F32)] * 2,
        scratch_shapes=[pltpu.VMEM((nb * GLA_HEADS, GLA_HV, 2 * GLA_HK), F32)],
        compiler_params=_cp("arbitrary", "arbitrary"),
        name="gla_scan",
    )(p_view, p_gla, p_view, p_gla, gup, gate_b.reshape(1, 2 * GLA_DK))


def _gla_readout_kernel(of_ref, ob_ref, gv_ref, gc_ref, ng_ref, ylat_ref, yctx_ref, *, n_lat_tiles, cols):
    t = pl.program_id(1)
    o = of_ref[0] + ob_ref[0]
    bd = _head_blockdiag(GLA_DV, GLA_HV)
    ms = _x_exact01(o * o, bd) * (1.0 / GLA_HV)
    g_lo = 2 * GLA_DK + GLA_DV
    g = jnp.where(t < n_lat_tiles, _col_major_rows(gv_ref, 0, 0, cols, g_lo, g_lo + GLA_DV), gc_ref[0])
    y = (o * lax.rsqrt(ms + RMS_EPS) * ng_ref[...] * _silu(g)).astype(ylat_ref.dtype)
    rows = y.shape[0] // cols

    @pl.when(t < n_lat_tiles)
    def _():
        for j in range(cols):
            ylat_ref[0, :, j, :] = y[j * rows:(j + 1) * rows]

    @pl.when(t >= n_lat_tiles)
    def _():
        yctx_ref[0] = y


def _gla_readout(of, ob, p_gla, norm_g, n_lat_rows):
    b, t, c = of.shape
    n = p_gla.shape[2]
    n_lat_tiles = n_lat_rows // TOKEN_TILE
    grid_rows = n_lat_rows // GRID_W
    cols = _grid_cols(n_lat_rows, TOKEN_TILE)
    assert cols == GRID_COL_BLOCK
    tile = pl.BlockSpec((1, TOKEN_TILE, c), lambda b, t: (b, t, 0))
    p_view = p_gla.reshape(b, t // GRID_W, GRID_W, n)
    y_lat, y_ctx = pl.pallas_call(
        functools.partial(_gla_readout_kernel, n_lat_tiles=n_lat_tiles, cols=cols),
        grid=(b, t // TOKEN_TILE),
        in_specs=[tile, tile,
                  pl.BlockSpec((1, grid_rows, cols, n), lambda b, t: (b, 0, jnp.minimum(t, n_lat_tiles - 1), 0)),
                  pl.BlockSpec((1, TOKEN_TILE, c), lambda b, t: (b, jnp.maximum(t, n_lat_tiles), (2 * GLA_DK + GLA_DV) // c)),
                  pl.BlockSpec((1, c), lambda b, t: (0, 0))],
        out_specs=[pl.BlockSpec((1, grid_rows, cols, c), lambda b, t: (b, 0, jnp.minimum(t, n_lat_tiles - 1), 0)),
                   pl.BlockSpec((1, TOKEN_TILE, c), lambda b, t: (b, 0, 0))],
        out_shape=[jax.ShapeDtypeStruct((b, grid_rows, GRID_W, c), BF16), jax.ShapeDtypeStruct((b, t - n_lat_rows, c), BF16)],
        compiler_params=_cp("arbitrary", "arbitrary"),
        name="gla_readout",
    )(of, ob, p_view, p_gla, jnp.tile(norm_g, GLA_HEADS).reshape(1, c))
    return y_lat.reshape(b, n_lat_rows, c), y_ctx


def _merge_kernel(h_ref, mod_ref, gate_ref, yhyl_ref, yhyc_ref, rof_ref, rob_ref, rbonus_ref, rg_ref, ys5_ref, us5_ref,
                  yglal_ref, yglac_ref, lng_ref, lnb_ref, dskip_ref, gluw_ref, glub_ref, wbr_ref, wout_ref, o_ref,
                  *, n_lat_tiles):
    d = D_MODEL
    is_lat = pl.program_id(1) < n_lat_tiles
    y_hy = jnp.where(is_lat, yhyl_ref[0], yhyc_ref[0])
    y_gla = jnp.where(is_lat, yglal_ref[0], yglac_ref[0])
    y_rw = _rw_readout(rof_ref[0], rob_ref[0], rbonus_ref[0], rg_ref[0], lng_ref[...], lnb_ref[...])
    y_s5 = _s5_output(ys5_ref[0], us5_ref[0], dskip_ref[...], gluw_ref[...], glub_ref[...])
    acc = jnp.zeros((h_ref.shape[1], d), F32)
    for m, y in enumerate((y_hy, y_rw, y_s5, y_gla)):
        acc = acc + (_sigmoid(gate_ref[0, :, m * d:(m + 1) * d].astype(F32))
                     * jnp.dot(y.astype(BF16), wbr_ref[m], preferred_element_type=F32))
    out = jnp.dot(acc.astype(BF16), wout_ref[...], preferred_element_type=F32)
    o_ref[0] = h_ref[0] + mod_ref[0, 0, 2:3, :] * out


def _merge(h_all, modsel, n_tiles, n_lat_tiles, gate, y_hy_lat, y_hy_ctx, rof, rob, rbonus, rg, y_s5, u_s5, y_gla_lat, y_gla_ctx,
           ln_g, ln_b, d_skip, glu_w, glu_b, w_branch, w_out):
    b, t, d = h_all.shape
    c = MIX_W
    tile = lambda w: pl.BlockSpec((1, TOKEN_TILE, w), lambda b, t: (b, t, 0))
    lat = pl.BlockSpec((1, TOKEN_TILE, c), lambda b, t: (b, jnp.minimum(t, n_lat_tiles - 1), 0))
    ctx = pl.BlockSpec((1, TOKEN_TILE, c), lambda b, t: (b, jnp.maximum(t - n_lat_tiles, 0), 0))
    vec = lambda x: x.reshape(1, -1)
    consts = [vec(ln_g), vec(ln_b), vec(d_skip), glu_w.astype(BF16), vec(glu_b), w_branch.astype(BF16), w_out.astype(BF16)]
    cspecs = [pl.BlockSpec(x.shape, (lambda n: lambda b, t: (0,) * n)(x.ndim)) for x in consts]
    return pl.pallas_call(
        functools.partial(_merge_kernel, n_lat_tiles=n_lat_tiles),
        grid=(b, n_tiles),
        in_specs=[tile(d), pl.BlockSpec((1, 1, 6, d), _mod_index(n_lat_tiles)), tile(4 * d), lat, ctx] + [tile(c)] * 6
                 + [lat, ctx] + cspecs,
        out_specs=tile(d),
        out_shape=jax.ShapeDtypeStruct((b, t, d), F32),
        input_output_aliases={0: 0},
        compiler_params=_cp("arbitrary", "arbitrary"),
        name="merge",
    )(h_all, modsel, gate, y_hy_lat, y_hy_ctx, rof, rob, rbonus, rg, y_s5, u_s5, y_gla_lat, y_gla_ctx, *consts)


def _ffn_kernel(h_ref, g_ref, mod_ref, w1_ref, w3_ref, w2_ref, o_ref):
    h = h_ref[0]
    xm = _norm_mod(h, g_ref[...], mod_ref[0, 0], 3).astype(BF16)
    a = jnp.dot(xm, w1_ref[...], preferred_element_type=F32)
    b = jnp.dot(xm, w3_ref[...], preferred_element_type=F32)
    y = jnp.dot((_silu(a) * b).astype(BF16), w2_ref[...], preferred_element_type=F32)
    o_ref[0] = h + mod_ref[0, 0, 5:6, :] * y


def _ffn_dense(h_all, g, modsel, n_lat_tiles, w1, w3, w2):
    b, t, d = h_all.shape
    tile = pl.BlockSpec((1, TOKEN_TILE, d), lambda b, t: (b, t, 0))
    ws = [w1.astype(BF16), w3.astype(BF16), w2.astype(BF16)]
    wspec = [pl.BlockSpec(w.shape, lambda b, t: (0, 0), pipeline_mode=pl.Buffered(1)) for w in ws]
    return pl.pallas_call(
        _ffn_kernel,
        grid=(b, t // TOKEN_TILE),
        in_specs=[tile, pl.BlockSpec((1, d), lambda b, t: (0, 0)), pl.BlockSpec((1, 1, 6, d), _mod_index(n_lat_tiles))] + wspec,
        out_specs=tile,
        out_shape=jax.ShapeDtypeStruct((b, t, d), F32),
        input_output_aliases={0: 0},
        compiler_params=_cp("arbitrary", "arbitrary"),
        name="ffn_dense",
    )(h_all, g.reshape(1, d), modsel, *ws)


def _router_kernel(h_ref, g_ref, mod_ref, rw_ref, rb_ref, x_ref, logit_ref):
    xm = _norm_mod(h_ref[0], g_ref[...], mod_ref[0, 0], 3)
    x_ref[0] = xm.astype(BF16)
    logit_ref[0] = _mm_hp(xm, rw_ref[...]) + rb_ref[...]


def _router(h_all, g, modsel, n_lat_tiles, router_w, router_b):
    b, t, d = h_all.shape
    n_lat = n_lat_tiles * TOKEN_TILE
    rw = jnp.pad(router_w, ((0, 0), (0, 128 - N_EXPERTS)))
    rb = jnp.pad(router_b, (0, 128 - N_EXPERTS), constant_values=-1e30).reshape(1, 128)
    tile = lambda w: pl.BlockSpec((1, TOKEN_TILE, w), lambda b, t: (b, t, 0))
    return pl.pallas_call(
        _router_kernel,
        grid=(b, n_lat_tiles),
        in_specs=[tile(d), pl.BlockSpec((1, d), lambda b, t: (0, 0)), pl.BlockSpec((1, 1, 6, d), _mod_index(n_lat_tiles)),
                  pl.BlockSpec((d, 128), lambda b, t: (0, 0)), pl.BlockSpec((1, 128), lambda b, t: (0, 0))],
        out_specs=[tile(d), tile(128)],
        out_shape=[jax.ShapeDtypeStruct((b, n_lat, d), BF16), jax.ShapeDtypeStruct((b, n_lat, 128), F32)],
        compiler_params=_cp("arbitrary", "arbitrary"),
        name="moe_router",
    )(h_all, g.reshape(1, d), modsel, rw, rb)


def _moe_expert_kernel(be_ref, x_ref, w1_ref, w3_ref, w2_ref, o_ref):
    x = x_ref[...]
    a = jnp.dot(x, w1_ref[0], preferred_element_type=F32)
    b = jnp.dot(x, w3_ref[0], preferred_element_type=F32)
    o_ref[...] = jnp.dot((_silu(a) * b).astype(BF16), w2_ref[0], preferred_element_type=F32).astype(o_ref.dtype)


def _moe_experts(xb, block_exp, w1, w3, w2):
    rows, d = xb.shape
    nb = rows // MOE_BLOCK_ROWS
    f = w1.shape[2]
    grid_spec = pltpu.PrefetchScalarGridSpec(
        num_scalar_prefetch=1,
        grid=(nb,),
        in_specs=[pl.BlockSpec((MOE_BLOCK_ROWS, d), lambda i, be: (i, 0)),
                  pl.BlockSpec((1, d, f), lambda i, be: (be[i], 0, 0), pipeline_mode=pl.Buffered(1)),
                  pl.BlockSpec((1, d, f), lambda i, be: (be[i], 0, 0), pipeline_mode=pl.Buffered(1)),
                  pl.BlockSpec((1, f, d), lambda i, be: (be[i], 0, 0), pipeline_mode=pl.Buffered(1))],
        out_specs=pl.BlockSpec((MOE_BLOCK_ROWS, d), lambda i, be: (i, 0)),
    )
    return pl.pallas_call(
        _moe_expert_kernel,
        grid_spec=grid_spec,
        out_shape=jax.ShapeDtypeStruct((rows, d), BF16),
        compiler_params=_cp("arbitrary"),
        name="moe_experts",
    )(block_exp, xb, w1, w3, w2)


def _moe_combine_kernel(h_ref, mod_ref, y0_ref, y1_ref, gate_ref, fg_ref, o_ref):
    y = y0_ref[0, 0].astype(F32) * gate_ref[0, :, 0:1] + y1_ref[0, 0].astype(F32) * gate_ref[0, :, 1:2]
    h = h_ref[0] + mod_ref[0, 0, 5:6, :] * y
    o_ref[0] = h * lax.rsqrt(jnp.mean(h * h, axis=-1, keepdims=True) + RMS_EPS) * fg_ref[...]


def _moe_combine(h_all, modsel, n_lat_tiles, y2, gates, final_g):
    b, t, d = h_all.shape
    n_lat = n_lat_tiles * TOKEN_TILE
    tile = lambda w: pl.BlockSpec((1, TOKEN_TILE, w), lambda b, t: (b, t, 0))
    pick = lambda k: pl.BlockSpec((1, 1, TOKEN_TILE, d), lambda b, t: (k, b, t, 0))
    return pl.pallas_call(
        _moe_combine_kernel,
        grid=(b, n_lat_tiles),
        in_specs=[tile(d), pl.BlockSpec((1, 1, 6, d), _mod_index(n_lat_tiles)), pick(0), pick(1),
                  tile(128), pl.BlockSpec((1, d), lambda b, t: (0, 0))],
        out_specs=tile(d),
        out_shape=jax.ShapeDtypeStruct((b, n_lat, d), F32),
        compiler_params=_cp("arbitrary", "arbitrary"),
        name="moe_combine",
    )(h_all, modsel, y2, y2, gates, final_g.reshape(1, d))


def _moe_ffn_and_final_norm(h_all, g, modsel, n_lat_tiles, router_w, router_b, w1, w3, w2, final_g):
    b, t, d = h_all.shape
    n_lat = n_lat_tiles * TOKEN_TILE
    n_tok = b * n_lat
    xn, logits = _router(h_all, g, modsel, n_lat_tiles, router_w, router_b)
    logits = logits.reshape(n_tok, 128)[:, :N_EXPERTS]
    top_v, top_i = lax.top_k(logits, TOP_K)
    gates = jax.nn.softmax(top_v, axis=-1)
    n_assign = n_tok * TOP_K
    e_flat = top_i.reshape(-1).astype(jnp.int32)
    order = jnp.argsort(e_flat).astype(jnp.int32)
    rank = jnp.argsort(order).astype(jnp.int32)
    experts = jnp.arange(N_EXPERTS, dtype=jnp.int32)[None, :]
    is_e = e_flat[:, None] == experts
    counts = jnp.sum(is_e.astype(jnp.int32), axis=0)
    padded = (counts + MOE_BLOCK_ROWS - 1) // MOE_BLOCK_ROWS * MOE_BLOCK_ROWS
    start = jnp.cumsum(counts) - counts
    pend = jnp.cumsum(padded)
    pstart = pend - padded
    n_blocks = n_assign // MOE_BLOCK_ROWS + N_EXPERTS
    n_rows = n_blocks * MOE_BLOCK_ROWS
    row_ids = jnp.arange(n_rows, dtype=jnp.int32)
    row_exp = jnp.minimum(jnp.sum((row_ids[:, None] >= pend[None, :]).astype(jnp.int32), axis=1), N_EXPERTS - 1)
    row_is_e = row_exp[:, None] == experts
    lookup = lambda onehot, table: jnp.sum(jnp.where(onehot, table[None, :], 0), axis=1)
    row_off = row_ids - lookup(row_is_e, pstart)
    row_slot = jnp.clip(lookup(row_is_e, start) + row_off, 0, n_assign - 1)
    row_tok = jnp.where(row_off < lookup(row_is_e, counts), order[row_slot] // TOP_K, n_tok).astype(jnp.int32)
    block_exp = row_exp[::MOE_BLOCK_ROWS]
    tok_pad = jnp.concatenate([xn.reshape(n_tok, d), jnp.zeros((1, d), BF16)], axis=0)
    xb = tok_pad.at[row_tok].get(mode="promise_in_bounds")
    yb = _moe_experts(xb, block_exp, w1.astype(BF16), w3.astype(BF16), w2.astype(BF16))
    dest_orig = (lookup(is_e, pstart) + rank - lookup(is_e, start)).astype(jnp.int32)
    dest_by_k = dest_orig.reshape(n_tok, TOP_K).T.reshape(-1)
    y2 = yb.at[dest_by_k].get(mode="promise_in_bounds").reshape(TOP_K, b, n_lat, d)
    gates_pad = jnp.pad(gates, ((0, 0), (0, 128 - TOP_K))).reshape(b, n_lat, 128)
    return _moe_combine(h_all, modsel, n_lat_tiles, y2, gates_pad, final_g)


def _pad_cols(w, n):
    return jnp.pad(w, ((0, 0), (0, n - w.shape[1])))


def kernel(x, c, ctx, c_ctx, ada_w, ada_b, norm_mix_g, norm_ffn_g, w_in, hy_conv_w, hy_conv_b, hy_f_w1, hy_f_b1, hy_f_w2, hy_f_b2, hy_f_w3, hy_f_b3, hy_bias, rw_mu, rw_w0, rw_w_up, rw_a0, rw_a_up, rw_g_up, rw_k_k, rw_k_a, rw_r_k, rw_ln_g, rw_ln_b, s5_a_re, s5_a_im, s5_log_dt, s5_b_re, s5_b_im, s5_c_re, s5_c_im, s5_d, s5_glu_w, s5_glu_b, gla_gate_up, gla_gate_b, gla_norm_g, w_branch, w_out, ffn_w1, ffn_w3, ffn_w2, moe_router_w, moe_router_b, moe_w1, moe_w3, moe_w2, final_norm_g):
    bsz, n_lat, d = x.shape
    n_ctx = ctx.shape[1]
    depth = w_in.shape[0]
    assert depth == 2 and d == D_MODEL
    assert n_lat % TOKEN_TILE == 0 and n_ctx % TOKEN_TILE == 0 and n_lat % GRID_W == 0
    t_all = n_lat + n_ctx
    n_lat_tiles = n_lat // TOKEN_TILE
    n_tiles = t_all // TOKEN_TILE
    assert n_tiles <= 2 * n_lat_tiles

    rows = (bsz + 1 + 7) // 8 * 8
    c_all = jnp.zeros((rows, d), F32).at[:bsz].set(c.astype(F32)).at[bsz].set(c_ctx.astype(F32))
    mod = _ada_mod(c_all, ada_w, ada_b)
    h_all = jnp.concatenate([x.astype(F32), ctx.astype(F32)], axis=1)

    dft_lat = _dft_tables(n_lat)
    dft_ctx = _dft_tables(n_ctx)
    seg = np.cumsum([0, 3 * MIX_W, RW_IN, MIX_W, 2 * GLA_DK + 2 * GLA_DV + 2 * GLA_GATE_RANK, 4 * d])

    out = None
    for l in range(depth):
        last = l == depth - 1
        mod_l = mod[l, :bsz].reshape(bsz, 6, d)
        mod_c = jnp.broadcast_to(mod[l, bsz].reshape(1, 6, d), (bsz, 6, d))
        modsel = jnp.stack([mod_l, mod_c], axis=1)

        wl = w_in[l].astype(BF16)
        ws = [wl[:, seg[0]:seg[1]], wl[:, seg[1]:seg[2]], wl[:, seg[2]:seg[3]],
              _pad_cols(wl[:, seg[3]:seg[4]], GLA_IN_PAD), wl[:, seg[4]:seg[5]]]
        hy_x0, hy_z, p_rw, p_s5, p_gla, gate = _in_proj(h_all, norm_mix_g[l], modsel, ws, hy_conv_w[l], hy_conv_b[l], n_lat_tiles)

        filt_args = (hy_f_w1[l], hy_f_b1[l], hy_f_w2[l], hy_f_b2[l], hy_f_w3[l], hy_f_b3[l])
        spec_lat = _hy_spectrum(_hy_filter(n_lat, *filt_args), dft_lat[0], dft_lat[1])
        y_hy_lat = _hyena(hy_z, hy_x0, 0, n_lat, hy_bias[l], dft_lat, spec_lat)
        y_hy_ctx = y_hy_lat
        if not last:
            spec_ctx = _hy_spectrum(_hy_filter(n_ctx, *filt_args), dft_ctx[0], dft_ctx[1])
            y_hy_ctx = _hyena(hy_z, hy_x0, n_lat // n_ctx, n_ctx, hy_bias[l], dft_ctx, spec_ctx)

        r, v, kap, lwf, lwb, kf, kb, bf, bb, rg, rbonus = _rw_prep(
            p_rw, n_lat_tiles, rw_mu[l], rw_w0[l], rw_w_up[l], rw_a0[l], rw_a_up[l], rw_g_up[l], rw_k_k[l], rw_k_a[l], rw_r_k[l])
        rof, rob = _rw_scan(r, v, kap, lwf, lwb, kf, kb, bf, bb, n_lat)

        y_s5 = _s5_core(p_s5, n_lat, _s5_matrices(s5_a_re[l], s5_a_im[l], s5_log_dt[l], s5_b_re[l], s5_b_im[l], s5_c_re[l], s5_c_im[l]))

        gof, gob = _gla_scan(p_gla, n_lat, gla_gate_up[l], gla_gate_b[l])
        y_gla_lat, y_gla_ctx = _gla_readout(gof, gob, p_gla, gla_norm_g[l], n_lat)

        nt_l = n_lat_tiles if last else n_tiles
        h_all = _merge(h_all, modsel, nt_l, n_lat_tiles, gate, y_hy_lat, y_hy_ctx, rof, rob, rbonus, rg, y_s5, p_s5,
                       y_gla_lat, y_gla_ctx, rw_ln_g[l], rw_ln_b[l], s5_d[l], s5_glu_w[l], s5_glu_b[l], w_branch[l], w_out[l])

        i = l // 2
        if l % 2 == 0:
            h_all = _ffn_dense(h_all, norm_ffn_g[l], modsel, n_lat_tiles, ffn_w1[i], ffn_w3[i], ffn_w2[i])
        else:
            out = _moe_ffn_and_final_norm(h_all, norm_ffn_g[l], modsel, n_lat_tiles, moe_router_w[i], moe_router_b[i],
                                          moe_w1[i], moe_w3[i], moe_w2[i], final_norm_g)
    return out.astype(x.dtype)
```

```python
import functools
import math

import numpy as np
import jax
import jax.numpy as jnp
from jax import lax
from jax.experimental import pallas as pl
from jax.experimental.pallas import tpu as pltpu

F32 = jnp.float32
BF16 = jnp.bfloat16

D_MODEL = 1024
GRID_W = 64
MIX_W = 512
RMS_EPS = 1e-6
HY_BANDS = 16
HY_DECAY_TARGET = 1e-2
HY_SHORT_PCT = 0.3
HY_LONG_PCT = 1.5
RW_HEADS = 8
RW_HEAD_DIM = 64
RW_DECAY_SCALE = 0.606531
RW_GN_EPS = 64e-5
RW_IN = 1920
S5_GROUP = 16
S5_GROUPS = 32
S5_STATE = 64
S5_CHUNK = 16
GLA_HEADS = 4
GLA_HK = 64
GLA_HV = 128
GLA_DK = 256
GLA_DV = 512
GLA_GATE_RANK = 16
GLA_GATE_NORM = 16.0
GLA_IN_PAD = 1152
N_EXPERTS = 8
TOP_K = 2
MOE_BLOCK_ROWS = 256
TOKEN_TILE = 256
SEQ_CHUNK = 64
RW_SCAN_BATCH = 4
GLA_SCAN_BATCH = 2
VMEM_LIMIT = 56 * 1024 * 1024

NT_DIMS = (((1,), (1,)), ((), ()))
TN_DIMS = (((0,), (0,)), ((), ()))


def _cp(*sem):
    return pltpu.CompilerParams(dimension_semantics=sem, vmem_limit_bytes=VMEM_LIMIT)


def _mm(a, b, dims=None):
    a = a.astype(BF16)
    b = b.astype(BF16)
    if dims is None:
        return jnp.dot(a, b, preferred_element_type=F32)
    return lax.dot_general(a, b, dims, preferred_element_type=F32)


def _split_bf16(x):
    hi = x.astype(BF16)
    lo = (x - hi.astype(F32)).astype(BF16)
    return hi, lo


def _mm_hp(a, b):
    ah, al = _split_bf16(a)
    bh, bl = _split_bf16(b)
    d = functools.partial(jnp.dot, preferred_element_type=F32)
    return d(ah, bh) + (d(ah, bl) + d(al, bh))


def _mm_exact01(m01, x):
    m = m01.astype(BF16)
    hi = x.astype(BF16)
    r1 = x - hi.astype(F32)
    mid = r1.astype(BF16)
    lo = (r1 - mid.astype(F32)).astype(BF16)
    d = functools.partial(jnp.dot, preferred_element_type=F32)
    return d(m, hi) + (d(m, mid) + d(m, lo))


def _x_exact01(x, m01):
    m = m01.astype(BF16)
    hi = x.astype(BF16)
    mid = (x - hi.astype(F32)).astype(BF16)
    d = functools.partial(jnp.dot, preferred_element_type=F32)
    return d(hi, m) + d(mid, m)


def _sigmoid(x):
    return 0.5 * jnp.tanh(0.5 * x) + 0.5


def _silu(x):
    return x * _sigmoid(x)


def _tri(c, rev, strict):
    i = lax.broadcasted_iota(jnp.int32, (c, c), 0)
    j = lax.broadcasted_iota(jnp.int32, (c, c), 1)
    if rev:
        return (j > i) if strict else (j >= i)
    return (j < i) if strict else (j <= i)


def _head_blockdiag(n, head):
    i = lax.broadcasted_iota(jnp.int32, (n, n), 0) // head
    j = lax.broadcasted_iota(jnp.int32, (n, n), 1) // head
    return (i == j).astype(F32)


def _shift_rows(x, prev_row, next_row):
    n = x.shape[0]
    row = lax.broadcasted_iota(jnp.int32, x.shape, 0)
    xm = jnp.where(row == 0, prev_row, pltpu.roll(x, 1, 0))
    xp = jnp.where(row == n - 1, next_row, pltpu.roll(x, n - 1, 0))
    return xm, xp


def _ada_kernel(c_ref, w_ref, b_ref, o_ref):
    o_ref[0] = _mm_hp(_silu(c_ref[...]), w_ref[0]) + b_ref[0]


def _ada_mod(c_all, ada_w, ada_b):
    depth, d, n = ada_w.shape
    rows = c_all.shape[0]
    return pl.pallas_call(
        _ada_kernel,
        grid=(depth, n // d),
        in_specs=[pl.BlockSpec((rows, d), lambda l, j: (0, 0)),
                  pl.BlockSpec((1, d, d), lambda l, j: (l, 0, j)),
                  pl.BlockSpec((1, 1, d), lambda l, j: (l, 0, j))],
        out_specs=pl.BlockSpec((1, rows, d), lambda l, j: (l, 0, j)),
        out_shape=jax.ShapeDtypeStruct((depth, rows, n), F32),
        compiler_params=_cp("arbitrary", "arbitrary"),
        name="ada_mod",
    )(c_all, ada_w, ada_b.reshape(depth, 1, n))


def _norm_mod(x, g, mod, sh_row):
    xn = x * lax.rsqrt(jnp.mean(x * x, axis=-1, keepdims=True) + RMS_EPS) * g
    return xn * (1.0 + mod[sh_row + 1:sh_row + 2]) + mod[sh_row:sh_row + 1]


def _seq_edges(t, n_lat_tiles, n_tiles):
    first = jnp.logical_or(t == 0, t == n_lat_tiles)
    last = jnp.logical_or(t == n_lat_tiles - 1, t == n_tiles - 1)
    return first, last


def _in_proj_kernel(x_ref, xprev_ref, xnext_ref, g_ref, mod_ref, cw_ref, cb_ref, w_hy, w_rw, w_s5, w_gla, w_glag, w_gate,
                    o_x0, o_z, o_rw, o_s5, o_gla, o_glag, o_gate, *, n_lat_tiles):
    t = pl.program_id(1)
    first, last = _seq_edges(t, n_lat_tiles, pl.num_programs(1))
    mod = mod_ref[0, 0]
    xm = _norm_mod(x_ref[0], g_ref[...], mod, 0).astype(BF16)
    for w, o in ((w_rw, o_rw), (w_s5, o_s5), (w_gla, o_gla), (w_glag, o_glag), (w_gate, o_gate)):
        o[0] = jnp.dot(xm, w[...], preferred_element_type=F32).astype(o.dtype)
    p = jnp.dot(xm, w_hy[...], preferred_element_type=F32)
    halo = jnp.concatenate([xprev_ref[0, 7:8, :], xnext_ref[0, 0:1, :], jnp.zeros((6, x_ref.shape[2]), F32)], axis=0)
    ph = jnp.dot(_norm_mod(halo, g_ref[...], mod, 0).astype(BF16), w_hy[...], preferred_element_type=F32)
    pm, pp = _shift_rows(p, jnp.where(first, 0.0, ph[0:1]), jnp.where(last, 0.0, ph[1:2]))
    u = pm * cw_ref[0:1] + p * cw_ref[1:2] + pp * cw_ref[2:3] + cb_ref[...]
    c = MIX_W
    o_x0[0] = u[:, 0:c].astype(o_x0.dtype)
    o_z[0] = (u[:, 2 * c:3 * c] * u[:, c:2 * c]).astype(o_z.dtype)


def _mod_index(n_lat_tiles):
    return lambda b, t: (b, t // n_lat_tiles, 0, 0)


def _halo_specs(width, n_rows):
    sub = TOKEN_TILE // 8
    return [pl.BlockSpec((1, 8, width), lambda b, t: (b, jnp.maximum(t * sub - 1, 0), 0)),
            pl.BlockSpec((1, 8, width), lambda b, t: (b, jnp.minimum((t + 1) * sub, n_rows // 8 - 1), 0))]


def _in_proj(h_all, g, modsel, ws, conv_w, conv_b, n_lat_tiles):
    b, t, d = h_all.shape
    nt = t // TOKEN_TILE
    c = MIX_W
    wspec = [pl.BlockSpec(w.shape, lambda b, t: (0, 0), pipeline_mode=pl.Buffered(1)) for w in ws]
    widths = [c, c] + [w.shape[1] for w in ws[1:]]
    dtypes = [BF16, BF16, F32, F32, F32, F32, BF16]
    return pl.pallas_call(
        functools.partial(_in_proj_kernel, n_lat_tiles=n_lat_tiles),
        grid=(b, nt),
        in_specs=[pl.BlockSpec((1, TOKEN_TILE, d), lambda b, t: (b, t, 0))] + _halo_specs(d, t)
                 + [pl.BlockSpec((1, d), lambda b, t: (0, 0)),
                    pl.BlockSpec((1, 1, 6, d), _mod_index(n_lat_tiles)),
                    pl.BlockSpec((3, 3 * c), lambda b, t: (0, 0)),
                    pl.BlockSpec((1, 3 * c), lambda b, t: (0, 0))] + wspec,
        out_specs=[pl.BlockSpec((1, TOKEN_TILE, w), lambda b, t: (b, t, 0)) for w in widths],
        out_shape=[jax.ShapeDtypeStruct((b, t, w), dt) for w, dt in zip(widths, dtypes)],
        compiler_params=_cp("arbitrary", "arbitrary"),
        name="in_proj",
    )(h_all, h_all, h_all, g.reshape(1, d), modsel, conv_w, conv_b.reshape(1, 3 * c), *ws)


def _grid_cols(n_lat_rows, rows):
    grid_rows = n_lat_rows // GRID_W
    assert grid_rows % 8 == 0 and rows % grid_rows == 0 and GRID_W % (rows // grid_rows) == 0
    return rows // grid_rows


GRID_COL_BLOCK = 8


def _col_major_rows(view_ref, e, j0, cols, lo, hi):
    return jnp.concatenate([view_ref[e, :, pl.ds(j0 + j, 1), lo:hi][:, 0, :] for j in range(cols)], axis=0)


def _hy_filter_kernel(z_ref, w1, b1, w2, b2, w3, b3, dec_ref, o_ref):
    h = jnp.sin(_mm_hp(z_ref[...], w1[...]) + b1[...])
    h = jnp.sin(_mm_hp(h, w2[...]) + b2[...])
    o_ref[...] = (_mm_hp(h, w3[...]) + b3[...]) * dec_ref[...]


def _hy_features(length):
    pos = jnp.arange(length, dtype=F32)
    t = pos / max(length - 1, 1)
    freqs = jnp.linspace(1e-4, HY_BANDS - 1, HY_BANDS, dtype=F32)
    ang = (2.0 * math.pi / length) * pos[:, None] * freqs[None, :]
    z = jnp.concatenate([t[:, None], jnp.cos(ang), -jnp.sin(ang)], axis=-1)
    rates = jnp.abs(jnp.linspace(math.log(HY_DECAY_TARGET) / HY_LONG_PCT,
                                 math.log(HY_DECAY_TARGET) / HY_SHORT_PCT, MIX_W, dtype=F32))
    rates = jnp.concatenate([rates, rates])
    decay = jnp.exp(-t[:, None] * rates[None, :])
    return jnp.pad(z, ((0, 0), (0, 128 - z.shape[1]))), decay


def _hy_filter(length, fw1, fb1, fw2, fb2, fw3, fb3):
    z, decay = _hy_features(length)
    w1 = jnp.pad(fw1, ((0, 128 - fw1.shape[0]), (0, 128 - fw1.shape[1])))
    b1 = jnp.pad(fb1, (0, 128 - fb1.shape[0])).reshape(1, 128)
    w2 = jnp.pad(fw2, ((0, 128 - fw2.shape[0]), (0, 128 - fw2.shape[1])))
    b2 = jnp.pad(fb2, (0, 128 - fb2.shape[0])).reshape(1, 128)
    w3 = jnp.pad(fw3, ((0, 128 - fw3.shape[0]), (0, 0)))
    n = fw3.shape[1]
    rows = min(length, 512)
    full = lambda s: pl.BlockSpec(s, lambda i: (0, 0))
    return pl.pallas_call(
        _hy_filter_kernel,
        grid=(length // rows,),
        in_specs=[pl.BlockSpec((rows, 128), lambda i: (i, 0)), full((128, 128)), full((1, 128)), full((128, 128)),
                  full((1, 128)), full((128, n)), full((1, n)), pl.BlockSpec((rows, n), lambda i: (i, 0))],
        out_specs=pl.BlockSpec((rows, n), lambda i: (i, 0)),
        out_shape=jax.ShapeDtypeStruct((length, n), F32),
        compiler_params=_cp("arbitrary"),
        name="hy_filter",
    )(z, w1, b1, w2, b2, w3, fb3.reshape(1, n), decay)


def _dft_tables(length):
    n = 2 * length
    k = jnp.arange(length, dtype=jnp.int32)
    m = (k[:, None] * k[None, :]) % n
    ang = m.astype(F32) * (2.0 * math.pi / n)
    wc = jnp.cos(ang).astype(BF16)
    sin = jnp.sin(ang)
    nyq = jnp.where(k % 2 == 0, 1.0, -1.0).astype(F32)
    ws = jnp.where(k[:, None] == 0, nyq[None, :], sin).astype(BF16)
    wst = jnp.where(k[None, :] == 0, nyq[:, None], sin).astype(BF16)
    return wc, ws, wc, wst


def _dft_fwd_kernel(wc_ref, ws_ref, xh_ref, xl_ref, oc_ref, os_ref):
    d = functools.partial(jnp.dot, preferred_element_type=F32)
    oc_ref[...] = d(wc_ref[...], xh_ref[...]) + d(wc_ref[...], xl_ref[...])
    os_ref[...] = d(ws_ref[...], xh_ref[...]) + d(ws_ref[...], xl_ref[...])


def _hy_spectrum(filt, wc, ws):
    length, c2 = filt.shape
    c = c2 // 2
    n = 2 * length
    hb = filt[:, c:].at[0].set(0.0)
    x = jnp.concatenate([filt[:, :c], hb], axis=1)
    xh, xl = _split_bf16(x)
    fr = min(length, 512)
    fc, fs = pl.pallas_call(
        _dft_fwd_kernel,
        grid=(length // fr,),
        in_specs=[pl.BlockSpec((fr, length), lambda i: (i, 0)), pl.BlockSpec((fr, length), lambda i: (i, 0)),
                  pl.BlockSpec((length, c2), lambda i: (0, 0)), pl.BlockSpec((length, c2), lambda i: (0, 0))],
        out_specs=[pl.BlockSpec((fr, c2), lambda i: (i, 0)), pl.BlockSpec((fr, c2), lambda i: (i, 0))],
        out_shape=[jax.ShapeDtypeStruct((length, c2), F32)] * 2,
        compiler_params=_cp("arbitrary"),
        name="hy_spectrum",
    )(wc, ws, xh, xl)
    k_re = fc[:, :c] + fc[:, c:]
    k_im = fs[:, c:] - fs[:, :c]
    k_nyq = fs[0, :c] + fs[0, c:]
    ka = (k_re * (2.0 / n)).at[0].set(k_re[0] / n)
    kb = (k_im * (2.0 / n)).at[0].set(0.0)
    ka2 = ka.at[0].set(k_nyq / n)
    return ka, ka2, kb


HYENA_BATCH = 2


def _hyena_kernel(z_ref, x0_ref, bias_ref, wc_ref, ws_ref, wct_ref, wst_ref, ka_ref, ka2_ref, kb_ref, o_ref, acc_scr):
    f = pl.program_id(1)
    nb = z_ref.shape[0]
    c = z_ref.shape[2]
    z = jnp.concatenate([z_ref[e] for e in range(nb)], axis=1)
    tile = lambda ref: jnp.concatenate([ref[...]] * nb, axis=1)

    @pl.when(f == 0)
    def _():
        acc_scr[...] = z.astype(F32) * tile(bias_ref)

    zc = jnp.dot(wc_ref[...], z, preferred_element_type=F32)
    zs = jnp.dot(ws_ref[...], z, preferred_element_type=F32)
    ka, ka2, kb = tile(ka_ref), tile(ka2_ref), tile(kb_ref)
    a = zc * ka + zs * kb
    bv = zs * ka2 - zc * kb
    acc_scr[...] += (jnp.dot(wct_ref[...], a.astype(BF16), preferred_element_type=F32)
                     + jnp.dot(wst_ref[...], bv.astype(BF16), preferred_element_type=F32))

    @pl.when(f == pl.num_programs(1) - 1)
    def _():
        for e in range(nb):
            o_ref[e] = (x0_ref[e].astype(F32) * acc_scr[:, e * c:(e + 1) * c]).astype(o_ref.dtype)


def _hyena(z, x0, row_block, length, bias, tables, spec):
    b, t, c = z.shape
    wc, ws, wct, wst = tables
    ka, ka2, kb = spec
    fr = min(length, 256)
    nf = length // fr
    nb = HYENA_BATCH
    seq = pl.BlockSpec((nb, length, c), lambda b, f: (b, row_block, 0))
    in_specs = [seq, seq,
                pl.BlockSpec((1, c), lambda b, f: (0, 0)),
                pl.BlockSpec((fr, length), lambda b, f: (f, 0)),
                pl.BlockSpec((fr, length), lambda b, f: (f, 0)),
                pl.BlockSpec((length, fr), lambda b, f: (0, f)),
                pl.BlockSpec((length, fr), lambda b, f: (0, f)),
                pl.BlockSpec((fr, c), lambda b, f: (f, 0)),
                pl.BlockSpec((fr, c), lambda b, f: (f, 0)),
                pl.BlockSpec((fr, c), lambda b, f: (f, 0))]
    return pl.pallas_call(
        _hyena_kernel,
        grid=(b // nb, nf),
        in_specs=in_specs,
        out_specs=pl.BlockSpec((nb, length, c), lambda b, f: (b, 0, 0)),
        out_shape=jax.ShapeDtypeStruct((b, length, c), BF16),
        scratch_shapes=[pltpu.VMEM((length, nb * c), F32)],
        compiler_params=_cp("arbitrary", "arbitrary"),
        name="hyena_%d" % length,
    )(z, x0, bias.reshape(1, c), wc, ws, wct, wst, ka, ka2, kb)


def _rw_prep_kernel(p_ref, pprev_ref, pnext_ref, mu_ref, w0_ref, wup_ref, a0_ref, aup_ref, gup_ref, kk_ref, ka_ref, rk_ref,
                    r_o, v_o, kap_o, lwf_o, lwb_o, kf_o, kb_o, bf_o, bb_o, g_o, bonus_o, *, n_lat_tiles):
    t = pl.program_id(1)
    nt = pl.num_programs(1)
    p = p_ref[0]
    first = jnp.logical_or(t == 0, t == n_lat_tiles)
    last = jnp.logical_or(t == n_lat_tiles - 1, t == nt - 1)
    prev_row = jnp.where(first, 0.0, pprev_ref[0, 7:8, :])
    next_row = jnp.where(last, 0.0, pnext_ref[0, 0:1, :])
    pm, pp = _shift_rows(p, prev_row, next_row)
    p = p + mu_ref[0:1] * (pm - p) + mu_ref[1:2] * (pp - p)
    c = MIX_W
    r, k, v = p[:, 0:c], p[:, c:2 * c], p[:, 2 * c:3 * c]
    wd, ad, gd = p[:, 3 * c:3 * c + 128], p[:, 3 * c + 128:3 * c + 256], p[:, 3 * c + 256:3 * c + 384]
    bd = _head_blockdiag(c, RW_HEAD_DIM)
    g = _mm(_sigmoid(gd), gup_ref[...])
    kk = k * kk_ref[...]
    kap = kk * lax.rsqrt(jnp.maximum(_x_exact01(kk * kk, bd), 1e-24))
    logw = -RW_DECAY_SCALE * _sigmoid(w0_ref[...] + _mm(jnp.tanh(wd), wup_ref[...]))
    a = _sigmoid(a0_ref[...] + _mm(ad, aup_ref[...]))
    k_sum = 0.0
    outs_k, outs_b = (kf_o, kb_o), (bf_o, bb_o)
    for d in range(2):
        a_d = a[:, d * c:(d + 1) * c]
        k_d = k * (1.0 + (a_d - 1.0) * ka_ref[...])
        outs_k[d][0] = k_d.astype(BF16)
        outs_b[d][0] = (a_d * kap).astype(BF16)
        k_sum = k_sum + k_d
    r_o[0] = r.astype(BF16)
    v_o[0] = v.astype(BF16)
    kap_o[0] = kap.astype(BF16)
    lwf_o[0] = logw[:, 0:c]
    lwb_o[0] = logw[:, c:2 * c]
    g_o[0] = g
    bonus_o[0] = _x_exact01(r * k_sum * rk_ref[...], bd) * v


def _blockdiag2(a, b):
    z1 = jnp.zeros((a.shape[0], b.shape[1]), a.dtype)
    z2 = jnp.zeros((b.shape[0], a.shape[1]), a.dtype)
    return jnp.concatenate([jnp.concatenate([a, z1], 1), jnp.concatenate([z2, b], 1)], 0)


def _rw_prep(p_rw, n_lat_tiles, mu, w0, w_up, a0, a_up, g_up, k_k, k_a, r_k):
    b, t, n = p_rw.shape
    nt = t // TOKEN_TILE
    c = MIX_W
    sub = TOKEN_TILE // 8
    n8 = t // 8
    consts = [mu, w0.reshape(1, 2 * c), _blockdiag2(w_up[0], w_up[1]), a0.reshape(1, 2 * c), _blockdiag2(a_up[0], a_up[1]),
              g_up, k_k.reshape(1, c), k_a.reshape(1, c), r_k.reshape(1, c)]
    cspecs = [pl.BlockSpec(x.shape, lambda b, t: (0, 0)) for x in consts]
    tile = lambda w: pl.BlockSpec((1, TOKEN_TILE, w), lambda b, t: (b, t, 0))
    return pl.pallas_call(
        functools.partial(_rw_prep_kernel, n_lat_tiles=n_lat_tiles),
        grid=(b, nt),
        in_specs=[tile(n),
                  pl.BlockSpec((1, 8, n), lambda b, t: (b, jnp.maximum(t * sub - 1, 0), 0)),
                  pl.BlockSpec((1, 8, n), lambda b, t: (b, jnp.minimum((t + 1) * sub, n8 - 1), 0))] + cspecs,
        out_specs=[tile(c)] * 11,
        out_shape=[jax.ShapeDtypeStruct((b, t, c), dt) for dt in (BF16, BF16, BF16, F32, F32, BF16, BF16, BF16, BF16, F32, F32)],
        compiler_params=_cp("arbitrary", "arbitrary"),
        name="rw_prep",
    )(p_rw, p_rw, p_rw, *consts)


def _rw_scan_kernel(rf, vf, kapf, lwf, kf, bbf, rb, vb, kapb, lwb, kb, bbb, of_ref, ob_ref, s_scr):
    @pl.when(pl.program_id(1) == 0)
    def _():
        s_scr[...] = jnp.zeros_like(s_scr)

    c = rf.shape[1]
    pw_ = 2 * RW_HEAD_DIM
    n_pairs = MIX_W // pw_
    lane_a = lax.broadcasted_iota(jnp.int32, (c, pw_), 1) < RW_HEAD_DIM
    lane_a2 = lax.broadcasted_iota(jnp.int32, (2 * c, pw_), 1) < RW_HEAD_DIM
    blk = ((lax.broadcasted_iota(jnp.int32, (pw_, pw_), 0) < RW_HEAD_DIM)
           == (lax.broadcasted_iota(jnp.int32, (pw_, pw_), 1) < RW_HEAD_DIM))

    def sel(x):
        return jnp.where(lane_a, x[:c], x[c:])

    def bf(x):
        return x.astype(BF16)

    nb = rf.shape[0]
    chains = []
    for e in range(nb):
        for d, (r, v, kap, lw, k, b) in enumerate(((rf, vf, kapf, lwf, kf, bbf), (rb, vb, kapb, lwb, kb, bbb))):
            rev = d == 1
            incl = _tri(c, rev, False)
            strict = _tri(c, rev, True)
            logw = lw[e]
            cum = _mm_exact01(incl.astype(F32), logw)
            tot = cum[0:1] if rev else cum[c - 1:c]
            ginv = jnp.exp(-cum)
            gt = jnp.exp(tot - cum)
            rt = r[e] * jnp.exp(cum)
            kt = kap[e] * jnp.exp(cum - logw)
            kd, bd = k[e] * ginv, b[e] * ginv
            kg, bg = k[e] * gt, b[e] * gt
            dec = jnp.exp(tot)
            vv = v[e]
            for p in range(n_pairs):
                sl = slice(p * pw_, (p + 1) * pw_)
                lhs2 = jnp.concatenate([rt[:, sl], kt[:, sl]], 0)
                chains.append(dict(
                    e=e, d=d, p=p, idx=(e * 2 + d) * n_pairs + p, incl=incl, strict=strict,
                    lhs2=bf(lhs2),
                    lhs4=bf(jnp.concatenate([jnp.where(lane_a2, lhs2, 0.0), jnp.where(lane_a2, 0.0, lhs2)], 0)),
                    kd=bf(kd[:, sl]), bd=bf(bd[:, sl]), v=bf(vv[:, sl]),
                    kgbg=bf(jnp.concatenate([kg[:, sl], bg[:, sl]], 0)), dec=dec[:, sl]))

    dot = functools.partial(lax.dot_general, preferred_element_type=F32)
    nn = (((1,), (0,)), ((), ()))
    for ch in chains:
        ch["s"] = s_scr[ch["idx"]]
    for ch in chains:
        ch["a1k"] = dot(ch["lhs4"], ch["kd"], NT_DIMS)
        ch["a1b"] = dot(ch["lhs4"], ch["bd"], NT_DIMS)
        ch["a2"] = dot(ch["lhs2"], bf(ch["s"]), NT_DIMS)
    for ch in chains:
        a1k, a1b, st = ch["a1k"], ch["a1b"], ch["strict"]
        lk = jnp.where(jnp.concatenate([st, st], 0), jnp.concatenate([a1k[c:2 * c], a1k[3 * c:]], 0), 0.0)
        ch["lb_a"] = bf(jnp.where(st, a1b[c:2 * c], 0.0))
        ch["lb_b"] = bf(jnp.where(st, a1b[3 * c:], 0.0))
        ch["u"] = ch["a2"][c:] + sel(dot(bf(lk), ch["v"], nn))
    for ch in chains:
        ch["u"] = ch["u"] - sel(dot(jnp.concatenate([ch["lb_a"], ch["lb_b"]], 0), bf(ch["u"]), nn))
        ch["p_a"] = bf(dot(ch["lb_a"], ch["lb_a"], nn))
        ch["p_b"] = bf(dot(ch["lb_b"], ch["lb_b"], nn))
    n = int(math.log2(c)) - 1
    for i in range(n):
        for ch in chains:
            ch["u"] = ch["u"] + sel(dot(jnp.concatenate([ch["p_a"], ch["p_b"]], 0), bf(ch["u"]), nn))
            if i < n - 1:
                ch["p_a"] = bf(dot(ch["p_a"], ch["p_a"], nn))
                ch["p_b"] = bf(dot(ch["p_b"], ch["p_b"], nn))
    outs = [[[None] * n_pairs, [None] * n_pairs] for _ in range(nb)]
    for ch in chains:
        a1k, a1b, inc = ch["a1k"], ch["a1b"], ch["incl"]
        inc2 = jnp.concatenate([inc, inc], 0)
        rk = jnp.where(inc2, jnp.concatenate([a1k[:c], a1k[2 * c:3 * c]], 0), 0.0)
        rb_ = jnp.where(inc2, jnp.concatenate([a1b[:c], a1b[2 * c:3 * c]], 0), 0.0)
        ub = bf(ch["u"])
        o = ch["a2"][:c] + sel(dot(bf(rk), ch["v"], nn)) - sel(dot(bf(rb_), ub, nn))
        outs[ch["e"]][ch["d"]][ch["p"]] = o
        upd = dot(jnp.concatenate([ch["v"], -ub], 0), ch["kgbg"], TN_DIMS)
        s_scr[ch["idx"]] = ch["s"] * ch["dec"] + jnp.where(blk, upd, 0.0)
    for e in range(nb):
        of_ref[e] = jnp.concatenate(outs[e][0], axis=1)
        ob_ref[e] = jnp.concatenate(outs[e][1], axis=1)


def _chunk_orders(n_lat, n_ctx):
    n = n_lat + n_ctx
    fwd = lambda b, s: (b, (s + n_lat) % n, 0)
    rev = lambda b, s: (b, n - 1 - s, 0)
    return fwd, rev


def _rw_scan(r, v, kap, lwf, lwb, kf, kb, bf, bb, n_lat_rows):
    b, t, c = r.shape
    ch = SEQ_CHUNK
    n = t // ch
    fwd, rev = _chunk_orders(n_lat_rows // ch, (t - n_lat_rows) // ch)
    nb = RW_SCAN_BATCH
    blk = lambda im: pl.BlockSpec((nb, ch, c), im)
    return pl.pallas_call(
        _rw_scan_kernel,
        grid=(b // nb, n),
        in_specs=[blk(fwd)] * 6 + [blk(rev)] * 6,
        out_specs=[blk(fwd), blk(rev)],
        out_shape=[jax.ShapeDtypeStruct((b, t, c), F32)] * 2,
        scratch_shapes=[pltpu.VMEM((nb * RW_HEADS, 2 * RW_HEAD_DIM, 2 * RW_HEAD_DIM), F32)],
        compiler_params=_cp("arbitrary", "arbitrary"),
        name="rw_scan",
    )(r, v, kap, lwf, kf, bf, r, v, kap, lwb, kb, bb)


def _rw_readout(of, ob, bonus, g, ln_g, ln_b):
    o = of + ob
    bd = _head_blockdiag(MIX_W, RW_HEAD_DIM)
    mean = _x_exact01(o, bd) * (1.0 / RW_HEAD_DIM)
    oc = o - mean
    var = _x_exact01(oc * oc, bd) * (1.0 / RW_HEAD_DIM)
    on = oc * lax.rsqrt(var + RW_GN_EPS) * ln_g + ln_b
    return (on + bonus) * g


def _s5_matrices(a_re, a_im, log_dt, b_re, b_im, c_re, c_im):
    tc, g, n, p = S5_CHUNK, S5_GROUPS, S5_STATE, S5_GROUP
    hp = lax.Precision.HIGHEST
    dt = jnp.exp(log_dt)[:, :, None]
    lam_re, lam_im = a_re * dt, a_im * dt
    mag1 = jnp.exp(lam_re)
    ab_re, ab_im = mag1 * jnp.cos(lam_im), mag1 * jnp.sin(lam_im)
    nr, ni = ab_re - 1.0, ab_im
    den = a_re * a_re + a_im * a_im
    f_re, f_im = (nr * a_re + ni * a_im) / den, (ni * a_re - nr * a_im) / den
    tau = jnp.arange(tc + 1, dtype=F32)[:, None, None, None]
    pw_mag = jnp.exp(lam_re[None] * tau)
    pw_re, pw_im = pw_mag * jnp.cos(lam_im[None] * tau), pw_mag * jnp.sin(lam_im[None] * tau)
    fb_re = f_re[..., None] * b_re[None] - f_im[..., None] * b_im[None]
    fb_im = f_re[..., None] * b_im[None] + f_im[..., None] * b_re[None]
    e_re = pw_re[..., None] * fb_re[None] - pw_im[..., None] * fb_im[None]
    e_im = pw_re[..., None] * fb_im[None] + pw_im[..., None] * fb_re[None]
    m = (jnp.einsum('gpn,tdgnq->tdgpq', c_re, e_re[:tc], precision=hp)
         - jnp.einsum('gpn,tdgnq->tdgpq', c_im, e_im[:tc], precision=hp))
    i = jnp.arange(tc)
    lag = i[None, :] - i[:, None]
    mf = jnp.where((lag >= 0)[:, :, None, None, None], m[jnp.clip(lag, 0, tc - 1), 0], 0.0)
    mb = jnp.where((lag <= 0)[:, :, None, None, None], m[jnp.clip(-lag, 0, tc - 1), 1], 0.0)
    w = jnp.transpose(mf + mb, (2, 0, 4, 1, 3)).reshape(g, tc * p, tc * p)
    idx_f = tc - 1 - i
    pf_re, pf_im = e_re[idx_f, 0], e_im[idx_f, 0]
    pb_re, pb_im = e_re[i, 1], e_im[i, 1]
    pm = jnp.concatenate([pf_re, pb_re, pf_im, pb_im], axis=2)
    pm = jnp.transpose(pm, (1, 0, 3, 2)).reshape(g, tc * p, 4 * n)
    cf_re = c_re[None, :, :, :] * pw_re[i + 1, 0][:, :, None, :] - c_im[None] * pw_im[i + 1, 0][:, :, None, :]
    cf_im = c_re[None] * pw_im[i + 1, 0][:, :, None, :] + c_im[None] * pw_re[i + 1, 0][:, :, None, :]
    cb_re = c_re[None] * pw_re[tc - i, 1][:, :, None, :] - c_im[None] * pw_im[tc - i, 1][:, :, None, :]
    cb_im = c_re[None] * pw_im[tc - i, 1][:, :, None, :] + c_im[None] * pw_re[tc - i, 1][:, :, None, :]
    q = jnp.concatenate([cf_re, cb_re, -cf_im, -cb_im], axis=3)
    q = jnp.transpose(q, (1, 3, 0, 2)).reshape(g, 4 * n, tc * p)
    dec_re = jnp.concatenate([pw_re[tc, 0], pw_re[tc, 1]], axis=1)
    dec_im = jnp.concatenate([pw_im[tc, 0], pw_im[tc, 1]], axis=1)
    dec = jnp.stack([dec_re, dec_im], axis=1)
    return w.astype(BF16), pm.astype(BF16), q.astype(BF16), dec


S5_PACK_BATCH = 2
S5_CORE_GROUPS = 2
S5_LANE_GROUPS = 128 // S5_GROUP


def _s5_pack_kernel(x_ref, o_ref):
    rows = o_ref.shape[0]
    grp = lax.broadcasted_iota(jnp.int32, (rows, 128), 1) // S5_GROUP
    tiles = [jnp.zeros((rows, 128), F32) for _ in range(2 * S5_LANE_GROUPS)]
    for j in range(S5_CHUNK):
        xj = x_ref[pl.ds(j, rows, stride=S5_CHUNK), :]
        jl, jh = j % S5_LANE_GROUPS, j // S5_LANE_GROUPS
        for g8 in range(S5_LANE_GROUPS):
            sh = ((jl - g8) * S5_GROUP) % 128
            moved = pltpu.roll(xj, sh, 1) if sh else xj
            tiles[2 * g8 + jh] = jnp.where(grp == jl, moved, tiles[2 * g8 + jh])
    o_ref[...] = jnp.concatenate(tiles, axis=1).astype(o_ref.dtype)


def _s5_unpack_kernel(y_ref, o_ref):
    rows = y_ref.shape[0]
    grp = lax.broadcasted_iota(jnp.int32, (rows, 128), 1) // S5_GROUP
    for i in range(S5_CHUNK):
        il, ih = i % S5_LANE_GROUPS, i // S5_LANE_GROUPS
        out = jnp.zeros((rows, 128), F32)
        for g8 in range(S5_LANE_GROUPS):
            src = y_ref[:, (2 * g8 + ih) * 128:(2 * g8 + ih + 1) * 128]
            sh = ((g8 - il) * S5_GROUP) % 128
            moved = pltpu.roll(src, sh, 1) if sh else src
            out = jnp.where(grp == g8, moved, out)
        o_ref[pl.ds(i, rows, stride=S5_CHUNK), :] = out


def _s5_kernel(u_ref, w_ref, p_ref, q_ref, dec_ref, y_ref, inc_re, inc_im, sf_re, sf_im, sb_re, sb_im, *, n_lat, n_ctx, bsz):
    gs = w_ref.shape[0]
    k = w_ref.shape[1]
    n2 = 2 * S5_STATE
    us = [u_ref[:, g * k:(g + 1) * k] for g in range(gs)]
    for g in range(gs):
        inc = jnp.dot(us[g], p_ref[g], preferred_element_type=F32)
        inc_re[g] = inc[:, :n2]
        inc_im[g] = inc[:, n2:]
    n = n_lat + n_ctx
    is_f = lax.broadcasted_iota(jnp.int32, (bsz, n2), 1) < S5_STATE

    def body(s, carry):
        rf = pl.ds((s + n_lat) % n, bsz, stride=n)
        rb = pl.ds(n - 1 - s, bsz, stride=n)
        new = []
        for g in range(gs):
            s_re, s_im = carry[g]
            d_re, d_im = dec_ref[g, 0:1, :], dec_ref[g, 1:2, :]
            sf_re[g, rf, :] = s_re
            sf_im[g, rf, :] = s_im
            sb_re[g, rb, :] = s_re
            sb_im[g, rb, :] = s_im
            i_re = jnp.where(is_f, inc_re[g, rf, :], inc_re[g, rb, :])
            i_im = jnp.where(is_f, inc_im[g, rf, :], inc_im[g, rb, :])
            new.append((d_re * s_re - d_im * s_im + i_re, d_re * s_im + d_im * s_re + i_im))
        return tuple(new)

    zero = jnp.zeros((bsz, n2), F32)
    lax.fori_loop(0, n, body, tuple((zero, zero) for _ in range(gs)))
    is_f_all = lax.broadcasted_iota(jnp.int32, sf_re.shape[1:], 1) < S5_STATE
    for g in range(gs):
        s_all = jnp.concatenate([jnp.where(is_f_all, sf_re[g], sb_re[g]),
                                 jnp.where(is_f_all, sf_im[g], sb_im[g])], axis=1).astype(BF16)
        y_ref[:, g * k:(g + 1) * k] = (jnp.dot(us[g], w_ref[g], preferred_element_type=F32)
                                       + jnp.dot(s_all, q_ref[g], preferred_element_type=F32))


def _s5_core(p_s5, n_lat_rows, mats):
    b, t, c = p_s5.shape
    tc, g = S5_CHUNK, S5_GROUPS
    nch = t // tc
    w, pm, q, dec = mats
    k = tc * S5_GROUP
    nbb = S5_PACK_BATCH
    gs = S5_CORE_GROUPS
    n_tiles = c // 128
    tok_blk = pl.BlockSpec((nbb * t, 128), lambda i, o: (i, o))
    chk_blk = pl.BlockSpec((nbb * nch, S5_LANE_GROUPS * k), lambda i, o: (i, o))
    u = pl.pallas_call(
        _s5_pack_kernel,
        grid=(b // nbb, n_tiles),
        in_specs=[tok_blk],
        out_specs=chk_blk,
        out_shape=jax.ShapeDtypeStruct((b * nch, g * k), BF16),
        compiler_params=_cp("arbitrary", "arbitrary"),
        name="s5_pack",
    )(p_s5.reshape(b * t, c))
    rows = nch * b
    y = pl.pallas_call(
        functools.partial(_s5_kernel, n_lat=n_lat_rows // tc, n_ctx=(t - n_lat_rows) // tc, bsz=b),
        grid=(g // gs,),
        in_specs=[pl.BlockSpec((rows, gs * k), lambda i: (0, i)),
                  pl.BlockSpec((gs, k, k), lambda i: (i, 0, 0)),
                  pl.BlockSpec((gs, k, 4 * S5_STATE), lambda i: (i, 0, 0)),
                  pl.BlockSpec((gs, 4 * S5_STATE, k), lambda i: (i, 0, 0)),
                  pl.BlockSpec((gs, 2, 2 * S5_STATE), lambda i: (i, 0, 0))],
        out_specs=pl.BlockSpec((rows, gs * k), lambda i: (0, i)),
        out_shape=jax.ShapeDtypeStruct((rows, g * k), F32),
        scratch_shapes=[pltpu.VMEM((gs, rows, 2 * S5_STATE), F32)] * 6,
        compiler_params=_cp("arbitrary"),
        name="s5_core",
    )(u, w, pm, q, dec)
    out = pl.pallas_call(
        _s5_unpack_kernel,
        grid=(b // nbb, n_tiles),
        in_specs=[chk_blk],
        out_specs=tok_blk,
        out_shape=jax.ShapeDtypeStruct((b * t, c), F32),
        compiler_params=_cp("arbitrary", "arbitrary"),
        name="s5_unpack",
    )(y)
    return out.reshape(b, t, c)


def _s5_output(y, u, d_skip, glu_w, glu_b):
    z = y + d_skip * u
    z = 0.5 * z * (1.0 + jnp.tanh(math.sqrt(2.0 / math.pi) * (z + 0.044715 * (z * z * z))))
    return z * _sigmoid(_mm(z, glu_w) + glu_b)


def _gla_scan_kernel(pfv_ref, pfc_ref, pbv_ref, pbc_ref, gup_ref, gb_ref, of_ref, ob_ref, s_scr, *, n_lat, n_ctx, cols, c):
    step = pl.program_id(1)

    @pl.when(step == 0)
    def _():
        s_scr[...] = jnp.zeros_like(s_scr)

    subs = pfc_ref.shape[1] // c
    n = n_lat + n_ctx
    block_ids = ((step + n_lat) % n, n - 1 - step)
    pw_ = 2 * GLA_HK
    n_pairs = GLA_DK // pw_
    lane_a = lax.broadcasted_iota(jnp.int32, (c, pw_), 1) < GLA_HK
    lane_av = lax.broadcasted_iota(jnp.int32, (GLA_HV, pw_), 1) < GLA_HK
    dot = functools.partial(lax.dot_general, preferred_element_type=F32)
    nn = (((1,), (0,)), ((), ()))
    nb = pfc_ref.shape[0]
    mask2 = lambda a: jnp.concatenate([jnp.where(lane_a, a, 0.0), jnp.where(lane_a, 0.0, a)], 0).astype(BF16)
    for sub in range(subs):
        chains = []
        for e in range(nb):
            for d, (v_ref, c_ref) in enumerate(((pfv_ref, pfc_ref), (pbv_ref, pbc_ref))):
                rev = d == 1
                cs = subs - 1 - sub if rev else sub
                incl = _tri(c, rev, False)
                lat_rows = jnp.concatenate([v_ref[e, :, cs * cols + j, :] for j in range(cols)], axis=0)
                pe = jnp.where(block_ids[d] < n_lat, lat_rows, c_ref[e, cs * c:(cs + 1) * c, :])
                gd = pe[:, 2 * GLA_DK + GLA_DV:]
                x = _mm(gd, gup_ref[...])[:, d * GLA_DK:(d + 1) * GLA_DK] + gb_ref[0:1, d * GLA_DK:(d + 1) * GLA_DK]
                logg = (jnp.minimum(x, 0.0) - jnp.log(1.0 + jnp.exp(-jnp.abs(x)))) * (1.0 / GLA_GATE_NORM)
                cum = _mm_exact01(incl.astype(F32), logg)
                mid = c // 2
                ref = cum[mid:mid + 1] if rev else cum[mid - 1:mid]
                tot = cum[0:1] if rev else cum[c - 1:c]
                q = pe[:, 0:GLA_DK] * (GLA_HK ** -0.5)
                k = pe[:, GLA_DK:2 * GLA_DK]
                qs = q * jnp.exp(cum - ref)
                ks = k * jnp.exp(ref - cum)
                qc = q * jnp.exp(cum)
                kg = k * jnp.exp(tot - cum)
                dec = jnp.exp(tot)
                for p in range(n_pairs):
                    sl = slice(p * pw_, (p + 1) * pw_)
                    chains.append(dict(
                        e=e, d=d, p=p, idx=(e * 2 + d) * n_pairs + p, incl=incl, qs4=mask2(qs[:, sl]), qc4=mask2(qc[:, sl]),
                        ks=ks[:, sl].astype(BF16), kg=kg[:, sl].astype(BF16), dec=dec[:, sl],
                        va=pe[:, 2 * GLA_DK + 2 * p * GLA_HV:2 * GLA_DK + (2 * p + 1) * GLA_HV].astype(BF16),
                        vb=pe[:, 2 * GLA_DK + (2 * p + 1) * GLA_HV:2 * GLA_DK + (2 * p + 2) * GLA_HV].astype(BF16)))
        for ch in chains:
            ch["s"] = s_scr[ch["idx"]]
        for ch in chains:
            ch["sc"] = dot(ch["qs4"], ch["ks"], NT_DIMS)
            ch["st"] = dot(ch["qc4"], ch["s"].astype(BF16), NT_DIMS)
        outs = [[[None] * GLA_HEADS, [None] * GLA_HEADS] for _ in range(nb)]
        for ch in chains:
            e, d, p = ch["e"], ch["d"], ch["p"]
            sa = jnp.where(ch["incl"], ch["sc"][:c], 0.0).astype(BF16)
            sb = jnp.where(ch["incl"], ch["sc"][c:], 0.0).astype(BF16)
            outs[e][d][2 * p] = ch["st"][:c] + dot(sa, ch["va"], nn)
            outs[e][d][2 * p + 1] = ch["st"][c:] + dot(sb, ch["vb"], nn)
            upd = jnp.where(lane_av, dot(ch["va"], ch["kg"], TN_DIMS), dot(ch["vb"], ch["kg"], TN_DIMS))
            s_scr[ch["idx"]] = ch["s"] * ch["dec"] + upd
        for e in range(nb):
            of_ref[e, sub * c:(sub + 1) * c, :] = jnp.concatenate(outs[e][0], axis=1)
            ob_ref[e, (subs - 1 - sub) * c:(subs - sub) * c, :] = jnp.concatenate(outs[e][1], axis=1)


def _gla_scan(p_gla, n_lat_rows, gate_up, gate_b):
    b, t, n = p_gla.shape
    ch = SEQ_CHUNK
    grid_rows = n_lat_rows // GRID_W
    cols = _grid_cols(n_lat_rows, ch)
    blk_rows = GRID_COL_BLOCK * grid_rows
    assert GRID_COL_BLOCK % cols == 0 and blk_rows % ch == 0 and (t - n_lat_rows) % blk_rows == 0
    n_lat, n_ctx = n_lat_rows // blk_rows, (t - n_lat_rows) // blk_rows
    n_blk = n_lat + n_ctx
    fwd, rev = _chunk_orders(n_lat, n_ctx)
    gup = jnp.zeros((128, 2 * GLA_DK), F32)
    gup = gup.at[0:GLA_GATE_RANK, 0:GLA_DK].set(gate_up[0]).at[GLA_GATE_RANK:2 * GLA_GATE_RANK, GLA_DK:].set(gate_up[1])
    nb = GLA_SCAN_BATCH
    p_view = p_gla.reshape(b, t // GRID_W, GRID_W, n)
    view_blk = lambda bid: pl.BlockSpec((nb, grid_rows, GRID_COL_BLOCK, n), lambda b, s: (b, 0, jnp.minimum(bid(s), n_lat - 1), 0))
    ctx_blk = lambda bid: pl.BlockSpec((nb, blk_rows, n), lambda b, s: (b, jnp.maximum(bid(s), n_lat), 0))
    bid_f = lambda s: (s + n_lat) % n_blk
    bid_b = lambda s: n_blk - 1 - s
    return pl.pallas_call(
        functools.partial(_gla_scan_kernel, n_lat=n_lat, n_ctx=n_ctx, cols=cols, c=ch),
        grid=(b // nb, n_blk),
        in_specs=[view_blk(bid_f), ctx_blk(bid_f), view_blk(bid_b), ctx_blk(bid_b),
                  pl.BlockSpec((128, 2 * GLA_DK), lambda b, s: (0, 0)),
                  pl.BlockSpec((1, 2 * GLA_DK), lambda b, s: (0, 0))],
        out_specs=[pl.BlockSpec((nb, blk_rows, GLA_DV), fwd), pl.BlockSpec((nb, blk_rows, GLA_DV), rev)],
        out_shape=[jax.ShapeDtypeStruct((b, t, GLA_DV), F32)] * 2,
        scratch_shapes=[pltpu.VMEM((nb * GLA_HEADS, GLA_HV, 2 * GLA_HK), F32)],
        compiler_params=_cp("arbitrary", "arbitrary"),
        name="gla_scan",
    )(p_view, p_gla, p_view, p_gla, gup, gate_b.reshape(1, 2 * GLA_DK))


def _gla_readout_kernel(of_ref, ob_ref, gv_ref, gc_ref, ng_ref, ylat_ref, yctx_ref, *, n_lat_tiles, cols):
    t = pl.program_id(1)
    o = of_ref[0] + ob_ref[0]
    bd = _head_blockdiag(GLA_DV, GLA_HV)
    ms = _x_exact01(o * o, bd) * (1.0 / GLA_HV)
    g = jnp.where(t < n_lat_tiles, _col_major_rows(gv_ref, 0, 0, cols, 0, GLA_DV), gc_ref[0])
    y = (o * lax.rsqrt(ms + RMS_EPS) * ng_ref[...] * _silu(g)).astype(ylat_ref.dtype)
    rows = y.shape[0] // cols

    @pl.when(t < n_lat_tiles)
    def _():
        for j in range(cols):
            ylat_ref[0, :, j, :] = y[j * rows:(j + 1) * rows]

    @pl.when(t >= n_lat_tiles)
    def _():
        yctx_ref[0] = y


def _gla_readout(of, ob, g_gla, norm_g, n_lat_rows):
    b, t, c = of.shape
    n_lat_tiles = n_lat_rows // TOKEN_TILE
    grid_rows = n_lat_rows // GRID_W
    cols = _grid_cols(n_lat_rows, TOKEN_TILE)
    assert cols == GRID_COL_BLOCK
    tile = pl.BlockSpec((1, TOKEN_TILE, c), lambda b, t: (b, t, 0))
    g_view = g_gla.reshape(b, t // GRID_W, GRID_W, c)
    y_lat, y_ctx = pl.pallas_call(
        functools.partial(_gla_readout_kernel, n_lat_tiles=n_lat_tiles, cols=cols),
        grid=(b, t // TOKEN_TILE),
        in_specs=[tile, tile,
                  pl.BlockSpec((1, grid_rows, cols, c), lambda b, t: (b, 0, jnp.minimum(t, n_lat_tiles - 1), 0)),
                  pl.BlockSpec((1, TOKEN_TILE, c), lambda b, t: (b, jnp.maximum(t, n_lat_tiles), 0)),
                  pl.BlockSpec((1, c), lambda b, t: (0, 0))],
        out_specs=[pl.BlockSpec((1, grid_rows, cols, c), lambda b, t: (b, 0, jnp.minimum(t, n_lat_tiles - 1), 0)),
                   pl.BlockSpec((1, TOKEN_TILE, c), lambda b, t: (b, 0, 0))],
        out_shape=[jax.ShapeDtypeStruct((b, grid_rows, GRID_W, c), BF16), jax.ShapeDtypeStruct((b, t - n_lat_rows, c), BF16)],
        compiler_params=_cp("arbitrary", "arbitrary"),
        name="gla_readout",
    )(of, ob, g_view, g_gla, jnp.tile(norm_g, GLA_HEADS).reshape(1, c))
    return y_lat.reshape(b, n_lat_rows, c), y_ctx


def _merge_kernel(h_ref, mod_ref, gate_ref, yhyl_ref, yhyc_ref, rof_ref, rob_ref, rbonus_ref, rg_ref, ys5_ref, us5_ref,
                  yglal_ref, yglac_ref, lng_ref, lnb_ref, dskip_ref, gluw_ref, glub_ref, wbr_ref, wout_ref, o_ref,
                  *, n_lat_tiles):
    d = D_MODEL
    is_lat = pl.program_id(1) < n_lat_tiles
    y_hy = jnp.where(is_lat, yhyl_ref[0], yhyc_ref[0])
    y_gla = jnp.where(is_lat, yglal_ref[0], yglac_ref[0])
    y_rw = _rw_readout(rof_ref[0], rob_ref[0], rbonus_ref[0], rg_ref[0], lng_ref[...], lnb_ref[...])
    y_s5 = _s5_output(ys5_ref[0], us5_ref[0], dskip_ref[...], gluw_ref[...], glub_ref[...])
    acc = jnp.zeros((h_ref.shape[1], d), F32)
    for m, y in enumerate((y_hy, y_rw, y_s5, y_gla)):
        acc = acc + (_sigmoid(gate_ref[0, :, m * d:(m + 1) * d].astype(F32))
                     * jnp.dot(y.astype(BF16), wbr_ref[m], preferred_element_type=F32))
    out = jnp.dot(acc.astype(BF16), wout_ref[...], preferred_element_type=F32)
    o_ref[0] = h_ref[0] + mod_ref[0, 0, 2:3, :] * out


def _merge(h_all, modsel, n_tiles, n_lat_tiles, gate, y_hy_lat, y_hy_ctx, rof, rob, rbonus, rg, y_s5, u_s5, y_gla_lat, y_gla_ctx,
           ln_g, ln_b, d_skip, glu_w, glu_b, w_branch, w_out):
    b, t, d = h_all.shape
    c = MIX_W
    tile = lambda w: pl.BlockSpec((1, TOKEN_TILE, w), lambda b, t: (b, t, 0))
    lat = pl.BlockSpec((1, TOKEN_TILE, c), lambda b, t: (b, jnp.minimum(t, n_lat_tiles - 1), 0))
    ctx = pl.BlockSpec((1, TOKEN_TILE, c), lambda b, t: (b, jnp.maximum(t - n_lat_tiles, 0), 0))
    vec = lambda x: x.reshape(1, -1)
    consts = [vec(ln_g), vec(ln_b), vec(d_skip), glu_w.astype(BF16), vec(glu_b), w_branch.astype(BF16), w_out.astype(BF16)]
    cspecs = [pl.BlockSpec(x.shape, (lambda n: lambda b, t: (0,) * n)(x.ndim)) for x in consts]
    return pl.pallas_call(
        functools.partial(_merge_kernel, n_lat_tiles=n_lat_tiles),
        grid=(b, n_tiles),
        in_specs=[tile(d), pl.BlockSpec((1, 1, 6, d), _mod_index(n_lat_tiles)), tile(4 * d), lat, ctx] + [tile(c)] * 6
                 + [lat, ctx] + cspecs,
        out_specs=tile(d),
        out_shape=jax.ShapeDtypeStruct((b, t, d), F32),
        input_output_aliases={0: 0},
        compiler_params=_cp("arbitrary", "arbitrary"),
        name="merge",
    )(h_all, modsel, gate, y_hy_lat, y_hy_ctx, rof, rob, rbonus, rg, y_s5, u_s5, y_gla_lat, y_gla_ctx, *consts)


def _ffn_kernel(h_ref, g_ref, mod_ref, w1_ref, w3_ref, w2_ref, o_ref):
    h = h_ref[0]
    xm = _norm_mod(h, g_ref[...], mod_ref[0, 0], 3).astype(BF16)
    a = jnp.dot(xm, w1_ref[...], preferred_element_type=F32)
    b = jnp.dot(xm, w3_ref[...], preferred_element_type=F32)
    y = jnp.dot((_silu(a) * b).astype(BF16), w2_ref[...], preferred_element_type=F32)
    o_ref[0] = h + mod_ref[0, 0, 5:6, :] * y


def _ffn_dense(h_all, g, modsel, n_lat_tiles, w1, w3, w2):
    b, t, d = h_all.shape
    tile = pl.BlockSpec((1, TOKEN_TILE, d), lambda b, t: (b, t, 0))
    ws = [w1.astype(BF16), w3.astype(BF16), w2.astype(BF16)]
    wspec = [pl.BlockSpec(w.shape, lambda b, t: (0, 0), pipeline_mode=pl.Buffered(1)) for w in ws]
    return pl.pallas_call(
        _ffn_kernel,
        grid=(b, t // TOKEN_TILE),
        in_specs=[tile, pl.BlockSpec((1, d), lambda b, t: (0, 0)), pl.BlockSpec((1, 1, 6, d), _mod_index(n_lat_tiles))] + wspec,
        out_specs=tile,
        out_shape=jax.ShapeDtypeStruct((b, t, d), F32),
        input_output_aliases={0: 0},
        compiler_params=_cp("arbitrary", "arbitrary"),
        name="ffn_dense",
    )(h_all, g.reshape(1, d), modsel, *ws)


def _router_kernel(h_ref, g_ref, mod_ref, rw_ref, rb_ref, x_ref, logit_ref):
    xm = _norm_mod(h_ref[0], g_ref[...], mod_ref[0, 0], 3)
    x_ref[0] = xm.astype(BF16)
    logit_ref[0] = _mm_hp(xm, rw_ref[...]) + rb_ref[...]


def _router(h_all, g, modsel, n_lat_tiles, router_w, router_b):
    b, t, d = h_all.shape
    n_lat = n_lat_tiles * TOKEN_TILE
    rw = jnp.pad(router_w, ((0, 0), (0, 128 - N_EXPERTS)))
    rb = jnp.pad(router_b, (0, 128 - N_EXPERTS), constant_values=-1e30).reshape(1, 128)
    tile = lambda w: pl.BlockSpec((1, TOKEN_TILE, w), lambda b, t: (b, t, 0))
    return pl.pallas_call(
        _router_kernel,
        grid=(b, n_lat_tiles),
        in_specs=[tile(d), pl.BlockSpec((1, d), lambda b, t: (0, 0)), pl.BlockSpec((1, 1, 6, d), _mod_index(n_lat_tiles)),
                  pl.BlockSpec((d, 128), lambda b, t: (0, 0)), pl.BlockSpec((1, 128), lambda b, t: (0, 0))],
        out_specs=[tile(d), tile(128)],
        out_shape=[jax.ShapeDtypeStruct((b, n_lat, d), BF16), jax.ShapeDtypeStruct((b, n_lat, 128), F32)],
        compiler_params=_cp("arbitrary", "arbitrary"),
        name="moe_router",
    )(h_all, g.reshape(1, d), modsel, rw, rb)


def _moe_expert_kernel(be_ref, x_ref, w1_ref, w3_ref, w2_ref, o_ref):
    x = x_ref[...]
    a = jnp.dot(x, w1_ref[0], preferred_element_type=F32)
    b = jnp.dot(x, w3_ref[0], preferred_element_type=F32)
    o_ref[...] = jnp.dot((_silu(a) * b).astype(BF16), w2_ref[0], preferred_element_type=F32).astype(o_ref.dtype)


def _moe_experts(xb, block_exp, w1, w3, w2):
    rows, d = xb.shape
    nb = rows // MOE_BLOCK_ROWS
    f = w1.shape[2]
    grid_spec = pltpu.PrefetchScalarGridSpec(
        num_scalar_prefetch=1,
        grid=(nb,),
        in_specs=[pl.BlockSpec((MOE_BLOCK_ROWS, d), lambda i, be: (i, 0)),
                  pl.BlockSpec((1, d, f), lambda i, be: (be[i], 0, 0), pipeline_mode=pl.Buffered(1)),
                  pl.BlockSpec((1, d, f), lambda i, be: (be[i], 0, 0), pipeline_mode=pl.Buffered(1)),
                  pl.BlockSpec((1, f, d), lambda i, be: (be[i], 0, 0), pipeline_mode=pl.Buffered(1))],
        out_specs=pl.BlockSpec((MOE_BLOCK_ROWS, d), lambda i, be: (i, 0)),
    )
    return pl.pallas_call(
        _moe_expert_kernel,
        grid_spec=grid_spec,
        out_shape=jax.ShapeDtypeStruct((rows, d), BF16),
        compiler_params=_cp("arbitrary"),
        name="moe_experts",
    )(block_exp, xb, w1, w3, w2)


def _moe_combine_kernel(h_ref, mod_ref, y0_ref, y1_ref, gate_ref, fg_ref, o_ref):
    y = y0_ref[0, 0].astype(F32) * gate_ref[0, :, 0:1] + y1_ref[0, 0].astype(F32) * gate_ref[0, :, 1:2]
    h = h_ref[0] + mod_ref[0, 0, 5:6, :] * y
    o_ref[0] = h * lax.rsqrt(jnp.mean(h * h, axis=-1, keepdims=True) + RMS_EPS) * fg_ref[...]


def _moe_combine(h_all, modsel, n_lat_tiles, y2, gates, final_g):
    b, t, d = h_all.shape
    n_lat = n_lat_tiles * TOKEN_TILE
    tile = lambda w: pl.BlockSpec((1, TOKEN_TILE, w), lambda b, t: (b, t, 0))
    pick = lambda k: pl.BlockSpec((1, 1, TOKEN_TILE, d), lambda b, t: (k, b, t, 0))
    return pl.pallas_call(
        _moe_combine_kernel,
        grid=(b, n_lat_tiles),
        in_specs=[tile(d), pl.BlockSpec((1, 1, 6, d), _mod_index(n_lat_tiles)), pick(0), pick(1),
                  tile(128), pl.BlockSpec((1, d), lambda b, t: (0, 0))],
        out_specs=tile(d),
        out_shape=jax.ShapeDtypeStruct((b, n_lat, d), F32),
        compiler_params=_cp("arbitrary", "arbitrary"),
        name="moe_combine",
    )(h_all, modsel, y2, y2, gates, final_g.reshape(1, d))


def _moe_ffn_and_final_norm(h_all, g, modsel, n_lat_tiles, router_w, router_b, w1, w3, w2, final_g):
    b, t, d = h_all.shape
    n_lat = n_lat_tiles * TOKEN_TILE
    n_tok = b * n_lat
    xn, logits = _router(h_all, g, modsel, n_lat_tiles, router_w, router_b)
    logits = logits.reshape(n_tok, 128)[:, :N_EXPERTS]
    top_v, top_i = lax.top_k(logits, TOP_K)
    gates = jax.nn.softmax(top_v, axis=-1)
    n_assign = n_tok * TOP_K
    e_flat = top_i.reshape(-1).astype(jnp.int32)
    order = jnp.argsort(e_flat).astype(jnp.int32)
    rank = jnp.argsort(order).astype(jnp.int32)
    experts = jnp.arange(N_EXPERTS, dtype=jnp.int32)[None, :]
    is_e = e_flat[:, None] == experts
    counts = jnp.sum(is_e.astype(jnp.int32), axis=0)
    padded = (counts + MOE_BLOCK_ROWS - 1) // MOE_BLOCK_ROWS * MOE_BLOCK_ROWS
    start = jnp.cumsum(counts) - counts
    pend = jnp.cumsum(padded)
    pstart = pend - padded
    n_blocks = n_assign // MOE_BLOCK_ROWS + N_EXPERTS
    n_rows = n_blocks * MOE_BLOCK_ROWS
    row_ids = jnp.arange(n_rows, dtype=jnp.int32)
    row_exp = jnp.minimum(jnp.sum((row_ids[:, None] >= pend[None, :]).astype(jnp.int32), axis=1), N_EXPERTS - 1)
    row_is_e = row_exp[:, None] == experts
    lookup = lambda onehot, table: jnp.sum(jnp.where(onehot, table[None, :], 0), axis=1)
    row_off = row_ids - lookup(row_is_e, pstart)
    row_slot = jnp.clip(lookup(row_is_e, start) + row_off, 0, n_assign - 1)
    row_tok = jnp.where(row_off < lookup(row_is_e, counts), order[row_slot] // TOP_K, n_tok).astype(jnp.int32)
    block_exp = row_exp[::MOE_BLOCK_ROWS]
    tok_pad = jnp.concatenate([xn.reshape(n_tok, d), jnp.zeros((1, d), BF16)], axis=0)
    xb = tok_pad.at[row_tok].get(mode="promise_in_bounds")
    yb = _moe_experts(xb, block_exp, w1.astype(BF16), w3.astype(BF16), w2.astype(BF16))
    dest_orig = (lookup(is_e, pstart) + rank - lookup(is_e, start)).astype(jnp.int32)
    dest_by_k = dest_orig.reshape(n_tok, TOP_K).T.reshape(-1)
    y2 = yb.at[dest_by_k].get(mode="promise_in_bounds").reshape(TOP_K, b, n_lat, d)
    gates_pad = jnp.pad(gates, ((0, 0), (0, 128 - TOP_K))).reshape(b, n_lat, 128)
    return _moe_combine(h_all, modsel, n_lat_tiles, y2, gates_pad, final_g)


def _pad_cols(w, n):
    return jnp.pad(w, ((0, 0), (0, n - w.shape[1])))


def kernel(x, c, ctx, c_ctx, ada_w, ada_b, norm_mix_g, norm_ffn_g, w_in, hy_conv_w, hy_conv_b, hy_f_w1, hy_f_b1, hy_f_w2, hy_f_b2, hy_f_w3, hy_f_b3, hy_bias, rw_mu, rw_w0, rw_w_up, rw_a0, rw_a_up, rw_g_up, rw_k_k, rw_k_a, rw_r_k, rw_ln_g, rw_ln_b, s5_a_re, s5_a_im, s5_log_dt, s5_b_re, s5_b_im, s5_c_re, s5_c_im, s5_d, s5_glu_w, s5_glu_b, gla_gate_up, gla_gate_b, gla_norm_g, w_branch, w_out, ffn_w1, ffn_w3, ffn_w2, moe_router_w, moe_router_b, moe_w1, moe_w3, moe_w2, final_norm_g):
    bsz, n_lat, d = x.shape
    n_ctx = ctx.shape[1]
    depth = w_in.shape[0]
    assert depth == 2 and d == D_MODEL
    assert n_lat % TOKEN_TILE == 0 and n_ctx % TOKEN_TILE == 0 and n_lat % GRID_W == 0
    t_all = n_lat + n_ctx
    n_lat_tiles = n_lat // TOKEN_TILE
    n_tiles = t_all // TOKEN_TILE
    assert n_tiles <= 2 * n_lat_tiles

    rows = (bsz + 1 + 7) // 8 * 8
    c_all = jnp.zeros((rows, d), F32).at[:bsz].set(c.astype(F32)).at[bsz].set(c_ctx.astype(F32))
    mod = _ada_mod(c_all, ada_w, ada_b)
    h_all = jnp.concatenate([x.astype(F32), ctx.astype(F32)], axis=1)

    dft_lat = _dft_tables(n_lat)
    dft_ctx = _dft_tables(n_ctx)
    seg = np.cumsum([0, 3 * MIX_W, RW_IN, MIX_W, 2 * GLA_DK + 2 * GLA_DV + 2 * GLA_GATE_RANK, 4 * d])

    out = None
    for l in range(depth):
        last = l == depth - 1
        mod_l = mod[l, :bsz].reshape(bsz, 6, d)
        mod_c = jnp.broadcast_to(mod[l, bsz].reshape(1, 6, d), (bsz, 6, d))
        modsel = jnp.stack([mod_l, mod_c], axis=1)

        wl = w_in[l].astype(BF16)
        g_lo, g_hi = seg[3] + 2 * GLA_DK + GLA_DV, seg[3] + 2 * GLA_DK + 2 * GLA_DV
        w_gla = _pad_cols(jnp.concatenate([wl[:, seg[3]:g_lo], wl[:, g_hi:seg[4]]], axis=1), GLA_IN_PAD)
        ws = [wl[:, seg[0]:seg[1]], wl[:, seg[1]:seg[2]], wl[:, seg[2]:seg[3]], w_gla, wl[:, g_lo:g_hi], wl[:, seg[4]:seg[5]]]
        hy_x0, hy_z, p_rw, p_s5, p_gla, g_gla, gate = _in_proj(h_all, norm_mix_g[l], modsel, ws, hy_conv_w[l], hy_conv_b[l],
                                                               n_lat_tiles)

        filt_args = (hy_f_w1[l], hy_f_b1[l], hy_f_w2[l], hy_f_b2[l], hy_f_w3[l], hy_f_b3[l])
        spec_lat = _hy_spectrum(_hy_filter(n_lat, *filt_args), dft_lat[0], dft_lat[1])
        y_hy_lat = _hyena(hy_z, hy_x0, 0, n_lat, hy_bias[l], dft_lat, spec_lat)
        y_hy_ctx = y_hy_lat
        if not last:
            spec_ctx = _hy_spectrum(_hy_filter(n_ctx, *filt_args), dft_ctx[0], dft_ctx[1])
            y_hy_ctx = _hyena(hy_z, hy_x0, n_lat // n_ctx, n_ctx, hy_bias[l], dft_ctx, spec_ctx)

        r, v, kap, lwf, lwb, kf, kb, bf, bb, rg, rbonus = _rw_prep(
            p_rw, n_lat_tiles, rw_mu[l], rw_w0[l], rw_w_up[l], rw_a0[l], rw_a_up[l], rw_g_up[l], rw_k_k[l], rw_k_a[l], rw_r_k[l])
        rof, rob = _rw_scan(r, v, kap, lwf, lwb, kf, kb, bf, bb, n_lat)

        y_s5 = _s5_core(p_s5, n_lat, _s5_matrices(s5_a_re[l], s5_a_im[l], s5_log_dt[l], s5_b_re[l], s5_b_im[l], s5_c_re[l], s5_c_im[l]))

        gof, gob = _gla_scan(p_gla, n_lat, gla_gate_up[l], gla_gate_b[l])
        y_gla_lat, y_gla_ctx = _gla_readout(gof, gob, g_gla, gla_norm_g[l], n_lat)

        nt_l = n_lat_tiles if last else n_tiles
        h_all = _merge(h_all, modsel, nt_l, n_lat_tiles, gate, y_hy_lat, y_hy_ctx, rof, rob, rbonus, rg, y_s5, p_s5,
                       y_gla_lat, y_gla_ctx, rw_ln_g[l], rw_ln_b[l], s5_d[l], s5_glu_w[l], s5_glu_b[l], w_branch[l], w_out[l])

        i = l // 2
        if l % 2 == 0:
            h_all = _ffn_dense(h_all, norm_ffn_g[l], modsel, n_lat_tiles, ffn_w1[i], ffn_w3[i], ffn_w2[i])
        else:
            out = _moe_ffn_and_final_norm(h_all, norm_ffn_g[l], modsel, n_lat_tiles, moe_router_w[i], moe_router_b[i],
                                          moe_w1[i], moe_w3[i], moe_w2[i], final_norm_g)
    return out.astype(x.dtype)
```

```python
import functools
import math

import numpy as np
import jax
import jax.numpy as jnp
from jax import lax
from jax.experimental import pallas as pl
from jax.experimental.pallas import tpu as pltpu

F32 = jnp.float32
BF16 = jnp.bfloat16

D_MODEL = 1024
GRID_W = 64
MIX_W = 512
RMS_EPS = 1e-6
HY_BANDS = 16
HY_DECAY_TARGET = 1e-2
HY_SHORT_PCT = 0.3
HY_LONG_PCT = 1.5
RW_HEADS = 8
RW_HEAD_DIM = 64
RW_DECAY_SCALE = 0.606531
RW_GN_EPS = 64e-5
RW_IN = 1920
S5_GROUP = 16
S5_GROUPS = 32
S5_STATE = 64
S5_CHUNK = 16
GLA_HEADS = 4
GLA_HK = 64
GLA_HV = 128
GLA_DK = 256
GLA_DV = 512
GLA_GATE_RANK = 16
GLA_GATE_NORM = 16.0
GLA_IN_PAD = 1152
N_EXPERTS = 8
TOP_K = 2
MOE_BLOCK_ROWS = 256
TOKEN_TILE = 256
SEQ_CHUNK = 64
RW_SCAN_BATCH = 4
GLA_SCAN_BATCH = 2
VMEM_LIMIT = 56 * 1024 * 1024

NT_DIMS = (((1,), (1,)), ((), ()))
TN_DIMS = (((0,), (0,)), ((), ()))


def _cp(*sem):
    return pltpu.CompilerParams(dimension_semantics=sem, vmem_limit_bytes=VMEM_LIMIT)


def _mm(a, b, dims=None):
    a = a.astype(BF16)
    b = b.astype(BF16)
    if dims is None:
        return jnp.dot(a, b, preferred_element_type=F32)
    return lax.dot_general(a, b, dims, preferred_element_type=F32)


def _split_bf16(x):
    hi = x.astype(BF16)
    lo = (x - hi.astype(F32)).astype(BF16)
    return hi, lo


def _mm_hp(a, b):
    ah, al = _split_bf16(a)
    bh, bl = _split_bf16(b)
    d = functools.partial(jnp.dot, preferred_element_type=F32)
    return d(ah, bh) + (d(ah, bl) + d(al, bh))


def _mm_exact01(m01, x):
    m = m01.astype(BF16)
    hi = x.astype(BF16)
    r1 = x - hi.astype(F32)
    mid = r1.astype(BF16)
    lo = (r1 - mid.astype(F32)).astype(BF16)
    d = functools.partial(jnp.dot, preferred_element_type=F32)
    return d(m, hi) + (d(m, mid) + d(m, lo))


def _x_exact01(x, m01):
    m = m01.astype(BF16)
    hi = x.astype(BF16)
    mid = (x - hi.astype(F32)).astype(BF16)
    d = functools.partial(jnp.dot, preferred_element_type=F32)
    return d(hi, m) + d(mid, m)


def _sigmoid(x):
    return 0.5 * jnp.tanh(0.5 * x) + 0.5


def _silu(x):
    return x * _sigmoid(x)


def _tri(c, rev, strict):
    i = lax.broadcasted_iota(jnp.int32, (c, c), 0)
    j = lax.broadcasted_iota(jnp.int32, (c, c), 1)
    if rev:
        return (j > i) if strict else (j >= i)
    return (j < i) if strict else (j <= i)


def _head_blockdiag(n, head):
    i = lax.broadcasted_iota(jnp.int32, (n, n), 0) // head
    j = lax.broadcasted_iota(jnp.int32, (n, n), 1) // head
    return (i == j).astype(F32)


def _shift_rows(x, prev_row, next_row):
    n = x.shape[0]
    row = lax.broadcasted_iota(jnp.int32, x.shape, 0)
    xm = jnp.where(row == 0, prev_row, pltpu.roll(x, 1, 0))
    xp = jnp.where(row == n - 1, next_row, pltpu.roll(x, n - 1, 0))
    return xm, xp


def _ada_kernel(c_ref, w_ref, b_ref, o_ref):
    o_ref[0] = _mm_hp(_silu(c_ref[...]), w_ref[0]) + b_ref[0]


def _ada_mod(c_all, ada_w, ada_b):
    depth, d, n = ada_w.shape
    rows = c_all.shape[0]
    return pl.pallas_call(
        _ada_kernel,
        grid=(depth, n // d),
        in_specs=[pl.BlockSpec((rows, d), lambda l, j: (0, 0)),
                  pl.BlockSpec((1, d, d), lambda l, j: (l, 0, j)),
                  pl.BlockSpec((1, 1, d), lambda l, j: (l, 0, j))],
        out_specs=pl.BlockSpec((1, rows, d), lambda l, j: (l, 0, j)),
        out_shape=jax.ShapeDtypeStruct((depth, rows, n), F32),
        compiler_params=_cp("arbitrary", "arbitrary"),
        name="ada_mod",
    )(c_all, ada_w, ada_b.reshape(depth, 1, n))


def _norm_mod(x, g, mod, sh_row):
    xn = x * lax.rsqrt(jnp.mean(x * x, axis=-1, keepdims=True) + RMS_EPS) * g
    return xn * (1.0 + mod[sh_row + 1:sh_row + 2]) + mod[sh_row:sh_row + 1]


def _seq_edges(t, n_lat_tiles, n_tiles):
    first = jnp.logical_or(t == 0, t == n_lat_tiles)
    last = jnp.logical_or(t == n_lat_tiles - 1, t == n_tiles - 1)
    return first, last


def _in_proj_kernel(x_ref, xprev_ref, xnext_ref, g_ref, mod_ref, cw_ref, cb_ref, w_hy, w_rw, w_s5, w_gla, w_glag, w_gate,
                    o_x0, o_z, o_rw, o_s5, o_gla, o_glag, o_gate, *, n_lat_tiles):
    t = pl.program_id(1)
    first, last = _seq_edges(t, n_lat_tiles, pl.num_programs(1))
    mod = mod_ref[0, 0]
    xm = _norm_mod(x_ref[0], g_ref[...], mod, 0).astype(BF16)
    for w, o in ((w_rw, o_rw), (w_s5, o_s5), (w_gla, o_gla), (w_glag, o_glag), (w_gate, o_gate)):
        o[0] = jnp.dot(xm, w[...], preferred_element_type=F32).astype(o.dtype)
    p = jnp.dot(xm, w_hy[...], preferred_element_type=F32)
    halo = jnp.concatenate([xprev_ref[0, 7:8, :], xnext_ref[0, 0:1, :], jnp.zeros((6, x_ref.shape[2]), F32)], axis=0)
    ph = jnp.dot(_norm_mod(halo, g_ref[...], mod, 0).astype(BF16), w_hy[...], preferred_element_type=F32)
    pm, pp = _shift_rows(p, jnp.where(first, 0.0, ph[0:1]), jnp.where(last, 0.0, ph[1:2]))
    u = pm * cw_ref[0:1] + p * cw_ref[1:2] + pp * cw_ref[2:3] + cb_ref[...]
    c = MIX_W
    o_x0[0] = u[:, 0:c].astype(o_x0.dtype)
    o_z[0] = (u[:, 2 * c:3 * c] * u[:, c:2 * c]).astype(o_z.dtype)


def _mod_index(n_lat_tiles):
    return lambda b, t: (b, t // n_lat_tiles, 0, 0)


def _halo_specs(width, n_rows):
    sub = TOKEN_TILE // 8
    return [pl.BlockSpec((1, 8, width), lambda b, t: (b, jnp.maximum(t * sub - 1, 0), 0)),
            pl.BlockSpec((1, 8, width), lambda b, t: (b, jnp.minimum((t + 1) * sub, n_rows // 8 - 1), 0))]


def _in_proj(h_all, g, modsel, ws, conv_w, conv_b, n_lat_tiles):
    b, t, d = h_all.shape
    nt = t // TOKEN_TILE
    c = MIX_W
    wspec = [pl.BlockSpec(w.shape, lambda b, t: (0, 0), pipeline_mode=pl.Buffered(1)) for w in ws]
    widths = [c, c] + [w.shape[1] for w in ws[1:]]
    dtypes = [BF16, BF16, F32, F32, F32, F32, BF16]
    return pl.pallas_call(
        functools.partial(_in_proj_kernel, n_lat_tiles=n_lat_tiles),
        grid=(b, nt),
        in_specs=[pl.BlockSpec((1, TOKEN_TILE, d), lambda b, t: (b, t, 0))] + _halo_specs(d, t)
                 + [pl.BlockSpec((1, d), lambda b, t: (0, 0)),
                    pl.BlockSpec((1, 1, 6, d), _mod_index(n_lat_tiles)),
                    pl.BlockSpec((3, 3 * c), lambda b, t: (0, 0)),
                    pl.BlockSpec((1, 3 * c), lambda b, t: (0, 0))] + wspec,
        out_specs=[pl.BlockSpec((1, TOKEN_TILE, w), lambda b, t: (b, t, 0)) for w in widths],
        out_shape=[jax.ShapeDtypeStruct((b, t, w), dt) for w, dt in zip(widths, dtypes)],
        compiler_params=_cp("arbitrary", "arbitrary"),
        name="in_proj",
    )(h_all, h_all, h_all, g.reshape(1, d), modsel, conv_w, conv_b.reshape(1, 3 * c), *ws)


def _grid_cols(n_lat_rows, rows):
    grid_rows = n_lat_rows // GRID_W
    assert grid_rows % 8 == 0 and rows % grid_rows == 0 and GRID_W % (rows // grid_rows) == 0
    return rows // grid_rows


GRID_COL_BLOCK = 8


def _col_major_rows(view_ref, e, j0, cols, lo, hi):
    return jnp.concatenate([view_ref[e, :, pl.ds(j0 + j, 1), lo:hi][:, 0, :] for j in range(cols)], axis=0)


def _hy_filter_kernel(z_ref, w1, b1, w2, b2, w3, b3, dec_ref, o_ref):
    h = jnp.sin(_mm_hp(z_ref[...], w1[...]) + b1[...])
    h = jnp.sin(_mm_hp(h, w2[...]) + b2[...])
    o_ref[...] = (_mm_hp(h, w3[...]) + b3[...]) * dec_ref[...]


def _hy_features(length):
    pos = jnp.arange(length, dtype=F32)
    t = pos / max(length - 1, 1)
    freqs = jnp.linspace(1e-4, HY_BANDS - 1, HY_BANDS, dtype=F32)
    ang = (2.0 * math.pi / length) * pos[:, None] * freqs[None, :]
    z = jnp.concatenate([t[:, None], jnp.cos(ang), -jnp.sin(ang)], axis=-1)
    rates = jnp.abs(jnp.linspace(math.log(HY_DECAY_TARGET) / HY_LONG_PCT,
                                 math.log(HY_DECAY_TARGET) / HY_SHORT_PCT, MIX_W, dtype=F32))
    rates = jnp.concatenate([rates, rates])
    decay = jnp.exp(-t[:, None] * rates[None, :])
    return jnp.pad(z, ((0, 0), (0, 128 - z.shape[1]))), decay


def _hy_filter(length, fw1, fb1, fw2, fb2, fw3, fb3):
    z, decay = _hy_features(length)
    w1 = jnp.pad(fw1, ((0, 128 - fw1.shape[0]), (0, 128 - fw1.shape[1])))
    b1 = jnp.pad(fb1, (0, 128 - fb1.shape[0])).reshape(1, 128)
    w2 = jnp.pad(fw2, ((0, 128 - fw2.shape[0]), (0, 128 - fw2.shape[1])))
    b2 = jnp.pad(fb2, (0, 128 - fb2.shape[0])).reshape(1, 128)
    w3 = jnp.pad(fw3, ((0, 128 - fw3.shape[0]), (0, 0)))
    n = fw3.shape[1]
    rows = min(length, 512)
    full = lambda s: pl.BlockSpec(s, lambda i: (0, 0))
    return pl.pallas_call(
        _hy_filter_kernel,
        grid=(length // rows,),
        in_specs=[pl.BlockSpec((rows, 128), lambda i: (i, 0)), full((128, 128)), full((1, 128)), full((128, 128)),
                  full((1, 128)), full((128, n)), full((1, n)), pl.BlockSpec((rows, n), lambda i: (i, 0))],
        out_specs=pl.BlockSpec((rows, n), lambda i: (i, 0)),
        out_shape=jax.ShapeDtypeStruct((length, n), F32),
        compiler_params=_cp("arbitrary"),
        name="hy_filter",
    )(z, w1, b1, w2, b2, w3, fb3.reshape(1, n), decay)


def _dft_tables(length):
    n = 2 * length
    k = jnp.arange(length, dtype=jnp.int32)
    m = (k[:, None] * k[None, :]) % n
    ang = m.astype(F32) * (2.0 * math.pi / n)
    wc = jnp.cos(ang).astype(BF16)
    sin = jnp.sin(ang)
    nyq = jnp.where(k % 2 == 0, 1.0, -1.0).astype(F32)
    ws = jnp.where(k[:, None] == 0, nyq[None, :], sin).astype(BF16)
    wst = jnp.where(k[None, :] == 0, nyq[:, None], sin).astype(BF16)
    return wc, ws, wc, wst


def _dft_fwd_kernel(wc_ref, ws_ref, xh_ref, xl_ref, oc_ref, os_ref):
    d = functools.partial(jnp.dot, preferred_element_type=F32)
    oc_ref[...] = d(wc_ref[...], xh_ref[...]) + d(wc_ref[...], xl_ref[...])
    os_ref[...] = d(ws_ref[...], xh_ref[...]) + d(ws_ref[...], xl_ref[...])


def _hy_spectrum(filt, wc, ws):
    length, c2 = filt.shape
    c = c2 // 2
    n = 2 * length
    hb = filt[:, c:].at[0].set(0.0)
    x = jnp.concatenate([filt[:, :c], hb], axis=1)
    xh, xl = _split_bf16(x)
    fr = min(length, 512)
    fc, fs = pl.pallas_call(
        _dft_fwd_kernel,
        grid=(length // fr,),
        in_specs=[pl.BlockSpec((fr, length), lambda i: (i, 0)), pl.BlockSpec((fr, length), lambda i: (i, 0)),
                  pl.BlockSpec((length, c2), lambda i: (0, 0)), pl.BlockSpec((length, c2), lambda i: (0, 0))],
        out_specs=[pl.BlockSpec((fr, c2), lambda i: (i, 0)), pl.BlockSpec((fr, c2), lambda i: (i, 0))],
        out_shape=[jax.ShapeDtypeStruct((length, c2), F32)] * 2,
        compiler_params=_cp("arbitrary"),
        name="hy_spectrum",
    )(wc, ws, xh, xl)
    k_re = fc[:, :c] + fc[:, c:]
    k_im = fs[:, c:] - fs[:, :c]
    k_nyq = fs[0, :c] + fs[0, c:]
    ka = (k_re * (2.0 / n)).at[0].set(k_re[0] / n)
    kb = (k_im * (2.0 / n)).at[0].set(0.0)
    ka2 = ka.at[0].set(k_nyq / n)
    return ka, ka2, kb


HYENA_BATCH = 2


def _hyena_kernel(z_ref, x0_ref, bias_ref, wc_ref, ws_ref, wct_ref, wst_ref, ka_ref, ka2_ref, kb_ref, o_ref, acc_scr):
    f = pl.program_id(1)
    nb = z_ref.shape[0]
    c = z_ref.shape[2]
    z = jnp.concatenate([z_ref[e] for e in range(nb)], axis=1)
    tile = lambda ref: jnp.concatenate([ref[...]] * nb, axis=1)

    @pl.when(f == 0)
    def _():
        acc_scr[...] = z.astype(F32) * tile(bias_ref)

    zc = jnp.dot(wc_ref[...], z, preferred_element_type=F32)
    zs = jnp.dot(ws_ref[...], z, preferred_element_type=F32)
    ka, ka2, kb = tile(ka_ref), tile(ka2_ref), tile(kb_ref)
    a = zc * ka + zs * kb
    bv = zs * ka2 - zc * kb
    acc_scr[...] += (jnp.dot(wct_ref[...], a.astype(BF16), preferred_element_type=F32)
                     + jnp.dot(wst_ref[...], bv.astype(BF16), preferred_element_type=F32))

    @pl.when(f == pl.num_programs(1) - 1)
    def _():
        for e in range(nb):
            o_ref[e] = (x0_ref[e].astype(F32) * acc_scr[:, e * c:(e + 1) * c]).astype(o_ref.dtype)


def _hyena(z, x0, row_block, length, bias, tables, spec):
    b, t, c = z.shape
    wc, ws, wct, wst = tables
    ka, ka2, kb = spec
    fr = min(length, 256)
    nf = length // fr
    nb = HYENA_BATCH
    seq = pl.BlockSpec((nb, length, c), lambda b, f: (b, row_block, 0))
    in_specs = [seq, seq,
                pl.BlockSpec((1, c), lambda b, f: (0, 0)),
                pl.BlockSpec((fr, length), lambda b, f: (f, 0)),
                pl.BlockSpec((fr, length), lambda b, f: (f, 0)),
                pl.BlockSpec((length, fr), lambda b, f: (0, f)),
                pl.BlockSpec((length, fr), lambda b, f: (0, f)),
                pl.BlockSpec((fr, c), lambda b, f: (f, 0)),
                pl.BlockSpec((fr, c), lambda b, f: (f, 0)),
                pl.BlockSpec((fr, c), lambda b, f: (f, 0))]
    return pl.pallas_call(
        _hyena_kernel,
        grid=(b // nb, nf),
        in_specs=in_specs,
        out_specs=pl.BlockSpec((nb, length, c), lambda b, f: (b, 0, 0)),
        out_shape=jax.ShapeDtypeStruct((b, length, c), BF16),
        scratch_shapes=[pltpu.VMEM((length, nb * c), F32)],
        compiler_params=_cp("arbitrary", "arbitrary"),
        name="hyena_%d" % length,
    )(z, x0, bias.reshape(1, c), wc, ws, wct, wst, ka, ka2, kb)


def _rw_prep_kernel(p_ref, pprev_ref, pnext_ref, mu_ref, w0_ref, wup_ref, a0_ref, aup_ref, gup_ref, kk_ref, ka_ref, rk_ref,
                    r_o, v_o, kap_o, lwf_o, lwb_o, kf_o, kb_o, bf_o, bb_o, g_o, bonus_o, *, n_lat_tiles):
    p = p_ref[0]
    first, last = _seq_edges(pl.program_id(1), n_lat_tiles, pl.num_programs(1))
    prev_row = jnp.where(first, 0.0, pprev_ref[0, 7:8, :])
    next_row = jnp.where(last, 0.0, pnext_ref[0, 0:1, :])
    pm, pp = _shift_rows(p, prev_row, next_row)
    p = p + mu_ref[0:1] * (pm - p) + mu_ref[1:2] * (pp - p)
    c = MIX_W
    r, k, v = p[:, 0:c], p[:, c:2 * c], p[:, 2 * c:3 * c]
    wd, ad, gd = p[:, 3 * c:3 * c + 128], p[:, 3 * c + 128:3 * c + 256], p[:, 3 * c + 256:3 * c + 384]
    bd = _head_blockdiag(c, RW_HEAD_DIM)
    g = _mm(_sigmoid(gd), gup_ref[...])
    kk = k * kk_ref[...]
    kap = kk * lax.rsqrt(jnp.maximum(_x_exact01(kk * kk, bd), 1e-24))
    logw = -RW_DECAY_SCALE * _sigmoid(w0_ref[...] + _mm(jnp.tanh(wd), wup_ref[...]))
    a = _sigmoid(a0_ref[...] + _mm(ad, aup_ref[...]))
    k_sum = 0.0
    outs_k, outs_b = (kf_o, kb_o), (bf_o, bb_o)
    for d in range(2):
        a_d = a[:, d * c:(d + 1) * c]
        k_d = k * (1.0 + (a_d - 1.0) * ka_ref[...])
        outs_k[d][0] = k_d.astype(BF16)
        outs_b[d][0] = (a_d * kap).astype(BF16)
        k_sum = k_sum + k_d
    r_o[0] = r.astype(BF16)
    v_o[0] = v.astype(BF16)
    kap_o[0] = kap.astype(BF16)
    lwf_o[0] = logw[:, 0:c]
    lwb_o[0] = logw[:, c:2 * c]
    g_o[0] = g
    bonus_o[0] = _x_exact01(r * k_sum * rk_ref[...], bd) * v


def _blockdiag2(a, b):
    z1 = jnp.zeros((a.shape[0], b.shape[1]), a.dtype)
    z2 = jnp.zeros((b.shape[0], a.shape[1]), a.dtype)
    return jnp.concatenate([jnp.concatenate([a, z1], 1), jnp.concatenate([z2, b], 1)], 0)


def _rw_prep(p_rw, n_lat_tiles, mu, w0, w_up, a0, a_up, g_up, k_k, k_a, r_k):
    b, t, n = p_rw.shape
    nt = t // TOKEN_TILE
    c = MIX_W
    consts = [mu, w0.reshape(1, 2 * c), _blockdiag2(w_up[0], w_up[1]), a0.reshape(1, 2 * c), _blockdiag2(a_up[0], a_up[1]),
              g_up, k_k.reshape(1, c), k_a.reshape(1, c), r_k.reshape(1, c)]
    cspecs = [pl.BlockSpec(x.shape, lambda b, t: (0, 0)) for x in consts]
    tile = lambda w: pl.BlockSpec((1, TOKEN_TILE, w), lambda b, t: (b, t, 0))
    return pl.pallas_call(
        functools.partial(_rw_prep_kernel, n_lat_tiles=n_lat_tiles),
        grid=(b, nt),
        in_specs=[tile(n)] + _halo_specs(n, t) + cspecs,
        out_specs=[tile(c)] * 11,
        out_shape=[jax.ShapeDtypeStruct((b, t, c), dt) for dt in (BF16, BF16, BF16, F32, F32, BF16, BF16, BF16, BF16, F32, F32)],
        compiler_params=_cp("arbitrary", "arbitrary"),
        name="rw_prep",
    )(p_rw, p_rw, p_rw, *consts)


def _rw_scan_kernel(rf, vf, kapf, lwf, kf, bbf, rb, vb, kapb, lwb, kb, bbb, of_ref, ob_ref, s_scr):
    @pl.when(pl.program_id(1) == 0)
    def _():
        s_scr[...] = jnp.zeros_like(s_scr)

    c = rf.shape[1]
    pw_ = 2 * RW_HEAD_DIM
    n_pairs = MIX_W // pw_
    lane_a = lax.broadcasted_iota(jnp.int32, (c, pw_), 1) < RW_HEAD_DIM
    lane_a2 = lax.broadcasted_iota(jnp.int32, (2 * c, pw_), 1) < RW_HEAD_DIM
    blk = ((lax.broadcasted_iota(jnp.int32, (pw_, pw_), 0) < RW_HEAD_DIM)
           == (lax.broadcasted_iota(jnp.int32, (pw_, pw_), 1) < RW_HEAD_DIM))

    def sel(x):
        return jnp.where(lane_a, x[:c], x[c:])

    def bf(x):
        return x.astype(BF16)

    nb = rf.shape[0]
    chains = []
    for e in range(nb):
        for d, (r, v, kap, lw, k, b) in enumerate(((rf, vf, kapf, lwf, kf, bbf), (rb, vb, kapb, lwb, kb, bbb))):
            rev = d == 1
            incl = _tri(c, rev, False)
            strict = _tri(c, rev, True)
            logw = lw[e]
            cum = _mm_exact01(incl.astype(F32), logw)
            tot = cum[0:1] if rev else cum[c - 1:c]
            ginv = jnp.exp(-cum)
            gt = jnp.exp(tot - cum)
            rt = r[e] * jnp.exp(cum)
            kt = kap[e] * jnp.exp(cum - logw)
            kd, bd = k[e] * ginv, b[e] * ginv
            kg, bg = k[e] * gt, b[e] * gt
            dec = jnp.exp(tot)
            vv = v[e]
            for p in range(n_pairs):
                sl = slice(p * pw_, (p + 1) * pw_)
                lhs2 = jnp.concatenate([rt[:, sl], kt[:, sl]], 0)
                chains.append(dict(
                    e=e, d=d, p=p, idx=(e * 2 + d) * n_pairs + p, incl=incl, strict=strict,
                    lhs2=bf(lhs2),
                    lhs4=bf(jnp.concatenate([jnp.where(lane_a2, lhs2, 0.0), jnp.where(lane_a2, 0.0, lhs2)], 0)),
                    kd=bf(kd[:, sl]), bd=bf(bd[:, sl]), v=bf(vv[:, sl]),
                    kgbg=bf(jnp.concatenate([kg[:, sl], bg[:, sl]], 0)), dec=dec[:, sl]))

    dot = functools.partial(lax.dot_general, preferred_element_type=F32)
    nn = (((1,), (0,)), ((), ()))
    for ch in chains:
        ch["s"] = s_scr[ch["idx"]]
    for ch in chains:
        ch["a1k"] = dot(ch["lhs4"], ch["kd"], NT_DIMS)
        ch["a1b"] = dot(ch["lhs4"], ch["bd"], NT_DIMS)
        ch["a2"] = dot(ch["lhs2"], bf(ch["s"]), NT_DIMS)
    for ch in chains:
        a1k, a1b, st = ch["a1k"], ch["a1b"], ch["strict"]
        lk = jnp.where(jnp.concatenate([st, st], 0), jnp.concatenate([a1k[c:2 * c], a1k[3 * c:]], 0), 0.0)
        ch["lb_a"] = bf(jnp.where(st, a1b[c:2 * c], 0.0))
        ch["lb_b"] = bf(jnp.where(st, a1b[3 * c:], 0.0))
        ch["u"] = ch["a2"][c:] + sel(dot(bf(lk), ch["v"], nn))
    for ch in chains:
        ch["u"] = ch["u"] - sel(dot(jnp.concatenate([ch["lb_a"], ch["lb_b"]], 0), bf(ch["u"]), nn))
        ch["p_a"] = bf(dot(ch["lb_a"], ch["lb_a"], nn))
        ch["p_b"] = bf(dot(ch["lb_b"], ch["lb_b"], nn))
    n = int(math.log2(c)) - 1
    for i in range(n):
        for ch in chains:
            ch["u"] = ch["u"] + sel(dot(jnp.concatenate([ch["p_a"], ch["p_b"]], 0), bf(ch["u"]), nn))
            if i < n - 1:
                ch["p_a"] = bf(dot(ch["p_a"], ch["p_a"], nn))
                ch["p_b"] = bf(dot(ch["p_b"], ch["p_b"], nn))
    outs = [[[None] * n_pairs, [None] * n_pairs] for _ in range(nb)]
    for ch in chains:
        a1k, a1b, inc = ch["a1k"], ch["a1b"], ch["incl"]
        inc2 = jnp.concatenate([inc, inc], 0)
        rk = jnp.where(inc2, jnp.concatenate([a1k[:c], a1k[2 * c:3 * c]], 0), 0.0)
        rb_ = jnp.where(inc2, jnp.concatenate([a1b[:c], a1b[2 * c:3 * c]], 0), 0.0)
        ub = bf(ch["u"])
        o = ch["a2"][:c] + sel(dot(bf(rk), ch["v"], nn)) - sel(dot(bf(rb_), ub, nn))
        outs[ch["e"]][ch["d"]][ch["p"]] = o
        upd = dot(jnp.concatenate([ch["v"], -ub], 0), ch["kgbg"], TN_DIMS)
        s_scr[ch["idx"]] = ch["s"] * ch["dec"] + jnp.where(blk, upd, 0.0)
    for e in range(nb):
        of_ref[e] = jnp.concatenate(outs[e][0], axis=1)
        ob_ref[e] = jnp.concatenate(outs[e][1], axis=1)


def _chunk_orders(n_lat, n_ctx):
    n = n_lat + n_ctx
    fwd = lambda b, s: (b, (s + n_lat) % n, 0)
    rev = lambda b, s: (b, n - 1 - s, 0)
    return fwd, rev


def _rw_scan(r, v, kap, lwf, lwb, kf, kb, bf, bb, n_lat_rows):
    b, t, c = r.shape
    ch = SEQ_CHUNK
    n = t // ch
    fwd, rev = _chunk_orders(n_lat_rows // ch, (t - n_lat_rows) // ch)
    nb = RW_SCAN_BATCH
    blk = lambda im: pl.BlockSpec((nb, ch, c), im)
    return pl.pallas_call(
        _rw_scan_kernel,
        grid=(b // nb, n),
        in_specs=[blk(fwd)] * 6 + [blk(rev)] * 6,
        out_specs=[blk(fwd), blk(rev)],
        out_shape=[jax.ShapeDtypeStruct((b, t, c), F32)] * 2,
        scratch_shapes=[pltpu.VMEM((nb * RW_HEADS, 2 * RW_HEAD_DIM, 2 * RW_HEAD_DIM), F32)],
        compiler_params=_cp("arbitrary", "arbitrary"),
        name="rw_scan",
    )(r, v, kap, lwf, kf, bf, r, v, kap, lwb, kb, bb)


def _rw_readout(of, ob, bonus, g, ln_g, ln_b):
    o = of + ob
    bd = _head_blockdiag(MIX_W, RW_HEAD_DIM)
    mean = _x_exact01(o, bd) * (1.0 / RW_HEAD_DIM)
    oc = o - mean
    var = _x_exact01(oc * oc, bd) * (1.0 / RW_HEAD_DIM)
    on = oc * lax.rsqrt(var + RW_GN_EPS) * ln_g + ln_b
    return (on + bonus) * g


def _s5_matrices(a_re, a_im, log_dt, b_re, b_im, c_re, c_im):
    tc, g, n, p = S5_CHUNK, S5_GROUPS, S5_STATE, S5_GROUP
    hp = lax.Precision.HIGHEST
    dt = jnp.exp(log_dt)[:, :, None]
    lam_re, lam_im = a_re * dt, a_im * dt
    mag1 = jnp.exp(lam_re)
    ab_re, ab_im = mag1 * jnp.cos(lam_im), mag1 * jnp.sin(lam_im)
    nr, ni = ab_re - 1.0, ab_im
    den = a_re * a_re + a_im * a_im
    f_re, f_im = (nr * a_re + ni * a_im) / den, (ni * a_re - nr * a_im) / den
    tau = jnp.arange(tc + 1, dtype=F32)[:, None, None, None]
    pw_mag = jnp.exp(lam_re[None] * tau)
    pw_re, pw_im = pw_mag * jnp.cos(lam_im[None] * tau), pw_mag * jnp.sin(lam_im[None] * tau)
    fb_re = f_re[..., None] * b_re[None] - f_im[..., None] * b_im[None]
    fb_im = f_re[..., None] * b_im[None] + f_im[..., None] * b_re[None]
    e_re = pw_re[..., None] * fb_re[None] - pw_im[..., None] * fb_im[None]
    e_im = pw_re[..., None] * fb_im[None] + pw_im[..., None] * fb_re[None]
    m = (jnp.einsum('gpn,tdgnq->tdgpq', c_re, e_re[:tc], precision=hp)
         - jnp.einsum('gpn,tdgnq->tdgpq', c_im, e_im[:tc], precision=hp))
    i = jnp.arange(tc)
    lag = i[None, :] - i[:, None]
    mf = jnp.where((lag >= 0)[:, :, None, None, None], m[jnp.clip(lag, 0, tc - 1), 0], 0.0)
    mb = jnp.where((lag <= 0)[:, :, None, None, None], m[jnp.clip(-lag, 0, tc - 1), 1], 0.0)
    w = jnp.transpose(mf + mb, (2, 0, 4, 1, 3)).reshape(g, tc * p, tc * p)
    idx_f = tc - 1 - i
    pf_re, pf_im = e_re[idx_f, 0], e_im[idx_f, 0]
    pb_re, pb_im = e_re[i, 1], e_im[i, 1]
    pm = jnp.concatenate([pf_re, pb_re, pf_im, pb_im], axis=2)
    pm = jnp.transpose(pm, (1, 0, 3, 2)).reshape(g, tc * p, 4 * n)
    cf_re = c_re[None, :, :, :] * pw_re[i + 1, 0][:, :, None, :] - c_im[None] * pw_im[i + 1, 0][:, :, None, :]
    cf_im = c_re[None] * pw_im[i + 1, 0][:, :, None, :] + c_im[None] * pw_re[i + 1, 0][:, :, None, :]
    cb_re = c_re[None] * pw_re[tc - i, 1][:, :, None, :] - c_im[None] * pw_im[tc - i, 1][:, :, None, :]
    cb_im = c_re[None] * pw_im[tc - i, 1][:, :, None, :] + c_im[None] * pw_re[tc - i, 1][:, :, None, :]
    q = jnp.concatenate([cf_re, cb_re, -cf_im, -cb_im], axis=3)
    q = jnp.transpose(q, (1, 3, 0, 2)).reshape(g, 4 * n, tc * p)
    dec_re = jnp.concatenate([pw_re[tc, 0], pw_re[tc, 1]], axis=1)
    dec_im = jnp.concatenate([pw_im[tc, 0], pw_im[tc, 1]], axis=1)
    dec = jnp.stack([dec_re, dec_im], axis=1)
    return w.astype(BF16), pm.astype(BF16), q.astype(BF16), dec


S5_PACK_BATCH = 2
S5_CORE_GROUPS = 2
S5_LANE_GROUPS = 128 // S5_GROUP


def _s5_pack_kernel(x_ref, o_ref):
    rows = o_ref.shape[0]
    grp = lax.broadcasted_iota(jnp.int32, (rows, 128), 1) // S5_GROUP
    tiles = [jnp.zeros((rows, 128), F32) for _ in range(2 * S5_LANE_GROUPS)]
    for j in range(S5_CHUNK):
        xj = x_ref[pl.ds(j, rows, stride=S5_CHUNK), :]
        jl, jh = j % S5_LANE_GROUPS, j // S5_LANE_GROUPS
        for g8 in range(S5_LANE_GROUPS):
            sh = ((jl - g8) * S5_GROUP) % 128
            moved = pltpu.roll(xj, sh, 1) if sh else xj
            tiles[2 * g8 + jh] = jnp.where(grp == jl, moved, tiles[2 * g8 + jh])
    o_ref[...] = jnp.concatenate(tiles, axis=1).astype(o_ref.dtype)


def _s5_unpack_kernel(y_ref, o_ref):
    rows = y_ref.shape[0]
    grp = lax.broadcasted_iota(jnp.int32, (rows, 128), 1) // S5_GROUP
    for i in range(S5_CHUNK):
        il, ih = i % S5_LANE_GROUPS, i // S5_LANE_GROUPS
        out = jnp.zeros((rows, 128), F32)
        for g8 in range(S5_LANE_GROUPS):
            src = y_ref[:, (2 * g8 + ih) * 128:(2 * g8 + ih + 1) * 128]
            sh = ((g8 - il) * S5_GROUP) % 128
            moved = pltpu.roll(src, sh, 1) if sh else src
            out = jnp.where(grp == g8, moved, out)
        o_ref[pl.ds(i, rows, stride=S5_CHUNK), :] = out


def _s5_kernel(u_ref, w_ref, p_ref, q_ref, dec_ref, y_ref, inc_re, inc_im, sf_re, sf_im, sb_re, sb_im, *, n_lat, n_ctx, bsz):
    gs = w_ref.shape[0]
    k = w_ref.shape[1]
    n2 = 2 * S5_STATE
    us = [u_ref[:, g * k:(g + 1) * k] for g in range(gs)]
    for g in range(gs):
        inc = jnp.dot(us[g], p_ref[g], preferred_element_type=F32)
        inc_re[g] = inc[:, :n2]
        inc_im[g] = inc[:, n2:]
    n = n_lat + n_ctx
    is_f = lax.broadcasted_iota(jnp.int32, (bsz, n2), 1) < S5_STATE

    def body(s, carry):
        rf = pl.ds((s + n_lat) % n, bsz, stride=n)
        rb = pl.ds(n - 1 - s, bsz, stride=n)
        new = []
        for g in range(gs):
            s_re, s_im = carry[g]
            d_re, d_im = dec_ref[g, 0:1, :], dec_ref[g, 1:2, :]
            sf_re[g, rf, :] = s_re
            sf_im[g, rf, :] = s_im
            sb_re[g, rb, :] = s_re
            sb_im[g, rb, :] = s_im
            i_re = jnp.where(is_f, inc_re[g, rf, :], inc_re[g, rb, :])
            i_im = jnp.where(is_f, inc_im[g, rf, :], inc_im[g, rb, :])
            new.append((d_re * s_re - d_im * s_im + i_re, d_re * s_im + d_im * s_re + i_im))
        return tuple(new)

    zero = jnp.zeros((bsz, n2), F32)
    lax.fori_loop(0, n, body, tuple((zero, zero) for _ in range(gs)))
    is_f_all = lax.broadcasted_iota(jnp.int32, sf_re.shape[1:], 1) < S5_STATE
    for g in range(gs):
        s_all = jnp.concatenate([jnp.where(is_f_all, sf_re[g], sb_re[g]),
                                 jnp.where(is_f_all, sf_im[g], sb_im[g])], axis=1).astype(BF16)
        y_ref[:, g * k:(g + 1) * k] = (jnp.dot(us[g], w_ref[g], preferred_element_type=F32)
                                       + jnp.dot(s_all, q_ref[g], preferred_element_type=F32))


def _s5_core(p_s5, n_lat_rows, mats):
    b, t, c = p_s5.shape
    tc, g = S5_CHUNK, S5_GROUPS
    nch = t // tc
    w, pm, q, dec = mats
    k = tc * S5_GROUP
    nbb = S5_PACK_BATCH
    gs = S5_CORE_GROUPS
    n_tiles = c // 128
    tok_blk = pl.BlockSpec((nbb * t, 128), lambda i, o: (i, o))
    chk_blk = pl.BlockSpec((nbb * nch, S5_LANE_GROUPS * k), lambda i, o: (i, o))
    u = pl.pallas_call(
        _s5_pack_kernel,
        grid=(b // nbb, n_tiles),
        in_specs=[tok_blk],
        out_specs=chk_blk,
        out_shape=jax.ShapeDtypeStruct((b * nch, g * k), BF16),
        compiler_params=_cp("arbitrary", "arbitrary"),
        name="s5_pack",
    )(p_s5.reshape(b * t, c))
    rows = nch * b
    y = pl.pallas_call(
        functools.partial(_s5_kernel, n_lat=n_lat_rows // tc, n_ctx=(t - n_lat_rows) // tc, bsz=b),
        grid=(g // gs,),
        in_specs=[pl.BlockSpec((rows, gs * k), lambda i: (0, i)),
                  pl.BlockSpec((gs, k, k), lambda i: (i, 0, 0)),
                  pl.BlockSpec((gs, k, 4 * S5_STATE), lambda i: (i, 0, 0)),
                  pl.BlockSpec((gs, 4 * S5_STATE, k), lambda i: (i, 0, 0)),
                  pl.BlockSpec((gs, 2, 2 * S5_STATE), lambda i: (i, 0, 0))],
        out_specs=pl.BlockSpec((rows, gs * k), lambda i: (0, i)),
        out_shape=jax.ShapeDtypeStruct((rows, g * k), F32),
        scratch_shapes=[pltpu.VMEM((gs, rows, 2 * S5_STATE), F32)] * 6,
        compiler_params=_cp("arbitrary"),
        name="s5_core",
    )(u, w, pm, q, dec)
    out = pl.pallas_call(
        _s5_unpack_kernel,
        grid=(b // nbb, n_tiles),
        in_specs=[chk_blk],
        out_specs=tok_blk,
        out_shape=jax.ShapeDtypeStruct((b * t, c), F32),
        compiler_params=_cp("arbitrary", "arbitrary"),
        name="s5_unpack",
    )(y)
    return out.reshape(b, t, c)


def _s5_output(y, u, d_skip, glu_w, glu_b):
    z = y + d_skip * u
    z = 0.5 * z * (1.0 + jnp.tanh(math.sqrt(2.0 / math.pi) * (z + 0.044715 * (z * z * z))))
    return z * _sigmoid(_mm(z, glu_w) + glu_b)


def _gla_scan_kernel(pfv_ref, pfc_ref, pbv_ref, pbc_ref, gup_ref, gb_ref, of_ref, ob_ref, s_scr, *, n_lat, n_ctx, cols, c):
    step = pl.program_id(1)

    @pl.when(step == 0)
    def _():
        s_scr[...] = jnp.zeros_like(s_scr)

    subs = pfc_ref.shape[1] // c
    n = n_lat + n_ctx
    block_ids = ((step + n_lat) % n, n - 1 - step)
    pw_ = 2 * GLA_HK
    n_pairs = GLA_DK // pw_
    lane_a = lax.broadcasted_iota(jnp.int32, (c, pw_), 1) < GLA_HK
    lane_av = lax.broadcasted_iota(jnp.int32, (GLA_HV, pw_), 1) < GLA_HK
    dot = functools.partial(lax.dot_general, preferred_element_type=F32)
    nn = (((1,), (0,)), ((), ()))
    nb = pfc_ref.shape[0]
    mask2 = lambda a: jnp.concatenate([jnp.where(lane_a, a, 0.0), jnp.where(lane_a, 0.0, a)], 0).astype(BF16)
    for sub in range(subs):
        chains = []
        for e in range(nb):
            for d, (v_ref, c_ref) in enumerate(((pfv_ref, pfc_ref), (pbv_ref, pbc_ref))):
                rev = d == 1
                cs = subs - 1 - sub if rev else sub
                incl = _tri(c, rev, False)
                lat_rows = jnp.concatenate([v_ref[e, :, cs * cols + j, :] for j in range(cols)], axis=0)
                pe = jnp.where(block_ids[d] < n_lat, lat_rows, c_ref[e, cs * c:(cs + 1) * c, :])
                gd = pe[:, 2 * GLA_DK + GLA_DV:]
                x = _mm(gd, gup_ref[...])[:, d * GLA_DK:(d + 1) * GLA_DK] + gb_ref[0:1, d * GLA_DK:(d + 1) * GLA_DK]
                logg = (jnp.minimum(x, 0.0) - jnp.log(1.0 + jnp.exp(-jnp.abs(x)))) * (1.0 / GLA_GATE_NORM)
                cum = _mm_exact01(incl.astype(F32), logg)
                mid = c // 2
                ref = cum[mid:mid + 1] if rev else cum[mid - 1:mid]
                tot = cum[0:1] if rev else cum[c - 1:c]
                q = pe[:, 0:GLA_DK] * (GLA_HK ** -0.5)
                k = pe[:, GLA_DK:2 * GLA_DK]
                qs = q * jnp.exp(cum - ref)
                ks = k * jnp.exp(ref - cum)
                qc = q * jnp.exp(cum)
                kg = k * jnp.exp(tot - cum)
                dec = jnp.exp(tot)
                for p in range(n_pairs):
                    sl = slice(p * pw_, (p + 1) * pw_)
                    chains.append(dict(
                        e=e, d=d, p=p, idx=(e * 2 + d) * n_pairs + p, incl=incl, qs4=mask2(qs[:, sl]), qc4=mask2(qc[:, sl]),
                        ks=ks[:, sl].astype(BF16), kg=kg[:, sl].astype(BF16), dec=dec[:, sl],
                        va=pe[:, 2 * GLA_DK + 2 * p * GLA_HV:2 * GLA_DK + (2 * p + 1) * GLA_HV].astype(BF16),
                        vb=pe[:, 2 * GLA_DK + (2 * p + 1) * GLA_HV:2 * GLA_DK + (2 * p + 2) * GLA_HV].astype(BF16)))
        for ch in chains:
            ch["s"] = s_scr[ch["idx"]]
        for ch in chains:
            ch["sc"] = dot(ch["qs4"], ch["ks"], NT_DIMS)
            ch["st"] = dot(ch["qc4"], ch["s"].astype(BF16), NT_DIMS)
        outs = [[[None] * GLA_HEADS, [None] * GLA_HEADS] for _ in range(nb)]
        for ch in chains:
            e, d, p = ch["e"], ch["d"], ch["p"]
            sa = jnp.where(ch["incl"], ch["sc"][:c], 0.0).astype(BF16)
            sb = jnp.where(ch["incl"], ch["sc"][c:], 0.0).astype(BF16)
            outs[e][d][2 * p] = ch["st"][:c] + dot(sa, ch["va"], nn)
            outs[e][d][2 * p + 1] = ch["st"][c:] + dot(sb, ch["vb"], nn)
            upd = jnp.where(lane_av, dot(ch["va"], ch["kg"], TN_DIMS), dot(ch["vb"], ch["kg"], TN_DIMS))
            s_scr[ch["idx"]] = ch["s"] * ch["dec"] + upd
        for e in range(nb):
            of_ref[e, sub * c:(sub + 1) * c, :] = jnp.concatenate(outs[e][0], axis=1)
            ob_ref[e, (subs - 1 - sub) * c:(subs - sub) * c, :] = jnp.concatenate(outs[e][1], axis=1)


def _gla_scan(p_gla, n_lat_rows, gate_up, gate_b):
    b, t, n = p_gla.shape
    ch = SEQ_CHUNK
    grid_rows = n_lat_rows // GRID_W
    cols = _grid_cols(n_lat_rows, ch)
    blk_rows = GRID_COL_BLOCK * grid_rows
    assert GRID_COL_BLOCK % cols == 0 and blk_rows % ch == 0 and (t - n_lat_rows) % blk_rows == 0
    n_lat, n_ctx = n_lat_rows // blk_rows, (t - n_lat_rows) // blk_rows
    n_blk = n_lat + n_ctx
    fwd, rev = _chunk_orders(n_lat, n_ctx)
    gup = jnp.zeros((128, 2 * GLA_DK), F32)
    gup = gup.at[0:GLA_GATE_RANK, 0:GLA_DK].set(gate_up[0]).at[GLA_GATE_RANK:2 * GLA_GATE_RANK, GLA_DK:].set(gate_up[1])
    nb = GLA_SCAN_BATCH
    p_view = p_gla.reshape(b, t // GRID_W, GRID_W, n)
    view_blk = lambda bid: pl.BlockSpec((nb, grid_rows, GRID_COL_BLOCK, n), lambda b, s: (b, 0, jnp.minimum(bid(s), n_lat - 1), 0))
    ctx_blk = lambda bid: pl.BlockSpec((nb, blk_rows, n), lambda b, s: (b, jnp.maximum(bid(s), n_lat), 0))
    bid_f = lambda s: (s + n_lat) % n_blk
    bid_b = lambda s: n_blk - 1 - s
    return pl.pallas_call(
        functools.partial(_gla_scan_kernel, n_lat=n_lat, n_ctx=n_ctx, cols=cols, c=ch),
        grid=(b // nb, n_blk),
        in_specs=[view_blk(bid_f), ctx_blk(bid_f), view_blk(bid_b), ctx_blk(bid_b),
                  pl.BlockSpec((128, 2 * GLA_DK), lambda b, s: (0, 0)),
                  pl.BlockSpec((1, 2 * GLA_DK), lambda b, s: (0, 0))],
        out_specs=[pl.BlockSpec((nb, blk_rows, GLA_DV), fwd), pl.BlockSpec((nb, blk_rows, GLA_DV), rev)],
        out_shape=[jax.ShapeDtypeStruct((b, t, GLA_DV), F32)] * 2,
        scratch_shapes=[pltpu.VMEM((nb * GLA_HEADS, GLA_HV, 2 * GLA_HK), F32)],
        compiler_params=_cp("arbitrary", "arbitrary"),
        name="gla_scan",
    )(p_view, p_gla, p_view, p_gla, gup, gate_b.reshape(1, 2 * GLA_DK))


def _gla_readout_kernel(of_ref, ob_ref, gv_ref, gc_ref, ng_ref, ylat_ref, yctx_ref, *, n_lat_tiles, cols):
    t = pl.program_id(1)
    o = of_ref[0] + ob_ref[0]
    bd = _head_blockdiag(GLA_DV, GLA_HV)
    ms = _x_exact01(o * o, bd) * (1.0 / GLA_HV)
    g = jnp.where(t < n_lat_tiles, _col_major_rows(gv_ref, 0, 0, cols, 0, GLA_DV), gc_ref[0])
    y = (o * lax.rsqrt(ms + RMS_EPS) * ng_ref[...] * _silu(g)).astype(ylat_ref.dtype)
    rows = y.shape[0] // cols

    @pl.when(t < n_lat_tiles)
    def _():
        for j in range(cols):
            ylat_ref[0, :, j, :] = y[j * rows:(j + 1) * rows]

    @pl.when(t >= n_lat_tiles)
    def _():
        yctx_ref[0] = y


def _gla_readout(of, ob, g_gla, norm_g, n_lat_rows):
    b, t, c = of.shape
    n_lat_tiles = n_lat_rows // TOKEN_TILE
    grid_rows = n_lat_rows // GRID_W
    cols = _grid_cols(n_lat_rows, TOKEN_TILE)
    assert cols == GRID_COL_BLOCK
    tile = pl.BlockSpec((1, TOKEN_TILE, c), lambda b, t: (b, t, 0))
    g_view = g_gla.reshape(b, t // GRID_W, GRID_W, c)
    y_lat, y_ctx = pl.pallas_call(
        functools.partial(_gla_readout_kernel, n_lat_tiles=n_lat_tiles, cols=cols),
        grid=(b, t // TOKEN_TILE),
        in_specs=[tile, tile,
                  pl.BlockSpec((1, grid_rows, cols, c), lambda b, t: (b, 0, jnp.minimum(t, n_lat_tiles - 1), 0)),
                  pl.BlockSpec((1, TOKEN_TILE, c), lambda b, t: (b, jnp.maximum(t, n_lat_tiles), 0)),
                  pl.BlockSpec((1, c), lambda b, t: (0, 0))],
        out_specs=[pl.BlockSpec((1, grid_rows, cols, c), lambda b, t: (b, 0, jnp.minimum(t, n_lat_tiles - 1), 0)),
                   pl.BlockSpec((1, TOKEN_TILE, c), lambda b, t: (b, 0, 0))],
        out_shape=[jax.ShapeDtypeStruct((b, grid_rows, GRID_W, c), BF16), jax.ShapeDtypeStruct((b, t - n_lat_rows, c), BF16)],
        compiler_params=_cp("arbitrary", "arbitrary"),
        name="gla_readout",
    )(of, ob, g_view, g_gla, jnp.tile(norm_g, GLA_HEADS).reshape(1, c))
    return y_lat.reshape(b, n_lat_rows, c), y_ctx


def _merge_kernel(h_ref, mod_ref, gate_ref, yhyl_ref, yhyc_ref, rof_ref, rob_ref, rbonus_ref, rg_ref, ys5_ref, us5_ref,
                  yglal_ref, yglac_ref, lng_ref, lnb_ref, dskip_ref, gluw_ref, glub_ref, wbr_ref, wout_ref, *rest,
                  n_lat_tiles, route):
    if route:
        gffn_ref, rw_ref, rb_ref, o_ref, xn_ref, logit_ref = rest
    else:
        (o_ref,) = rest
    d = D_MODEL
    is_lat = pl.program_id(1) < n_lat_tiles
    y_hy = jnp.where(is_lat, yhyl_ref[0], yhyc_ref[0])
    y_gla = jnp.where(is_lat, yglal_ref[0], yglac_ref[0])
    y_rw = _rw_readout(rof_ref[0], rob_ref[0], rbonus_ref[0], rg_ref[0], lng_ref[...], lnb_ref[...])
    y_s5 = _s5_output(ys5_ref[0], us5_ref[0], dskip_ref[...], gluw_ref[...], glub_ref[...])
    acc = jnp.zeros((h_ref.shape[1], d), F32)
    for m, y in enumerate((y_hy, y_rw, y_s5, y_gla)):
        acc = acc + (_sigmoid(gate_ref[0, :, m * d:(m + 1) * d].astype(F32))
                     * jnp.dot(y.astype(BF16), wbr_ref[m], preferred_element_type=F32))
    out = jnp.dot(acc.astype(BF16), wout_ref[...], preferred_element_type=F32)
    h = h_ref[0] + mod_ref[0, 0, 2:3, :] * out
    o_ref[0] = h
    if route:
        xm = _norm_mod(h, gffn_ref[...], mod_ref[0, 0], 3)
        xn_ref[0] = xm.astype(BF16)
        logit_ref[0] = _mm_hp(xm, rw_ref[...]) + rb_ref[...]


def _merge(h_all, modsel, n_tiles, n_lat_tiles, gate, y_hy_lat, y_hy_ctx, rof, rob, rbonus, rg, y_s5, u_s5, y_gla_lat, y_gla_ctx,
           ln_g, ln_b, d_skip, glu_w, glu_b, w_branch, w_out, router=None):
    b, t, d = h_all.shape
    c = MIX_W
    tile = lambda w: pl.BlockSpec((1, TOKEN_TILE, w), lambda b, t: (b, t, 0))
    lat = pl.BlockSpec((1, TOKEN_TILE, c), lambda b, t: (b, jnp.minimum(t, n_lat_tiles - 1), 0))
    ctx = pl.BlockSpec((1, TOKEN_TILE, c), lambda b, t: (b, jnp.maximum(t - n_lat_tiles, 0), 0))
    vec = lambda x: x.reshape(1, -1)
    consts = [vec(ln_g), vec(ln_b), vec(d_skip), glu_w.astype(BF16), vec(glu_b), w_branch.astype(BF16), w_out.astype(BF16)]
    out_specs = [tile(d)]
    out_shape = [jax.ShapeDtypeStruct((b, t, d), F32)]
    if router is not None:
        g_ffn, router_w, router_b = router
        consts += [vec(g_ffn), jnp.pad(router_w, ((0, 0), (0, 128 - N_EXPERTS))),
                   jnp.pad(router_b, (0, 128 - N_EXPERTS), constant_values=-1e30).reshape(1, 128)]
        out_specs += [tile(d), tile(128)]
        rows = n_tiles * TOKEN_TILE
        out_shape += [jax.ShapeDtypeStruct((b, rows, d), BF16), jax.ShapeDtypeStruct((b, rows, 128), F32)]
    cspecs = [pl.BlockSpec(x.shape, (lambda n: lambda b, t: (0,) * n)(x.ndim)) for x in consts]
    outs = pl.pallas_call(
        functools.partial(_merge_kernel, n_lat_tiles=n_lat_tiles, route=router is not None),
        grid=(b, n_tiles),
        in_specs=[tile(d), pl.BlockSpec((1, 1, 6, d), _mod_index(n_lat_tiles)), tile(4 * d), lat, ctx] + [tile(c)] * 6
                 + [lat, ctx] + cspecs,
        out_specs=out_specs,
        out_shape=out_shape,
        input_output_aliases={0: 0},
        compiler_params=_cp("arbitrary", "arbitrary"),
        name="merge",
    )(h_all, modsel, gate, y_hy_lat, y_hy_ctx, rof, rob, rbonus, rg, y_s5, u_s5, y_gla_lat, y_gla_ctx, *consts)
    return outs if router is not None else outs[0]


def _ffn_kernel(h_ref, g_ref, mod_ref, w1_ref, w3_ref, w2_ref, o_ref):
    h = h_ref[0]
    xm = _norm_mod(h, g_ref[...], mod_ref[0, 0], 3).astype(BF16)
    a = jnp.dot(xm, w1_ref[...], preferred_element_type=F32)
    b = jnp.dot(xm, w3_ref[...], preferred_element_type=F32)
    y = jnp.dot((_silu(a) * b).astype(BF16), w2_ref[...], preferred_element_type=F32)
    o_ref[0] = h + mod_ref[0, 0, 5:6, :] * y


def _ffn_dense(h_all, g, modsel, n_lat_tiles, w1, w3, w2):
    b, t, d = h_all.shape
    tile = pl.BlockSpec((1, TOKEN_TILE, d), lambda b, t: (b, t, 0))
    ws = [w1.astype(BF16), w3.astype(BF16), w2.astype(BF16)]
    wspec = [pl.BlockSpec(w.shape, lambda b, t: (0, 0), pipeline_mode=pl.Buffered(1)) for w in ws]
    return pl.pallas_call(
        _ffn_kernel,
        grid=(b, t // TOKEN_TILE),
        in_specs=[tile, pl.BlockSpec((1, d), lambda b, t: (0, 0)), pl.BlockSpec((1, 1, 6, d), _mod_index(n_lat_tiles))] + wspec,
        out_specs=tile,
        out_shape=jax.ShapeDtypeStruct((b, t, d), F32),
        input_output_aliases={0: 0},
        compiler_params=_cp("arbitrary", "arbitrary"),
        name="ffn_dense",
    )(h_all, g.reshape(1, d), modsel, *ws)


def _moe_expert_kernel(be_ref, x_ref, w1_ref, w3_ref, w2_ref, o_ref):
    x = x_ref[...]
    a = jnp.dot(x, w1_ref[0], preferred_element_type=F32)
    b = jnp.dot(x, w3_ref[0], preferred_element_type=F32)
    o_ref[...] = jnp.dot((_silu(a) * b).astype(BF16), w2_ref[0], preferred_element_type=F32).astype(o_ref.dtype)


def _moe_experts(xb, block_exp, w1, w3, w2):
    rows, d = xb.shape
    nb = rows // MOE_BLOCK_ROWS
    f = w1.shape[2]
    grid_spec = pltpu.PrefetchScalarGridSpec(
        num_scalar_prefetch=1,
        grid=(nb,),
        in_specs=[pl.BlockSpec((MOE_BLOCK_ROWS, d), lambda i, be: (i, 0)),
                  pl.BlockSpec((1, d, f), lambda i, be: (be[i], 0, 0), pipeline_mode=pl.Buffered(1)),
                  pl.BlockSpec((1, d, f), lambda i, be: (be[i], 0, 0), pipeline_mode=pl.Buffered(1)),
                  pl.BlockSpec((1, f, d), lambda i, be: (be[i], 0, 0), pipeline_mode=pl.Buffered(1))],
        out_specs=pl.BlockSpec((MOE_BLOCK_ROWS, d), lambda i, be: (i, 0)),
    )
    return pl.pallas_call(
        _moe_expert_kernel,
        grid_spec=grid_spec,
        out_shape=jax.ShapeDtypeStruct((rows, d), BF16),
        compiler_params=_cp("arbitrary"),
        name="moe_experts",
    )(block_exp, xb, w1, w3, w2)


def _moe_combine_kernel(h_ref, mod_ref, y0_ref, y1_ref, gate_ref, fg_ref, o_ref):
    y = y0_ref[0, 0].astype(F32) * gate_ref[0, :, 0:1] + y1_ref[0, 0].astype(F32) * gate_ref[0, :, 1:2]
    h = h_ref[0] + mod_ref[0, 0, 5:6, :] * y
    o_ref[0] = h * lax.rsqrt(jnp.mean(h * h, axis=-1, keepdims=True) + RMS_EPS) * fg_ref[...]


def _moe_combine(h_all, modsel, n_lat_tiles, y2, gates, final_g):
    b, t, d = h_all.shape
    n_lat = n_lat_tiles * TOKEN_TILE
    tile = lambda w: pl.BlockSpec((1, TOKEN_TILE, w), lambda b, t: (b, t, 0))
    pick = lambda k: pl.BlockSpec((1, 1, TOKEN_TILE, d), lambda b, t: (k, b, t, 0))
    return pl.pallas_call(
        _moe_combine_kernel,
        grid=(b, n_lat_tiles),
        in_specs=[tile(d), pl.BlockSpec((1, 1, 6, d), _mod_index(n_lat_tiles)), pick(0), pick(1),
                  tile(128), pl.BlockSpec((1, d), lambda b, t: (0, 0))],
        out_specs=tile(d),
        out_shape=jax.ShapeDtypeStruct((b, n_lat, d), F32),
        compiler_params=_cp("arbitrary", "arbitrary"),
        name="moe_combine",
    )(h_all, modsel, y2, y2, gates, final_g.reshape(1, d))


def _moe_ffn_and_final_norm(h_all, xn, logits, modsel, n_lat_tiles, w1, w3, w2, final_g):
    b, t, d = h_all.shape
    n_lat = n_lat_tiles * TOKEN_TILE
    n_tok = b * n_lat
    logits = logits.reshape(n_tok, 128)[:, :N_EXPERTS]
    top_v, top_i = lax.top_k(logits, TOP_K)
    gates = jax.nn.softmax(top_v, axis=-1)
    n_assign = n_tok * TOP_K
    e_flat = top_i.reshape(-1).astype(jnp.int32)
    order = jnp.argsort(e_flat).astype(jnp.int32)
    rank = jnp.argsort(order).astype(jnp.int32)
    experts = jnp.arange(N_EXPERTS, dtype=jnp.int32)[None, :]
    is_e = e_flat[:, None] == experts
    counts = jnp.sum(is_e.astype(jnp.int32), axis=0)
    padded = (counts + MOE_BLOCK_ROWS - 1) // MOE_BLOCK_ROWS * MOE_BLOCK_ROWS
    start = jnp.cumsum(counts) - counts
    pend = jnp.cumsum(padded)
    pstart = pend - padded
    n_blocks = n_assign // MOE_BLOCK_ROWS + N_EXPERTS
    n_rows = n_blocks * MOE_BLOCK_ROWS
    row_ids = jnp.arange(n_rows, dtype=jnp.int32)
    row_exp = jnp.minimum(jnp.sum((row_ids[:, None] >= pend[None, :]).astype(jnp.int32), axis=1), N_EXPERTS - 1)
    row_is_e = row_exp[:, None] == experts
    lookup = lambda onehot, table: jnp.sum(jnp.where(onehot, table[None, :], 0), axis=1)
    row_off = row_ids - lookup(row_is_e, pstart)
    row_slot = jnp.clip(lookup(row_is_e, start) + row_off, 0, n_assign - 1)
    row_tok = jnp.where(row_off < lookup(row_is_e, counts), order[row_slot] // TOP_K, n_tok).astype(jnp.int32)
    block_exp = row_exp[::MOE_BLOCK_ROWS]
    tok_pad = jnp.concatenate([xn.reshape(n_tok, d), jnp.zeros((1, d), BF16)], axis=0)
    xb = tok_pad.at[row_tok].get(mode="promise_in_bounds")
    yb = _moe_experts(xb, block_exp, w1.astype(BF16), w3.astype(BF16), w2.astype(BF16))
    dest_orig = (lookup(is_e, pstart) + rank - lookup(is_e, start)).astype(jnp.int32)
    dest_by_k = dest_orig.reshape(n_tok, TOP_K).T.reshape(-1)
    y2 = yb.at[dest_by_k].get(mode="promise_in_bounds").reshape(TOP_K, b, n_lat, d)
    gates_pad = jnp.pad(gates, ((0, 0), (0, 128 - TOP_K))).reshape(b, n_lat, 128)
    return _moe_combine(h_all, modsel, n_lat_tiles, y2, gates_pad, final_g)


def _pad_cols(w, n):
    return jnp.pad(w, ((0, 0), (0, n - w.shape[1])))


def kernel(x, c, ctx, c_ctx, ada_w, ada_b, norm_mix_g, norm_ffn_g, w_in, hy_conv_w, hy_conv_b, hy_f_w1, hy_f_b1, hy_f_w2, hy_f_b2, hy_f_w3, hy_f_b3, hy_bias, rw_mu, rw_w0, rw_w_up, rw_a0, rw_a_up, rw_g_up, rw_k_k, rw_k_a, rw_r_k, rw_ln_g, rw_ln_b, s5_a_re, s5_a_im, s5_log_dt, s5_b_re, s5_b_im, s5_c_re, s5_c_im, s5_d, s5_glu_w, s5_glu_b, gla_gate_up, gla_gate_b, gla_norm_g, w_branch, w_out, ffn_w1, ffn_w3, ffn_w2, moe_router_w, moe_router_b, moe_w1, moe_w3, moe_w2, final_norm_g):
    bsz, n_lat, d = x.shape
    n_ctx = ctx.shape[1]
    depth = w_in.shape[0]
    assert depth == 2 and d == D_MODEL
    assert n_lat % TOKEN_TILE == 0 and n_ctx % TOKEN_TILE == 0 and n_lat % GRID_W == 0
    t_all = n_lat + n_ctx
    n_lat_tiles = n_lat // TOKEN_TILE
    n_tiles = t_all // TOKEN_TILE
    assert n_tiles <= 2 * n_lat_tiles

    rows = (bsz + 1 + 7) // 8 * 8
    c_all = jnp.zeros((rows, d), F32).at[:bsz].set(c.astype(F32)).at[bsz].set(c_ctx.astype(F32))
    mod = _ada_mod(c_all, ada_w, ada_b)
    h_all = jnp.concatenate([x.astype(F32), ctx.astype(F32)], axis=1)

    dft_lat = _dft_tables(n_lat)
    dft_ctx = _dft_tables(n_ctx)
    seg = np.cumsum([0, 3 * MIX_W, RW_IN, MIX_W, 2 * GLA_DK + 2 * GLA_DV + 2 * GLA_GATE_RANK, 4 * d])

    out = None
    for l in range(depth):
        last = l == depth - 1
        mod_l = mod[l, :bsz].reshape(bsz, 6, d)
        mod_c = jnp.broadcast_to(mod[l, bsz].reshape(1, 6, d), (bsz, 6, d))
        modsel = jnp.stack([mod_l, mod_c], axis=1)

        wl = w_in[l].astype(BF16)
        g_lo, g_hi = seg[3] + 2 * GLA_DK + GLA_DV, seg[3] + 2 * GLA_DK + 2 * GLA_DV
        w_gla = _pad_cols(jnp.concatenate([wl[:, seg[3]:g_lo], wl[:, g_hi:seg[4]]], axis=1), GLA_IN_PAD)
        ws = [wl[:, seg[0]:seg[1]], wl[:, seg[1]:seg[2]], wl[:, seg[2]:seg[3]], w_gla, wl[:, g_lo:g_hi], wl[:, seg[4]:seg[5]]]
        hy_x0, hy_z, p_rw, p_s5, p_gla, g_gla, gate = _in_proj(h_all, norm_mix_g[l], modsel, ws, hy_conv_w[l], hy_conv_b[l],
                                                               n_lat_tiles)

        filt_args = (hy_f_w1[l], hy_f_b1[l], hy_f_w2[l], hy_f_b2[l], hy_f_w3[l], hy_f_b3[l])
        spec_lat = _hy_spectrum(_hy_filter(n_lat, *filt_args), dft_lat[0], dft_lat[1])
        y_hy_lat = _hyena(hy_z, hy_x0, 0, n_lat, hy_bias[l], dft_lat, spec_lat)
        y_hy_ctx = y_hy_lat
        if not last:
            spec_ctx = _hy_spectrum(_hy_filter(n_ctx, *filt_args), dft_ctx[0], dft_ctx[1])
            y_hy_ctx = _hyena(hy_z, hy_x0, n_lat // n_ctx, n_ctx, hy_bias[l], dft_ctx, spec_ctx)

        r, v, kap, lwf, lwb, kf, kb, bf, bb, rg, rbonus = _rw_prep(
            p_rw, n_lat_tiles, rw_mu[l], rw_w0[l], rw_w_up[l], rw_a0[l], rw_a_up[l], rw_g_up[l], rw_k_k[l], rw_k_a[l], rw_r_k[l])
        rof, rob = _rw_scan(r, v, kap, lwf, lwb, kf, kb, bf, bb, n_lat)

        y_s5 = _s5_core(p_s5, n_lat, _s5_matrices(s5_a_re[l], s5_a_im[l], s5_log_dt[l], s5_b_re[l], s5_b_im[l], s5_c_re[l], s5_c_im[l]))

        gof, gob = _gla_scan(p_gla, n_lat, gla_gate_up[l], gla_gate_b[l])
        y_gla_lat, y_gla_ctx = _gla_readout(gof, gob, g_gla, gla_norm_g[l], n_lat)

        i = l // 2
        merge_args = (gate, y_hy_lat, y_hy_ctx, rof, rob, rbonus, rg, y_s5, p_s5, y_gla_lat, y_gla_ctx,
                      rw_ln_g[l], rw_ln_b[l], s5_d[l], s5_glu_w[l], s5_glu_b[l], w_branch[l], w_out[l])
        if l % 2 == 0:
            h_all = _merge(h_all, modsel, n_tiles, n_lat_tiles, *merge_args)
            h_all = _ffn_dense(h_all, norm_ffn_g[l], modsel, n_lat_tiles, ffn_w1[i], ffn_w3[i], ffn_w2[i])
        else:
            h_all, xn, logits = _merge(h_all, modsel, n_lat_tiles, n_lat_tiles, *merge_args,
                                       router=(norm_ffn_g[l], moe_router_w[i], moe_router_b[i]))
            out = _moe_ffn_and_final_norm(h_all, xn, logits, modsel, n_lat_tiles, moe_w1[i], moe_w3[i], moe_w2[i], final_norm_g)
    return out.astype(x.dtype)
```
